```python
import jax, jax.numpy as jnp
from jax import lax
import numpy as np

D_MODEL = 1024
BATCH = 2
SEQ = 8192
DEPTH = 1

GRID_W = 64
CTX_LEN = 256
N_MOD = 6
POOL_WINDOWS = (2, 4, 8, 16)
N_POOL_GROUPS = 4
POOL_GROUP_DIM = D_MODEL // 8
POOL_WIDTH = N_POOL_GROUPS * POOL_GROUP_DIM
HEAD_DIM = 64
NA_HEADS = (D_MODEL // 2) // HEAD_DIM
NA_WIDTH = NA_HEADS * HEAD_DIM
NA_ROWS = 8
NA_COLS = 16
D_FF = 4 * D_MODEL
ROPE_THETA = 10000.0
EPS = 1e-6
IN_WIDTH = POOL_WIDTH + 3 * NA_WIDTH + 2 * D_MODEL
SPLIT_POINTS = (POOL_WIDTH, POOL_WIDTH + NA_WIDTH, POOL_WIDTH + 2 * NA_WIDTH,
                POOL_WIDTH + 3 * NA_WIDTH, POOL_WIDTH + 3 * NA_WIDTH + D_MODEL)
KV_START = POOL_WIDTH + NA_WIDTH
KV_END = POOL_WIDTH + 3 * NA_WIDTH

kernel_name = "hybrid_pool_natten_dit_block"


def rmsnorm(x, g):
    xf = x.astype(jnp.float32)
    y = xf * lax.rsqrt(jnp.mean(xf * xf, axis=-1, keepdims=True) + EPS)
    return (y * g.astype(jnp.float32)).astype(x.dtype)


def heads(a):
    return a.reshape(a.shape[0], a.shape[1], NA_HEADS, HEAD_DIM)


def axial_rope(a, ang_row, ang_col):
    n = HEAD_DIM // 4

    def rot(xh, ang):
        cs = jnp.cos(ang)[None, :, None, :]
        sn = jnp.sin(ang)[None, :, None, :]
        x1, x2 = xh[..., :n], xh[..., n:]
        return jnp.concatenate([x1 * cs - x2 * sn, x1 * sn + x2 * cs], axis=-1)

    af = a.astype(jnp.float32)
    out = jnp.concatenate([rot(af[..., :2 * n], ang_row), rot(af[..., 2 * n:], ang_col)], axis=-1)
    return out.astype(a.dtype)


def pool_mixer(u, pool_w, pool_scale):
    L = u.shape[1]
    uf = u.astype(jnp.float32)
    cs = jnp.concatenate([jnp.zeros_like(uf[:, :1]), jnp.cumsum(uf, axis=1)], axis=1)
    t = jnp.arange(L)
    outs = []
    for g, w in enumerate(POOL_WINDOWS):
        lo = jnp.clip(t - w // 2, 0, L)
        hi = jnp.clip(t + w // 2, 0, L)
        csg = cs[..., g * POOL_GROUP_DIM:(g + 1) * POOL_GROUP_DIM]
        mean = (csg[:, hi] - csg[:, lo]) / (hi - lo).astype(jnp.float32)[:, None]
        outs.append(mean - uf[..., g * POOL_GROUP_DIM:(g + 1) * POOL_GROUP_DIM])
    d = jnp.stack(outs, axis=2).astype(u.dtype)
    y = jnp.einsum('blgc,gce->blge', d, pool_w)
    return y.reshape(u.shape) * pool_scale


def neighbourhood_attention(q_rot, q_plain, k_rot, v, k_ctx, v_ctx, rpb):
    B, L = q_rot.shape[0], q_rot.shape[1]
    rows = L // GRID_W
    kr = min(NA_ROWS, rows)
    kc = NA_COLS
    scale = HEAD_DIM ** -0.5

    def grid(a):
        return a.reshape(B, rows, GRID_W, NA_HEADS, HEAD_DIM)

    qg, qpg, kg, vg = grid(q_rot), grid(q_plain), grid(k_rot), grid(v)
    col = jnp.arange(GRID_W)
    c0 = jnp.clip(col - kc // 2, 0, GRID_W - kc)
    col_idx = c0[:, None] + jnp.arange(kc)[None, :]
    col_rel = col_idx - col[:, None] + (NA_COLS - 1)

    def row_block(r):
        r0 = jnp.clip(r - kr // 2, 0, rows - kr)
        q_r = lax.dynamic_index_in_dim(qg, r, axis=1, keepdims=False)
        qp_r = lax.dynamic_index_in_dim(qpg, r, axis=1, keepdims=False)
        k_rows = lax.dynamic_slice_in_dim(kg, r0, kr, axis=1)
        v_rows = lax.dynamic_slice_in_dim(vg, r0, kr, axis=1)
        k_win = k_rows[:, :, col_idx]
        v_win = v_rows[:, :, col_idx]
        row_rel = r0 + jnp.arange(kr) - r + (NA_ROWS - 1)
        bias = rpb[:, row_rel][:, :, col_rel].transpose(0, 2, 1, 3)
        s_loc = jnp.einsum('bchd,brckhd->bhcrk', q_r, k_win).astype(jnp.float32) * scale
        s_loc = s_loc + bias.astype(jnp.float32)[None]
        s_ctx = jnp.einsum('bchd,bnhd->bhcn', qp_r, k_ctx).astype(jnp.float32) * scale
        s = jnp.concatenate([s_loc.reshape(B, NA_HEADS, GRID_W, kr * kc), s_ctx], axis=-1)
        p = jax.nn.softmax(s, axis=-1).astype(v.dtype)
        p_loc = p[..., :kr * kc].reshape(B, NA_HEADS, GRID_W, kr, kc)
        p_ctx = p[..., kr * kc:]
        return (jnp.einsum('bhcrk,brckhd->bchd', p_loc, v_win)
                + jnp.einsum('bhcn,bnhd->bchd', p_ctx, v_ctx))

    o = lax.map(row_block, jnp.arange(rows))
    return o.transpose(1, 0, 2, 3, 4).reshape(B, L, NA_WIDTH)


def context_attention(q, k, v):
    s = jnp.einsum('bqhd,bkhd->bhqk', q, k).astype(jnp.float32) * (HEAD_DIM ** -0.5)
    p = jax.nn.softmax(s, axis=-1).astype(v.dtype)
    o = jnp.einsum('bhqk,bkhd->bqhd', p, v)
    return o.reshape(o.shape[0], o.shape[1], NA_WIDTH)


def sq_relu_mlp(h, w1, w2):
    a = jax.nn.relu(h @ w1)
    return (a * a) @ w2


def setup_inputs(seed: int = 0) -> dict:
    key = jax.random.key(seed)
    ks = jax.random.split(key, 19)
    f32 = jnp.float32
    nrm = lambda k, shape, s: jax.random.normal(k, shape, f32) * s
    return {
        "x": nrm(ks[0], (BATCH, SEQ, D_MODEL), 1.0),
        "c": nrm(ks[1], (BATCH, D_MODEL), 1.0),
        "ctx": nrm(ks[2], (BATCH, CTX_LEN, D_MODEL), 1.0),
        "c_ctx": nrm(ks[3], (D_MODEL,), 1.0),
        "ada_w": nrm(ks[4], (DEPTH, D_MODEL, N_MOD * D_MODEL), D_MODEL ** -0.5),
        "ada_b": nrm(ks[5], (DEPTH, N_MOD * D_MODEL), 0.02),
        "norm1_g": 1.0 + nrm(ks[6], (DEPTH, D_MODEL), 0.05),
        "norm2_g": 1.0 + nrm(ks[7], (DEPTH, D_MODEL), 0.05),
        "w_in": nrm(ks[8], (DEPTH, D_MODEL, IN_WIDTH), D_MODEL ** -0.5),
        "pool_w": nrm(ks[9], (DEPTH, N_POOL_GROUPS, POOL_GROUP_DIM, POOL_GROUP_DIM), POOL_GROUP_DIM ** -0.5),
        "pool_scale": 1.0 + nrm(ks[10], (DEPTH, POOL_WIDTH), 0.1),
        "q_norm_g": 1.0 + nrm(ks[11], (DEPTH, HEAD_DIM), 0.05),
        "k_norm_g": 1.0 + nrm(ks[12], (DEPTH, HEAD_DIM), 0.05),
        "rpb": nrm(ks[13], (DEPTH, NA_HEADS, 2 * NA_ROWS - 1, 2 * NA_COLS - 1), 0.2),
        "w_branch_pool": nrm(ks[14], (DEPTH, POOL_WIDTH, D_MODEL), POOL_WIDTH ** -0.5),
        "w_branch_na": nrm(ks[15], (DEPTH, NA_WIDTH, D_MODEL), NA_WIDTH ** -0.5),
        "w_out": nrm(ks[16], (DEPTH, D_MODEL, D_MODEL), D_MODEL ** -0.5),
        "mlp_w1": nrm(ks[17], (DEPTH, D_MODEL, D_FF), D_MODEL ** -0.5),
        "mlp_w2": nrm(ks[18], (DEPTH, D_FF, D_MODEL), D_FF ** -0.5),
    }


def reference(x, c, ctx, c_ctx, ada_w, ada_b, norm1_g, norm2_g, w_in, pool_w, pool_scale,
              q_norm_g, k_norm_g, rpb, w_branch_pool, w_branch_na, w_out, mlp_w1, mlp_w2):
    L = x.shape[1]
    t = jnp.arange(L)
    n_freq = HEAD_DIM // 4
    inv = ROPE_THETA ** (-jnp.arange(n_freq, dtype=jnp.float32) / n_freq)
    ang_row = (t // GRID_W).astype(jnp.float32)[:, None] * inv
    ang_col = (t % GRID_W).astype(jnp.float32)[:, None] * inv
    silu_c = jax.nn.silu(c)
    silu_cc = jax.nn.silu(c_ctx)

    for l in range(DEPTH):
        last = l == DEPTH - 1
        mod = (silu_c @ ada_w[l] + ada_b[l])[:, None, :]
        sh1, s1, g1, sh2, s2, g2 = jnp.split(mod, N_MOD, axis=-1)
        mod_c = silu_cc @ ada_w[l] + ada_b[l]
        csh1, cs1, cg1, csh2, cs2, cg2 = jnp.split(mod_c, N_MOD, axis=-1)

        h = rmsnorm(x, norm1_g[l]) * (1.0 + s1) + sh1
        hc = rmsnorm(ctx, norm1_g[l]) * (1.0 + cs1) + csh1
        z = h @ w_in[l]
        u_pool, q, k, v, gate_pool, gate_na = jnp.split(z, SPLIT_POINTS, axis=-1)
        if last:
            zc_kv = hc @ w_in[l][:, KV_START:KV_END]
            kc_, vc_ = jnp.split(zc_kv, 2, axis=-1)
        else:
            zc = hc @ w_in[l]
            uc_pool, qc_, kc_, vc_, gc_pool, gc_na = jnp.split(zc, SPLIT_POINTS, axis=-1)
        k_ctx = rmsnorm(heads(kc_), k_norm_g[l])
        v_ctx = heads(vc_)

        q = rmsnorm(heads(q), q_norm_g[l])
        k = rmsnorm(heads(k), k_norm_g[l])
        q_rot = axial_rope(q, ang_row, ang_col)
        k_rot = axial_rope(k, ang_row, ang_col)
        na = neighbourhood_attention(q_rot, q, k_rot, heads(v), k_ctx, v_ctx, rpb[l])
        pool = pool_mixer(u_pool, pool_w[l], pool_scale[l])
        merged = (jax.nn.sigmoid(gate_pool) * (pool @ w_branch_pool[l])
                  + jax.nn.sigmoid(gate_na) * (na @ w_branch_na[l]))
        x_new = x + g1 * (merged @ w_out[l])
        h2 = rmsnorm(x_new, norm2_g[l]) * (1.0 + s2) + sh2
        x_new = x_new + g2 * sq_relu_mlp(h2, mlp_w1[l], mlp_w2[l])

        if not last:
            qc = rmsnorm(heads(qc_), q_norm_g[l])
            na_c = context_attention(qc, k_ctx, v_ctx)
            pool_c = pool_mixer(uc_pool, pool_w[l], pool_scale[l])
            merged_c = (jax.nn.sigmoid(gc_pool) * (pool_c @ w_branch_pool[l])
                        + jax.nn.sigmoid(gc_na) * (na_c @ w_branch_na[l]))
            ctx = ctx + cg1 * (merged_c @ w_out[l])
            h2c = rmsnorm(ctx, norm2_g[l]) * (1.0 + cs2) + csh2
            ctx = ctx + cg2 * sq_relu_mlp(h2c, mlp_w1[l], mlp_w2[l])
        x = x_new
    return x
```

```python
import functools

import numpy as np
import jax
import jax.numpy as jnp
from jax import lax
from jax.experimental import pallas as pl
from jax.experimental.pallas import tpu as pltpu

GRID_W = 64
N_MOD = 6
POOL_WINDOWS = (2, 4, 8, 16)
HEAD_DIM = 64
NA_ROWS = 8
NA_COLS = 16
ROPE_THETA = 10000.0
EPS = 1e-6
MASK_VALUE = -1e30

VMEM_LIMIT_BYTES = 56 * 1024 * 1024
POOL_HALO = 8
TILE_ROWS = 4
KEY_ROWS = TILE_ROWS + NA_ROWS

BF16 = jnp.bfloat16
F32 = jnp.float32


def _dot(a, b):
    return jnp.dot(a, b, preferred_element_type=F32)


def _dot_nt(a, b):
    return lax.dot_general(a, b, (((1,), (1,)), ((), ())), preferred_element_type=F32)


def _params(sem):
    return pltpu.CompilerParams(dimension_semantics=sem, vmem_limit_bytes=VMEM_LIMIT_BYTES)


def _rms(x):
    return x * lax.rsqrt(jnp.mean(x * x, axis=-1, keepdims=True) + EPS)


def _mod_kernel(c_ref, w_ref, b_ref, o_ref):
    c = c_ref[...]
    s = (c * jax.nn.sigmoid(c)).astype(BF16)
    o_ref[...] = _dot(s, w_ref[...].astype(BF16)) + b_ref[...]


def _modulation(cc, ada_w, ada_b):
    m, d = cc.shape
    n = ada_w.shape[1]
    return pl.pallas_call(
        _mod_kernel,
        grid=(n // d,),
        in_specs=[pl.BlockSpec((m, d), lambda j: (0, 0)),
                  pl.BlockSpec((d, d), lambda j: (0, j)),
                  pl.BlockSpec((1, d), lambda j: (0, j))],
        out_specs=pl.BlockSpec((m, d), lambda j: (0, j)),
        out_shape=jax.ShapeDtypeStruct((m, n), F32),
        compiler_params=_params(("arbitrary",)),
        name="modulation",
    )(cc, ada_w, ada_b.reshape(1, n))


def _head_norm(a, mean_mat, g):
    ms = _dot((a * a).astype(BF16), mean_mat)
    return a * lax.rsqrt(ms + EPS) * g


def _rope(a, cos, sin_signed, first_half):
    width = a.shape[-1]
    quarter = HEAD_DIM // 4
    up = pltpu.roll(a, width - quarter, axis=1)
    down = pltpu.roll(a, quarter, axis=1)
    swapped = jnp.where(first_half, up, down)
    return a * cos + swapped * sin_signed


def _ctx_kernel(x_ref, mod_ref, g_ref, w_ref, kg_ref, mean_ref, k_ref, v_ref):
    nw = k_ref.shape[-1]
    h = _rms(x_ref[0]) * g_ref[...] * (1.0 + mod_ref[0, 1:2, :]) + mod_ref[0, 0:1, :]
    z = _dot(h.astype(BF16), w_ref[...])
    k_ref[0] = _head_norm(z[:, :nw], mean_ref[...], kg_ref[...]).astype(BF16)
    v_ref[0] = z[:, nw:].astype(BF16)


def _ctx_proj(ctx, mod3, norm_g, w_in_bf, kg_t, mean_mat, kv_block, ctx_row):
    b, n, d = ctx.shape
    nw = kg_t.shape[-1]
    out = jax.ShapeDtypeStruct((b, n, nw), BF16)
    return pl.pallas_call(
        _ctx_kernel,
        grid=(b,),
        in_specs=[pl.BlockSpec((1, n, d), lambda i: (i, 0, 0)),
                  pl.BlockSpec((1, N_MOD, d), lambda i: (ctx_row, 0, 0)),
                  pl.BlockSpec((1, d), lambda i: (0, 0)),
                  pl.BlockSpec((d, 2 * nw), lambda i: (0, kv_block)),
                  pl.BlockSpec((1, nw), lambda i: (0, 0)),
                  pl.BlockSpec((nw, nw), lambda i: (0, 0))],
        out_specs=[pl.BlockSpec((1, n, nw), lambda i: (i, 0, 0))] * 2,
        out_shape=[out, out],
        compiler_params=_params(("arbitrary",)),
        name="ctx_proj",
    )(ctx, mod3, norm_g, w_in_bf, kg_t, mean_mat)


def _in_proj_kernel(x_ref, mod_ref, g_ref, w_ref, cos_ref, sin_ref, qg_ref, kg_ref, mean_ref,
                    u_ref, qr_ref, qp_ref, kr_ref, v_ref, gp_ref, gn_ref):
    pw = u_ref.shape[-1]
    nw = qr_ref.shape[-1]
    d = x_ref.shape[-1]
    scale = HEAD_DIM ** -0.5

    h = _rms(x_ref[0]) * g_ref[...] * (1.0 + mod_ref[0, 1:2, :]) + mod_ref[0, 0:1, :]
    hb = h.astype(BF16)

    reps = nw // cos_ref.shape[-1]
    cos = jnp.concatenate([cos_ref[...]] * reps, axis=1)
    sin = jnp.concatenate([sin_ref[...]] * reps, axis=1)
    lane = lax.broadcasted_iota(jnp.int32, (1, nw), 1)
    first_half = (lane % (HEAD_DIM // 2)) < (HEAD_DIM // 4)
    mean_mat = mean_ref[...]

    c0 = 0
    u_ref[0] = _dot(hb, w_ref[:, c0:c0 + pw])
    c0 += pw
    q = _head_norm(_dot(hb, w_ref[:, c0:c0 + nw]), mean_mat, qg_ref[...]) * scale
    qp_ref[0] = q.astype(BF16)
    qr_ref[0] = _rope(q, cos, sin, first_half).astype(BF16)
    c0 += nw
    k = _head_norm(_dot(hb, w_ref[:, c0:c0 + nw]), mean_mat, kg_ref[...])
    kr_ref[0] = _rope(k, cos, sin, first_half).astype(BF16)
    c0 += nw
    v_ref[0] = _dot(hb, w_ref[:, c0:c0 + nw]).astype(BF16)
    c0 += nw
    gp_ref[0] = jax.nn.sigmoid(_dot(hb, w_ref[:, c0:c0 + d])).astype(BF16)
    c0 += d
    gn_ref[0] = jax.nn.sigmoid(_dot(hb, w_ref[:, c0:c0 + d])).astype(BF16)


def _in_proj(x, mod3, norm_g, w_in_bf, cos_t, sin_t, qg_t, kg_t, mean_mat, pw, tm):
    b, l, d = x.shape
    nw = qg_t.shape[-1]
    tw = cos_t.shape[-1]
    tok = lambda bi, i: (bi, i, 0)
    const = lambda bi, i: (0, 0)
    sd = jax.ShapeDtypeStruct
    return pl.pallas_call(
        _in_proj_kernel,
        grid=(b, l // tm),
        in_specs=[pl.BlockSpec((1, tm, d), tok),
                  pl.BlockSpec((1, N_MOD, d), lambda bi, i: (bi, 0, 0)),
                  pl.BlockSpec((1, d), const),
                  pl.BlockSpec(w_in_bf.shape, const),
                  pl.BlockSpec((tm, tw), lambda bi, i: (i, 0)),
                  pl.BlockSpec((tm, tw), lambda bi, i: (i, 0)),
                  pl.BlockSpec((1, nw), const),
                  pl.BlockSpec((1, nw), const),
                  pl.BlockSpec((nw, nw), const)],
        out_specs=[pl.BlockSpec((1, tm, pw), tok)] + [pl.BlockSpec((1, tm, nw), tok)] * 4
                  + [pl.BlockSpec((1, tm, d), tok)] * 2,
        out_shape=[sd((b, l, pw), F32)] + [sd((b, l, nw), BF16)] * 4 + [sd((b, l, d), BF16)] * 2,
        compiler_params=_params(("arbitrary", "arbitrary")),
        name="in_proj",
    )(x, mod3, norm_g, w_in_bf, cos_t, sin_t, qg_t, kg_t, mean_mat)


def _mixer_kernel(x_ref, mod_ref, u_ref, up_ref, un_ref, qr_ref, qp_ref,
                  k0_ref, k1_ref, k2_ref, v0_ref, v1_ref, v2_ref, kc_ref, vc_ref, bias_ref,
                  gp_ref, gn_ref, pw_ref, ps_ref, wbp_ref, wbn_ref, wo_ref,
                  o_ref, na_scr, ue_scr, *, seq_len):
    i = pl.program_id(1)
    tm = x_ref.shape[1]
    nw = qr_ref.shape[-1]
    heads = nw // HEAD_DIM

    for h in range(heads):
        hs = slice(h * HEAD_DIM, (h + 1) * HEAD_DIM)
        q_rot = qr_ref[0, :, hs]
        q_plain = qp_ref[0, :, hs]
        k_loc = jnp.concatenate([k0_ref[0, :, hs], k1_ref[0, :, hs], k2_ref[0, :, hs]], axis=0)
        v_loc = jnp.concatenate([v0_ref[0, :, hs], v1_ref[0, :, hs], v2_ref[0, :, hs]], axis=0)
        s_loc = _dot_nt(q_rot, k_loc) + bias_ref[0, h]
        s_ctx = _dot_nt(q_plain, kc_ref[0, :, hs])
        m = jnp.maximum(jnp.max(s_loc, axis=-1, keepdims=True), jnp.max(s_ctx, axis=-1, keepdims=True))
        p_loc = jnp.exp(s_loc - m)
        p_ctx = jnp.exp(s_ctx - m)
        denom = jnp.sum(p_loc, axis=-1, keepdims=True) + jnp.sum(p_ctx, axis=-1, keepdims=True)
        o = _dot(p_loc.astype(BF16), v_loc) + _dot(p_ctx.astype(BF16), vc_ref[0, :, hs])
        na_scr[:, hs] = o / denom

    pos = i * tm + lax.broadcasted_iota(jnp.int32, (tm, 1), 0)
    ue_scr[0:POOL_HALO, :] = jnp.where(i > 0, up_ref[0], 0.0)
    ue_scr[POOL_HALO:POOL_HALO + tm, :] = u_ref[0]
    ue_scr[POOL_HALO + tm:, :] = jnp.where((i + 1) * tm < seq_len, un_ref[0], 0.0)
    gd = pw_ref.shape[-1]
    pooled = []
    for g, w in enumerate(POOL_WINDOWS):
        gs = slice(g * gd, (g + 1) * gd)
        acc = ue_scr[POOL_HALO - w // 2:POOL_HALO - w // 2 + tm, gs]
        for j in range(1 - w // 2, w // 2):
            acc = acc + ue_scr[POOL_HALO + j:POOL_HALO + j + tm, gs]
        cnt = (jnp.minimum(pos + w // 2, seq_len) - jnp.maximum(pos - w // 2, 0)).astype(F32)
        dlt = acc / cnt - u_ref[0, :, gs]
        pooled.append(_dot(dlt.astype(BF16), pw_ref[g]))
    pool = jnp.concatenate(pooled, axis=1) * ps_ref[...]

    merged = (gp_ref[0].astype(F32) * _dot(pool.astype(BF16), wbp_ref[...])
              + gn_ref[0].astype(F32) * _dot(na_scr[...].astype(BF16), wbn_ref[...]))
    o_ref[0] = x_ref[0] + mod_ref[0, 2:3, :] * _dot(merged.astype(BF16), wo_ref[...])


def _mixer(x, mod3, u, qr, qp, kr, v, kc, vc, bias, gp, gn, pool_w_bf, pool_scale, wbp, wbn, wo):
    b, l, d = x.shape
    pw = u.shape[-1]
    nw = qr.shape[-1]
    tm = TILE_ROWS * GRID_W
    nt = l // tm
    hb = tm // POOL_HALO
    n_halo = l // POOL_HALO
    nc = kc.shape[1]

    tok = lambda bi, i: (bi, i, 0)
    const2 = lambda bi, i: (0, 0)

    def kv_spec(j):
        return pl.BlockSpec((1, tm, nw), lambda bi, i: (bi, jnp.clip(i - 1, 0, nt - 3) + j, 0))

    def bias_idx(bi, i):
        return (jnp.where(i == 0, 0, jnp.where(i == nt - 1, 2, 1)), 0, 0, 0)

    in_specs = [
        pl.BlockSpec((1, tm, d), tok),
        pl.BlockSpec((1, N_MOD, d), lambda bi, i: (bi, 0, 0)),
        pl.BlockSpec((1, tm, pw), tok),
        pl.BlockSpec((1, POOL_HALO, pw), lambda bi, i: (bi, jnp.maximum(i * hb - 1, 0), 0)),
        pl.BlockSpec((1, POOL_HALO, pw), lambda bi, i: (bi, jnp.minimum((i + 1) * hb, n_halo - 1), 0)),
        pl.BlockSpec((1, tm, nw), tok),
        pl.BlockSpec((1, tm, nw), tok),
        kv_spec(0), kv_spec(1), kv_spec(2),
        kv_spec(0), kv_spec(1), kv_spec(2),
        pl.BlockSpec((1, nc, nw), lambda bi, i: (bi, 0, 0)),
        pl.BlockSpec((1, nc, nw), lambda bi, i: (bi, 0, 0)),
        pl.BlockSpec((1,) + bias.shape[1:], bias_idx),
        pl.BlockSpec((1, tm, d), tok),
        pl.BlockSpec((1, tm, d), tok),
        pl.BlockSpec(pool_w_bf.shape, lambda bi, i: (0, 0, 0)),
        pl.BlockSpec((1, pw), const2),
        pl.BlockSpec(wbp.shape, const2),
        pl.BlockSpec(wbn.shape, const2),
        pl.BlockSpec(wo.shape, const2),
    ]
    return pl.pallas_call(
        functools.partial(_mixer_kernel, seq_len=l),
        grid=(b, nt),
        in_specs=in_specs,
        out_specs=pl.BlockSpec((1, tm, d), tok),
        out_shape=jax.ShapeDtypeStruct((b, l, d), F32),
        scratch_shapes=[pltpu.VMEM((tm, nw), F32), pltpu.VMEM((tm + 2 * POOL_HALO, pw), F32)],
        compiler_params=_params(("arbitrary", "arbitrary")),
        name="mixer",
    )(x, mod3, u, u, u, qr, qp, kr, kr, kr, v, v, v, kc, vc, bias, gp, gn,
      pool_w_bf, pool_scale, wbp, wbn, wo)


def _mlp_kernel(x_ref, mod_ref, g_ref, w1_ref, w2_ref, o_ref, *, chunk):
    x = x_ref[0]
    h = _rms(x) * g_ref[...] * (1.0 + mod_ref[0, 4:5, :]) + mod_ref[0, 3:4, :]
    hb = h.astype(BF16)
    acc = jnp.zeros(x.shape, F32)
    for c0 in range(0, w1_ref.shape[1], chunk):
        a = jnp.maximum(_dot(hb, w1_ref[:, c0:c0 + chunk]), 0.0)
        acc = acc + _dot((a * a).astype(BF16), w2_ref[c0:c0 + chunk, :])
    o_ref[0] = x + mod_ref[0, 5:6, :] * acc


def _mlp(x, mod3, norm_g, w1, w2, tm):
    b, l, d = x.shape
    tok = lambda bi, i: (bi, i, 0)
    const = lambda bi, i: (0, 0)
    return pl.pallas_call(
        functools.partial(_mlp_kernel, chunk=d),
        grid=(b, l // tm),
        in_specs=[pl.BlockSpec((1, tm, d), tok),
                  pl.BlockSpec((1, N_MOD, d), lambda bi, i: (bi, 0, 0)),
                  pl.BlockSpec((1, d), const),
                  pl.BlockSpec(w1.shape, const, pipeline_mode=pl.Buffered(1)),
                  pl.BlockSpec(w2.shape, const, pipeline_mode=pl.Buffered(1))],
        out_specs=pl.BlockSpec((1, tm, d), tok),
        out_shape=jax.ShapeDtypeStruct((b, l, d), F32),
        compiler_params=_params(("arbitrary", "arbitrary")),
        name="mlp",
    )(x, mod3, norm_g, w1, w2)


def _rope_tables(seq_len):
    n_freq = HEAD_DIM // 4
    t = jnp.arange(seq_len)
    inv = ROPE_THETA ** (-jnp.arange(n_freq, dtype=F32) / n_freq)
    ang_row = (t // GRID_W).astype(F32)[:, None] * inv
    ang_col = (t % GRID_W).astype(F32)[:, None] * inv
    cr, sr, cc, sc = jnp.cos(ang_row), jnp.sin(ang_row), jnp.cos(ang_col), jnp.sin(ang_col)
    cos = jnp.concatenate([cr, cr, cc, cc], axis=1)
    sin = jnp.concatenate([-sr, sr, -sc, sc], axis=1)
    return jnp.tile(cos, (1, 2)), jnp.tile(sin, (1, 2))


def _bias_tables(rpb, rows):
    nt = rows // TILE_ROWS
    tables = []
    for tile in (0, nt // 2, nt - 1):
        r = tile * TILE_ROWS
        ks = min(max(r - TILE_ROWS, 0), rows - KEY_ROWS)
        qr = (r + np.arange(TILE_ROWS))[:, None, None, None]
        qc = np.arange(GRID_W)[None, :, None, None]
        kr = (ks + np.arange(KEY_ROWS))[None, None, :, None]
        kc = np.arange(GRID_W)[None, None, None, :]
        r0 = np.clip(qr - NA_ROWS // 2, 0, rows - NA_ROWS)
        c0 = np.clip(qc - NA_COLS // 2, 0, GRID_W - NA_COLS)
        valid = (kr >= r0) & (kr < r0 + NA_ROWS) & (kc >= c0) & (kc < c0 + NA_COLS)
        shape = (TILE_ROWS * GRID_W, KEY_ROWS * GRID_W)
        valid = np.broadcast_to(valid, (TILE_ROWS, GRID_W, KEY_ROWS, GRID_W)).reshape(shape)
        row_rel = np.broadcast_to(np.clip(kr - qr + NA_ROWS - 1, 0, 2 * NA_ROWS - 2),
                                  (TILE_ROWS, GRID_W, KEY_ROWS, GRID_W)).reshape(shape)
        col_rel = np.broadcast_to(np.clip(kc - qc + NA_COLS - 1, 0, 2 * NA_COLS - 2),
                                  (TILE_ROWS, GRID_W, KEY_ROWS, GRID_W)).reshape(shape)
        tables.append(jnp.where(valid[None], rpb[:, row_rel, col_rel].astype(F32), MASK_VALUE))
    return jnp.stack(tables)


def kernel(x, c, ctx, c_ctx, ada_w, ada_b, norm1_g, norm2_g, w_in, pool_w, pool_scale,
           q_norm_g, k_norm_g, rpb, w_branch_pool, w_branch_na, w_out, mlp_w1, mlp_w2):
    b, l, d = x.shape
    depth = ada_w.shape[0]
    pw = pool_scale.shape[-1]
    nw = w_branch_na.shape[1]
    heads = nw // HEAD_DIM
    rows = l // GRID_W
    assert l % (TILE_ROWS * GRID_W) == 0 and rows >= KEY_ROWS
    assert w_in.shape[-1] == pw + 3 * nw + 2 * d and (pw + nw) % (2 * nw) == 0
    kv_block = (pw + nw) // (2 * nw)
    ctx_row = b

    cos_t, sin_t = _rope_tables(l)
    mean_mat = jnp.asarray(np.kron(np.eye(heads), np.full((HEAD_DIM, HEAD_DIM), 1.0 / HEAD_DIM)), BF16)
    pad = (-(b + 1)) % 8
    cc = jnp.concatenate([c, c_ctx[None], jnp.zeros((pad, d), F32)], axis=0)

    for layer in range(depth):
        last = layer == depth - 1
        if not last:
            raise NotImplementedError("context-stream update is only needed for DEPTH > 1")
        mod = _modulation(cc, ada_w[layer], ada_b[layer])
        mod3 = mod.reshape(mod.shape[0], N_MOD, d)
        w_in_bf = w_in[layer].astype(BF16)
        qg_t = jnp.tile(q_norm_g[layer], heads)[None]
        kg_t = jnp.tile(k_norm_g[layer], heads)[None]
        n1 = norm1_g[layer][None]

        k_ctx, v_ctx = _ctx_proj(ctx, mod3, n1, w_in_bf, kg_t, mean_mat, kv_block, ctx_row)
        u, qr, qp, kr, v, gp, gn = _in_proj(x, mod3, n1, w_in_bf, cos_t, sin_t, qg_t, kg_t,
                                             mean_mat, pw, tm=512)
        bias = _bias_tables(rpb[layer], rows)
        x = _mixer(x, mod3, u, qr, qp, kr, v, k_ctx, v_ctx, bias, gp, gn,
                   pool_w[layer].astype(BF16), pool_scale[layer][None],
                   w_branch_pool[layer].astype(BF16), w_branch_na[layer].astype(BF16),
                   w_out[layer].astype(BF16))
        x = _mlp(x, mod3, norm2_g[layer][None], mlp_w1[layer].astype(BF16),
                 mlp_w2[layer].astype(BF16), tm=512)
    return x
```

```python
import functools

import numpy as np
import jax
import jax.numpy as jnp
from jax import lax
from jax.experimental import pallas as pl
from jax.experimental.pallas import tpu as pltpu

GRID_W = 64
N_MOD = 6
POOL_WINDOWS = (2, 4, 8, 16)
HEAD_DIM = 64
NA_ROWS = 8
NA_COLS = 16
ROPE_THETA = 10000.0
EPS = 1e-6
MASK_VALUE = -1e30

VMEM_LIMIT_BYTES = 56 * 1024 * 1024
POOL_HALO = 8
TILE_ROWS = 4
KEY_ROWS = TILE_ROWS + NA_ROWS

BF16 = jnp.bfloat16
F32 = jnp.float32


def _dot(a, b):
    return jnp.dot(a, b, preferred_element_type=F32)


def _dot_nt(a, b):
    return lax.dot_general(a, b, (((1,), (1,)), ((), ())), preferred_element_type=F32)


def _params(sem):
    return pltpu.CompilerParams(dimension_semantics=sem, vmem_limit_bytes=VMEM_LIMIT_BYTES)


def _rms(x):
    return x * lax.rsqrt(jnp.mean(x * x, axis=-1, keepdims=True) + EPS)


def _mod_kernel(c_ref, w_ref, b_ref, o_ref):
    c = c_ref[...]
    s = (c * jax.nn.sigmoid(c)).astype(BF16)
    o_ref[...] = _dot(s, w_ref[...].astype(BF16)) + b_ref[...]


def _modulation(cc, ada_w, ada_b):
    m, d = cc.shape
    n = ada_w.shape[1]
    return pl.pallas_call(
        _mod_kernel,
        grid=(n // d,),
        in_specs=[pl.BlockSpec((m, d), lambda j: (0, 0)),
                  pl.BlockSpec((d, d), lambda j: (0, j)),
                  pl.BlockSpec((1, d), lambda j: (0, j))],
        out_specs=pl.BlockSpec((m, d), lambda j: (0, j)),
        out_shape=jax.ShapeDtypeStruct((m, n), F32),
        compiler_params=_params(("arbitrary",)),
        name="modulation",
    )(cc, ada_w, ada_b.reshape(1, n))


def _head_norm(a, mean_mat, g):
    ms = _dot((a * a).astype(BF16), mean_mat)
    return a * lax.rsqrt(ms + EPS) * g


def _rope(a, cos, sin_signed, first_half):
    width = a.shape[-1]
    quarter = HEAD_DIM // 4
    up = pltpu.roll(a, width - quarter, axis=1)
    down = pltpu.roll(a, quarter, axis=1)
    swapped = jnp.where(first_half, up, down)
    return a * cos + swapped * sin_signed


def _ctx_kernel(x_ref, mod_ref, g_ref, w_ref, kg_ref, mean_ref, k_ref, v_ref):
    nw = k_ref.shape[-1]
    h = _rms(x_ref[0]) * g_ref[...] * (1.0 + mod_ref[0, 1:2, :]) + mod_ref[0, 0:1, :]
    z = _dot(h.astype(BF16), w_ref[...])
    k_ref[0] = _head_norm(z[:, :nw], mean_ref[...], kg_ref[...]).astype(BF16)
    v_ref[0] = z[:, nw:].astype(BF16)


def _ctx_proj(ctx, mod3, norm_g, w_in_bf, kg_t, mean_mat, kv_block, ctx_row):
    b, n, d = ctx.shape
    nw = kg_t.shape[-1]
    out = jax.ShapeDtypeStruct((b, n, nw), BF16)
    return pl.pallas_call(
        _ctx_kernel,
        grid=(b,),
        in_specs=[pl.BlockSpec((1, n, d), lambda i: (i, 0, 0)),
                  pl.BlockSpec((1, N_MOD, d), lambda i: (ctx_row, 0, 0)),
                  pl.BlockSpec((1, d), lambda i: (0, 0)),
                  pl.BlockSpec((d, 2 * nw), lambda i: (0, kv_block)),
                  pl.BlockSpec((1, nw), lambda i: (0, 0)),
                  pl.BlockSpec((nw, nw), lambda i: (0, 0))],
        out_specs=[pl.BlockSpec((1, n, nw), lambda i: (i, 0, 0))] * 2,
        out_shape=[out, out],
        compiler_params=_params(("arbitrary",)),
        name="ctx_proj",
    )(ctx, mod3, norm_g, w_in_bf, kg_t, mean_mat)


def _in_proj_kernel(x_ref, mod_ref, g_ref, w_ref, cos_ref, sin_ref, qg_ref, kg_ref, mean_ref,
                    u_ref, qr_ref, qp_ref, kr_ref, v_ref, gp_ref, gn_ref):
    pw = u_ref.shape[-1]
    nw = qr_ref.shape[-1]
    d = x_ref.shape[-1]
    scale = HEAD_DIM ** -0.5

    h = _rms(x_ref[0]) * g_ref[...] * (1.0 + mod_ref[0, 1:2, :]) + mod_ref[0, 0:1, :]
    hb = h.astype(BF16)

    reps = nw // cos_ref.shape[-1]
    cos = jnp.concatenate([cos_ref[...]] * reps, axis=1)
    sin = jnp.concatenate([sin_ref[...]] * reps, axis=1)
    lane = lax.broadcasted_iota(jnp.int32, (1, nw), 1)
    first_half = (lane % (HEAD_DIM // 2)) < (HEAD_DIM // 4)
    mean_mat = mean_ref[...]

    c0 = 0
    u_ref[0] = _dot(hb, w_ref[:, c0:c0 + pw])
    c0 += pw
    q = _head_norm(_dot(hb, w_ref[:, c0:c0 + nw]), mean_mat, qg_ref[...]) * scale
    qp_ref[0] = q.astype(BF16)
    qr_ref[0] = _rope(q, cos, sin, first_half).astype(BF16)
    c0 += nw
    k = _head_norm(_dot(hb, w_ref[:, c0:c0 + nw]), mean_mat, kg_ref[...])
    kr_ref[0] = _rope(k, cos, sin, first_half).astype(BF16)
    c0 += nw
    v_ref[0] = _dot(hb, w_ref[:, c0:c0 + nw]).astype(BF16)
    c0 += nw
    gp_ref[0] = jax.nn.sigmoid(_dot(hb, w_ref[:, c0:c0 + d])).astype(BF16)
    c0 += d
    gn_ref[0] = jax.nn.sigmoid(_dot(hb, w_ref[:, c0:c0 + d])).astype(BF16)


def _in_proj(x, mod3, norm_g, w_in_bf, cos_t, sin_t, qg_t, kg_t, mean_mat, pw, tm):
    b, l, d = x.shape
    nw = qg_t.shape[-1]
    tw = cos_t.shape[-1]
    tok = lambda bi, i: (bi, i, 0)
    const = lambda bi, i: (0, 0)
    sd = jax.ShapeDtypeStruct
    return pl.pallas_call(
        _in_proj_kernel,
        grid=(b, l // tm),
        in_specs=[pl.BlockSpec((1, tm, d), tok),
                  pl.BlockSpec((1, N_MOD, d), lambda bi, i: (bi, 0, 0)),
                  pl.BlockSpec((1, d), const),
                  pl.BlockSpec(w_in_bf.shape, const),
                  pl.BlockSpec((tm, tw), lambda bi, i: (i, 0)),
                  pl.BlockSpec((tm, tw), lambda bi, i: (i, 0)),
                  pl.BlockSpec((1, nw), const),
                  pl.BlockSpec((1, nw), const),
                  pl.BlockSpec((nw, nw), const)],
        out_specs=[pl.BlockSpec((1, tm, pw), tok)] + [pl.BlockSpec((1, tm, nw), tok)] * 4
                  + [pl.BlockSpec((1, tm, d), tok)] * 2,
        out_shape=[sd((b, l, pw), F32)] + [sd((b, l, nw), BF16)] * 4 + [sd((b, l, d), BF16)] * 2,
        compiler_params=_params(("arbitrary", "arbitrary")),
        name="in_proj",
    )(x, mod3, norm_g, w_in_bf, cos_t, sin_t, qg_t, kg_t, mean_mat)


def _mixer_kernel(x_ref, mod_ref, u_ref, up_ref, un_ref, qr_ref, qp_ref,
                  k0_ref, k1_ref, k2_ref, v0_ref, v1_ref, v2_ref, kc_ref, vc_ref, bias_ref,
                  gp_ref, gn_ref, pw_ref, ps_ref, wbp_ref, wbn_ref, wo_ref,
                  o_ref, na_scr, ue_scr, *, seq_len):
    i = pl.program_id(1)
    tm = x_ref.shape[1]
    nw = qr_ref.shape[-1]
    heads = nw // HEAD_DIM

    for h in range(heads):
        hs = slice(h * HEAD_DIM, (h + 1) * HEAD_DIM)
        q_rot = qr_ref[0, :, hs]
        q_plain = qp_ref[0, :, hs]
        k_loc = jnp.concatenate([k0_ref[0, :, hs], k1_ref[0, :, hs], k2_ref[0, :, hs]], axis=0)
        v_loc = jnp.concatenate([v0_ref[0, :, hs], v1_ref[0, :, hs], v2_ref[0, :, hs]], axis=0)
        s_loc = _dot_nt(q_rot, k_loc) + bias_ref[0, h]
        s_ctx = _dot_nt(q_plain, kc_ref[0, :, hs])
        m = jnp.maximum(jnp.max(s_loc, axis=-1, keepdims=True), jnp.max(s_ctx, axis=-1, keepdims=True))
        p_loc = jnp.exp(s_loc - m)
        p_ctx = jnp.exp(s_ctx - m)
        denom = jnp.sum(p_loc, axis=-1, keepdims=True) + jnp.sum(p_ctx, axis=-1, keepdims=True)
        o = _dot(p_loc.astype(BF16), v_loc) + _dot(p_ctx.astype(BF16), vc_ref[0, :, hs])
        na_scr[:, hs] = o / denom

    pos = i * tm + lax.broadcasted_iota(jnp.int32, (tm, 1), 0)
    ue_scr[0:POOL_HALO, :] = jnp.where(i > 0, up_ref[0], 0.0)
    ue_scr[POOL_HALO:POOL_HALO + tm, :] = u_ref[0]
    ue_scr[POOL_HALO + tm:, :] = jnp.where((i + 1) * tm < seq_len, un_ref[0], 0.0)
    gd = pw_ref.shape[-1]
    pooled = []
    for g, w in enumerate(POOL_WINDOWS):
        gs = slice(g * gd, (g + 1) * gd)
        acc = ue_scr[POOL_HALO - w // 2:POOL_HALO - w // 2 + tm, gs]
        for j in range(1 - w // 2, w // 2):
            acc = acc + ue_scr[POOL_HALO + j:POOL_HALO + j + tm, gs]
        cnt = (jnp.minimum(pos + w // 2, seq_len) - jnp.maximum(pos - w // 2, 0)).astype(F32)
        dlt = acc / cnt - u_ref[0, :, gs]
        pooled.append(_dot(dlt.astype(BF16), pw_ref[g]))
    pool = jnp.concatenate(pooled, axis=1) * ps_ref[...]

    merged = (gp_ref[0].astype(F32) * _dot(pool.astype(BF16), wbp_ref[...])
              + gn_ref[0].astype(F32) * _dot(na_scr[...].astype(BF16), wbn_ref[...]))
    o_ref[0] = x_ref[0] + mod_ref[0, 2:3, :] * _dot(merged.astype(BF16), wo_ref[...])


def _mixer(x, mod3, u, qr, qp, kr, v, kc, vc, bias, gp, gn, pool_w_bf, pool_scale, wbp, wbn, wo):
    b, l, d = x.shape
    pw = u.shape[-1]
    nw = qr.shape[-1]
    tm = TILE_ROWS * GRID_W
    nt = l // tm
    hb = tm // POOL_HALO
    n_halo = l // POOL_HALO
    nc = kc.shape[1]

    tok = lambda bi, i: (bi, i, 0)
    const2 = lambda bi, i: (0, 0)

    def kv_spec(j):
        return pl.BlockSpec((1, tm, nw), lambda bi, i: (bi, jnp.clip(i - 1, 0, nt - 3) + j, 0))

    def bias_idx(bi, i):
        return (jnp.where(i == 0, 0, jnp.where(i == nt - 1, 2, 1)), 0, 0, 0)

    in_specs = [
        pl.BlockSpec((1, tm, d), tok),
        pl.BlockSpec((1, N_MOD, d), lambda bi, i: (bi, 0, 0)),
        pl.BlockSpec((1, tm, pw), tok),
        pl.BlockSpec((1, POOL_HALO, pw), lambda bi, i: (bi, jnp.maximum(i * hb - 1, 0), 0)),
        pl.BlockSpec((1, POOL_HALO, pw), lambda bi, i: (bi, jnp.minimum((i + 1) * hb, n_halo - 1), 0)),
        pl.BlockSpec((1, tm, nw), tok),
        pl.BlockSpec((1, tm, nw), tok),
        kv_spec(0), kv_spec(1), kv_spec(2),
        kv_spec(0), kv_spec(1), kv_spec(2),
        pl.BlockSpec((1, nc, nw), lambda bi, i: (bi, 0, 0)),
        pl.BlockSpec((1, nc, nw), lambda bi, i: (bi, 0, 0)),
        pl.BlockSpec((1,) + bias.shape[1:], bias_idx),
        pl.BlockSpec((1, tm, d), tok),
        pl.BlockSpec((1, tm, d), tok),
        pl.BlockSpec(pool_w_bf.shape, lambda bi, i: (0, 0, 0)),
        pl.BlockSpec((1, pw), const2),
        pl.BlockSpec(wbp.shape, const2),
        pl.BlockSpec(wbn.shape, const2),
        pl.BlockSpec(wo.shape, const2),
    ]
    return pl.pallas_call(
        functools.partial(_mixer_kernel, seq_len=l),
        grid=(b, nt),
        in_specs=in_specs,
        out_specs=pl.BlockSpec((1, tm, d), tok),
        out_shape=jax.ShapeDtypeStruct((b, l, d), F32),
        scratch_shapes=[pltpu.VMEM((tm, nw), F32), pltpu.VMEM((tm + 2 * POOL_HALO, pw), F32)],
        compiler_params=_params(("arbitrary", "arbitrary")),
        name="mixer",
    )(x, mod3, u, u, u, qr, qp, kr, kr, kr, v, v, v, kc, vc, bias, gp, gn,
      pool_w_bf, pool_scale, wbp, wbn, wo)


def _mlp_kernel(x_ref, mod_ref, g_ref, w1_ref, w2_ref, o_ref, *, chunk):
    x = x_ref[0]
    h = _rms(x) * g_ref[...] * (1.0 + mod_ref[0, 4:5, :]) + mod_ref[0, 3:4, :]
    hb = h.astype(BF16)
    acc = jnp.zeros(x.shape, F32)
    for c0 in range(0, w1_ref.shape[1], chunk):
        a = jnp.maximum(_dot(hb, w1_ref[:, c0:c0 + chunk]), 0.0)
        acc = acc + _dot((a * a).astype(BF16), w2_ref[c0:c0 + chunk, :])
    o_ref[0] = x + mod_ref[0, 5:6, :] * acc


def _mlp(x, mod3, norm_g, w1, w2, tm):
    b, l, d = x.shape
    tok = lambda bi, i: (bi, i, 0)
    const = lambda bi, i: (0, 0)
    return pl.pallas_call(
        functools.partial(_mlp_kernel, chunk=d),
        grid=(b, l // tm),
        in_specs=[pl.BlockSpec((1, tm, d), tok),
                  pl.BlockSpec((1, N_MOD, d), lambda bi, i: (bi, 0, 0)),
                  pl.BlockSpec((1, d), const),
                  pl.BlockSpec(w1.shape, const, pipeline_mode=pl.Buffered(1)),
                  pl.BlockSpec(w2.shape, const, pipeline_mode=pl.Buffered(1))],
        out_specs=pl.BlockSpec((1, tm, d), tok),
        out_shape=jax.ShapeDtypeStruct((b, l, d), F32),
        compiler_params=_params(("arbitrary", "arbitrary")),
        name="mlp",
    )(x, mod3, norm_g, w1, w2)


def _rope_tables(seq_len):
    n_freq = HEAD_DIM // 4
    t = jnp.arange(seq_len)
    inv = ROPE_THETA ** (-jnp.arange(n_freq, dtype=F32) / n_freq)
    ang_row = (t // GRID_W).astype(F32)[:, None] * inv
    ang_col = (t % GRID_W).astype(F32)[:, None] * inv
    cr, sr, cc, sc = jnp.cos(ang_row), jnp.sin(ang_row), jnp.cos(ang_col), jnp.sin(ang_col)
    cos = jnp.concatenate([cr, cr, cc, cc], axis=1)
    sin = jnp.concatenate([-sr, sr, -sc, sc], axis=1)
    return jnp.tile(cos, (1, 2)), jnp.tile(sin, (1, 2))


def _bias_kernel(el_ref, er_ref, o_ref, *, rows):
    lanes = 2 * GRID_W
    qc = lax.broadcasted_iota(jnp.int32, (GRID_W, lanes), 0)
    lane = lax.broadcasted_iota(jnp.int32, (GRID_W, lanes), 1)
    kc = lane % GRID_W
    c0 = jnp.clip(qc - NA_COLS // 2, 0, GRID_W - NA_COLS)
    col_ok = (kc >= c0) & (kc < c0 + NA_COLS)
    left = lane < GRID_W
    masked = jnp.full((GRID_W, lanes), MASK_VALUE, F32)

    cache = {}

    def toeplitz(side, dr):
        if (side, dr) not in cache:
            ref = er_ref if side else el_ref
            dd = dr + NA_ROWS - 1
            vec = jnp.broadcast_to(ref[0, dd:dd + 1, :], (GRID_W, lanes))
            cache[(side, dr)] = pltpu.roll(vec, 0, axis=1, stride=1, stride_axis=0)
        return cache[(side, dr)]

    nt = rows // TILE_ROWS
    for t, tile in enumerate((0, nt // 2, nt - 1)):
        r = tile * TILE_ROWS
        ks = min(max(r - TILE_ROWS, 0), rows - KEY_ROWS)
        for i in range(TILE_ROWS):
            r0 = min(max(r + i - NA_ROWS // 2, 0), rows - NA_ROWS)
            for jp in range(KEY_ROWS // 2):
                halves = []
                for side in range(2):
                    kr = ks + 2 * jp + side
                    ok = r0 <= kr < r0 + NA_ROWS
                    halves.append(toeplitz(side, kr - (r + i)) if ok else masked)
                blk = jnp.where(col_ok, jnp.where(left, halves[0], halves[1]), MASK_VALUE)
                o_ref[t, 0, i * GRID_W:(i + 1) * GRID_W, jp * lanes:(jp + 1) * lanes] = blk


def _bias_tables(rpb, rows):
    heads, nr, ncol = rpb.shape
    lanes = 2 * GRID_W
    half = NA_COLS - 1
    zeros = lambda n: jnp.zeros((heads, nr, n), F32)
    e_left = jnp.concatenate([rpb[..., half:], zeros(lanes - ncol), rpb[..., :half]], axis=-1)
    e_right = jnp.concatenate([zeros(GRID_W - half), rpb, zeros(lanes - GRID_W + half - ncol)], axis=-1)
    pad = (-nr) % 8
    e_left = jnp.pad(e_left, ((0, 0), (0, pad), (0, 0)))
    e_right = jnp.pad(e_right, ((0, 0), (0, pad), (0, 0)))
    tq, tk = TILE_ROWS * GRID_W, KEY_ROWS * GRID_W
    vec_spec = pl.BlockSpec((1, nr + pad, lanes), lambda h: (h, 0, 0))
    return pl.pallas_call(
        functools.partial(_bias_kernel, rows=rows),
        grid=(heads,),
        in_specs=[vec_spec, vec_spec],
        out_specs=pl.BlockSpec((3, 1, tq, tk), lambda h: (0, h, 0, 0)),
        out_shape=jax.ShapeDtypeStruct((3, heads, tq, tk), F32),
        compiler_params=_params(("arbitrary",)),
        name="bias_tables",
    )(e_left, e_right)


def kernel(x, c, ctx, c_ctx, ada_w, ada_b, norm1_g, norm2_g, w_in, pool_w, pool_scale,
           q_norm_g, k_norm_g, rpb, w_branch_pool, w_branch_na, w_out, mlp_w1, mlp_w2):
    b, l, d = x.shape
    depth = ada_w.shape[0]
    pw = pool_scale.shape[-1]
    nw = w_branch_na.shape[1]
    heads = nw // HEAD_DIM
    rows = l // GRID_W
    assert l % (TILE_ROWS * GRID_W) == 0 and rows >= KEY_ROWS
    assert w_in.shape[-1] == pw + 3 * nw + 2 * d and (pw + nw) % (2 * nw) == 0
    kv_block = (pw + nw) // (2 * nw)
    ctx_row = b

    cos_t, sin_t = _rope_tables(l)
    mean_mat = jnp.asarray(np.kron(np.eye(heads), np.full((HEAD_DIM, HEAD_DIM), 1.0 / HEAD_DIM)), BF16)
    pad = (-(b + 1)) % 8
    cc = jnp.concatenate([c, c_ctx[None], jnp.zeros((pad, d), F32)], axis=0)

    for layer in range(depth):
        last = layer == depth - 1
        if not last:
            raise NotImplementedError("context-stream update is only needed for DEPTH > 1")
        mod = _modulation(cc, ada_w[layer], ada_b[layer])
        mod3 = mod.reshape(mod.shape[0], N_MOD, d)
        w_in_bf = w_in[layer].astype(BF16)
        qg_t = jnp.tile(q_norm_g[layer], heads)[None]
        kg_t = jnp.tile(k_norm_g[layer], heads)[None]
        n1 = norm1_g[layer][None]

        k_ctx, v_ctx = _ctx_proj(ctx, mod3, n1, w_in_bf, kg_t, mean_mat, kv_block, ctx_row)
        u, qr, qp, kr, v, gp, gn = _in_proj(x, mod3, n1, w_in_bf, cos_t, sin_t, qg_t, kg_t,
                                             mean_mat, pw, tm=512)
        bias = _bias_tables(rpb[layer], rows)
        x = _mixer(x, mod3, u, qr, qp, kr, v, k_ctx, v_ctx, bias, gp, gn,
                   pool_w[layer].astype(BF16), pool_scale[layer][None],
                   w_branch_pool[layer].astype(BF16), w_branch_na[layer].astype(BF16),
                   w_out[layer].astype(BF16))
        x = _mlp(x, mod3, norm2_g[layer][None], mlp_w1[layer].astype(BF16),
                 mlp_w2[layer].astype(BF16), tm=512)
    return x
```

```python
import functools

import numpy as np
import jax
import jax.numpy as jnp
from jax import lax
from jax.experimental import pallas as pl
from jax.experimental.pallas import tpu as pltpu

GRID_W = 64
N_MOD = 6
POOL_WINDOWS = (2, 4, 8, 16)
HEAD_DIM = 64
NA_ROWS = 8
NA_COLS = 16
ROPE_THETA = 10000.0
EPS = 1e-6
MASK_VALUE = -1e30
LOG2E = 1.4426950408889634

VMEM_LIMIT_BYTES = 56 * 1024 * 1024
SUBLANES = 8
BF16_ROWS = 16
POOL_HALO = 8
TILE_ROWS = 4
KEY_ROWS = TILE_ROWS + NA_ROWS

BF16 = jnp.bfloat16
F32 = jnp.float32


def _dot(a, b):
    return jnp.dot(a, b, preferred_element_type=F32)


def _dot_nt(a, b):
    return lax.dot_general(a, b, (((1,), (1,)), ((), ())), preferred_element_type=F32)


def _params(sem):
    return pltpu.CompilerParams(dimension_semantics=sem, vmem_limit_bytes=VMEM_LIMIT_BYTES)


def _rms(x):
    return x * lax.rsqrt(jnp.mean(x * x, axis=-1, keepdims=True) + EPS)


def _mod_kernel(c_ref, w_ref, b_ref, o_ref):
    c = c_ref[...]
    s = (c * jax.nn.sigmoid(c)).astype(BF16)
    o_ref[...] = _dot(s, w_ref[...].astype(BF16)) + b_ref[...]


def _modulation(cc, ada_w, ada_b):
    m, d = cc.shape
    n = ada_w.shape[1]
    return pl.pallas_call(
        _mod_kernel,
        grid=(n // d,),
        in_specs=[pl.BlockSpec((m, d), lambda j: (0, 0)),
                  pl.BlockSpec((d, d), lambda j: (0, j)),
                  pl.BlockSpec((1, d), lambda j: (0, j))],
        out_specs=pl.BlockSpec((m, d), lambda j: (0, j)),
        out_shape=jax.ShapeDtypeStruct((m, n), F32),
        compiler_params=_params(("arbitrary",)),
        name="modulation",
    )(cc, ada_w, ada_b.reshape(1, n))


def _head_norm(a, mean_mat, g):
    ms = _dot((a * a).astype(BF16), mean_mat)
    return a * lax.rsqrt(ms + EPS) * g


def _rope(a, cos, sin_signed, first_half):
    width = a.shape[-1]
    quarter = HEAD_DIM // 4
    up = pltpu.roll(a, width - quarter, axis=1)
    down = pltpu.roll(a, quarter, axis=1)
    swapped = jnp.where(first_half, up, down)
    return a * cos + swapped * sin_signed


def _modulated_norm(x, mod_ref, g_ref, shift_row):
    return _rms(x) * g_ref[...] * (1.0 + mod_ref[0, shift_row + 1:shift_row + 2, :]) \
        + mod_ref[0, shift_row:shift_row + 1, :]


def _ctx_kernel(x_ref, mod_ref, g_ref, wk_ref, wvt_ref, kg_ref, mean_ref, k_ref, vt_ref):
    hb = _modulated_norm(x_ref[0], mod_ref, g_ref, 0).astype(BF16)
    k_ref[0] = _head_norm(_dot(hb, wk_ref[...]), mean_ref[...], kg_ref[...]).astype(BF16)
    vt_ref[0] = _dot_nt(wvt_ref[...], hb).astype(BF16)


def _ctx_proj(ctx, mod3, norm_g, w_nat, w_qvt, kg_t, mean_mat, ctx_row):
    b, n, d = ctx.shape
    nw = kg_t.shape[-1]
    sd = jax.ShapeDtypeStruct
    return pl.pallas_call(
        _ctx_kernel,
        grid=(b,),
        in_specs=[pl.BlockSpec((1, n, d), lambda i: (i, 0, 0)),
                  pl.BlockSpec((1, N_MOD, d), lambda i: (ctx_row, 0, 0)),
                  pl.BlockSpec((1, d), lambda i: (0, 0)),
                  pl.BlockSpec((d, nw), lambda i: (0, 1)),
                  pl.BlockSpec((nw, d), lambda i: (1, 0)),
                  pl.BlockSpec((1, nw), lambda i: (0, 0)),
                  pl.BlockSpec((nw, nw), lambda i: (0, 0))],
        out_specs=[pl.BlockSpec((1, n, nw), lambda i: (i, 0, 0)),
                   pl.BlockSpec((1, nw, n), lambda i: (i, 0, 0))],
        out_shape=[sd((b, n, nw), BF16), sd((b, nw, n), BF16)],
        compiler_params=_params(("arbitrary",)),
        name="ctx_proj",
    )(ctx, mod3, norm_g, w_nat, w_qvt, kg_t, mean_mat)


def _in_proj_kernel(x_ref, mod_ref, g_ref, w_ref, wt_ref, cos_ref, sin_ref, cost_ref, sint_ref,
                    qgt_ref, kg_ref, mean_ref,
                    u_ref, qrt_ref, qpt_ref, kr_ref, vt_ref, gp_ref, gn_ref):
    pw = u_ref.shape[-1]
    nw = kr_ref.shape[-1]
    d = x_ref.shape[-1]
    heads = nw // HEAD_DIM
    quarter = HEAD_DIM // 4
    scale = HEAD_DIM ** -0.5 * LOG2E

    hb = _modulated_norm(x_ref[0], mod_ref, g_ref, 0).astype(BF16)

    c0 = 0
    u_ref[0] = _dot(hb, w_ref[:, c0:c0 + pw])
    c0 += pw
    reps = nw // cos_ref.shape[-1]
    cos = jnp.concatenate([cos_ref[...]] * reps, axis=1)
    sin = jnp.concatenate([sin_ref[...]] * reps, axis=1)
    lane = lax.broadcasted_iota(jnp.int32, (1, nw), 1)
    first_half = (lane % (HEAD_DIM // 2)) < quarter
    k = _head_norm(_dot(hb, w_ref[:, c0:c0 + nw]), mean_ref[...], kg_ref[...])
    kr_ref[0] = _rope(k, cos, sin, first_half).astype(BF16)
    c0 += nw
    gp_ref[0] = jax.nn.sigmoid(_dot(hb, w_ref[:, c0:c0 + d])).astype(BF16)
    c0 += d
    gn_ref[0] = jax.nn.sigmoid(_dot(hb, w_ref[:, c0:c0 + d])).astype(BF16)

    qt = _dot_nt(wt_ref[0:nw, :], hb)
    cost, sint, qgt = cost_ref[...], sint_ref[...], qgt_ref[...]
    for h in range(heads):
        rows = slice(h * HEAD_DIM, (h + 1) * HEAD_DIM)
        z = qt[rows, :]
        qn = z * lax.rsqrt(jnp.mean(z * z, axis=0, keepdims=True) + EPS) * (qgt * scale)
        swapped = jnp.concatenate([qn[quarter:2 * quarter], qn[0:quarter],
                                   qn[3 * quarter:], qn[2 * quarter:3 * quarter]], axis=0)
        qpt_ref[0, rows, :] = qn.astype(BF16)
        qrt_ref[0, rows, :] = (qn * cost + swapped * sint).astype(BF16)
    vt_ref[0] = _dot_nt(wt_ref[nw:, :], hb).astype(BF16)


def _in_proj(x, mod3, norm_g, w_nat, w_qvt, cos_t, sin_t, cos_tt, sin_tt, qg_tt, kg_t, mean_mat, pw, tm):
    b, l, d = x.shape
    nw = kg_t.shape[-1]
    tw = cos_t.shape[-1]
    tok = lambda bi, i: (bi, i, 0)
    chan = lambda bi, i: (bi, 0, i)
    const = lambda bi, i: (0, 0)
    sd = jax.ShapeDtypeStruct
    return pl.pallas_call(
        _in_proj_kernel,
        grid=(b, l // tm),
        in_specs=[pl.BlockSpec((1, tm, d), tok),
                  pl.BlockSpec((1, N_MOD, d), lambda bi, i: (bi, 0, 0)),
                  pl.BlockSpec((1, d), const),
                  pl.BlockSpec(w_nat.shape, const),
                  pl.BlockSpec(w_qvt.shape, const),
                  pl.BlockSpec((tm, tw), lambda bi, i: (i, 0)),
                  pl.BlockSpec((tm, tw), lambda bi, i: (i, 0)),
                  pl.BlockSpec((HEAD_DIM, tm), lambda bi, i: (0, i)),
                  pl.BlockSpec((HEAD_DIM, tm), lambda bi, i: (0, i)),
                  pl.BlockSpec((HEAD_DIM, tm), const),
                  pl.BlockSpec((1, nw), const),
                  pl.BlockSpec((nw, nw), const)],
        out_specs=[pl.BlockSpec((1, tm, pw), tok),
                   pl.BlockSpec((1, nw, tm), chan), pl.BlockSpec((1, nw, tm), chan),
                   pl.BlockSpec((1, tm, nw), tok),
                   pl.BlockSpec((1, nw, tm), chan),
                   pl.BlockSpec((1, tm, d), tok), pl.BlockSpec((1, tm, d), tok)],
        out_shape=[sd((b, l, pw), F32),
                   sd((b, nw, l), BF16), sd((b, nw, l), BF16),
                   sd((b, l, nw), BF16),
                   sd((b, nw, l), BF16),
                   sd((b, l, d), BF16), sd((b, l, d), BF16)],
        compiler_params=_params(("arbitrary", "arbitrary")),
        name="in_proj",
    )(x, mod3, norm_g, w_nat, w_qvt, cos_t, sin_t, cos_tt, sin_tt, qg_tt, kg_t, mean_mat)


def _mixer_kernel(x_ref, mod_ref, u_ref, up_ref, un_ref, qrt_ref, qpt_ref,
                  k0_ref, k1_ref, k2_ref, vt0_ref, vt1_ref, vt2_ref, kc_ref, vct_ref, bias_ref,
                  gp_ref, gn_ref, pw_ref, ps_ref, wbp_ref, wbn_ref, wo_ref,
                  o_ref, nat_scr, e_scr, a2_scr, a4_scr, a8_scr, *, seq_len):
    i = pl.program_id(1)
    tm = x_ref.shape[1]
    nw = qrt_ref.shape[1]
    heads = nw // HEAD_DIM
    pair = 2 * HEAD_DIM

    zeros = jnp.zeros((HEAD_DIM, tm), BF16)
    ones = jnp.ones((BF16_ROWS, tm), BF16)
    k_refs = (k0_ref, k1_ref, k2_ref)
    vt_refs = (vt0_ref, vt1_ref, vt2_ref)
    def head_rows(h):
        return slice(h * HEAD_DIM, (h + 1) * HEAD_DIM)

    def probabilities(h):
        rows = head_rows(h)
        lanes = slice((h // 2) * pair, (h // 2 + 1) * pair)
        q_rot, q_plain = qrt_ref[0, rows, :], qpt_ref[0, rows, :]
        if h % 2 == 0:
            w_rot = jnp.concatenate([q_rot, zeros], axis=0)
            w_plain = jnp.concatenate([q_plain, zeros], axis=0)
        else:
            w_rot = jnp.concatenate([zeros, q_rot], axis=0)
            w_plain = jnp.concatenate([zeros, q_plain], axis=0)
        s = [_dot(k_refs[j][0, :, lanes], w_rot) + bias_ref[0, h, j * tm:(j + 1) * tm, :]
             for j in range(3)]
        s.append(_dot(kc_ref[0, :, lanes], w_plain))
        m = s[0].max(axis=0, keepdims=True)
        for sj in s[1:]:
            m = jnp.maximum(m, sj.max(axis=0, keepdims=True))
        return [jnp.exp2(sj - m).astype(BF16) for sj in s]

    def weighted_values(h, p):
        rows = head_rows(h)
        ot = _dot(jnp.concatenate([vct_ref[0, rows, :], ones], axis=0), p[3])
        for j in range(3):
            ot = ot + _dot(jnp.concatenate([vt_refs[j][0, rows, :], ones], axis=0), p[j])
        nat_scr[rows, :] = ot[:HEAD_DIM] / ot[HEAD_DIM:HEAD_DIM + 1]

    gd = pw_ref.shape[-1]
    n_e = tm + 2 * POOL_HALO
    lo, hi = SUBLANES, SUBLANES + n_e
    first = SUBLANES + POOL_HALO
    g0, g1 = slice(0, gd), slice(gd, 2 * gd)

    def rows_at(ref, off, cols):
        return ref[first + off:first + off + tm, cols]

    def pool_fill():
        e_scr[0:lo, :] = jnp.zeros((lo, e_scr.shape[1]), F32)
        a2_scr[0:lo, :] = jnp.zeros((lo, a2_scr.shape[1]), F32)
        a4_scr[0:lo, :] = jnp.zeros((lo, a4_scr.shape[1]), F32)
        e_scr[lo:first, :] = jnp.where(i > 0, up_ref[0], 0.0)
        e_scr[first:first + tm, :] = u_ref[0]
        e_scr[first + tm:hi, :] = jnp.where((i + 1) * tm < seq_len, un_ref[0], 0.0)
        a2_scr[lo:hi, :] = e_scr[lo:hi, 2 * gd:] + e_scr[lo - 1:hi - 1, 2 * gd:]

    def pool_double():
        a4_scr[lo:hi, :] = a2_scr[lo:hi, :] + a2_scr[lo - 2:hi - 2, :]
        a8_scr[lo:hi, :] = a4_scr[lo:hi, gd:] + a4_scr[lo - 4:hi - 4, gd:]

    def pool_group(g):
        w = POOL_WINDOWS[g]
        if g == 0:
            total = rows_at(e_scr, -1, g0) + rows_at(e_scr, 0, g0)
        elif g == 1:
            total = ((rows_at(e_scr, -2, g1) + rows_at(e_scr, -1, g1))
                     + (rows_at(e_scr, 0, g1) + rows_at(e_scr, 1, g1)))
        elif g == 2:
            total = rows_at(a4_scr, 3, g0) + rows_at(a4_scr, -1, g0)
        else:
            total = rows_at(a8_scr, 7, g0) + rows_at(a8_scr, -1, g0)
        pos = i * tm + lax.broadcasted_iota(jnp.int32, (tm, 1), 0)
        cnt = (jnp.minimum(pos + w // 2, seq_len) - jnp.maximum(pos - w // 2, 0)).astype(F32)
        dlt = total / cnt - u_ref[0, :, g * gd:(g + 1) * gd]
        cols = slice(g * gd, (g + 1) * gd)
        return (_dot(dlt.astype(BF16), pw_ref[g]) * ps_ref[:, cols]).astype(BF16)

    pooled = []
    pool_stages = [pool_fill, pool_double] + [functools.partial(pool_group, g)
                                               for g in range(len(POOL_WINDOWS))]

    p_next = probabilities(0)
    for h in range(heads):
        p_cur = p_next
        if h + 1 < heads:
            p_next = probabilities(h + 1)
        if h < len(pool_stages):
            out = pool_stages[h]()
            if out is not None:
                pooled.append(out)
        weighted_values(h, p_cur)

    pool = jnp.concatenate(pooled, axis=1)
    na = nat_scr[...].T.astype(BF16)
    merged = (gp_ref[0].astype(F32) * _dot(pool, wbp_ref[...])
              + gn_ref[0].astype(F32) * _dot(na, wbn_ref[...]))
    o_ref[0] = x_ref[0] + mod_ref[0, 2:3, :] * _dot(merged.astype(BF16), wo_ref[...])


def _mixer(x, mod3, u, qrt, qpt, kr, vt, kc, vct, bias, gp, gn, pool_w_bf, pool_scale, wbp, wbn, wo):
    b, l, d = x.shape
    pw = u.shape[-1]
    nw = kr.shape[-1]
    gd = pool_w_bf.shape[-1]
    tm = TILE_ROWS * GRID_W
    nt = l // tm
    hb = tm // POOL_HALO
    n_halo = l // POOL_HALO
    nc = kc.shape[1]
    assert pw == len(POOL_WINDOWS) * gd and POOL_HALO == max(POOL_WINDOWS) // 2

    tok = lambda bi, i: (bi, i, 0)
    chan = lambda bi, i: (bi, 0, i)
    const2 = lambda bi, i: (0, 0)
    first_block = lambda i: jnp.clip(i - 1, 0, nt - 3)

    def k_spec(j):
        return pl.BlockSpec((1, tm, nw), lambda bi, i: (bi, first_block(i) + j, 0))

    def vt_spec(j):
        return pl.BlockSpec((1, nw, tm), lambda bi, i: (bi, 0, first_block(i) + j))

    def bias_idx(bi, i):
        return (jnp.where(i == 0, 0, jnp.where(i == nt - 1, 2, 1)), 0, 0, 0)

    in_specs = [
        pl.BlockSpec((1, tm, d), tok),
        pl.BlockSpec((1, N_MOD, d), lambda bi, i: (bi, 0, 0)),
        pl.BlockSpec((1, tm, pw), tok),
        pl.BlockSpec((1, POOL_HALO, pw), lambda bi, i: (bi, jnp.maximum(i * hb - 1, 0), 0)),
        pl.BlockSpec((1, POOL_HALO, pw), lambda bi, i: (bi, jnp.minimum((i + 1) * hb, n_halo - 1), 0)),
        pl.BlockSpec((1, nw, tm), chan),
        pl.BlockSpec((1, nw, tm), chan),
        k_spec(0), k_spec(1), k_spec(2),
        vt_spec(0), vt_spec(1), vt_spec(2),
        pl.BlockSpec((1, nc, nw), lambda bi, i: (bi, 0, 0)),
        pl.BlockSpec((1, nw, nc), lambda bi, i: (bi, 0, 0)),
        pl.BlockSpec((1,) + bias.shape[1:], bias_idx),
        pl.BlockSpec((1, tm, d), tok),
        pl.BlockSpec((1, tm, d), tok),
        pl.BlockSpec(pool_w_bf.shape, lambda bi, i: (0, 0, 0)),
        pl.BlockSpec((1, pw), const2),
        pl.BlockSpec(wbp.shape, const2),
        pl.BlockSpec(wbn.shape, const2),
        pl.BlockSpec(wo.shape, const2),
    ]
    n_scr = SUBLANES + tm + 2 * POOL_HALO
    return pl.pallas_call(
        functools.partial(_mixer_kernel, seq_len=l),
        grid=(b, nt),
        in_specs=in_specs,
        out_specs=pl.BlockSpec((1, tm, d), tok),
        out_shape=jax.ShapeDtypeStruct((b, l, d), F32),
        scratch_shapes=[pltpu.VMEM((nw, tm), F32),
                        pltpu.VMEM((n_scr, pw), F32),
                        pltpu.VMEM((n_scr, 2 * gd), F32),
                        pltpu.VMEM((n_scr, 2 * gd), F32),
                        pltpu.VMEM((n_scr, gd), F32)],
        compiler_params=_params(("arbitrary", "arbitrary")),
        name="mixer",
    )(x, mod3, u, u, u, qrt, qpt, kr, kr, kr, vt, vt, vt, kc, vct, bias, gp, gn,
      pool_w_bf, pool_scale, wbp, wbn, wo)


def _mlp_kernel(x_ref, mod_ref, g_ref, w1_ref, w2_ref, o_ref, *, chunk):
    x = x_ref[0]
    hb = _modulated_norm(x, mod_ref, g_ref, 3).astype(BF16)
    acc = jnp.zeros(x.shape, F32)
    for c0 in range(0, w1_ref.shape[1], chunk):
        a = jnp.maximum(_dot(hb, w1_ref[:, c0:c0 + chunk]), 0.0)
        acc = acc + _dot((a * a).astype(BF16), w2_ref[c0:c0 + chunk, :])
    o_ref[0] = x + mod_ref[0, 5:6, :] * acc


def _mlp(x, mod3, norm_g, w1, w2, tm):
    b, l, d = x.shape
    tok = lambda bi, i: (bi, i, 0)
    const = lambda bi, i: (0, 0)
    return pl.pallas_call(
        functools.partial(_mlp_kernel, chunk=d),
        grid=(b, l // tm),
        in_specs=[pl.BlockSpec((1, tm, d), tok),
                  pl.BlockSpec((1, N_MOD, d), lambda bi, i: (bi, 0, 0)),
                  pl.BlockSpec((1, d), const),
                  pl.BlockSpec(w1.shape, const, pipeline_mode=pl.Buffered(1)),
                  pl.BlockSpec(w2.shape, const, pipeline_mode=pl.Buffered(1))],
        out_specs=pl.BlockSpec((1, tm, d), tok),
        out_shape=jax.ShapeDtypeStruct((b, l, d), F32),
        compiler_params=_params(("arbitrary", "arbitrary")),
        name="mlp",
    )(x, mod3, norm_g, w1, w2)


def _rope_tables(seq_len):
    n_freq = HEAD_DIM // 4
    t = jnp.arange(seq_len)
    inv = ROPE_THETA ** (-jnp.arange(n_freq, dtype=F32) / n_freq)
    ang_row = (t // GRID_W).astype(F32)[:, None] * inv
    ang_col = (t % GRID_W).astype(F32)[:, None] * inv
    cr, sr, cc, sc = jnp.cos(ang_row), jnp.sin(ang_row), jnp.cos(ang_col), jnp.sin(ang_col)
    cos = jnp.concatenate([cr, cr, cc, cc], axis=1)
    sin = jnp.concatenate([-sr, sr, -sc, sc], axis=1)
    return jnp.tile(cos, (1, 2)), jnp.tile(sin, (1, 2)), cos.T, sin.T


def _bias_kernel(el_ref, er_ref, o_ref, *, rows):
    lanes = 2 * GRID_W
    kc = lax.broadcasted_iota(jnp.int32, (GRID_W, lanes), 0)
    lane = lax.broadcasted_iota(jnp.int32, (GRID_W, lanes), 1)
    qc = lane % GRID_W
    c0 = jnp.clip(qc - NA_COLS // 2, 0, GRID_W - NA_COLS)
    col_ok = (kc >= c0) & (kc < c0 + NA_COLS)
    left = lane < GRID_W
    masked = jnp.full((GRID_W, lanes), MASK_VALUE, F32)

    cache = {}

    def toeplitz(side, dr):
        if (side, dr) not in cache:
            ref = er_ref if side else el_ref
            dd = dr + NA_ROWS - 1
            vec = jnp.broadcast_to(ref[0, dd:dd + 1, :], (GRID_W, lanes))
            cache[(side, dr)] = pltpu.roll(vec, 0, axis=1, stride=1, stride_axis=0) * LOG2E
        return cache[(side, dr)]

    nt = rows // TILE_ROWS
    for t, tile in enumerate((0, nt // 2, nt - 1)):
        r = tile * TILE_ROWS
        ks = min(max(r - TILE_ROWS, 0), rows - KEY_ROWS)
        for j in range(KEY_ROWS):
            kr = ks + j
            for ip in range(TILE_ROWS // 2):
                halves = []
                for side in range(2):
                    qr = r + 2 * ip + side
                    r0 = min(max(qr - NA_ROWS // 2, 0), rows - NA_ROWS)
                    ok = r0 <= kr < r0 + NA_ROWS
                    halves.append(toeplitz(side, kr - qr) if ok else masked)
                blk = jnp.where(col_ok, jnp.where(left, halves[0], halves[1]), MASK_VALUE)
                o_ref[t, 0, j * GRID_W:(j + 1) * GRID_W, ip * lanes:(ip + 1) * lanes] = blk


def _bias_tables(rpb, rows):
    heads, nr, ncol = rpb.shape
    lanes = 2 * GRID_W
    half = NA_COLS - 1
    zeros = lambda n: jnp.zeros((heads, nr, n), F32)
    flipped = rpb[..., ::-1]
    e_left = jnp.concatenate([flipped[..., half:], zeros(lanes - ncol), flipped[..., :half]], axis=-1)
    e_right = jnp.concatenate([zeros(GRID_W - half), flipped, zeros(lanes - GRID_W + half - ncol)], axis=-1)
    pad = (-nr) % SUBLANES
    e_left = jnp.pad(e_left, ((0, 0), (0, pad), (0, 0)))
    e_right = jnp.pad(e_right, ((0, 0), (0, pad), (0, 0)))
    tq, tk = TILE_ROWS * GRID_W, KEY_ROWS * GRID_W
    vec_spec = pl.BlockSpec((1, nr + pad, lanes), lambda h: (h, 0, 0))
    return pl.pallas_call(
        functools.partial(_bias_kernel, rows=rows),
        grid=(heads,),
        in_specs=[vec_spec, vec_spec],
        out_specs=pl.BlockSpec((3, 1, tk, tq), lambda h: (0, h, 0, 0)),
        out_shape=jax.ShapeDtypeStruct((3, heads, tk, tq), F32),
        compiler_params=_params(("arbitrary",)),
        name="bias_tables",
    )(e_left, e_right)


def kernel(x, c, ctx, c_ctx, ada_w, ada_b, norm1_g, norm2_g, w_in, pool_w, pool_scale,
           q_norm_g, k_norm_g, rpb, w_branch_pool, w_branch_na, w_out, mlp_w1, mlp_w2):
    b, l, d = x.shape
    depth = ada_w.shape[0]
    pw = pool_scale.shape[-1]
    nw = w_branch_na.shape[1]
    heads = nw // HEAD_DIM
    rows = l // GRID_W
    assert depth == 1, "the context-stream update is only needed when another layer follows"
    assert l % (TILE_ROWS * GRID_W) == 0 and rows >= KEY_ROWS
    assert w_in.shape[-1] == pw + 3 * nw + 2 * d and pw == nw
    ctx_row = b
    tm_proj = 512

    cos_t, sin_t, cos_tt, sin_tt = _rope_tables(l)
    mean_mat = jnp.asarray(np.kron(np.eye(heads), np.full((HEAD_DIM, HEAD_DIM), 1.0 / HEAD_DIM)), BF16)
    pad = (-(b + 1)) % SUBLANES
    cc = jnp.concatenate([c, c_ctx[None], jnp.zeros((pad, d), F32)], axis=0)

    layer = 0
    mod = _modulation(cc, ada_w[layer], ada_b[layer])
    mod3 = mod.reshape(mod.shape[0], N_MOD, d)
    wl = w_in[layer]
    q0, k0, v0, g0 = pw, pw + nw, pw + 2 * nw, pw + 3 * nw
    w_nat = jnp.concatenate([wl[:, :q0], wl[:, k0:v0], wl[:, g0:]], axis=1).astype(BF16)
    w_qvt = jnp.concatenate([wl[:, q0:k0].T, wl[:, v0:g0].T], axis=0).astype(BF16)
    qg_tt = jnp.broadcast_to(q_norm_g[layer][:, None], (HEAD_DIM, tm_proj))
    kg_t = jnp.tile(k_norm_g[layer], heads)[None]
    n1 = norm1_g[layer][None]

    k_ctx, vt_ctx = _ctx_proj(ctx, mod3, n1, w_nat, w_qvt, kg_t, mean_mat, ctx_row)
    u, qrt, qpt, kr, vt, gp, gn = _in_proj(x, mod3, n1, w_nat, w_qvt, cos_t, sin_t, cos_tt, sin_tt,
                                           qg_tt, kg_t, mean_mat, pw, tm=tm_proj)
    bias = _bias_tables(rpb[layer], rows)
    x = _mixer(x, mod3, u, qrt, qpt, kr, vt, k_ctx, vt_ctx, bias, gp, gn,
               pool_w[layer].astype(BF16), pool_scale[layer][None],
               w_branch_pool[layer].astype(BF16), w_branch_na[layer].astype(BF16),
               w_out[layer].astype(BF16))
    x = _mlp(x, mod3, norm2_g[layer][None], mlp_w1[layer].astype(BF16),
             mlp_w2[layer].astype(BF16), tm=tm_proj)
    return x
```

```python
import functools

import numpy as np
import jax
import jax.numpy as jnp
from jax import lax
from jax.experimental import pallas as pl
from jax.experimental.pallas import tpu as pltpu

GRID_W = 64
N_MOD = 6
POOL_WINDOWS = (2, 4, 8, 16)
HEAD_DIM = 64
NA_ROWS = 8
NA_COLS = 16
ROPE_THETA = 10000.0
EPS = 1e-6
MASK_VALUE = -1e30
LOG2E = 1.4426950408889634

VMEM_LIMIT_BYTES = 56 * 1024 * 1024
SUBLANES = 8
BF16_ROWS = 16
POOL_HALO = 8
TILE_ROWS = 4
KEY_ROWS = TILE_ROWS + NA_ROWS

BF16 = jnp.bfloat16
F32 = jnp.float32


def _dot(a, b):
    return jnp.dot(a, b, preferred_element_type=F32)


def _dot_nt(a, b):
    return lax.dot_general(a, b, (((1,), (1,)), ((), ())), preferred_element_type=F32)


def _params(sem):
    return pltpu.CompilerParams(dimension_semantics=sem, vmem_limit_bytes=VMEM_LIMIT_BYTES)


def _rms(x):
    return x * lax.rsqrt(jnp.mean(x * x, axis=-1, keepdims=True) + EPS)


def _mod_kernel(c_ref, w_ref, b_ref, o_ref):
    c = c_ref[...]
    s = (c * jax.nn.sigmoid(c)).astype(BF16)
    o_ref[...] = _dot(s, w_ref[...].astype(BF16)) + b_ref[...]


def _modulation(cc, ada_w, ada_b):
    m, d = cc.shape
    n = ada_w.shape[1]
    return pl.pallas_call(
        _mod_kernel,
        grid=(n // d,),
        in_specs=[pl.BlockSpec((m, d), lambda j: (0, 0)),
                  pl.BlockSpec((d, d), lambda j: (0, j)),
                  pl.BlockSpec((1, d), lambda j: (0, j))],
        out_specs=pl.BlockSpec((m, d), lambda j: (0, j)),
        out_shape=jax.ShapeDtypeStruct((m, n), F32),
        compiler_params=_params(("arbitrary",)),
        name="modulation",
    )(cc, ada_w, ada_b.reshape(1, n))


def _head_norm(a, mean_mat, g):
    ms = _dot((a * a).astype(BF16), mean_mat)
    return a * lax.rsqrt(ms + EPS) * g


def _rope(a, cos, sin_signed, first_half):
    width = a.shape[-1]
    quarter = HEAD_DIM // 4
    up = pltpu.roll(a, width - quarter, axis=1)
    down = pltpu.roll(a, quarter, axis=1)
    swapped = jnp.where(first_half, up, down)
    return a * cos + swapped * sin_signed


def _modulated_norm(x, mod_ref, g_ref, shift_row):
    return _rms(x) * g_ref[...] * (1.0 + mod_ref[0, shift_row + 1:shift_row + 2, :]) \
        + mod_ref[0, shift_row:shift_row + 1, :]


def _ctx_kernel(x_ref, mod_ref, g_ref, wk_ref, wvt_ref, kg_ref, mean_ref, k_ref, vt_ref):
    hb = _modulated_norm(x_ref[0], mod_ref, g_ref, 0).astype(BF16)
    k_ref[0] = _head_norm(_dot(hb, wk_ref[...]), mean_ref[...], kg_ref[...]).astype(BF16)
    vt_ref[0] = _dot_nt(wvt_ref[...], hb).astype(BF16)


def _ctx_proj(ctx, mod3, norm_g, w_all, w_qvt, kg_t, mean_mat, key_block, ctx_row):
    b, n, d = ctx.shape
    nw = kg_t.shape[-1]
    sd = jax.ShapeDtypeStruct
    return pl.pallas_call(
        _ctx_kernel,
        grid=(b,),
        in_specs=[pl.BlockSpec((1, n, d), lambda i: (i, 0, 0)),
                  pl.BlockSpec((1, N_MOD, d), lambda i: (ctx_row, 0, 0)),
                  pl.BlockSpec((1, d), lambda i: (0, 0)),
                  pl.BlockSpec((d, nw), lambda i: (0, key_block)),
                  pl.BlockSpec((nw, d), lambda i: (1, 0)),
                  pl.BlockSpec((1, nw), lambda i: (0, 0)),
                  pl.BlockSpec((nw, nw), lambda i: (0, 0))],
        out_specs=[pl.BlockSpec((1, n, nw), lambda i: (i, 0, 0)),
                   pl.BlockSpec((1, nw, n), lambda i: (i, 0, 0))],
        out_shape=[sd((b, n, nw), BF16), sd((b, nw, n), BF16)],
        compiler_params=_params(("arbitrary",)),
        name="ctx_proj",
    )(ctx, mod3, norm_g, w_all, w_qvt, kg_t, mean_mat)


def _in_proj_kernel(x_ref, mod_ref, g_ref, w_ref, wt_ref, cos_ref, sin_ref, cost_ref, sint_ref,
                    qgt_ref, kg_ref, mean_ref,
                    u_ref, qrt_ref, qpt_ref, kr_ref, vt_ref, gp_ref, gn_ref, *, col_starts):
    pw = u_ref.shape[-1]
    nw = kr_ref.shape[-1]
    d = x_ref.shape[-1]
    heads = nw // HEAD_DIM
    quarter = HEAD_DIM // 4
    scale = HEAD_DIM ** -0.5 * LOG2E

    hb = _modulated_norm(x_ref[0], mod_ref, g_ref, 0).astype(BF16)

    u0, k0, gp0, gn0 = col_starts
    u_ref[0] = _dot(hb, w_ref[:, u0:u0 + pw])
    reps = nw // cos_ref.shape[-1]
    cos = jnp.concatenate([cos_ref[...]] * reps, axis=1)
    sin = jnp.concatenate([sin_ref[...]] * reps, axis=1)
    lane = lax.broadcasted_iota(jnp.int32, (1, nw), 1)
    first_half = (lane % (HEAD_DIM // 2)) < quarter
    k = _head_norm(_dot(hb, w_ref[:, k0:k0 + nw]), mean_ref[...], kg_ref[...])
    kr_ref[0] = _rope(k, cos, sin, first_half).astype(BF16)
    gp_ref[0] = jax.nn.sigmoid(_dot(hb, w_ref[:, gp0:gp0 + d])).astype(BF16)
    gn_ref[0] = jax.nn.sigmoid(_dot(hb, w_ref[:, gn0:gn0 + d])).astype(BF16)

    qt = _dot_nt(wt_ref[0:nw, :], hb)
    cost, sint, qgt = cost_ref[...], sint_ref[...], qgt_ref[...]
    for h in range(heads):
        rows = slice(h * HEAD_DIM, (h + 1) * HEAD_DIM)
        z = qt[rows, :]
        qn = z * lax.rsqrt(jnp.mean(z * z, axis=0, keepdims=True) + EPS) * (qgt * scale)
        swapped = jnp.concatenate([qn[quarter:2 * quarter], qn[0:quarter],
                                   qn[3 * quarter:], qn[2 * quarter:3 * quarter]], axis=0)
        qpt_ref[0, rows, :] = qn.astype(BF16)
        qrt_ref[0, rows, :] = (qn * cost + swapped * sint).astype(BF16)
    vt_ref[0] = _dot_nt(wt_ref[nw:, :], hb).astype(BF16)


def _in_proj(x, mod3, norm_g, w_all, w_qvt, cos_t, sin_t, cos_tt, sin_tt, qg_tt, kg_t, mean_mat,
             col_starts, pw, tm):
    b, l, d = x.shape
    nw = kg_t.shape[-1]
    tw = cos_t.shape[-1]
    tok = lambda bi, i: (bi, i, 0)
    chan = lambda bi, i: (bi, 0, i)
    const = lambda bi, i: (0, 0)
    sd = jax.ShapeDtypeStruct
    return pl.pallas_call(
        functools.partial(_in_proj_kernel, col_starts=col_starts),
        grid=(b, l // tm),
        in_specs=[pl.BlockSpec((1, tm, d), tok),
                  pl.BlockSpec((1, N_MOD, d), lambda bi, i: (bi, 0, 0)),
                  pl.BlockSpec((1, d), const),
                  pl.BlockSpec(w_all.shape, const, pipeline_mode=pl.Buffered(1)),
                  pl.BlockSpec(w_qvt.shape, const, pipeline_mode=pl.Buffered(1)),
                  pl.BlockSpec((tm, tw), lambda bi, i: (i, 0)),
                  pl.BlockSpec((tm, tw), lambda bi, i: (i, 0)),
                  pl.BlockSpec((HEAD_DIM, tm), lambda bi, i: (0, i)),
                  pl.BlockSpec((HEAD_DIM, tm), lambda bi, i: (0, i)),
                  pl.BlockSpec((HEAD_DIM, tm), const),
                  pl.BlockSpec((1, nw), const),
                  pl.BlockSpec((nw, nw), const)],
        out_specs=[pl.BlockSpec((1, tm, pw), tok),
                   pl.BlockSpec((1, nw, tm), chan), pl.BlockSpec((1, nw, tm), chan),
                   pl.BlockSpec((1, tm, nw), tok),
                   pl.BlockSpec((1, nw, tm), chan),
                   pl.BlockSpec((1, tm, d), tok), pl.BlockSpec((1, tm, d), tok)],
        out_shape=[sd((b, l, pw), F32),
                   sd((b, nw, l), BF16), sd((b, nw, l), BF16),
                   sd((b, l, nw), BF16),
                   sd((b, nw, l), BF16),
                   sd((b, l, d), BF16), sd((b, l, d), BF16)],
        compiler_params=_params(("arbitrary", "arbitrary")),
        name="in_proj",
    )(x, mod3, norm_g, w_all, w_qvt, cos_t, sin_t, cos_tt, sin_tt, qg_tt, kg_t, mean_mat)


def _mixer_kernel(u_ref, up_ref, un_ref, qrt_ref, qpt_ref,
                  k0_ref, k1_ref, k2_ref, vt0_ref, vt1_ref, vt2_ref, kc_ref, vct_ref, bias_ref,
                  pw_ref, ps_ref,
                  na_ref, pool_ref, nat_scr, e_scr, a2_scr, a4_scr, a8_scr, *, seq_len):
    i = pl.program_id(1)
    tm = u_ref.shape[1]
    nw = qrt_ref.shape[1]
    heads = nw // HEAD_DIM
    pair = 2 * HEAD_DIM

    zeros = jnp.zeros((HEAD_DIM, tm), BF16)
    ones = jnp.ones((BF16_ROWS, tm), BF16)
    k_refs = (k0_ref, k1_ref, k2_ref)
    vt_refs = (vt0_ref, vt1_ref, vt2_ref)
    def head_rows(h):
        return slice(h * HEAD_DIM, (h + 1) * HEAD_DIM)

    def scores(h):
        rows = head_rows(h)
        lanes = slice((h // 2) * pair, (h // 2 + 1) * pair)
        q_rot, q_plain = qrt_ref[0, rows, :], qpt_ref[0, rows, :]
        if h % 2 == 0:
            w_rot = jnp.concatenate([q_rot, zeros], axis=0)
            w_plain = jnp.concatenate([q_plain, zeros], axis=0)
        else:
            w_rot = jnp.concatenate([zeros, q_rot], axis=0)
            w_plain = jnp.concatenate([zeros, q_plain], axis=0)
        s = [_dot(k_refs[j][0, :, lanes], w_rot) + bias_ref[0, h, j * tm:(j + 1) * tm, :]
             for j in range(3)]
        s.append(_dot(kc_ref[0, :, lanes], w_plain))
        m = s[0].max(axis=0, keepdims=True)
        for sj in s[1:]:
            m = jnp.maximum(m, sj.max(axis=0, keepdims=True))
        return s, m

    def probabilities(s_and_m):
        s, m = s_and_m
        return [jnp.exp2(sj - m).astype(BF16) for sj in s]

    def weighted_values(h, p):
        rows = head_rows(h)
        ot = _dot(jnp.concatenate([vct_ref[0, rows, :], ones], axis=0), p[3])
        for j in range(3):
            ot = ot + _dot(jnp.concatenate([vt_refs[j][0, rows, :], ones], axis=0), p[j])
        nat_scr[rows, :] = ot[:HEAD_DIM] / ot[HEAD_DIM:HEAD_DIM + 1]

    gd = pw_ref.shape[-1]
    n_e = tm + 2 * POOL_HALO
    lo, hi = SUBLANES, SUBLANES + n_e
    first = SUBLANES + POOL_HALO
    g0, g1 = slice(0, gd), slice(gd, 2 * gd)

    def rows_at(ref, off, cols):
        return ref[first + off:first + off + tm, cols]

    def pool_fill():
        e_scr[0:lo, :] = jnp.zeros((lo, e_scr.shape[1]), F32)
        a2_scr[0:lo, :] = jnp.zeros((lo, a2_scr.shape[1]), F32)
        a4_scr[0:lo, :] = jnp.zeros((lo, a4_scr.shape[1]), F32)
        e_scr[lo:first, :] = jnp.where(i > 0, up_ref[0], 0.0)
        e_scr[first:first + tm, :] = u_ref[0]
        e_scr[first + tm:hi, :] = jnp.where((i + 1) * tm < seq_len, un_ref[0], 0.0)
        a2_scr[lo:hi, :] = e_scr[lo:hi, 2 * gd:] + e_scr[lo - 1:hi - 1, 2 * gd:]

    def pool_double():
        a4_scr[lo:hi, :] = a2_scr[lo:hi, :] + a2_scr[lo - 2:hi - 2, :]
        a8_scr[lo:hi, :] = a4_scr[lo:hi, gd:] + a4_scr[lo - 4:hi - 4, gd:]

    def pool_group(g):
        w = POOL_WINDOWS[g]
        if g == 0:
            total = rows_at(e_scr, -1, g0) + rows_at(e_scr, 0, g0)
        elif g == 1:
            total = ((rows_at(e_scr, -2, g1) + rows_at(e_scr, -1, g1))
                     + (rows_at(e_scr, 0, g1) + rows_at(e_scr, 1, g1)))
        elif g == 2:
            total = rows_at(a4_scr, 3, g0) + rows_at(a4_scr, -1, g0)
        else:
            total = rows_at(a8_scr, 7, g0) + rows_at(a8_scr, -1, g0)
        pos = i * tm + lax.broadcasted_iota(jnp.int32, (tm, 1), 0)
        cnt = (jnp.minimum(pos + w // 2, seq_len) - jnp.maximum(pos - w // 2, 0)).astype(F32)
        dlt = total / cnt - u_ref[0, :, g * gd:(g + 1) * gd]
        cols = slice(g * gd, (g + 1) * gd)
        return (_dot(dlt.astype(BF16), pw_ref[g]) * ps_ref[:, cols]).astype(BF16)

    pooled = []
    pool_stages = [pool_fill, pool_double] + [functools.partial(pool_group, g)
                                               for g in range(len(POOL_WINDOWS))]

    p_next = probabilities(scores(0))
    for h in range(heads):
        p_cur = p_next
        if h + 1 < heads:
            p_next = probabilities(scores(h + 1))
        if h < len(pool_stages):
            out = pool_stages[h]()
            if out is not None:
                pooled.append(out)
        weighted_values(h, p_cur)

    pool_ref[0] = jnp.concatenate(pooled, axis=1)
    na_ref[0] = nat_scr[...].T.astype(BF16)


def _mixer(u, qrt, qpt, kr, vt, kc, vct, bias, pool_w_bf, pool_scale):
    b, l, pw = u.shape
    nw = kr.shape[-1]
    gd = pool_w_bf.shape[-1]
    tm = TILE_ROWS * GRID_W
    nt = l // tm
    hb = tm // POOL_HALO
    n_halo = l // POOL_HALO
    nc = kc.shape[1]
    assert pw == len(POOL_WINDOWS) * gd and POOL_HALO == max(POOL_WINDOWS) // 2

    tok = lambda bi, i: (bi, i, 0)
    chan = lambda bi, i: (bi, 0, i)
    const2 = lambda bi, i: (0, 0)
    first_block = lambda i: jnp.clip(i - 1, 0, nt - 3)

    def k_spec(j):
        return pl.BlockSpec((1, tm, nw), lambda bi, i: (bi, first_block(i) + j, 0))

    def vt_spec(j):
        return pl.BlockSpec((1, nw, tm), lambda bi, i: (bi, 0, first_block(i) + j))

    def bias_idx(bi, i):
        return (jnp.where(i == 0, 0, jnp.where(i == nt - 1, 2, 1)), 0, 0, 0)

    in_specs = [
        pl.BlockSpec((1, tm, pw), tok),
        pl.BlockSpec((1, POOL_HALO, pw), lambda bi, i: (bi, jnp.maximum(i * hb - 1, 0), 0)),
        pl.BlockSpec((1, POOL_HALO, pw), lambda bi, i: (bi, jnp.minimum((i + 1) * hb, n_halo - 1), 0)),
        pl.BlockSpec((1, nw, tm), chan),
        pl.BlockSpec((1, nw, tm), chan),
        k_spec(0), k_spec(1), k_spec(2),
        vt_spec(0), vt_spec(1), vt_spec(2),
        pl.BlockSpec((1, nc, nw), lambda bi, i: (bi, 0, 0)),
        pl.BlockSpec((1, nw, nc), lambda bi, i: (bi, 0, 0)),
        pl.BlockSpec((1,) + bias.shape[1:], bias_idx),
        pl.BlockSpec(pool_w_bf.shape, lambda bi, i: (0, 0, 0)),
        pl.BlockSpec((1, pw), const2),
    ]
    n_scr = SUBLANES + tm + 2 * POOL_HALO
    return pl.pallas_call(
        functools.partial(_mixer_kernel, seq_len=l),
        grid=(b, nt),
        in_specs=in_specs,
        out_specs=[pl.BlockSpec((1, tm, nw), tok), pl.BlockSpec((1, tm, pw), tok)],
        out_shape=[jax.ShapeDtypeStruct((b, l, nw), BF16), jax.ShapeDtypeStruct((b, l, pw), BF16)],
        scratch_shapes=[pltpu.VMEM((nw, tm), F32),
                        pltpu.VMEM((n_scr, pw), F32),
                        pltpu.VMEM((n_scr, 2 * gd), F32),
                        pltpu.VMEM((n_scr, 2 * gd), F32),
                        pltpu.VMEM((n_scr, gd), F32)],
        compiler_params=_params(("arbitrary", "arbitrary")),
        name="mixer",
    )(u, u, u, qrt, qpt, kr, kr, kr, vt, vt, vt, kc, vct, bias, pool_w_bf, pool_scale)


def _post_kernel(x_ref, mod_ref, g_ref, na_ref, pool_ref, gp_ref, gn_ref,
                 wbp_ref, wbn_ref, wo_ref, w1_ref, w2_ref, o_ref, *, chunk):
    merged = (gp_ref[0].astype(F32) * _dot(pool_ref[0], wbp_ref[...])
              + gn_ref[0].astype(F32) * _dot(na_ref[0], wbn_ref[...]))
    x = x_ref[0] + mod_ref[0, 2:3, :] * _dot(merged.astype(BF16), wo_ref[...])
    hb = _modulated_norm(x, mod_ref, g_ref, 3).astype(BF16)
    acc = jnp.zeros(x.shape, F32)
    for c0 in range(0, w1_ref.shape[1], chunk):
        a = jnp.maximum(_dot(hb, w1_ref[:, c0:c0 + chunk]), 0.0)
        acc = acc + _dot((a * a).astype(BF16), w2_ref[c0:c0 + chunk, :])
    o_ref[0] = x + mod_ref[0, 5:6, :] * acc


def _post(x, mod3, norm_g, na, pool, gp, gn, wbp, wbn, wo, w1, w2, tm):
    b, l, d = x.shape
    tok = lambda bi, i: (bi, i, 0)
    const = lambda bi, i: (0, 0)
    resident = lambda w: pl.BlockSpec(w.shape, const, pipeline_mode=pl.Buffered(1))
    return pl.pallas_call(
        functools.partial(_post_kernel, chunk=d),
        grid=(b, l // tm),
        in_specs=[pl.BlockSpec((1, tm, d), tok),
                  pl.BlockSpec((1, N_MOD, d), lambda bi, i: (bi, 0, 0)),
                  pl.BlockSpec((1, d), const),
                  pl.BlockSpec((1, tm, na.shape[-1]), tok),
                  pl.BlockSpec((1, tm, pool.shape[-1]), tok),
                  pl.BlockSpec((1, tm, d), tok),
                  pl.BlockSpec((1, tm, d), tok),
                  resident(wbp), resident(wbn), resident(wo), resident(w1), resident(w2)],
        out_specs=pl.BlockSpec((1, tm, d), tok),
        out_shape=jax.ShapeDtypeStruct((b, l, d), F32),
        compiler_params=_params(("arbitrary", "arbitrary")),
        name="post",
    )(x, mod3, norm_g, na, pool, gp, gn, wbp, wbn, wo, w1, w2)


def _rope_tables(seq_len):
    n_freq = HEAD_DIM // 4
    t = np.arange(seq_len)
    inv = (ROPE_THETA ** (-np.arange(n_freq, dtype=np.float32) / n_freq)).astype(np.float32)
    ang_row = (t // GRID_W).astype(np.float32)[:, None] * inv
    ang_col = (t % GRID_W).astype(np.float32)[:, None] * inv
    cr, sr, cc, sc = np.cos(ang_row), np.sin(ang_row), np.cos(ang_col), np.sin(ang_col)
    cos = np.concatenate([cr, cr, cc, cc], axis=1).astype(np.float32)
    sin = np.concatenate([-sr, sr, -sc, sc], axis=1).astype(np.float32)
    return (jnp.asarray(np.tile(cos, (1, 2))), jnp.asarray(np.tile(sin, (1, 2))),
            jnp.asarray(np.ascontiguousarray(cos.T)), jnp.asarray(np.ascontiguousarray(sin.T)))


def _bias_kernel(el_ref, er_ref, o_ref, *, rows):
    lanes = 2 * GRID_W
    kc = lax.broadcasted_iota(jnp.int32, (GRID_W, lanes), 0)
    lane = lax.broadcasted_iota(jnp.int32, (GRID_W, lanes), 1)
    qc = lane % GRID_W
    c0 = jnp.clip(qc - NA_COLS // 2, 0, GRID_W - NA_COLS)
    col_ok = (kc >= c0) & (kc < c0 + NA_COLS)
    left = lane < GRID_W
    masked = jnp.full((GRID_W, lanes), MASK_VALUE, F32)

    cache = {}

    def toeplitz(side, dr):
        if (side, dr) not in cache:
            ref = er_ref if side else el_ref
            dd = dr + NA_ROWS - 1
            vec = jnp.broadcast_to(ref[0, dd:dd + 1, :], (GRID_W, lanes))
            cache[(side, dr)] = pltpu.roll(vec, 0, axis=1, stride=1, stride_axis=0) * LOG2E
        return cache[(side, dr)]

    nt = rows // TILE_ROWS
    for t, tile in enumerate((0, nt // 2, nt - 1)):
        r = tile * TILE_ROWS
        ks = min(max(r - TILE_ROWS, 0), rows - KEY_ROWS)
        for j in range(KEY_ROWS):
            kr = ks + j
            for ip in range(TILE_ROWS // 2):
                halves = []
                for side in range(2):
                    qr = r + 2 * ip + side
                    r0 = min(max(qr - NA_ROWS // 2, 0), rows - NA_ROWS)
                    ok = r0 <= kr < r0 + NA_ROWS
                    halves.append(toeplitz(side, kr - qr) if ok else masked)
                blk = jnp.where(col_ok, jnp.where(left, halves[0], halves[1]), MASK_VALUE)
                o_ref[t, 0, j * GRID_W:(j + 1) * GRID_W, ip * lanes:(ip + 1) * lanes] = blk


def _bias_tables(rpb, rows):
    heads, nr, ncol = rpb.shape
    lanes = 2 * GRID_W
    half = NA_COLS - 1
    zeros = lambda n: jnp.zeros((heads, nr, n), F32)
    flipped = rpb[..., ::-1]
    e_left = jnp.concatenate([flipped[..., half:], zeros(lanes - ncol), flipped[..., :half]], axis=-1)
    e_right = jnp.concatenate([zeros(GRID_W - half), flipped, zeros(lanes - GRID_W + half - ncol)], axis=-1)
    pad = (-nr) % SUBLANES
    e_left = jnp.pad(e_left, ((0, 0), (0, pad), (0, 0)))
    e_right = jnp.pad(e_right, ((0, 0), (0, pad), (0, 0)))
    tq, tk = TILE_ROWS * GRID_W, KEY_ROWS * GRID_W
    vec_spec = pl.BlockSpec((1, nr + pad, lanes), lambda h: (h, 0, 0))
    return pl.pallas_call(
        functools.partial(_bias_kernel, rows=rows),
        grid=(heads,),
        in_specs=[vec_spec, vec_spec],
        out_specs=pl.BlockSpec((3, 1, tk, tq), lambda h: (0, h, 0, 0)),
        out_shape=jax.ShapeDtypeStruct((3, heads, tk, tq), F32),
        compiler_params=_params(("arbitrary",)),
        name="bias_tables",
    )(e_left, e_right)


def kernel(x, c, ctx, c_ctx, ada_w, ada_b, norm1_g, norm2_g, w_in, pool_w, pool_scale,
           q_norm_g, k_norm_g, rpb, w_branch_pool, w_branch_na, w_out, mlp_w1, mlp_w2):
    b, l, d = x.shape
    depth = ada_w.shape[0]
    pw = pool_scale.shape[-1]
    nw = w_branch_na.shape[1]
    heads = nw // HEAD_DIM
    rows = l // GRID_W
    assert depth == 1, "the context-stream update is only needed when another layer follows"
    assert l % (TILE_ROWS * GRID_W) == 0 and rows >= KEY_ROWS
    assert w_in.shape[-1] == pw + 3 * nw + 2 * d and pw == nw
    ctx_row = b
    tm_proj = 512

    cos_t, sin_t, cos_tt, sin_tt = _rope_tables(l)
    mean_mat = jnp.asarray(np.kron(np.eye(heads), np.full((HEAD_DIM, HEAD_DIM), 1.0 / HEAD_DIM)), BF16)
    pad = (-(b + 1)) % SUBLANES
    cc = jnp.concatenate([c, c_ctx[None], jnp.zeros((pad, d), F32)], axis=0)

    layer = 0
    mod = _modulation(cc, ada_w[layer], ada_b[layer])
    mod3 = mod.reshape(mod.shape[0], N_MOD, d)
    wl = w_in[layer]
    q0, k0, v0, g0 = pw, pw + nw, pw + 2 * nw, pw + 3 * nw
    w_all = wl.astype(BF16)
    w_qvt = jnp.concatenate([wl[:, q0:k0].T, wl[:, v0:g0].T], axis=0).astype(BF16)
    qg_tt = jnp.broadcast_to(q_norm_g[layer][:, None], (HEAD_DIM, tm_proj))
    kg_t = jnp.tile(k_norm_g[layer], heads)[None]
    n1 = norm1_g[layer][None]

    k_ctx, vt_ctx = _ctx_proj(ctx, mod3, n1, w_all, w_qvt, kg_t, mean_mat, k0 // nw, ctx_row)
    u, qrt, qpt, kr, vt, gp, gn = _in_proj(x, mod3, n1, w_all, w_qvt, cos_t, sin_t, cos_tt, sin_tt,
                                           qg_tt, kg_t, mean_mat, (0, k0, g0, g0 + d), pw, tm=tm_proj)
    bias = _bias_tables(rpb[layer], rows)
    na, pool = _mixer(u, qrt, qpt, kr, vt, k_ctx, vt_ctx, bias,
                      pool_w[layer].astype(BF16), pool_scale[layer][None])
    return _post(x, mod3, norm2_g[layer][None], na, pool, gp, gn,
                 w_branch_pool[layer].astype(BF16), w_branch_na[layer].astype(BF16),
                 w_out[layer].astype(BF16), mlp_w1[layer].astype(BF16), mlp_w2[layer].astype(BF16),
                 tm=tm_proj)
```

```python
import functools

import numpy as np
import jax
import jax.numpy as jnp
from jax import lax
from jax.experimental import pallas as pl
from jax.experimental.pallas import tpu as pltpu

GRID_W = 64
N_MOD = 6
POOL_WINDOWS = (2, 4, 8, 16)
HEAD_DIM = 64
NA_ROWS = 8
NA_COLS = 16
ROPE_THETA = 10000.0
EPS = 1e-6
MASK_VALUE = -1e30
LOG2E = 1.4426950408889634

VMEM_LIMIT_BYTES = 56 * 1024 * 1024
SUBLANES = 8
BF16_ROWS = 16
MXU_COLS = 256
POOL_HALO = 8
TILE_ROWS = 4
KEY_ROWS = TILE_ROWS + NA_ROWS

BF16 = jnp.bfloat16
F32 = jnp.float32


def _dot(a, b):
    return jnp.dot(a, b, preferred_element_type=F32)


def _dot_nt(a, b):
    return lax.dot_general(a, b, (((1,), (1,)), ((), ())), preferred_element_type=F32)


def _params(sem):
    return pltpu.CompilerParams(dimension_semantics=sem, vmem_limit_bytes=VMEM_LIMIT_BYTES)


def _rms(x):
    return x * lax.rsqrt(jnp.mean(x * x, axis=-1, keepdims=True) + EPS)


def _mod_kernel(c_ref, w_ref, b_ref, o_ref):
    c = c_ref[...]
    s = (c * jax.nn.sigmoid(c)).astype(BF16)
    o_ref[...] = _dot(s, w_ref[...].astype(BF16)) + b_ref[...]


def _modulation(cc, ada_w, ada_b):
    m, d = cc.shape
    n = ada_w.shape[1]
    return pl.pallas_call(
        _mod_kernel,
        grid=(n // d,),
        in_specs=[pl.BlockSpec((m, d), lambda j: (0, 0)),
                  pl.BlockSpec((d, d), lambda j: (0, j)),
                  pl.BlockSpec((1, d), lambda j: (0, j))],
        out_specs=pl.BlockSpec((m, d), lambda j: (0, j)),
        out_shape=jax.ShapeDtypeStruct((m, n), F32),
        compiler_params=_params(("arbitrary",)),
        name="modulation",
    )(cc, ada_w, ada_b.reshape(1, n))


def _head_norm(a, mean_mat, g):
    ms = _dot((a * a).astype(BF16), mean_mat)
    return a * lax.rsqrt(ms + EPS) * g


def _rope(a, cos, sin_signed, first_half):
    width = a.shape[-1]
    quarter = HEAD_DIM // 4
    up = pltpu.roll(a, width - quarter, axis=1)
    down = pltpu.roll(a, quarter, axis=1)
    swapped = jnp.where(first_half, up, down)
    return a * cos + swapped * sin_signed


def _modulated_norm(x, mod_ref, g_ref, shift_row):
    return _rms(x) * g_ref[...] * (1.0 + mod_ref[0, shift_row + 1:shift_row + 2, :]) \
        + mod_ref[0, shift_row:shift_row + 1, :]


def _ctx_kernel(x_ref, mod_ref, g_ref, wk_ref, wvt_ref, kg_ref, mean_ref, k_ref, vt_ref):
    hb = _modulated_norm(x_ref[0], mod_ref, g_ref, 0).astype(BF16)
    k_ref[0] = _head_norm(_dot(hb, wk_ref[...]), mean_ref[...], kg_ref[...]).astype(BF16)
    vt_ref[0] = _dot_nt(wvt_ref[...], hb).astype(BF16)


def _ctx_proj(ctx, mod3, norm_g, w_all, w_qvt, kg_t, mean_mat, key_block, ctx_row):
    b, n, d = ctx.shape
    nw = kg_t.shape[-1]
    sd = jax.ShapeDtypeStruct
    return pl.pallas_call(
        _ctx_kernel,
        grid=(b,),
        in_specs=[pl.BlockSpec((1, n, d), lambda i: (i, 0, 0)),
                  pl.BlockSpec((1, N_MOD, d), lambda i: (ctx_row, 0, 0)),
                  pl.BlockSpec((1, d), lambda i: (0, 0)),
                  pl.BlockSpec((d, nw), lambda i: (0, key_block)),
                  pl.BlockSpec((nw, d), lambda i: (1, 0)),
                  pl.BlockSpec((1, nw), lambda i: (0, 0)),
                  pl.BlockSpec((nw, nw), lambda i: (0, 0))],
        out_specs=[pl.BlockSpec((1, n, nw), lambda i: (i, 0, 0)),
                   pl.BlockSpec((1, nw, n), lambda i: (i, 0, 0))],
        out_shape=[sd((b, n, nw), BF16), sd((b, nw, n), BF16)],
        compiler_params=_params(("arbitrary",)),
        name="ctx_proj",
    )(ctx, mod3, norm_g, w_all, w_qvt, kg_t, mean_mat)


def _in_proj_kernel(x_ref, mod_ref, g_ref, w_ref, wt_ref, cos_ref, sin_ref, cost_ref, sint_ref,
                    qgt_ref, kg_ref, mean_ref,
                    u_ref, qrt_ref, qpt_ref, kr_ref, vt_ref, gp_ref, gn_ref, *, col_starts):
    pw = u_ref.shape[-1]
    nw = kr_ref.shape[-1]
    d = x_ref.shape[-1]
    heads = nw // HEAD_DIM
    quarter = HEAD_DIM // 4
    scale = HEAD_DIM ** -0.5 * LOG2E

    hb = _modulated_norm(x_ref[0], mod_ref, g_ref, 0).astype(BF16)

    u0, k0, gp0, gn0 = col_starts
    u_ref[0] = _dot(hb, w_ref[:, u0:u0 + pw])
    reps = nw // cos_ref.shape[-1]
    cos = jnp.concatenate([cos_ref[...]] * reps, axis=1)
    sin = jnp.concatenate([sin_ref[...]] * reps, axis=1)
    lane = lax.broadcasted_iota(jnp.int32, (1, nw), 1)
    first_half = (lane % (HEAD_DIM // 2)) < quarter
    k = _head_norm(_dot(hb, w_ref[:, k0:k0 + nw]), mean_ref[...], kg_ref[...])
    kr_ref[0] = _rope(k, cos, sin, first_half).astype(BF16)
    gp_ref[0] = jax.nn.sigmoid(_dot(hb, w_ref[:, gp0:gp0 + d])).astype(BF16)
    gn_ref[0] = jax.nn.sigmoid(_dot(hb, w_ref[:, gn0:gn0 + d])).astype(BF16)

    qt = _dot_nt(wt_ref[0:nw, :], hb)
    cost, sint, qgt = cost_ref[...], sint_ref[...], qgt_ref[...]
    for h in range(heads):
        rows = slice(h * HEAD_DIM, (h + 1) * HEAD_DIM)
        z = qt[rows, :]
        qn = z * lax.rsqrt(jnp.mean(z * z, axis=0, keepdims=True) + EPS) * (qgt * scale)
        swapped = jnp.concatenate([qn[quarter:2 * quarter], qn[0:quarter],
                                   qn[3 * quarter:], qn[2 * quarter:3 * quarter]], axis=0)
        qpt_ref[0, rows, :] = qn.astype(BF16)
        qrt_ref[0, rows, :] = (qn * cost + swapped * sint).astype(BF16)
    vt_ref[0] = _dot_nt(wt_ref[nw:, :], hb).astype(BF16)


def _in_proj(x, mod3, norm_g, w_all, w_qvt, cos_t, sin_t, cos_tt, sin_tt, qg_tt, kg_t, mean_mat,
             col_starts, pw, tm):
    b, l, d = x.shape
    nw = kg_t.shape[-1]
    tw = cos_t.shape[-1]
    tok = lambda bi, i: (bi, i, 0)
    chan = lambda bi, i: (bi, 0, i)
    const = lambda bi, i: (0, 0)
    sd = jax.ShapeDtypeStruct
    return pl.pallas_call(
        functools.partial(_in_proj_kernel, col_starts=col_starts),
        grid=(b, l // tm),
        in_specs=[pl.BlockSpec((1, tm, d), tok),
                  pl.BlockSpec((1, N_MOD, d), lambda bi, i: (bi, 0, 0)),
                  pl.BlockSpec((1, d), const),
                  pl.BlockSpec(w_all.shape, const, pipeline_mode=pl.Buffered(1)),
                  pl.BlockSpec(w_qvt.shape, const, pipeline_mode=pl.Buffered(1)),
                  pl.BlockSpec((tm, tw), lambda bi, i: (i, 0)),
                  pl.BlockSpec((tm, tw), lambda bi, i: (i, 0)),
                  pl.BlockSpec((HEAD_DIM, tm), lambda bi, i: (0, i)),
                  pl.BlockSpec((HEAD_DIM, tm), lambda bi, i: (0, i)),
                  pl.BlockSpec((HEAD_DIM, tm), const),
                  pl.BlockSpec((1, nw), const),
                  pl.BlockSpec((nw, nw), const)],
        out_specs=[pl.BlockSpec((1, tm, pw), tok),
                   pl.BlockSpec((1, nw, tm), chan), pl.BlockSpec((1, nw, tm), chan),
                   pl.BlockSpec((1, tm, nw), tok),
                   pl.BlockSpec((1, nw, tm), chan),
                   pl.BlockSpec((1, tm, d), tok), pl.BlockSpec((1, tm, d), tok)],
        out_shape=[sd((b, l, pw), F32),
                   sd((b, nw, l), BF16), sd((b, nw, l), BF16),
                   sd((b, l, nw), BF16),
                   sd((b, nw, l), BF16),
                   sd((b, l, d), BF16), sd((b, l, d), BF16)],
        compiler_params=_params(("arbitrary", "arbitrary")),
        name="in_proj",
    )(x, mod3, norm_g, w_all, w_qvt, cos_t, sin_t, cos_tt, sin_tt, qg_tt, kg_t, mean_mat)


def _block_kernel(u_ref, up_ref, un_ref, qrt_ref, qpt_ref,
                  k0_ref, k1_ref, k2_ref, vt0_ref, vt1_ref, vt2_ref, kc_ref, vct_ref, bias_ref,
                  pw_ref, ps_ref,
                  x_ref, mod_ref, g_ref, gp_ref, gn_ref, wbp_ref, wbn_ref, wo_ref, w1_ref, w2_ref,
                  o_ref,
                  nat_scr, e_scr, a2_scr, a4_scr, a8_scr, na_prev, pool_prev,
                  *, seq_len, n_tiles, chunk):
    step = pl.program_id(1)
    i = jnp.minimum(step, n_tiles - 1)
    tm = u_ref.shape[1]
    nw = qrt_ref.shape[1]
    heads = nw // HEAD_DIM
    pair = 2 * HEAD_DIM

    @pl.when(step == 0)
    def _():
        na_prev[...] = jnp.zeros(na_prev.shape, BF16)
        pool_prev[...] = jnp.zeros(pool_prev.shape, BF16)

    d = x_ref.shape[-1]
    col_blocks = [slice(n0, n0 + MXU_COLS) for n0 in range(0, d, MXU_COLS)]
    post = {"merged": [], "x": [], "a": [], "acc": [None] * len(col_blocks)}

    def merge_piece(cols):
        val = (gp_ref[0, :, cols].astype(F32) * _dot(pool_prev[...], wbp_ref[:, cols])
               + gn_ref[0, :, cols].astype(F32) * _dot(na_prev[...], wbn_ref[:, cols]))
        post["merged"].append(val.astype(BF16))

    def out_proj_piece(cols):
        if not post["x"]:
            post["merged_all"] = jnp.concatenate(post["merged"], axis=1)
        post["x"].append(x_ref[0, :, cols]
                         + mod_ref[0, 2:3, cols] * _dot(post["merged_all"], wo_ref[:, cols]))

    def norm_piece():
        post["x_all"] = jnp.concatenate(post["x"], axis=1)
        post["hb"] = _modulated_norm(post["x_all"], mod_ref, g_ref, 3).astype(BF16)

    def up_piece(c0, cols):
        hidden = slice(c0 + cols.start, c0 + cols.stop)
        a = jnp.maximum(_dot(post["hb"], w1_ref[:, hidden]), 0.0)
        post["a"].append((a * a).astype(BF16))

    def down_piece(c0, n, cols):
        if n == 0:
            post["a_all"] = jnp.concatenate(post["a"], axis=1)
            post["a"] = []
        part = _dot(post["a_all"], w2_ref[c0:c0 + chunk, cols])
        post["acc"][n] = part if post["acc"][n] is None else post["acc"][n] + part

    def store_piece():
        o_ref[0] = post["x_all"] + mod_ref[0, 5:6, :] * jnp.concatenate(post["acc"], axis=1)

    post_pieces = [functools.partial(merge_piece, cols) for cols in col_blocks]
    post_pieces += [functools.partial(out_proj_piece, cols) for cols in col_blocks]
    post_pieces.append(norm_piece)
    n_hidden_blocks = chunk // MXU_COLS
    for c0 in range(0, w1_ref.shape[1], chunk):
        post_pieces += [functools.partial(up_piece, c0, cols) for cols in col_blocks[:n_hidden_blocks]]
        post_pieces += [functools.partial(down_piece, c0, n, cols) for n, cols in enumerate(col_blocks)]
    post_tail = len(col_blocks) + 1
    post_pieces.append(store_piece)

    zeros = jnp.zeros((HEAD_DIM, tm), BF16)
    ones = jnp.ones((BF16_ROWS, tm), BF16)
    k_refs = (k0_ref, k1_ref, k2_ref)
    vt_refs = (vt0_ref, vt1_ref, vt2_ref)
    def head_rows(h):
        return slice(h * HEAD_DIM, (h + 1) * HEAD_DIM)

    n_kblocks = len(k_refs) + 1
    att = {}

    def score_piece(h, j):
        rows = head_rows(h)
        lanes = slice((h // 2) * pair, (h // 2 + 1) * pair)
        if j == 0:
            att["s", h], att["m", h] = [], []
        q = (qpt_ref if j == n_kblocks - 1 else qrt_ref)[0, rows, :]
        w = jnp.concatenate([q, zeros] if h % 2 == 0 else [zeros, q], axis=0)
        if j < n_kblocks - 1:
            s = _dot(k_refs[j][0, :, lanes], w) + bias_ref[0, h, j * tm:(j + 1) * tm, :]
        else:
            s = _dot(kc_ref[0, :, lanes], w)
        att["s", h].append(s)
        att["m", h].append(s.max(axis=0, keepdims=True))

    def exp_piece(h, j):
        if j == 0:
            m = att["m", h][0]
            for mj in att["m", h][1:]:
                m = jnp.maximum(m, mj)
            att["max", h], att["p", h] = m, []
        att["p", h].append(jnp.exp2(att["s", h][j] - att["max", h]).astype(BF16))

    def value_piece(h):
        rows = head_rows(h)
        p = att["p", h]
        ot = _dot(jnp.concatenate([vct_ref[0, rows, :], ones], axis=0), p[n_kblocks - 1])
        for j in range(n_kblocks - 1):
            ot = ot + _dot(jnp.concatenate([vt_refs[j][0, rows, :], ones], axis=0), p[j])
        nat_scr[rows, :] = ot[:HEAD_DIM] / ot[HEAD_DIM:HEAD_DIM + 1]
        del att["s", h], att["p", h]

    def softmax_pieces(h):
        return ([functools.partial(score_piece, h, j) for j in range(n_kblocks)]
                + [functools.partial(exp_piece, h, j) for j in range(n_kblocks)])

    gd = pw_ref.shape[-1]
    n_e = tm + 2 * POOL_HALO
    lo, hi = SUBLANES, SUBLANES + n_e
    first = SUBLANES + POOL_HALO
    g0, g1 = slice(0, gd), slice(gd, 2 * gd)

    def rows_at(ref, off, cols):
        return ref[first + off:first + off + tm, cols]

    def pool_fill():
        e_scr[0:lo, :] = jnp.zeros((lo, e_scr.shape[1]), F32)
        a2_scr[0:lo, :] = jnp.zeros((lo, a2_scr.shape[1]), F32)
        a4_scr[0:lo, :] = jnp.zeros((lo, a4_scr.shape[1]), F32)
        e_scr[lo:first, :] = jnp.where(i > 0, up_ref[0], 0.0)
        e_scr[first:first + tm, :] = u_ref[0]
        e_scr[first + tm:hi, :] = jnp.where((i + 1) * tm < seq_len, un_ref[0], 0.0)
        a2_scr[lo:hi, :] = e_scr[lo:hi, 2 * gd:] + e_scr[lo - 1:hi - 1, 2 * gd:]

    def pool_double():
        a4_scr[lo:hi, :] = a2_scr[lo:hi, :] + a2_scr[lo - 2:hi - 2, :]
        a8_scr[lo:hi, :] = a4_scr[lo:hi, gd:] + a4_scr[lo - 4:hi - 4, gd:]

    def pool_group(g):
        w = POOL_WINDOWS[g]
        if g == 0:
            total = rows_at(e_scr, -1, g0) + rows_at(e_scr, 0, g0)
        elif g == 1:
            total = ((rows_at(e_scr, -2, g1) + rows_at(e_scr, -1, g1))
                     + (rows_at(e_scr, 0, g1) + rows_at(e_scr, 1, g1)))
        elif g == 2:
            total = rows_at(a4_scr, 3, g0) + rows_at(a4_scr, -1, g0)
        else:
            total = rows_at(a8_scr, 7, g0) + rows_at(a8_scr, -1, g0)
        pos = i * tm + lax.broadcasted_iota(jnp.int32, (tm, 1), 0)
        cnt = (jnp.minimum(pos + w // 2, seq_len) - jnp.maximum(pos - w // 2, 0)).astype(F32)
        dlt = total / cnt - u_ref[0, :, g * gd:(g + 1) * gd]
        cols = slice(g * gd, (g + 1) * gd)
        return (_dot(dlt.astype(BF16), pw_ref[g]) * ps_ref[:, cols]).astype(BF16)

    pooled = []
    pool_stages = [pool_fill, pool_double] + [functools.partial(pool_group, g)
                                               for g in range(len(POOL_WINDOWS))]

    def run_pool_stage(stage):
        out = stage()
        if out is not None:
            pooled.append(out)

    def publish_branches():
        pool_prev[...] = jnp.concatenate(pooled, axis=1)
        na_prev[...] = nat_scr[...].T.astype(BF16)

    mix_pieces = softmax_pieces(0)
    for h in range(heads):
        if h + 1 < heads:
            mix_pieces += softmax_pieces(h + 1)
        if h < len(pool_stages):
            mix_pieces.append(functools.partial(run_pool_stage, pool_stages[h]))
        mix_pieces.append(functools.partial(value_piece, h))
    mix_pieces.append(publish_branches)

    n_spread = len(post_pieces) - post_tail
    issued = 0
    for n, piece in enumerate(mix_pieces):
        piece()
        target = (n + 1) * n_spread // len(mix_pieces)
        while issued < target:
            post_pieces[issued]()
            issued += 1
    for piece in post_pieces[issued:]:
        piece()


def _block(x, mod3, norm_g, u, qrt, qpt, kr, vt, kc, vct, bias, gp, gn,
           pool_w_bf, pool_scale, wbp, wbn, wo, w1, w2):
    b, l, d = x.shape
    pw = u.shape[-1]
    nw = kr.shape[-1]
    gd = pool_w_bf.shape[-1]
    tm = TILE_ROWS * GRID_W
    nt = l // tm
    hb = tm // POOL_HALO
    n_halo = l // POOL_HALO
    nc = kc.shape[1]
    assert pw == len(POOL_WINDOWS) * gd and POOL_HALO == max(POOL_WINDOWS) // 2

    mix_tile = lambda s: jnp.minimum(s, nt - 1)
    mlp_tile = lambda s: jnp.maximum(s - 1, 0)
    first_block = lambda s: jnp.clip(mix_tile(s) - 1, 0, nt - 3)
    mix_tok = lambda bi, s: (bi, mix_tile(s), 0)
    mix_chan = lambda bi, s: (bi, 0, mix_tile(s))
    mlp_tok = lambda bi, s: (bi, mlp_tile(s), 0)
    const2 = lambda bi, s: (0, 0)
    resident = lambda w: pl.BlockSpec(w.shape, const2, pipeline_mode=pl.Buffered(1))

    def k_spec(j):
        return pl.BlockSpec((1, tm, nw), lambda bi, s: (bi, first_block(s) + j, 0))

    def vt_spec(j):
        return pl.BlockSpec((1, nw, tm), lambda bi, s: (bi, 0, first_block(s) + j))

    def bias_idx(bi, s):
        i = mix_tile(s)
        return (jnp.where(i == 0, 0, jnp.where(i == nt - 1, 2, 1)), 0, 0, 0)

    in_specs = [
        pl.BlockSpec((1, tm, pw), mix_tok),
        pl.BlockSpec((1, POOL_HALO, pw), lambda bi, s: (bi, jnp.maximum(mix_tile(s) * hb - 1, 0), 0)),
        pl.BlockSpec((1, POOL_HALO, pw),
                     lambda bi, s: (bi, jnp.minimum((mix_tile(s) + 1) * hb, n_halo - 1), 0)),
        pl.BlockSpec((1, nw, tm), mix_chan),
        pl.BlockSpec((1, nw, tm), mix_chan),
        k_spec(0), k_spec(1), k_spec(2),
        vt_spec(0), vt_spec(1), vt_spec(2),
        pl.BlockSpec((1, nc, nw), lambda bi, s: (bi, 0, 0)),
        pl.BlockSpec((1, nw, nc), lambda bi, s: (bi, 0, 0)),
        pl.BlockSpec((1,) + bias.shape[1:], bias_idx),
        pl.BlockSpec(pool_w_bf.shape, lambda bi, s: (0, 0, 0)),
        pl.BlockSpec((1, pw), const2),
        pl.BlockSpec((1, tm, d), mlp_tok),
        pl.BlockSpec((1, N_MOD, d), lambda bi, s: (bi, 0, 0)),
        pl.BlockSpec((1, d), const2),
        pl.BlockSpec((1, tm, d), mlp_tok),
        pl.BlockSpec((1, tm, d), mlp_tok),
        resident(wbp), resident(wbn), resident(wo), resident(w1), resident(w2),
    ]
    n_scr = SUBLANES + tm + 2 * POOL_HALO
    return pl.pallas_call(
        functools.partial(_block_kernel, seq_len=l, n_tiles=nt, chunk=d),
        grid=(b, nt + 1),
        in_specs=in_specs,
        out_specs=pl.BlockSpec((1, tm, d), mlp_tok),
        out_shape=jax.ShapeDtypeStruct((b, l, d), F32),
        scratch_shapes=[pltpu.VMEM((nw, tm), F32),
                        pltpu.VMEM((n_scr, pw), F32),
                        pltpu.VMEM((n_scr, 2 * gd), F32),
                        pltpu.VMEM((n_scr, 2 * gd), F32),
                        pltpu.VMEM((n_scr, gd), F32),
                        pltpu.VMEM((tm, nw), BF16),
                        pltpu.VMEM((tm, pw), BF16)],
        compiler_params=_params(("arbitrary", "arbitrary")),
        name="block",
    )(u, u, u, qrt, qpt, kr, kr, kr, vt, vt, vt, kc, vct, bias, pool_w_bf, pool_scale,
      x, mod3, norm_g, gp, gn, wbp, wbn, wo, w1, w2)


def _rope_tables(seq_len):
    n_freq = HEAD_DIM // 4
    t = np.arange(seq_len)
    inv = (ROPE_THETA ** (-np.arange(n_freq, dtype=np.float32) / n_freq)).astype(np.float32)
    ang_row = (t // GRID_W).astype(np.float32)[:, None] * inv
    ang_col = (t % GRID_W).astype(np.float32)[:, None] * inv
    cr, sr, cc, sc = np.cos(ang_row), np.sin(ang_row), np.cos(ang_col), np.sin(ang_col)
    cos = np.concatenate([cr, cr, cc, cc], axis=1).astype(np.float32)
    sin = np.concatenate([-sr, sr, -sc, sc], axis=1).astype(np.float32)
    return (jnp.asarray(np.tile(cos, (1, 2))), jnp.asarray(np.tile(sin, (1, 2))),
            jnp.asarray(np.ascontiguousarray(cos.T)), jnp.asarray(np.ascontiguousarray(sin.T)))


def _bias_kernel(el_ref, er_ref, o_ref, *, rows):
    lanes = 2 * GRID_W
    kc = lax.broadcasted_iota(jnp.int32, (GRID_W, lanes), 0)
    lane = lax.broadcasted_iota(jnp.int32, (GRID_W, lanes), 1)
    qc = lane % GRID_W
    c0 = jnp.clip(qc - NA_COLS // 2, 0, GRID_W - NA_COLS)
    col_ok = (kc >= c0) & (kc < c0 + NA_COLS)
    left = lane < GRID_W
    masked = jnp.full((GRID_W, lanes), MASK_VALUE, F32)

    cache = {}

    def toeplitz(side, dr):
        if (side, dr) not in cache:
            ref = er_ref if side else el_ref
            dd = dr + NA_ROWS - 1
            vec = jnp.broadcast_to(ref[0, dd:dd + 1, :], (GRID_W, lanes))
            cache[(side, dr)] = pltpu.roll(vec, 0, axis=1, stride=1, stride_axis=0) * LOG2E
        return cache[(side, dr)]

    nt = rows // TILE_ROWS
    for t, tile in enumerate((0, nt // 2, nt - 1)):
        r = tile * TILE_ROWS
        ks = min(max(r - TILE_ROWS, 0), rows - KEY_ROWS)
        for j in range(KEY_ROWS):
            kr = ks + j
            for ip in range(TILE_ROWS // 2):
                halves = []
                for side in range(2):
                    qr = r + 2 * ip + side
                    r0 = min(max(qr - NA_ROWS // 2, 0), rows - NA_ROWS)
                    ok = r0 <= kr < r0 + NA_ROWS
                    halves.append(toeplitz(side, kr - qr) if ok else masked)
                blk = jnp.where(col_ok, jnp.where(left, halves[0], halves[1]), MASK_VALUE)
                o_ref[t, 0, j * GRID_W:(j + 1) * GRID_W, ip * lanes:(ip + 1) * lanes] = blk


def _bias_tables(rpb, rows):
    heads, nr, ncol = rpb.shape
    lanes = 2 * GRID_W
    half = NA_COLS - 1
    zeros = lambda n: jnp.zeros((heads, nr, n), F32)
    flipped = rpb[..., ::-1]
    e_left = jnp.concatenate([flipped[..., half:], zeros(lanes - ncol), flipped[..., :half]], axis=-1)
    e_right = jnp.concatenate([zeros(GRID_W - half), flipped, zeros(lanes - GRID_W + half - ncol)], axis=-1)
    pad = (-nr) % SUBLANES
    e_left = jnp.pad(e_left, ((0, 0), (0, pad), (0, 0)))
    e_right = jnp.pad(e_right, ((0, 0), (0, pad), (0, 0)))
    tq, tk = TILE_ROWS * GRID_W, KEY_ROWS * GRID_W
    vec_spec = pl.BlockSpec((1, nr + pad, lanes), lambda h: (h, 0, 0))
    return pl.pallas_call(
        functools.partial(_bias_kernel, rows=rows),
        grid=(heads,),
        in_specs=[vec_spec, vec_spec],
        out_specs=pl.BlockSpec((3, 1, tk, tq), lambda h: (0, h, 0, 0)),
        out_shape=jax.ShapeDtypeStruct((3, heads, tk, tq), F32),
        compiler_params=_params(("arbitrary",)),
        name="bias_tables",
    )(e_left, e_right)


def kernel(x, c, ctx, c_ctx, ada_w, ada_b, norm1_g, norm2_g, w_in, pool_w, pool_scale,
           q_norm_g, k_norm_g, rpb, w_branch_pool, w_branch_na, w_out, mlp_w1, mlp_w2):
    b, l, d = x.shape
    depth = ada_w.shape[0]
    pw = pool_scale.shape[-1]
    nw = w_branch_na.shape[1]
    heads = nw // HEAD_DIM
    rows = l // GRID_W
    assert depth == 1, "the context-stream update is only needed when another layer follows"
    assert l % (TILE_ROWS * GRID_W) == 0 and rows >= KEY_ROWS
    assert w_in.shape[-1] == pw + 3 * nw + 2 * d and pw == nw
    ctx_row = b
    tm_proj = 512

    cos_t, sin_t, cos_tt, sin_tt = _rope_tables(l)
    mean_mat = jnp.asarray(np.kron(np.eye(heads), np.full((HEAD_DIM, HEAD_DIM), 1.0 / HEAD_DIM)), BF16)
    pad = (-(b + 1)) % SUBLANES
    cc = jnp.concatenate([c, c_ctx[None], jnp.zeros((pad, d), F32)], axis=0)

    layer = 0
    mod = _modulation(cc, ada_w[layer], ada_b[layer])
    mod3 = mod.reshape(mod.shape[0], N_MOD, d)
    wl = w_in[layer]
    q0, k0, v0, g0 = pw, pw + nw, pw + 2 * nw, pw + 3 * nw
    w_all = wl.astype(BF16)
    w_qvt = jnp.concatenate([wl[:, q0:k0].T, wl[:, v0:g0].T], axis=0).astype(BF16)
    qg_tt = jnp.broadcast_to(q_norm_g[layer][:, None], (HEAD_DIM, tm_proj))
    kg_t = jnp.tile(k_norm_g[layer], heads)[None]
    n1 = norm1_g[layer][None]

    k_ctx, vt_ctx = _ctx_proj(ctx, mod3, n1, w_all, w_qvt, kg_t, mean_mat, k0 // nw, ctx_row)
    u, qrt, qpt, kr, vt, gp, gn = _in_proj(x, mod3, n1, w_all, w_qvt, cos_t, sin_t, cos_tt, sin_tt,
                                           qg_tt, kg_t, mean_mat, (0, k0, g0, g0 + d), pw, tm=tm_proj)
    bias = _bias_tables(rpb[layer], rows)
    return _block(x, mod3, norm2_g[layer][None], u, qrt, qpt, kr, vt, k_ctx, vt_ctx, bias, gp, gn,
                  pool_w[layer].astype(BF16), pool_scale[layer][None],
                  w_branch_pool[layer].astype(BF16), w_branch_na[layer].astype(BF16),
                  w_out[layer].astype(BF16), mlp_w1[layer].astype(BF16), mlp_w2[layer].astype(BF16))
```

```python
import functools

import numpy as np
import jax
import jax.numpy as jnp
from jax import lax
from jax.experimental import pallas as pl
from jax.experimental.pallas import tpu as pltpu

GRID_W = 64
N_MOD = 6
POOL_WINDOWS = (2, 4, 8, 16)
HEAD_DIM = 64
NA_ROWS = 8
NA_COLS = 16
ROPE_THETA = 10000.0
EPS = 1e-6
MASK_VALUE = -1e30
LOG2E = 1.4426950408889634

VMEM_LIMIT_BYTES = 56 * 1024 * 1024
SUBLANES = 8
BF16_ROWS = 16
POOL_HALO = 8
TILE_ROWS = 4
KEY_ROWS = TILE_ROWS + NA_ROWS

BF16 = jnp.bfloat16
F32 = jnp.float32


def _dot(a, b):
    return jnp.dot(a, b, preferred_element_type=F32)


def _dot_nt(a, b):
    return lax.dot_general(a, b, (((1,), (1,)), ((), ())), preferred_element_type=F32)


def _params(sem):
    return pltpu.CompilerParams(dimension_semantics=sem, vmem_limit_bytes=VMEM_LIMIT_BYTES)


def _rms(x):
    return x * lax.rsqrt(jnp.mean(x * x, axis=-1, keepdims=True) + EPS)


def _sigmoid(z):
    return 0.5 * jnp.tanh(0.5 * z) + 0.5


def _mod_kernel(c_ref, w_ref, b_ref, o_ref):
    c = c_ref[...]
    s = (c * jax.nn.sigmoid(c)).astype(BF16)
    o_ref[...] = _dot(s, w_ref[...].astype(BF16)) + b_ref[...]


def _modulation(cc, ada_w, ada_b):
    m, d = cc.shape
    n = ada_w.shape[1]
    return pl.pallas_call(
        _mod_kernel,
        grid=(n // d,),
        in_specs=[pl.BlockSpec((m, d), lambda j: (0, 0)),
                  pl.BlockSpec((d, d), lambda j: (0, j)),
                  pl.BlockSpec((1, d), lambda j: (0, j))],
        out_specs=pl.BlockSpec((m, d), lambda j: (0, j)),
        out_shape=jax.ShapeDtypeStruct((m, n), F32),
        compiler_params=_params(("arbitrary",)),
        name="modulation",
    )(cc, ada_w, ada_b.reshape(1, n))


def _transpose_cast_kernel(w_ref, o_ref):
    o_ref[...] = w_ref[...].T.astype(BF16)


def _qv_weights(w, q_block, v_block, width):
    d = w.shape[0]
    return pl.pallas_call(
        _transpose_cast_kernel,
        grid=(2,),
        in_specs=[pl.BlockSpec((d, width), lambda s: (0, q_block + s * (v_block - q_block)))],
        out_specs=pl.BlockSpec((width, d), lambda s: (s, 0)),
        out_shape=jax.ShapeDtypeStruct((2 * width, d), BF16),
        compiler_params=_params(("arbitrary",)),
        name="qv_weights",
    )(w)


def _head_norm(a, mean_mat, g):
    ms = _dot((a * a).astype(BF16), mean_mat)
    return a * lax.rsqrt(ms + EPS) * g


def _rope(a, cos, sin_signed, first_half):
    width = a.shape[-1]
    quarter = HEAD_DIM // 4
    up = pltpu.roll(a, width - quarter, axis=1)
    down = pltpu.roll(a, quarter, axis=1)
    swapped = jnp.where(first_half, up, down)
    return a * cos + swapped * sin_signed


def _modulated_norm(x, mod_ref, g_ref, shift_row):
    return _rms(x) * g_ref[...] * (1.0 + mod_ref[0, shift_row + 1:shift_row + 2, :]) \
        + mod_ref[0, shift_row:shift_row + 1, :]


def _ctx_kernel(x_ref, mod_ref, g_ref, wk_ref, wvt_ref, kg_ref, mean_ref, k_ref, vt_ref):
    hb = _modulated_norm(x_ref[0], mod_ref, g_ref, 0).astype(BF16)
    k_ref[0] = _head_norm(_dot(hb, wk_ref[...]), mean_ref[...], kg_ref[...]).astype(BF16)
    vt_ref[0] = _dot_nt(wvt_ref[...], hb).astype(BF16)


def _ctx_proj(ctx, mod3, norm_g, w_all, w_qvt, kg_t, mean_mat, key_block, ctx_row):
    b, n, d = ctx.shape
    nw = kg_t.shape[-1]
    sd = jax.ShapeDtypeStruct
    return pl.pallas_call(
        _ctx_kernel,
        grid=(b,),
        in_specs=[pl.BlockSpec((1, n, d), lambda i: (i, 0, 0)),
                  pl.BlockSpec((1, N_MOD, d), lambda i: (ctx_row, 0, 0)),
                  pl.BlockSpec((1, d), lambda i: (0, 0)),
                  pl.BlockSpec((d, nw), lambda i: (0, key_block)),
                  pl.BlockSpec((nw, d), lambda i: (1, 0)),
                  pl.BlockSpec((1, nw), lambda i: (0, 0)),
                  pl.BlockSpec((nw, nw), lambda i: (0, 0))],
        out_specs=[pl.BlockSpec((1, n, nw), lambda i: (i, 0, 0)),
                   pl.BlockSpec((1, nw, n), lambda i: (i, 0, 0))],
        out_shape=[sd((b, n, nw), BF16), sd((b, nw, n), BF16)],
        compiler_params=_params(("arbitrary",)),
        name="ctx_proj",
    )(ctx, mod3, norm_g, w_all, w_qvt, kg_t, mean_mat)


def _in_proj_kernel(x_ref, mod_ref, g_ref, w_ref, wt_ref, cos_ref, sin_ref, cost_ref, sint_ref,
                    qgt_ref, kg_ref, mean_ref,
                    u_ref, qrt_ref, qpt_ref, kr_ref, vt_ref, gp_ref, gn_ref, *, col_starts):
    pw = u_ref.shape[-1]
    nw = kr_ref.shape[-1]
    d = x_ref.shape[-1]
    heads = nw // HEAD_DIM
    quarter = HEAD_DIM // 4
    scale = HEAD_DIM ** -0.5 * LOG2E
    u0, k0, gp0, gn0 = col_starts

    hb = _modulated_norm(x_ref[0], mod_ref, g_ref, 0).astype(BF16)

    qt = _dot_nt(wt_ref[0:nw, :], hb)
    cost, sint, qgt = cost_ref[...], sint_ref[...], qgt_ref[...]

    def query_heads(first, last):
        for h in range(first, last):
            rows = slice(h * HEAD_DIM, (h + 1) * HEAD_DIM)
            z = qt[rows, :]
            qn = z * lax.rsqrt(jnp.mean(z * z, axis=0, keepdims=True) + EPS) * (qgt * scale)
            swapped = jnp.concatenate([qn[quarter:2 * quarter], qn[0:quarter],
                                       qn[3 * quarter:], qn[2 * quarter:3 * quarter]], axis=0)
            qpt_ref[0, rows, :] = qn.astype(BF16)
            qrt_ref[0, rows, :] = (qn * cost + swapped * sint).astype(BF16)

    z_gp = _dot(hb, w_ref[:, gp0:gp0 + d])
    query_heads(0, heads // 2)
    z_gn = _dot(hb, w_ref[:, gn0:gn0 + d])
    gp_ref[0] = _sigmoid(z_gp).astype(BF16)
    query_heads(heads // 2, heads)
    z_k = _dot(hb, w_ref[:, k0:k0 + nw])
    gn_ref[0] = _sigmoid(z_gn).astype(BF16)
    z_vt = _dot_nt(wt_ref[nw:, :], hb)

    reps = nw // cos_ref.shape[-1]
    cos = jnp.concatenate([cos_ref[...]] * reps, axis=1)
    sin = jnp.concatenate([sin_ref[...]] * reps, axis=1)
    lane = lax.broadcasted_iota(jnp.int32, (1, nw), 1)
    first_half = (lane % (HEAD_DIM // 2)) < quarter
    k = _head_norm(z_k, mean_ref[...], kg_ref[...])
    kr_ref[0] = _rope(k, cos, sin, first_half).astype(BF16)
    vt_ref[0] = z_vt.astype(BF16)
    u_ref[0] = _dot(hb, w_ref[:, u0:u0 + pw])


def _in_proj(x, mod3, norm_g, w_all, w_qvt, cos_t, sin_t, cos_tt, sin_tt, qg_tt, kg_t, mean_mat,
             col_starts, pw, tm):
    b, l, d = x.shape
    nw = kg_t.shape[-1]
    tw = cos_t.shape[-1]
    tok = lambda bi, i: (bi, i, 0)
    chan = lambda bi, i: (bi, 0, i)
    const = lambda bi, i: (0, 0)
    sd = jax.ShapeDtypeStruct
    return pl.pallas_call(
        functools.partial(_in_proj_kernel, col_starts=col_starts),
        grid=(b, l // tm),
        in_specs=[pl.BlockSpec((1, tm, d), tok),
                  pl.BlockSpec((1, N_MOD, d), lambda bi, i: (bi, 0, 0)),
                  pl.BlockSpec((1, d), const),
                  pl.BlockSpec(w_all.shape, const, pipeline_mode=pl.Buffered(1)),
                  pl.BlockSpec(w_qvt.shape, const, pipeline_mode=pl.Buffered(1)),
                  pl.BlockSpec((tm, tw), lambda bi, i: (i, 0)),
                  pl.BlockSpec((tm, tw), lambda bi, i: (i, 0)),
                  pl.BlockSpec((HEAD_DIM, tm), lambda bi, i: (0, i)),
                  pl.BlockSpec((HEAD_DIM, tm), lambda bi, i: (0, i)),
                  pl.BlockSpec((HEAD_DIM, tm), const),
                  pl.BlockSpec((1, nw), const),
                  pl.BlockSpec((nw, nw), const)],
        out_specs=[pl.BlockSpec((1, tm, pw), tok),
                   pl.BlockSpec((1, nw, tm), chan), pl.BlockSpec((1, nw, tm), chan),
                   pl.BlockSpec((1, tm, nw), tok),
                   pl.BlockSpec((1, nw, tm), chan),
                   pl.BlockSpec((1, tm, d), tok), pl.BlockSpec((1, tm, d), tok)],
        out_shape=[sd((b, l, pw), F32),
                   sd((b, nw, l), BF16), sd((b, nw, l), BF16),
                   sd((b, l, nw), BF16),
                   sd((b, nw, l), BF16),
                   sd((b, l, d), BF16), sd((b, l, d), BF16)],
        compiler_params=_params(("arbitrary", "arbitrary")),
        name="in_proj",
    )(x, mod3, norm_g, w_all, w_qvt, cos_t, sin_t, cos_tt, sin_tt, qg_tt, kg_t, mean_mat)


def _mixer_kernel(u_ref, up_ref, un_ref, qrt_ref, qpt_ref,
                  k0_ref, k1_ref, k2_ref, vt0_ref, vt1_ref, vt2_ref, kc_ref, vct_ref, bias_ref,
                  pw_ref, ps_ref,
                  na_ref, pool_ref, nat_scr, e_scr, a2_scr, a4_scr, a8_scr, *, seq_len):
    i = pl.program_id(1)
    tm = u_ref.shape[1]
    nw = qrt_ref.shape[1]
    heads = nw // HEAD_DIM
    pair = 2 * HEAD_DIM

    zeros = jnp.zeros((HEAD_DIM, tm), BF16)
    ones = jnp.ones((BF16_ROWS, tm), BF16)
    k_refs = (k0_ref, k1_ref, k2_ref)
    vt_refs = (vt0_ref, vt1_ref, vt2_ref)

    def head_rows(h):
        return slice(h * HEAD_DIM, (h + 1) * HEAD_DIM)

    def scores(h):
        rows = head_rows(h)
        lanes = slice((h // 2) * pair, (h // 2 + 1) * pair)
        q_rot, q_plain = qrt_ref[0, rows, :], qpt_ref[0, rows, :]
        if h % 2 == 0:
            w_rot = jnp.concatenate([q_rot, zeros], axis=0)
            w_plain = jnp.concatenate([q_plain, zeros], axis=0)
        else:
            w_rot = jnp.concatenate([zeros, q_rot], axis=0)
            w_plain = jnp.concatenate([zeros, q_plain], axis=0)
        s = [_dot(k_refs[j][0, :, lanes], w_rot) + bias_ref[0, h, j * tm:(j + 1) * tm, :]
             for j in range(3)]
        s.append(_dot(kc_ref[0, :, lanes], w_plain))
        m = s[0].max(axis=0, keepdims=True)
        for sj in s[1:]:
            m = jnp.maximum(m, sj.max(axis=0, keepdims=True))
        return s, m

    def probabilities(s_and_m):
        s, m = s_and_m
        return [jnp.exp2(sj - m).astype(BF16) for sj in s]

    def weighted_values(h, p):
        rows = head_rows(h)
        ot = _dot(jnp.concatenate([vct_ref[0, rows, :], ones], axis=0), p[3])
        for j in range(3):
            ot = ot + _dot(jnp.concatenate([vt_refs[j][0, rows, :], ones], axis=0), p[j])
        nat_scr[rows, :] = ot[:HEAD_DIM] / ot[HEAD_DIM:HEAD_DIM + 1]

    gd = pw_ref.shape[-1]
    n_e = tm + 2 * POOL_HALO
    lo, hi = SUBLANES, SUBLANES + n_e
    first = SUBLANES + POOL_HALO
    g0, g1 = slice(0, gd), slice(gd, 2 * gd)

    def rows_at(ref, off, cols):
        return ref[first + off:first + off + tm, cols]

    def pool_fill():
        e_scr[0:lo, :] = jnp.zeros((lo, e_scr.shape[1]), F32)
        a2_scr[0:lo, :] = jnp.zeros((lo, a2_scr.shape[1]), F32)
        a4_scr[0:lo, :] = jnp.zeros((lo, a4_scr.shape[1]), F32)
        e_scr[lo:first, :] = jnp.where(i > 0, up_ref[0], 0.0)
        e_scr[first:first + tm, :] = u_ref[0]
        e_scr[first + tm:hi, :] = jnp.where((i + 1) * tm < seq_len, un_ref[0], 0.0)
        a2_scr[lo:hi, :] = e_scr[lo:hi, 2 * gd:] + e_scr[lo - 1:hi - 1, 2 * gd:]

    def pool_double():
        a4_scr[lo:hi, :] = a2_scr[lo:hi, :] + a2_scr[lo - 2:hi - 2, :]
        a8_scr[lo:hi, :] = a4_scr[lo:hi, gd:] + a4_scr[lo - 4:hi - 4, gd:]

    def pool_group(g):
        w = POOL_WINDOWS[g]
        if g == 0:
            total = rows_at(e_scr, -1, g0) + rows_at(e_scr, 0, g0)
        elif g == 1:
            total = ((rows_at(e_scr, -2, g1) + rows_at(e_scr, -1, g1))
                     + (rows_at(e_scr, 0, g1) + rows_at(e_scr, 1, g1)))
        elif g == 2:
            total = rows_at(a4_scr, 3, g0) + rows_at(a4_scr, -1, g0)
        else:
            total = rows_at(a8_scr, 7, g0) + rows_at(a8_scr, -1, g0)
        pos = i * tm + lax.broadcasted_iota(jnp.int32, (tm, 1), 0)
        cnt = (jnp.minimum(pos + w // 2, seq_len) - jnp.maximum(pos - w // 2, 0)).astype(F32)
        dlt = total / cnt - u_ref[0, :, g * gd:(g + 1) * gd]
        cols = slice(g * gd, (g + 1) * gd)
        return (_dot(dlt.astype(BF16), pw_ref[g]) * ps_ref[:, cols]).astype(BF16)

    pooled = []
    pool_stages = [pool_fill, pool_double] + [functools.partial(pool_group, g)
                                               for g in range(len(POOL_WINDOWS))]

    p_next = probabilities(scores(0))
    for h in range(heads):
        p_cur = p_next
        if h + 1 < heads:
            p_next = probabilities(scores(h + 1))
        if h < len(pool_stages):
            out = pool_stages[h]()
            if out is not None:
                pooled.append(out)
        weighted_values(h, p_cur)

    pool_ref[0] = jnp.concatenate(pooled, axis=1)
    na_ref[0] = nat_scr[...].T.astype(BF16)


def _mixer(u, qrt, qpt, kr, vt, kc, vct, bias, pool_w_bf, pool_scale):
    b, l, pw = u.shape
    nw = kr.shape[-1]
    gd = pool_w_bf.shape[-1]
    tm = TILE_ROWS * GRID_W
    nt = l // tm
    hb = tm // POOL_HALO
    n_halo = l // POOL_HALO
    nc = kc.shape[1]
    assert pw == len(POOL_WINDOWS) * gd and POOL_HALO == max(POOL_WINDOWS) // 2

    tok = lambda bi, i: (bi, i, 0)
    chan = lambda bi, i: (bi, 0, i)
    const2 = lambda bi, i: (0, 0)
    first_block = lambda i: jnp.clip(i - 1, 0, nt - 3)

    def k_spec(j):
        return pl.BlockSpec((1, tm, nw), lambda bi, i: (bi, first_block(i) + j, 0))

    def vt_spec(j):
        return pl.BlockSpec((1, nw, tm), lambda bi, i: (bi, 0, first_block(i) + j))

    def bias_idx(bi, i):
        return (jnp.where(i == 0, 0, jnp.where(i == nt - 1, 2, 1)), 0, 0, 0)

    in_specs = [
        pl.BlockSpec((1, tm, pw), tok),
        pl.BlockSpec((1, POOL_HALO, pw), lambda bi, i: (bi, jnp.maximum(i * hb - 1, 0), 0)),
        pl.BlockSpec((1, POOL_HALO, pw), lambda bi, i: (bi, jnp.minimum((i + 1) * hb, n_halo - 1), 0)),
        pl.BlockSpec((1, nw, tm), chan),
        pl.BlockSpec((1, nw, tm), chan),
        k_spec(0), k_spec(1), k_spec(2),
        vt_spec(0), vt_spec(1), vt_spec(2),
        pl.BlockSpec((1, nc, nw), lambda bi, i: (bi, 0, 0)),
        pl.BlockSpec((1, nw, nc), lambda bi, i: (bi, 0, 0)),
        pl.BlockSpec((1,) + bias.shape[1:], bias_idx),
        pl.BlockSpec(pool_w_bf.shape, lambda bi, i: (0, 0, 0)),
        pl.BlockSpec((1, pw), const2),
    ]
    n_scr = SUBLANES + tm + 2 * POOL_HALO
    return pl.pallas_call(
        functools.partial(_mixer_kernel, seq_len=l),
        grid=(b, nt),
        in_specs=in_specs,
        out_specs=[pl.BlockSpec((1, tm, nw), tok), pl.BlockSpec((1, tm, pw), tok)],
        out_shape=[jax.ShapeDtypeStruct((b, l, nw), BF16), jax.ShapeDtypeStruct((b, l, pw), BF16)],
        scratch_shapes=[pltpu.VMEM((nw, tm), F32),
                        pltpu.VMEM((n_scr, pw), F32),
                        pltpu.VMEM((n_scr, 2 * gd), F32),
                        pltpu.VMEM((n_scr, 2 * gd), F32),
                        pltpu.VMEM((n_scr, gd), F32)],
        compiler_params=_params(("arbitrary", "arbitrary")),
        name="mixer",
    )(u, u, u, qrt, qpt, kr, kr, kr, vt, vt, vt, kc, vct, bias, pool_w_bf, pool_scale)


def _post_kernel(x_ref, mod_ref, g_ref, na_ref, pool_ref, gp_ref, gn_ref,
                 wbp_ref, wbn_ref, wo_ref, w1_ref, w2_ref, o_ref, *, chunk):
    merged = (gp_ref[0].astype(F32) * _dot(pool_ref[0], wbp_ref[...])
              + gn_ref[0].astype(F32) * _dot(na_ref[0], wbn_ref[...]))
    x = x_ref[0] + mod_ref[0, 2:3, :] * _dot(merged.astype(BF16), wo_ref[...])
    hb = _modulated_norm(x, mod_ref, g_ref, 3).astype(BF16)
    acc = jnp.zeros(x.shape, F32)
    for c0 in range(0, w1_ref.shape[1], chunk):
        a = jnp.maximum(_dot(hb, w1_ref[:, c0:c0 + chunk]), 0.0)
        acc = acc + _dot((a * a).astype(BF16), w2_ref[c0:c0 + chunk, :])
    o_ref[0] = x + mod_ref[0, 5:6, :] * acc


def _post(x, mod3, norm_g, na, pool, gp, gn, wbp, wbn, wo, w1, w2, tm):
    b, l, d = x.shape
    tok = lambda bi, i: (bi, i, 0)
    const = lambda bi, i: (0, 0)
    resident = lambda w: pl.BlockSpec(w.shape, const, pipeline_mode=pl.Buffered(1))
    return pl.pallas_call(
        functools.partial(_post_kernel, chunk=d),
        grid=(b, l // tm),
        in_specs=[pl.BlockSpec((1, tm, d), tok),
                  pl.BlockSpec((1, N_MOD, d), lambda bi, i: (bi, 0, 0)),
                  pl.BlockSpec((1, d), const),
                  pl.BlockSpec((1, tm, na.shape[-1]), tok),
                  pl.BlockSpec((1, tm, pool.shape[-1]), tok),
                  pl.BlockSpec((1, tm, d), tok),
                  pl.BlockSpec((1, tm, d), tok),
                  resident(wbp), resident(wbn), resident(wo), resident(w1), resident(w2)],
        out_specs=pl.BlockSpec((1, tm, d), tok),
        out_shape=jax.ShapeDtypeStruct((b, l, d), F32),
        compiler_params=_params(("arbitrary", "arbitrary")),
        name="post",
    )(x, mod3, norm_g, na, pool, gp, gn, wbp, wbn, wo, w1, w2)


def _rope_tables(seq_len):
    n_freq = HEAD_DIM // 4
    t = np.arange(seq_len)
    inv = (ROPE_THETA ** (-np.arange(n_freq, dtype=np.float32) / n_freq)).astype(np.float32)
    ang_row = (t // GRID_W).astype(np.float32)[:, None] * inv
    ang_col = (t % GRID_W).astype(np.float32)[:, None] * inv
    cr, sr, cc, sc = np.cos(ang_row), np.sin(ang_row), np.cos(ang_col), np.sin(ang_col)
    cos = np.concatenate([cr, cr, cc, cc], axis=1).astype(np.float32)
    sin = np.concatenate([-sr, sr, -sc, sc], axis=1).astype(np.float32)
    return (jnp.asarray(np.tile(cos, (1, 2))), jnp.asarray(np.tile(sin, (1, 2))),
            jnp.asarray(np.ascontiguousarray(cos.T)), jnp.asarray(np.ascontiguousarray(sin.T)))


def _bias_kernel(el_ref, er_ref, o_ref, *, rows):
    lanes = 2 * GRID_W
    kc = lax.broadcasted_iota(jnp.int32, (GRID_W, lanes), 0)
    lane = lax.broadcasted_iota(jnp.int32, (GRID_W, lanes), 1)
    qc = lane % GRID_W
    c0 = jnp.clip(qc - NA_COLS // 2, 0, GRID_W - NA_COLS)
    col_ok = (kc >= c0) & (kc < c0 + NA_COLS)
    left = lane < GRID_W
    masked = jnp.full((GRID_W, lanes), MASK_VALUE, F32)

    cache = {}

    def toeplitz(side, dr):
        if (side, dr) not in cache:
            ref = er_ref if side else el_ref
            dd = dr + NA_ROWS - 1
            vec = jnp.broadcast_to(ref[0, dd:dd + 1, :], (GRID_W, lanes))
            cache[(side, dr)] = pltpu.roll(vec, 0, axis=1, stride=1, stride_axis=0) * LOG2E
        return cache[(side, dr)]

    nt = rows // TILE_ROWS
    for t, tile in enumerate((0, nt // 2, nt - 1)):
        r = tile * TILE_ROWS
        ks = min(max(r - TILE_ROWS, 0), rows - KEY_ROWS)
        for j in range(KEY_ROWS):
            kr = ks + j
            for ip in range(TILE_ROWS // 2):
                halves = []
                for side in range(2):
                    qr = r + 2 * ip + side
                    r0 = min(max(qr - NA_ROWS // 2, 0), rows - NA_ROWS)
                    ok = r0 <= kr < r0 + NA_ROWS
                    halves.append(toeplitz(side, kr - qr) if ok else masked)
                blk = jnp.where(col_ok, jnp.where(left, halves[0], halves[1]), MASK_VALUE)
                o_ref[t, 0, j * GRID_W:(j + 1) * GRID_W, ip * lanes:(ip + 1) * lanes] = blk


def _bias_tables(rpb, rows):
    heads, nr, ncol = rpb.shape
    lanes = 2 * GRID_W
    half = NA_COLS - 1
    zeros = lambda n: jnp.zeros((heads, nr, n), F32)
    flipped = rpb[..., ::-1]
    e_left = jnp.concatenate([flipped[..., half:], zeros(lanes - ncol), flipped[..., :half]], axis=-1)
    e_right = jnp.concatenate([zeros(GRID_W - half), flipped, zeros(lanes - GRID_W + half - ncol)], axis=-1)
    pad = (-nr) % SUBLANES
    e_left = jnp.pad(e_left, ((0, 0), (0, pad), (0, 0)))
    e_right = jnp.pad(e_right, ((0, 0), (0, pad), (0, 0)))
    tq, tk = TILE_ROWS * GRID_W, KEY_ROWS * GRID_W
    vec_spec = pl.BlockSpec((1, nr + pad, lanes), lambda h: (h, 0, 0))
    return pl.pallas_call(
        functools.partial(_bias_kernel, rows=rows),
        grid=(heads,),
        in_specs=[vec_spec, vec_spec],
        out_specs=pl.BlockSpec((3, 1, tk, tq), lambda h: (0, h, 0, 0)),
        out_shape=jax.ShapeDtypeStruct((3, heads, tk, tq), F32),
        compiler_params=_params(("arbitrary",)),
        name="bias_tables",
    )(e_left, e_right)


def kernel(x, c, ctx, c_ctx, ada_w, ada_b, norm1_g, norm2_g, w_in, pool_w, pool_scale,
           q_norm_g, k_norm_g, rpb, w_branch_pool, w_branch_na, w_out, mlp_w1, mlp_w2):
    b, l, d = x.shape
    depth = ada_w.shape[0]
    pw = pool_scale.shape[-1]
    nw = w_branch_na.shape[1]
    heads = nw // HEAD_DIM
    rows = l // GRID_W
    assert depth == 1, "the context-stream update is only needed when another layer follows"
    assert l % (TILE_ROWS * GRID_W) == 0 and rows >= KEY_ROWS
    assert w_in.shape[-1] == pw + 3 * nw + 2 * d and pw == nw
    ctx_row = b
    tm_proj = 512

    cos_t, sin_t, cos_tt, sin_tt = _rope_tables(l)
    mean_mat = jnp.asarray(np.kron(np.eye(heads), np.full((HEAD_DIM, HEAD_DIM), 1.0 / HEAD_DIM)), BF16)
    pad = (-(b + 1)) % SUBLANES
    cc = jnp.concatenate([c, c_ctx[None], jnp.zeros((pad, d), F32)], axis=0)

    layer = 0
    mod = _modulation(cc, ada_w[layer], ada_b[layer])
    mod3 = mod.reshape(mod.shape[0], N_MOD, d)
    wl = w_in[layer]
    q0, k0, v0, g0 = pw, pw + nw, pw + 2 * nw, pw + 3 * nw
    w_all = wl.astype(BF16)
    w_qvt = _qv_weights(wl, q0 // nw, v0 // nw, nw)
    qg_tt = jnp.broadcast_to(q_norm_g[layer][:, None], (HEAD_DIM, tm_proj))
    kg_t = jnp.tile(k_norm_g[layer], heads)[None]
    n1 = norm1_g[layer][None]

    k_ctx, vt_ctx = _ctx_proj(ctx, mod3, n1, w_all, w_qvt, kg_t, mean_mat, k0 // nw, ctx_row)
    u, qrt, qpt, kr, vt, gp, gn = _in_proj(x, mod3, n1, w_all, w_qvt, cos_t, sin_t, cos_tt, sin_tt,
                                           qg_tt, kg_t, mean_mat, (0, k0, g0, g0 + d), pw, tm=tm_proj)
    bias = _bias_tables(rpb[layer], rows)
    na, pool = _mixer(u, qrt, qpt, kr, vt, k_ctx, vt_ctx, bias,
                      pool_w[layer].astype(BF16), pool_scale[layer][None])
    return _post(x, mod3, norm2_g[layer][None], na, pool, gp, gn,
                 w_branch_pool[layer].astype(BF16), w_branch_na[layer].astype(BF16),
                 w_out[layer].astype(BF16), mlp_w1[layer].astype(BF16), mlp_w2[layer].astype(BF16),
                 tm=tm_proj)
```

```python
import functools

import numpy as np
import jax
import jax.numpy as jnp
from jax import lax
from jax.experimental import pallas as pl
from jax.experimental.pallas import tpu as pltpu

GRID_W = 64
N_MOD = 6
POOL_WINDOWS = (2, 4, 8, 16)
HEAD_DIM = 64
NA_ROWS = 8
NA_COLS = 16
ROPE_THETA = 10000.0
EPS = 1e-6
MASK_VALUE = -1e30
LOG2E = 1.4426950408889634

VMEM_LIMIT_BYTES = 56 * 1024 * 1024
SUBLANES = 8
BF16_ROWS = 16
POOL_HALO = 8
TILE_ROWS = 4
KEY_ROWS = TILE_ROWS + NA_ROWS

BF16 = jnp.bfloat16
F32 = jnp.float32


def _dot(a, b):
    return jnp.dot(a, b, preferred_element_type=F32)


def _dot_nt(a, b):
    return lax.dot_general(a, b, (((1,), (1,)), ((), ())), preferred_element_type=F32)


def _params(sem):
    return pltpu.CompilerParams(dimension_semantics=sem, vmem_limit_bytes=VMEM_LIMIT_BYTES)


def _rms(x):
    return x * lax.rsqrt(jnp.mean(x * x, axis=-1, keepdims=True) + EPS)


def _sigmoid(z):
    return 0.5 * jnp.tanh(0.5 * z) + 0.5


def _mod_kernel(c_ref, w_ref, b_ref, o_ref):
    c = c_ref[...]
    s = (c * jax.nn.sigmoid(c)).astype(BF16)
    o_ref[...] = _dot(s, w_ref[...].astype(BF16)) + b_ref[...]


def _modulation(cc, ada_w, ada_b):
    m, d = cc.shape
    n = ada_w.shape[1]
    return pl.pallas_call(
        _mod_kernel,
        grid=(n // d,),
        in_specs=[pl.BlockSpec((m, d), lambda j: (0, 0)),
                  pl.BlockSpec((d, d), lambda j: (0, j)),
                  pl.BlockSpec((1, d), lambda j: (0, j))],
        out_specs=pl.BlockSpec((m, d), lambda j: (0, j)),
        out_shape=jax.ShapeDtypeStruct((m, n), F32),
        compiler_params=_params(("arbitrary",)),
        name="modulation",
    )(cc, ada_w, ada_b.reshape(1, n))


def _transpose_cast_kernel(w_ref, o_ref):
    o_ref[...] = w_ref[...].T.astype(BF16)


def _qv_weights(w, q_block, v_block, width):
    d = w.shape[0]
    return pl.pallas_call(
        _transpose_cast_kernel,
        grid=(2,),
        in_specs=[pl.BlockSpec((d, width), lambda s: (0, q_block + s * (v_block - q_block)))],
        out_specs=pl.BlockSpec((width, d), lambda s: (s, 0)),
        out_shape=jax.ShapeDtypeStruct((2 * width, d), BF16),
        compiler_params=_params(("arbitrary",)),
        name="qv_weights",
    )(w)


def _head_norm(a, mean_mat, g):
    ms = _dot((a * a).astype(BF16), mean_mat)
    return a * lax.rsqrt(ms + EPS) * g


def _rope(a, cos, sin_signed, first_half):
    width = a.shape[-1]
    quarter = HEAD_DIM // 4
    up = pltpu.roll(a, width - quarter, axis=1)
    down = pltpu.roll(a, quarter, axis=1)
    swapped = jnp.where(first_half, up, down)
    return a * cos + swapped * sin_signed


def _modulated_norm(x, mod_ref, g_ref, shift_row):
    return _rms(x) * g_ref[...] * (1.0 + mod_ref[0, shift_row + 1:shift_row + 2, :]) \
        + mod_ref[0, shift_row:shift_row + 1, :]


def _ctx_kernel(x_ref, mod_ref, g_ref, wk_ref, wvt_ref, kg_ref, mean_ref, k_ref, vt_ref):
    hb = _modulated_norm(x_ref[0], mod_ref, g_ref, 0).astype(BF16)
    k_ref[0] = _head_norm(_dot(hb, wk_ref[...]), mean_ref[...], kg_ref[...]).astype(BF16)
    vt_ref[0] = _dot_nt(wvt_ref[...], hb).astype(BF16)


def _ctx_proj(ctx, mod3, norm_g, w_all, w_qvt, kg_t, mean_mat, key_block, ctx_row):
    b, n, d = ctx.shape
    nw = kg_t.shape[-1]
    sd = jax.ShapeDtypeStruct
    return pl.pallas_call(
        _ctx_kernel,
        grid=(b,),
        in_specs=[pl.BlockSpec((1, n, d), lambda i: (i, 0, 0)),
                  pl.BlockSpec((1, N_MOD, d), lambda i: (ctx_row, 0, 0)),
                  pl.BlockSpec((1, d), lambda i: (0, 0)),
                  pl.BlockSpec((d, nw), lambda i: (0, key_block)),
                  pl.BlockSpec((nw, d), lambda i: (1, 0)),
                  pl.BlockSpec((1, nw), lambda i: (0, 0)),
                  pl.BlockSpec((nw, nw), lambda i: (0, 0))],
        out_specs=[pl.BlockSpec((1, n, nw), lambda i: (i, 0, 0)),
                   pl.BlockSpec((1, nw, n), lambda i: (i, 0, 0))],
        out_shape=[sd((b, n, nw), BF16), sd((b, nw, n), BF16)],
        compiler_params=_params(("arbitrary",)),
        name="ctx_proj",
    )(ctx, mod3, norm_g, w_all, w_qvt, kg_t, mean_mat)


def _in_proj_kernel(*refs, col_starts, n_cast):
    (x_ref, mod_ref, g_ref, w_ref, wt_ref, cos_ref, sin_ref, cost_ref, sint_ref,
     qgt_ref, kg_ref, mean_ref) = refs[:12]
    cast_in = refs[12:12 + n_cast]
    u_ref, qrt_ref, qpt_ref, kr_ref, vt_ref, gp_ref, gn_ref = refs[12 + n_cast:19 + n_cast]
    cast_out = refs[19 + n_cast:]
    for src, dst in zip(cast_in, cast_out):
        dst[...] = src[...].astype(BF16)

    pw = u_ref.shape[-1]
    nw = kr_ref.shape[-1]
    d = x_ref.shape[-1]
    heads = nw // HEAD_DIM
    quarter = HEAD_DIM // 4
    scale = HEAD_DIM ** -0.5 * LOG2E
    u0, k0, gp0, gn0 = col_starts

    hb = _modulated_norm(x_ref[0], mod_ref, g_ref, 0).astype(BF16)

    qt = _dot_nt(wt_ref[0:nw, :], hb)
    cost, sint, qgt = cost_ref[...], sint_ref[...], qgt_ref[...]

    def query_heads(first, last):
        for h in range(first, last):
            rows = slice(h * HEAD_DIM, (h + 1) * HEAD_DIM)
            z = qt[rows, :]
            qn = z * lax.rsqrt(jnp.mean(z * z, axis=0, keepdims=True) + EPS) * (qgt * scale)
            swapped = jnp.concatenate([qn[quarter:2 * quarter], qn[0:quarter],
                                       qn[3 * quarter:], qn[2 * quarter:3 * quarter]], axis=0)
            qpt_ref[0, rows, :] = qn.astype(BF16)
            qrt_ref[0, rows, :] = (qn * cost + swapped * sint).astype(BF16)

    z_gp = _dot(hb, w_ref[:, gp0:gp0 + d])
    query_heads(0, heads // 2)
    z_gn = _dot(hb, w_ref[:, gn0:gn0 + d])
    gp_ref[0] = _sigmoid(z_gp).astype(BF16)
    query_heads(heads // 2, heads)
    z_k = _dot(hb, w_ref[:, k0:k0 + nw])
    gn_ref[0] = _sigmoid(z_gn).astype(BF16)
    z_vt = _dot_nt(wt_ref[nw:, :], hb)

    reps = nw // cos_ref.shape[-1]
    cos = jnp.concatenate([cos_ref[...]] * reps, axis=1)
    sin = jnp.concatenate([sin_ref[...]] * reps, axis=1)
    lane = lax.broadcasted_iota(jnp.int32, (1, nw), 1)
    first_half = (lane % (HEAD_DIM // 2)) < quarter
    k = _head_norm(z_k, mean_ref[...], kg_ref[...])
    kr_ref[0] = _rope(k, cos, sin, first_half).astype(BF16)
    vt_ref[0] = z_vt.astype(BF16)
    u_ref[0] = _dot(hb, w_ref[:, u0:u0 + pw])


def _in_proj(x, mod3, norm_g, w_all, w_qvt, cos_t, sin_t, cos_tt, sin_tt, qg_tt, kg_t, mean_mat,
             col_starts, pw, tm, later_weights):
    b, l, d = x.shape
    nw = kg_t.shape[-1]
    tw = cos_t.shape[-1]
    nt = l // tm
    tok = lambda bi, i: (bi, i, 0)
    chan = lambda bi, i: (bi, 0, i)
    const = lambda bi, i: (0, 0)
    sd = jax.ShapeDtypeStruct
    slab_specs = []
    for w in later_weights:
        assert w.shape[0] % (b * nt * BF16_ROWS) == 0
        slab_specs.append(pl.BlockSpec((w.shape[0] // (b * nt), w.shape[1]),
                                       lambda bi, i: (bi * nt + i, 0)))
    outs = pl.pallas_call(
        functools.partial(_in_proj_kernel, col_starts=col_starts, n_cast=len(later_weights)),
        grid=(b, nt),
        in_specs=[pl.BlockSpec((1, tm, d), tok),
                  pl.BlockSpec((1, N_MOD, d), lambda bi, i: (bi, 0, 0)),
                  pl.BlockSpec((1, d), const),
                  pl.BlockSpec(w_all.shape, const, pipeline_mode=pl.Buffered(1)),
                  pl.BlockSpec(w_qvt.shape, const, pipeline_mode=pl.Buffered(1)),
                  pl.BlockSpec((tm, tw), lambda bi, i: (i, 0)),
                  pl.BlockSpec((tm, tw), lambda bi, i: (i, 0)),
                  pl.BlockSpec((HEAD_DIM, tm), lambda bi, i: (0, i)),
                  pl.BlockSpec((HEAD_DIM, tm), lambda bi, i: (0, i)),
                  pl.BlockSpec((HEAD_DIM, tm), const),
                  pl.BlockSpec((1, nw), const),
                  pl.BlockSpec((nw, nw), const)] + slab_specs,
        out_specs=[pl.BlockSpec((1, tm, pw), tok),
                   pl.BlockSpec((1, nw, tm), chan), pl.BlockSpec((1, nw, tm), chan),
                   pl.BlockSpec((1, tm, nw), tok),
                   pl.BlockSpec((1, nw, tm), chan),
                   pl.BlockSpec((1, tm, d), tok), pl.BlockSpec((1, tm, d), tok)] + slab_specs,
        out_shape=[sd((b, l, pw), F32),
                   sd((b, nw, l), BF16), sd((b, nw, l), BF16),
                   sd((b, l, nw), BF16),
                   sd((b, nw, l), BF16),
                   sd((b, l, d), BF16), sd((b, l, d), BF16)]
                  + [sd(w.shape, BF16) for w in later_weights],
        compiler_params=_params(("arbitrary", "arbitrary")),
        name="in_proj",
    )(x, mod3, norm_g, w_all, w_qvt, cos_t, sin_t, cos_tt, sin_tt, qg_tt, kg_t, mean_mat,
      *later_weights)
    return outs[:7], outs[7:]


def _mixer_kernel(u_ref, up_ref, un_ref, qrt_ref, qpt_ref,
                  k0_ref, k1_ref, k2_ref, vt0_ref, vt1_ref, vt2_ref, kc_ref, vct_ref, bias_ref,
                  pw_ref, ps_ref,
                  na_ref, pool_ref, nat_scr, e_scr, a2_scr, a4_scr, a8_scr, *, seq_len):
    i = pl.program_id(1)
    tm = u_ref.shape[1]
    nw = qrt_ref.shape[1]
    heads = nw // HEAD_DIM
    pair = 2 * HEAD_DIM

    zeros = jnp.zeros((HEAD_DIM, tm), BF16)
    ones = jnp.ones((BF16_ROWS, tm), BF16)
    k_refs = (k0_ref, k1_ref, k2_ref)
    vt_refs = (vt0_ref, vt1_ref, vt2_ref)

    def head_rows(h):
        return slice(h * HEAD_DIM, (h + 1) * HEAD_DIM)

    def scores(h):
        rows = head_rows(h)
        lanes = slice((h // 2) * pair, (h // 2 + 1) * pair)
        q_rot, q_plain = qrt_ref[0, rows, :], qpt_ref[0, rows, :]
        if h % 2 == 0:
            w_rot = jnp.concatenate([q_rot, zeros], axis=0)
            w_plain = jnp.concatenate([q_plain, zeros], axis=0)
        else:
            w_rot = jnp.concatenate([zeros, q_rot], axis=0)
            w_plain = jnp.concatenate([zeros, q_plain], axis=0)
        s = [_dot(k_refs[j][0, :, lanes], w_rot) + bias_ref[0, h, j * tm:(j + 1) * tm, :]
             for j in range(3)]
        s.append(_dot(kc_ref[0, :, lanes], w_plain))
        m = s[0].max(axis=0, keepdims=True)
        for sj in s[1:]:
            m = jnp.maximum(m, sj.max(axis=0, keepdims=True))
        return s, m

    def probabilities(s_and_m):
        s, m = s_and_m
        return [jnp.exp2(sj - m).astype(BF16) for sj in s]

    def weighted_values(h, p):
        rows = head_rows(h)
        ot = _dot(jnp.concatenate([vct_ref[0, rows, :], ones], axis=0), p[3])
        for j in range(3):
            ot = ot + _dot(jnp.concatenate([vt_refs[j][0, rows, :], ones], axis=0), p[j])
        nat_scr[rows, :] = ot[:HEAD_DIM] / ot[HEAD_DIM:HEAD_DIM + 1]

    gd = pw_ref.shape[-1]
    n_e = tm + 2 * POOL_HALO
    lo, hi = SUBLANES, SUBLANES + n_e
    first = SUBLANES + POOL_HALO
    g0, g1 = slice(0, gd), slice(gd, 2 * gd)

    def rows_at(ref, off, cols):
        return ref[first + off:first + off + tm, cols]

    def pool_fill():
        e_scr[0:lo, :] = jnp.zeros((lo, e_scr.shape[1]), F32)
        a2_scr[0:lo, :] = jnp.zeros((lo, a2_scr.shape[1]), F32)
        a4_scr[0:lo, :] = jnp.zeros((lo, a4_scr.shape[1]), F32)
        e_scr[lo:first, :] = jnp.where(i > 0, up_ref[0], 0.0)
        e_scr[first:first + tm, :] = u_ref[0]
        e_scr[first + tm:hi, :] = jnp.where((i + 1) * tm < seq_len, un_ref[0], 0.0)
        a2_scr[lo:hi, :] = e_scr[lo:hi, 2 * gd:] + e_scr[lo - 1:hi - 1, 2 * gd:]

    def pool_double():
        a4_scr[lo:hi, :] = a2_scr[lo:hi, :] + a2_scr[lo - 2:hi - 2, :]
        a8_scr[lo:hi, :] = a4_scr[lo:hi, gd:] + a4_scr[lo - 4:hi - 4, gd:]

    def pool_group(g):
        w = POOL_WINDOWS[g]
        if g == 0:
            total = rows_at(e_scr, -1, g0) + rows_at(e_scr, 0, g0)
        elif g == 1:
            total = ((rows_at(e_scr, -2, g1) + rows_at(e_scr, -1, g1))
                     + (rows_at(e_scr, 0, g1) + rows_at(e_scr, 1, g1)))
        elif g == 2:
            total = rows_at(a4_scr, 3, g0) + rows_at(a4_scr, -1, g0)
        else:
            total = rows_at(a8_scr, 7, g0) + rows_at(a8_scr, -1, g0)
        pos = i * tm + lax.broadcasted_iota(jnp.int32, (tm, 1), 0)
        cnt = (jnp.minimum(pos + w // 2, seq_len) - jnp.maximum(pos - w // 2, 0)).astype(F32)
        dlt = total / cnt - u_ref[0, :, g * gd:(g + 1) * gd]
        cols = slice(g * gd, (g + 1) * gd)
        return (_dot(dlt.astype(BF16), pw_ref[g]) * ps_ref[:, cols]).astype(BF16)

    pooled = []
    pool_stages = [pool_fill, pool_double] + [functools.partial(pool_group, g)
                                               for g in range(len(POOL_WINDOWS))]

    p_next = probabilities(scores(0))
    for h in range(heads):
        p_cur = p_next
        if h + 1 < heads:
            p_next = probabilities(scores(h + 1))
        if h < len(pool_stages):
            out = pool_stages[h]()
            if out is not None:
                pooled.append(out)
        weighted_values(h, p_cur)

    pool_ref[0] = jnp.concatenate(pooled, axis=1)
    na_ref[0] = nat_scr[...].T.astype(BF16)


def _mixer(u, qrt, qpt, kr, vt, kc, vct, bias, pool_w_bf, pool_scale):
    b, l, pw = u.shape
    nw = kr.shape[-1]
    gd = pool_w_bf.shape[-1]
    tm = TILE_ROWS * GRID_W
    nt = l // tm
    hb = tm // POOL_HALO
    n_halo = l // POOL_HALO
    nc = kc.shape[1]
    assert pw == len(POOL_WINDOWS) * gd and POOL_HALO == max(POOL_WINDOWS) // 2

    tok = lambda bi, i: (bi, i, 0)
    chan = lambda bi, i: (bi, 0, i)
    const2 = lambda bi, i: (0, 0)
    first_block = lambda i: jnp.clip(i - 1, 0, nt - 3)

    def k_spec(j):
        return pl.BlockSpec((1, tm, nw), lambda bi, i: (bi, first_block(i) + j, 0))

    def vt_spec(j):
        return pl.BlockSpec((1, nw, tm), lambda bi, i: (bi, 0, first_block(i) + j))

    def bias_idx(bi, i):
        return (jnp.where(i == 0, 0, jnp.where(i == nt - 1, 2, 1)), 0, 0, 0)

    in_specs = [
        pl.BlockSpec((1, tm, pw), tok),
        pl.BlockSpec((1, POOL_HALO, pw), lambda bi, i: (bi, jnp.maximum(i * hb - 1, 0), 0)),
        pl.BlockSpec((1, POOL_HALO, pw), lambda bi, i: (bi, jnp.minimum((i + 1) * hb, n_halo - 1), 0)),
        pl.BlockSpec((1, nw, tm), chan),
        pl.BlockSpec((1, nw, tm), chan),
        k_spec(0), k_spec(1), k_spec(2),
        vt_spec(0), vt_spec(1), vt_spec(2),
        pl.BlockSpec((1, nc, nw), lambda bi, i: (bi, 0, 0)),
        pl.BlockSpec((1, nw, nc), lambda bi, i: (bi, 0, 0)),
        pl.BlockSpec((1,) + bias.shape[1:], bias_idx),
        pl.BlockSpec(pool_w_bf.shape, lambda bi, i: (0, 0, 0)),
        pl.BlockSpec((1, pw), const2),
    ]
    n_scr = SUBLANES + tm + 2 * POOL_HALO
    return pl.pallas_call(
        functools.partial(_mixer_kernel, seq_len=l),
        grid=(b, nt),
        in_specs=in_specs,
        out_specs=[pl.BlockSpec((1, tm, nw), tok), pl.BlockSpec((1, tm, pw), tok)],
        out_shape=[jax.ShapeDtypeStruct((b, l, nw), BF16), jax.ShapeDtypeStruct((b, l, pw), BF16)],
        scratch_shapes=[pltpu.VMEM((nw, tm), F32),
                        pltpu.VMEM((n_scr, pw), F32),
                        pltpu.VMEM((n_scr, 2 * gd), F32),
                        pltpu.VMEM((n_scr, 2 * gd), F32),
                        pltpu.VMEM((n_scr, gd), F32)],
        compiler_params=_params(("arbitrary", "arbitrary")),
        name="mixer",
    )(u, u, u, qrt, qpt, kr, kr, kr, vt, vt, vt, kc, vct, bias, pool_w_bf, pool_scale)


def _post_kernel(x_ref, mod_ref, g_ref, na_ref, pool_ref, gp_ref, gn_ref,
                 wbp_ref, wbn_ref, wo_ref, w1_ref, w2_ref, o_ref, *, chunk):
    merged = (gp_ref[0].astype(F32) * _dot(pool_ref[0], wbp_ref[...])
              + gn_ref[0].astype(F32) * _dot(na_ref[0], wbn_ref[...]))
    x = x_ref[0] + mod_ref[0, 2:3, :] * _dot(merged.astype(BF16), wo_ref[...])
    hb = _modulated_norm(x, mod_ref, g_ref, 3).astype(BF16)
    acc = jnp.zeros(x.shape, F32)
    for c0 in range(0, w1_ref.shape[1], chunk):
        a = jnp.maximum(_dot(hb, w1_ref[:, c0:c0 + chunk]), 0.0)
        acc = acc + _dot((a * a).astype(BF16), w2_ref[c0:c0 + chunk, :])
    o_ref[0] = x + mod_ref[0, 5:6, :] * acc


def _post(x, mod3, norm_g, na, pool, gp, gn, wbp, wbn, wo, w1, w2, tm):
    b, l, d = x.shape
    tok = lambda bi, i: (bi, i, 0)
    const = lambda bi, i: (0, 0)
    resident = lambda w: pl.BlockSpec(w.shape, const, pipeline_mode=pl.Buffered(1))
    return pl.pallas_call(
        functools.partial(_post_kernel, chunk=d),
        grid=(b, l // tm),
        in_specs=[pl.BlockSpec((1, tm, d), tok),
                  pl.BlockSpec((1, N_MOD, d), lambda bi, i: (bi, 0, 0)),
                  pl.BlockSpec((1, d), const),
                  pl.BlockSpec((1, tm, na.shape[-1]), tok),
                  pl.BlockSpec((1, tm, pool.shape[-1]), tok),
                  pl.BlockSpec((1, tm, d), tok),
                  pl.BlockSpec((1, tm, d), tok),
                  resident(wbp), resident(wbn), resident(wo), resident(w1), resident(w2)],
        out_specs=pl.BlockSpec((1, tm, d), tok),
        out_shape=jax.ShapeDtypeStruct((b, l, d), F32),
        compiler_params=_params(("arbitrary", "arbitrary")),
        name="post",
    )(x, mod3, norm_g, na, pool, gp, gn, wbp, wbn, wo, w1, w2)


def _rope_tables(seq_len):
    n_freq = HEAD_DIM // 4
    t = np.arange(seq_len)
    inv = (ROPE_THETA ** (-np.arange(n_freq, dtype=np.float32) / n_freq)).astype(np.float32)
    ang_row = (t // GRID_W).astype(np.float32)[:, None] * inv
    ang_col = (t % GRID_W).astype(np.float32)[:, None] * inv
    cr, sr, cc, sc = np.cos(ang_row), np.sin(ang_row), np.cos(ang_col), np.sin(ang_col)
    cos = np.concatenate([cr, cr, cc, cc], axis=1).astype(np.float32)
    sin = np.concatenate([-sr, sr, -sc, sc], axis=1).astype(np.float32)
    return (jnp.asarray(np.tile(cos, (1, 2))), jnp.asarray(np.tile(sin, (1, 2))),
            jnp.asarray(np.ascontiguousarray(cos.T)), jnp.asarray(np.ascontiguousarray(sin.T)))


def _bias_kernel(el_ref, er_ref, o_ref, *, rows):
    lanes = 2 * GRID_W
    kc = lax.broadcasted_iota(jnp.int32, (GRID_W, lanes), 0)
    lane = lax.broadcasted_iota(jnp.int32, (GRID_W, lanes), 1)
    qc = lane % GRID_W
    c0 = jnp.clip(qc - NA_COLS // 2, 0, GRID_W - NA_COLS)
    col_ok = (kc >= c0) & (kc < c0 + NA_COLS)
    left = lane < GRID_W
    masked = jnp.full((GRID_W, lanes), MASK_VALUE, F32)

    cache = {}

    def toeplitz(side, dr):
        if (side, dr) not in cache:
            ref = er_ref if side else el_ref
            dd = dr + NA_ROWS - 1
            vec = jnp.broadcast_to(ref[0, dd:dd + 1, :], (GRID_W, lanes))
            cache[(side, dr)] = pltpu.roll(vec, 0, axis=1, stride=1, stride_axis=0) * LOG2E
        return cache[(side, dr)]

    nt = rows // TILE_ROWS
    for t, tile in enumerate((0, nt // 2, nt - 1)):
        r = tile * TILE_ROWS
        ks = min(max(r - TILE_ROWS, 0), rows - KEY_ROWS)
        for j in range(KEY_ROWS):
            kr = ks + j
            for ip in range(TILE_ROWS // 2):
                halves = []
                for side in range(2):
                    qr = r + 2 * ip + side
                    r0 = min(max(qr - NA_ROWS // 2, 0), rows - NA_ROWS)
                    ok = r0 <= kr < r0 + NA_ROWS
                    halves.append(toeplitz(side, kr - qr) if ok else masked)
                blk = jnp.where(col_ok, jnp.where(left, halves[0], halves[1]), MASK_VALUE)
                o_ref[t, 0, j * GRID_W:(j + 1) * GRID_W, ip * lanes:(ip + 1) * lanes] = blk


def _bias_tables(rpb, rows):
    heads, nr, ncol = rpb.shape
    lanes = 2 * GRID_W
    half = NA_COLS - 1
    zeros = lambda n: jnp.zeros((heads, nr, n), F32)
    flipped = rpb[..., ::-1]
    e_left = jnp.concatenate([flipped[..., half:], zeros(lanes - ncol), flipped[..., :half]], axis=-1)
    e_right = jnp.concatenate([zeros(GRID_W - half), flipped, zeros(lanes - GRID_W + half - ncol)], axis=-1)
    pad = (-nr) % SUBLANES
    e_left = jnp.pad(e_left, ((0, 0), (0, pad), (0, 0)))
    e_right = jnp.pad(e_right, ((0, 0), (0, pad), (0, 0)))
    tq, tk = TILE_ROWS * GRID_W, KEY_ROWS * GRID_W
    vec_spec = pl.BlockSpec((1, nr + pad, lanes), lambda h: (h, 0, 0))
    return pl.pallas_call(
        functools.partial(_bias_kernel, rows=rows),
        grid=(heads,),
        in_specs=[vec_spec, vec_spec],
        out_specs=pl.BlockSpec((3, 1, tk, tq), lambda h: (0, h, 0, 0)),
        out_shape=jax.ShapeDtypeStruct((3, heads, tk, tq), F32),
        compiler_params=_params(("arbitrary",)),
        name="bias_tables",
    )(e_left, e_right)


def kernel(x, c, ctx, c_ctx, ada_w, ada_b, norm1_g, norm2_g, w_in, pool_w, pool_scale,
           q_norm_g, k_norm_g, rpb, w_branch_pool, w_branch_na, w_out, mlp_w1, mlp_w2):
    b, l, d = x.shape
    depth = ada_w.shape[0]
    pw = pool_scale.shape[-1]
    nw = w_branch_na.shape[1]
    heads = nw // HEAD_DIM
    rows = l // GRID_W
    assert depth == 1, "the context-stream update is only needed when another layer follows"
    assert l % (TILE_ROWS * GRID_W) == 0 and rows >= KEY_ROWS
    assert w_in.shape[-1] == pw + 3 * nw + 2 * d and pw == nw
    ctx_row = b
    tm_proj = 512

    cos_t, sin_t, cos_tt, sin_tt = _rope_tables(l)
    mean_mat = jnp.asarray(np.kron(np.eye(heads), np.full((HEAD_DIM, HEAD_DIM), 1.0 / HEAD_DIM)), BF16)
    pad = (-(b + 1)) % SUBLANES
    cc = jnp.concatenate([c, c_ctx[None], jnp.zeros((pad, d), F32)], axis=0)

    layer = 0
    mod = _modulation(cc, ada_w[layer], ada_b[layer])
    mod3 = mod.reshape(mod.shape[0], N_MOD, d)
    wl = w_in[layer]
    q0, k0, v0, g0 = pw, pw + nw, pw + 2 * nw, pw + 3 * nw
    w_all = wl.astype(BF16)
    w_qvt = _qv_weights(wl, q0 // nw, v0 // nw, nw)
    qg_tt = jnp.broadcast_to(q_norm_g[layer][:, None], (HEAD_DIM, tm_proj))
    kg_t = jnp.tile(k_norm_g[layer], heads)[None]
    n1 = norm1_g[layer][None]

    k_ctx, vt_ctx = _ctx_proj(ctx, mod3, n1, w_all, w_qvt, kg_t, mean_mat, k0 // nw, ctx_row)
    later_weights = (w_branch_pool[layer], w_branch_na[layer], w_out[layer],
                     mlp_w1[layer], mlp_w2[layer])
    (u, qrt, qpt, kr, vt, gp, gn), (wbp, wbn, wo, w1, w2) = _in_proj(
        x, mod3, n1, w_all, w_qvt, cos_t, sin_t, cos_tt, sin_tt, qg_tt, kg_t, mean_mat,
        (0, k0, g0, g0 + d), pw, tm_proj, later_weights)
    bias = _bias_tables(rpb[layer], rows)
    na, pool = _mixer(u, qrt, qpt, kr, vt, k_ctx, vt_ctx, bias,
                      pool_w[layer].astype(BF16), pool_scale[layer][None])
    return _post(x, mod3, norm2_g[layer][None], na, pool, gp, gn, wbp, wbn, wo, w1, w2, tm=tm_proj)
```

```python
import functools

import numpy as np
import jax
import jax.numpy as jnp
from jax import lax
from jax.experimental import pallas as pl
from jax.experimental.pallas import tpu as pltpu

GRID_W = 64
N_MOD = 6
POOL_WINDOWS = (2, 4, 8, 16)
HEAD_DIM = 64
NA_ROWS = 8
NA_COLS = 16
ROPE_THETA = 10000.0
EPS = 1e-6
MASK_VALUE = -1e30
LOG2E = 1.4426950408889634

VMEM_LIMIT_BYTES = 56 * 1024 * 1024
SUBLANES = 8
BF16_ROWS = 16
POOL_HALO = 8
TILE_ROWS = 4
KEY_ROWS = TILE_ROWS + NA_ROWS

BF16 = jnp.bfloat16
F32 = jnp.float32


def _dot(a, b):
    return jnp.dot(a, b, preferred_element_type=F32)


def _dot_nt(a, b):
    return lax.dot_general(a, b, (((1,), (1,)), ((), ())), preferred_element_type=F32)


def _params(sem):
    return pltpu.CompilerParams(dimension_semantics=sem, vmem_limit_bytes=VMEM_LIMIT_BYTES)


def _rms(x):
    return x * lax.rsqrt(jnp.mean(x * x, axis=-1, keepdims=True) + EPS)


def _sigmoid(z):
    return 0.5 * jnp.tanh(0.5 * z) + 0.5


def _mod_kernel(c_ref, w_ref, b_ref, o_ref):
    c = c_ref[...]
    s = (c * jax.nn.sigmoid(c)).astype(BF16)
    o_ref[...] = _dot(s, w_ref[...].astype(BF16)) + b_ref[...]


def _modulation(cc, ada_w, ada_b):
    m, d = cc.shape
    n = ada_w.shape[1]
    return pl.pallas_call(
        _mod_kernel,
        grid=(n // d,),
        in_specs=[pl.BlockSpec((m, d), lambda j: (0, 0)),
                  pl.BlockSpec((d, d), lambda j: (0, j)),
                  pl.BlockSpec((1, d), lambda j: (0, j))],
        out_specs=pl.BlockSpec((m, d), lambda j: (0, j)),
        out_shape=jax.ShapeDtypeStruct((m, n), F32),
        compiler_params=_params(("arbitrary",)),
        name="modulation",
    )(cc, ada_w, ada_b.reshape(1, n))


def _transpose_cast_kernel(w_ref, o_ref):
    o_ref[...] = w_ref[...].T.astype(BF16)


def _qv_weights(w, q_block, v_block, width):
    d = w.shape[0]
    return pl.pallas_call(
        _transpose_cast_kernel,
        grid=(2,),
        in_specs=[pl.BlockSpec((d, width), lambda s: (0, q_block + s * (v_block - q_block)))],
        out_specs=pl.BlockSpec((width, d), lambda s: (s, 0)),
        out_shape=jax.ShapeDtypeStruct((2 * width, d), BF16),
        compiler_params=_params(("arbitrary",)),
        name="qv_weights",
    )(w)


def _head_norm(a, mean_mat, g):
    ms = _dot((a * a).astype(BF16), mean_mat)
    return a * lax.rsqrt(ms + EPS) * g


def _rope(a, cos, sin_signed, first_half):
    width = a.shape[-1]
    quarter = HEAD_DIM // 4
    up = pltpu.roll(a, width - quarter, axis=1)
    down = pltpu.roll(a, quarter, axis=1)
    swapped = jnp.where(first_half, up, down)
    return a * cos + swapped * sin_signed


def _modulated_norm(x, mod_ref, g_ref, shift_row):
    return _rms(x) * g_ref[...] * (1.0 + mod_ref[0, shift_row + 1:shift_row + 2, :]) \
        + mod_ref[0, shift_row:shift_row + 1, :]


def _ctx_kernel(x_ref, mod_ref, g_ref, wk_ref, wvt_ref, kg_ref, mean_ref, k_ref, vt_ref):
    hb = _modulated_norm(x_ref[0], mod_ref, g_ref, 0).astype(BF16)
    k_ref[0] = _head_norm(_dot(hb, wk_ref[...]), mean_ref[...], kg_ref[...]).astype(BF16)
    vt_ref[0] = _dot_nt(wvt_ref[...], hb).astype(BF16)


def _ctx_proj(ctx, mod3, norm_g, w_all, w_qvt, kg_t, mean_mat, key_block, ctx_row):
    b, n, d = ctx.shape
    nw = kg_t.shape[-1]
    sd = jax.ShapeDtypeStruct
    return pl.pallas_call(
        _ctx_kernel,
        grid=(b,),
        in_specs=[pl.BlockSpec((1, n, d), lambda i: (i, 0, 0)),
                  pl.BlockSpec((1, N_MOD, d), lambda i: (ctx_row, 0, 0)),
                  pl.BlockSpec((1, d), lambda i: (0, 0)),
                  pl.BlockSpec((d, nw), lambda i: (0, key_block)),
                  pl.BlockSpec((nw, d), lambda i: (1, 0)),
                  pl.BlockSpec((1, nw), lambda i: (0, 0)),
                  pl.BlockSpec((nw, nw), lambda i: (0, 0))],
        out_specs=[pl.BlockSpec((1, n, nw), lambda i: (i, 0, 0)),
                   pl.BlockSpec((1, nw, n), lambda i: (i, 0, 0))],
        out_shape=[sd((b, n, nw), BF16), sd((b, nw, n), BF16)],
        compiler_params=_params(("arbitrary",)),
        name="ctx_proj",
    )(ctx, mod3, norm_g, w_all, w_qvt, kg_t, mean_mat)


def _in_proj_kernel(*refs, col_starts, n_cast):
    (x_ref, mod_ref, g_ref, w_ref, wt_ref, cos_ref, sin_ref, cost_ref, sint_ref,
     qgt_ref, kg_ref, mean_ref) = refs[:12]
    cast_in = refs[12:12 + n_cast]
    u_ref, qrt_ref, qpt_ref, kr_ref, vt_ref, gp_ref, gn_ref = refs[12 + n_cast:19 + n_cast]
    cast_out = refs[19 + n_cast:]
    for src, dst in zip(cast_in, cast_out):
        dst[...] = src[...].astype(BF16)

    pw = u_ref.shape[-1]
    nw = kr_ref.shape[-1]
    d = x_ref.shape[-1]
    heads = nw // HEAD_DIM
    quarter = HEAD_DIM // 4
    scale = HEAD_DIM ** -0.5 * LOG2E
    u0, k0, gp0, gn0 = col_starts

    hb = _modulated_norm(x_ref[0], mod_ref, g_ref, 0).astype(BF16)

    cost, sint, qgt = cost_ref[...], sint_ref[...], qgt_ref[...]

    def query_heads(first, last):
        qt = _dot_nt(wt_ref[first * HEAD_DIM:last * HEAD_DIM, :], hb)
        for h in range(first, last):
            rows = slice(h * HEAD_DIM, (h + 1) * HEAD_DIM)
            z = qt[(h - first) * HEAD_DIM:(h - first + 1) * HEAD_DIM, :]
            qn = z * lax.rsqrt(jnp.mean(z * z, axis=0, keepdims=True) + EPS) * (qgt * scale)
            swapped = jnp.concatenate([qn[quarter:2 * quarter], qn[0:quarter],
                                       qn[3 * quarter:], qn[2 * quarter:3 * quarter]], axis=0)
            qpt_ref[0, rows, :] = qn.astype(BF16)
            qrt_ref[0, rows, :] = (qn * cost + swapped * sint).astype(BF16)

    z_gp = _dot(hb, w_ref[:, gp0:gp0 + d])
    query_heads(0, heads // 2)
    z_gn = _dot(hb, w_ref[:, gn0:gn0 + d])
    gp_ref[0] = _sigmoid(z_gp).astype(BF16)
    query_heads(heads // 2, heads)
    z_k = _dot(hb, w_ref[:, k0:k0 + nw])
    gn_ref[0] = _sigmoid(z_gn).astype(BF16)
    z_vt = _dot_nt(wt_ref[nw:, :], hb)

    reps = nw // cos_ref.shape[-1]
    cos = jnp.concatenate([cos_ref[...]] * reps, axis=1)
    sin = jnp.concatenate([sin_ref[...]] * reps, axis=1)
    lane = lax.broadcasted_iota(jnp.int32, (1, nw), 1)
    first_half = (lane % (HEAD_DIM // 2)) < quarter
    k = _head_norm(z_k, mean_ref[...], kg_ref[...])
    kr_ref[0] = _rope(k, cos, sin, first_half).astype(BF16)
    vt_ref[0] = z_vt.astype(BF16)
    u_ref[0] = _dot(hb, w_ref[:, u0:u0 + pw])


def _in_proj(x, mod3, norm_g, w_all, w_qvt, cos_t, sin_t, cos_tt, sin_tt, qg_tt, kg_t, mean_mat,
             col_starts, pw, tm, later_weights):
    b, l, d = x.shape
    nw = kg_t.shape[-1]
    tw = cos_t.shape[-1]
    nt = l // tm
    tok = lambda bi, i: (bi, i, 0)
    chan = lambda bi, i: (bi, 0, i)
    const = lambda bi, i: (0, 0)
    sd = jax.ShapeDtypeStruct
    slab_specs = []
    for w in later_weights:
        assert w.shape[0] % (b * nt * BF16_ROWS) == 0
        slab_specs.append(pl.BlockSpec((w.shape[0] // (b * nt), w.shape[1]),
                                       lambda bi, i: (bi * nt + i, 0)))
    outs = pl.pallas_call(
        functools.partial(_in_proj_kernel, col_starts=col_starts, n_cast=len(later_weights)),
        grid=(b, nt),
        in_specs=[pl.BlockSpec((1, tm, d), tok),
                  pl.BlockSpec((1, N_MOD, d), lambda bi, i: (bi, 0, 0)),
                  pl.BlockSpec((1, d), const),
                  pl.BlockSpec(w_all.shape, const, pipeline_mode=pl.Buffered(1)),
                  pl.BlockSpec(w_qvt.shape, const, pipeline_mode=pl.Buffered(1)),
                  pl.BlockSpec((tm, tw), lambda bi, i: (i, 0)),
                  pl.BlockSpec((tm, tw), lambda bi, i: (i, 0)),
                  pl.BlockSpec((HEAD_DIM, tm), lambda bi, i: (0, i)),
                  pl.BlockSpec((HEAD_DIM, tm), lambda bi, i: (0, i)),
                  pl.BlockSpec((HEAD_DIM, tm), const),
                  pl.BlockSpec((1, nw), const),
                  pl.BlockSpec((nw, nw), const)] + slab_specs,
        out_specs=[pl.BlockSpec((1, tm, pw), tok),
                   pl.BlockSpec((1, nw, tm), chan), pl.BlockSpec((1, nw, tm), chan),
                   pl.BlockSpec((1, tm, nw), tok),
                   pl.BlockSpec((1, nw, tm), chan),
                   pl.BlockSpec((1, tm, d), tok), pl.BlockSpec((1, tm, d), tok)] + slab_specs,
        out_shape=[sd((b, l, pw), F32),
                   sd((b, nw, l), BF16), sd((b, nw, l), BF16),
                   sd((b, l, nw), BF16),
                   sd((b, nw, l), BF16),
                   sd((b, l, d), BF16), sd((b, l, d), BF16)]
                  + [sd(w.shape, BF16) for w in later_weights],
        compiler_params=_params(("arbitrary", "arbitrary")),
        name="in_proj",
    )(x, mod3, norm_g, w_all, w_qvt, cos_t, sin_t, cos_tt, sin_tt, qg_tt, kg_t, mean_mat,
      *later_weights)
    return outs[:7], outs[7:]


def _mixer_kernel(u_ref, up_ref, un_ref, qrt_ref, qpt_ref,
                  k0_ref, k1_ref, k2_ref, vt0_ref, vt1_ref, vt2_ref, kc_ref, vct_ref, bias_ref,
                  pw_ref, ps_ref,
                  na_ref, pool_ref, nat_scr, e_scr, a2_scr, a4_scr, a8_scr, *, seq_len):
    i = pl.program_id(1)
    tm = u_ref.shape[1]
    nw = qrt_ref.shape[1]
    heads = nw // HEAD_DIM
    pair = 2 * HEAD_DIM

    zeros = jnp.zeros((HEAD_DIM, tm), BF16)
    ones = jnp.ones((BF16_ROWS, tm), BF16)
    k_refs = (k0_ref, k1_ref, k2_ref)
    vt_refs = (vt0_ref, vt1_ref, vt2_ref)

    def head_rows(h):
        return slice(h * HEAD_DIM, (h + 1) * HEAD_DIM)

    def scores(h):
        rows = head_rows(h)
        lanes = slice((h // 2) * pair, (h // 2 + 1) * pair)
        q_rot, q_plain = qrt_ref[0, rows, :], qpt_ref[0, rows, :]
        if h % 2 == 0:
            w_rot = jnp.concatenate([q_rot, zeros], axis=0)
            w_plain = jnp.concatenate([q_plain, zeros], axis=0)
        else:
            w_rot = jnp.concatenate([zeros, q_rot], axis=0)
            w_plain = jnp.concatenate([zeros, q_plain], axis=0)
        s = [_dot(k_refs[j][0, :, lanes], w_rot) + bias_ref[0, h, j * tm:(j + 1) * tm, :]
             for j in range(3)]
        s.append(_dot(kc_ref[0, :, lanes], w_plain))
        m = s[0].max(axis=0, keepdims=True)
        for sj in s[1:]:
            m = jnp.maximum(m, sj.max(axis=0, keepdims=True))
        return s, m

    def probabilities(s_and_m):
        s, m = s_and_m
        return [jnp.exp2(sj - m).astype(BF16) for sj in s]

    def weighted_values(h, p):
        rows = head_rows(h)
        ot = _dot(jnp.concatenate([vct_ref[0, rows, :], ones], axis=0), p[3])
        for j in range(3):
            ot = ot + _dot(jnp.concatenate([vt_refs[j][0, rows, :], ones], axis=0), p[j])
        nat_scr[rows, :] = ot[:HEAD_DIM] / ot[HEAD_DIM:HEAD_DIM + 1]

    gd = pw_ref.shape[-1]
    hi = tm + 2 * POOL_HALO
    first = POOL_HALO
    g0 = slice(0, gd)

    def rows_at(ref, off, cols):
        return ref[first + off:first + off + tm, cols]

    def pool_fill():
        for scr in (e_scr, a2_scr, a4_scr):
            scr[hi:hi + SUBLANES, :] = jnp.zeros((SUBLANES, scr.shape[1]), F32)
        e_scr[0:first, :] = jnp.where(i > 0, up_ref[0], 0.0)
        e_scr[first:first + tm, :] = u_ref[0]
        e_scr[first + tm:hi, :] = jnp.where((i + 1) * tm < seq_len, un_ref[0], 0.0)
        a2_scr[0:hi, :] = e_scr[0:hi, gd:] + e_scr[1:hi + 1, gd:]

    def pool_double():
        a4_scr[0:hi, :] = a2_scr[0:hi, gd:] + a2_scr[2:hi + 2, gd:]
        a8_scr[0:hi, :] = a4_scr[0:hi, gd:] + a4_scr[4:hi + 4, gd:]

    def pool_group(g):
        w = POOL_WINDOWS[g]
        if g == 0:
            total = rows_at(e_scr, -1, g0) + rows_at(e_scr, 0, g0)
        elif g == 1:
            total = rows_at(a2_scr, -2, g0) + rows_at(a2_scr, 0, g0)
        elif g == 2:
            total = rows_at(a4_scr, -4, g0) + rows_at(a4_scr, 0, g0)
        else:
            total = rows_at(a8_scr, -8, g0) + rows_at(a8_scr, 0, g0)
        pos = i * tm + lax.broadcasted_iota(jnp.int32, (tm, 1), 0)
        cnt = (jnp.minimum(pos + w // 2, seq_len) - jnp.maximum(pos - w // 2, 0)).astype(F32)
        dlt = total / cnt - u_ref[0, :, g * gd:(g + 1) * gd]
        cols = slice(g * gd, (g + 1) * gd)
        return (_dot(dlt.astype(BF16), pw_ref[g]) * ps_ref[:, cols]).astype(BF16)

    pooled = []
    pool_stages = [pool_fill, pool_double] + [functools.partial(pool_group, g)
                                               for g in range(len(POOL_WINDOWS))]

    p_next = probabilities(scores(0))
    for h in range(heads):
        p_cur = p_next
        if h + 1 < heads:
            p_next = probabilities(scores(h + 1))
        if h < len(pool_stages):
            out = pool_stages[h]()
            if out is not None:
                pooled.append(out)
        weighted_values(h, p_cur)

    pool_ref[0] = jnp.concatenate(pooled, axis=1)
    na_ref[0] = nat_scr[...].T.astype(BF16)


def _mixer(u, qrt, qpt, kr, vt, kc, vct, bias, pool_w_bf, pool_scale):
    b, l, pw = u.shape
    nw = kr.shape[-1]
    gd = pool_w_bf.shape[-1]
    tm = TILE_ROWS * GRID_W
    nt = l // tm
    hb = tm // POOL_HALO
    n_halo = l // POOL_HALO
    nc = kc.shape[1]
    assert pw == len(POOL_WINDOWS) * gd and POOL_HALO == max(POOL_WINDOWS) // 2

    tok = lambda bi, i: (bi, i, 0)
    chan = lambda bi, i: (bi, 0, i)
    const2 = lambda bi, i: (0, 0)
    first_block = lambda i: jnp.clip(i - 1, 0, nt - 3)

    def k_spec(j):
        return pl.BlockSpec((1, tm, nw), lambda bi, i: (bi, first_block(i) + j, 0))

    def vt_spec(j):
        return pl.BlockSpec((1, nw, tm), lambda bi, i: (bi, 0, first_block(i) + j))

    def bias_idx(bi, i):
        return (jnp.where(i == 0, 0, jnp.where(i == nt - 1, 2, 1)), 0, 0, 0)

    in_specs = [
        pl.BlockSpec((1, tm, pw), tok),
        pl.BlockSpec((1, POOL_HALO, pw), lambda bi, i: (bi, jnp.maximum(i * hb - 1, 0), 0)),
        pl.BlockSpec((1, POOL_HALO, pw), lambda bi, i: (bi, jnp.minimum((i + 1) * hb, n_halo - 1), 0)),
        pl.BlockSpec((1, nw, tm), chan),
        pl.BlockSpec((1, nw, tm), chan),
        k_spec(0), k_spec(1), k_spec(2),
        vt_spec(0), vt_spec(1), vt_spec(2),
        pl.BlockSpec((1, nc, nw), lambda bi, i: (bi, 0, 0)),
        pl.BlockSpec((1, nw, nc), lambda bi, i: (bi, 0, 0)),
        pl.BlockSpec((1,) + bias.shape[1:], bias_idx),
        pl.BlockSpec(pool_w_bf.shape, lambda bi, i: (0, 0, 0)),
        pl.BlockSpec((1, pw), const2),
    ]
    n_scr = tm + 2 * POOL_HALO + SUBLANES
    return pl.pallas_call(
        functools.partial(_mixer_kernel, seq_len=l),
        grid=(b, nt),
        in_specs=in_specs,
        out_specs=[pl.BlockSpec((1, tm, nw), tok), pl.BlockSpec((1, tm, pw), tok)],
        out_shape=[jax.ShapeDtypeStruct((b, l, nw), BF16), jax.ShapeDtypeStruct((b, l, pw), BF16)],
        scratch_shapes=[pltpu.VMEM((nw, tm), F32),
                        pltpu.VMEM((n_scr, pw), F32),
                        pltpu.VMEM((n_scr, pw - gd), F32),
                        pltpu.VMEM((n_scr, pw - 2 * gd), F32),
                        pltpu.VMEM((n_scr, pw - 3 * gd), F32)],
        compiler_params=_params(("arbitrary", "arbitrary")),
        name="mixer",
    )(u, u, u, qrt, qpt, kr, kr, kr, vt, vt, vt, kc, vct, bias, pool_w_bf, pool_scale)


def _post_kernel(x_ref, mod_ref, g_ref, na_ref, pool_ref, gp_ref, gn_ref,
                 wbp_ref, wbn_ref, wo_ref, w1_ref, w2_ref, o_ref, *, chunk):
    merged = (gp_ref[0].astype(F32) * _dot(pool_ref[0], wbp_ref[...])
              + gn_ref[0].astype(F32) * _dot(na_ref[0], wbn_ref[...]))
    x = x_ref[0] + mod_ref[0, 2:3, :] * _dot(merged.astype(BF16), wo_ref[...])
    hb = _modulated_norm(x, mod_ref, g_ref, 3).astype(BF16)
    acc = jnp.zeros(x.shape, F32)
    for c0 in range(0, w1_ref.shape[1], chunk):
        a = jnp.maximum(_dot(hb, w1_ref[:, c0:c0 + chunk]), 0.0)
        acc = acc + _dot((a * a).astype(BF16), w2_ref[c0:c0 + chunk, :])
    o_ref[0] = x + mod_ref[0, 5:6, :] * acc


def _post(x, mod3, norm_g, na, pool, gp, gn, wbp, wbn, wo, w1, w2, tm):
    b, l, d = x.shape
    tok = lambda bi, i: (bi, i, 0)
    const = lambda bi, i: (0, 0)
    resident = lambda w: pl.BlockSpec(w.shape, const, pipeline_mode=pl.Buffered(1))
    return pl.pallas_call(
        functools.partial(_post_kernel, chunk=d),
        grid=(b, l // tm),
        in_specs=[pl.BlockSpec((1, tm, d), tok),
                  pl.BlockSpec((1, N_MOD, d), lambda bi, i: (bi, 0, 0)),
                  pl.BlockSpec((1, d), const),
                  pl.BlockSpec((1, tm, na.shape[-1]), tok),
                  pl.BlockSpec((1, tm, pool.shape[-1]), tok),
                  pl.BlockSpec((1, tm, d), tok),
                  pl.BlockSpec((1, tm, d), tok),
                  resident(wbp), resident(wbn), resident(wo), resident(w1), resident(w2)],
        out_specs=pl.BlockSpec((1, tm, d), tok),
        out_shape=jax.ShapeDtypeStruct((b, l, d), F32),
        compiler_params=_params(("arbitrary", "arbitrary")),
        name="post",
    )(x, mod3, norm_g, na, pool, gp, gn, wbp, wbn, wo, w1, w2)


def _rope_tables(seq_len):
    n_freq = HEAD_DIM // 4
    t = np.arange(seq_len)
    inv = (ROPE_THETA ** (-np.arange(n_freq, dtype=np.float32) / n_freq)).astype(np.float32)
    ang_row = (t // GRID_W).astype(np.float32)[:, None] * inv
    ang_col = (t % GRID_W).astype(np.float32)[:, None] * inv
    cr, sr, cc, sc = np.cos(ang_row), np.sin(ang_row), np.cos(ang_col), np.sin(ang_col)
    cos = np.concatenate([cr, cr, cc, cc], axis=1).astype(np.float32)
    sin = np.concatenate([-sr, sr, -sc, sc], axis=1).astype(np.float32)
    return (jnp.asarray(np.tile(cos, (1, 2))), jnp.asarray(np.tile(sin, (1, 2))),
            jnp.asarray(np.ascontiguousarray(cos.T)), jnp.asarray(np.ascontiguousarray(sin.T)))


def _bias_kernel(el_ref, er_ref, o_ref, *, rows):
    lanes = 2 * GRID_W
    kc = lax.broadcasted_iota(jnp.int32, (GRID_W, lanes), 0)
    lane = lax.broadcasted_iota(jnp.int32, (GRID_W, lanes), 1)
    qc = lane % GRID_W
    c0 = jnp.clip(qc - NA_COLS // 2, 0, GRID_W - NA_COLS)
    col_ok = (kc >= c0) & (kc < c0 + NA_COLS)
    left = lane < GRID_W
    masked = jnp.full((GRID_W, lanes), MASK_VALUE, F32)

    cache = {}

    def toeplitz(side, dr):
        if (side, dr) not in cache:
            ref = er_ref if side else el_ref
            dd = dr + NA_ROWS - 1
            vec = jnp.broadcast_to(ref[0, dd:dd + 1, :], (GRID_W, lanes))
            cache[(side, dr)] = pltpu.roll(vec, 0, axis=1, stride=1, stride_axis=0) * LOG2E
        return cache[(side, dr)]

    nt = rows // TILE_ROWS
    for t, tile in enumerate((0, nt // 2, nt - 1)):
        r = tile * TILE_ROWS
        ks = min(max(r - TILE_ROWS, 0), rows - KEY_ROWS)
        for j in range(KEY_ROWS):
            kr = ks + j
            for ip in range(TILE_ROWS // 2):
                halves = []
                for side in range(2):
                    qr = r + 2 * ip + side
                    r0 = min(max(qr - NA_ROWS // 2, 0), rows - NA_ROWS)
                    ok = r0 <= kr < r0 + NA_ROWS
                    halves.append(toeplitz(side, kr - qr) if ok else masked)
                blk = jnp.where(col_ok, jnp.where(left, halves[0], halves[1]), MASK_VALUE)
                o_ref[t, 0, j * GRID_W:(j + 1) * GRID_W, ip * lanes:(ip + 1) * lanes] = blk


def _bias_tables(rpb, rows):
    heads, nr, ncol = rpb.shape
    lanes = 2 * GRID_W
    half = NA_COLS - 1
    zeros = lambda n: jnp.zeros((heads, nr, n), F32)
    flipped = rpb[..., ::-1]
    e_left = jnp.concatenate([flipped[..., half:], zeros(lanes - ncol), flipped[..., :half]], axis=-1)
    e_right = jnp.concatenate([zeros(GRID_W - half), flipped, zeros(lanes - GRID_W + half - ncol)], axis=-1)
    pad = (-nr) % SUBLANES
    e_left = jnp.pad(e_left, ((0, 0), (0, pad), (0, 0)))
    e_right = jnp.pad(e_right, ((0, 0), (0, pad), (0, 0)))
    tq, tk = TILE_ROWS * GRID_W, KEY_ROWS * GRID_W
    vec_spec = pl.BlockSpec((1, nr + pad, lanes), lambda h: (h, 0, 0))
    return pl.pallas_call(
        functools.partial(_bias_kernel, rows=rows),
        grid=(heads,),
        in_specs=[vec_spec, vec_spec],
        out_specs=pl.BlockSpec((3, 1, tk, tq), lambda h: (0, h, 0, 0)),
        out_shape=jax.ShapeDtypeStruct((3, heads, tk, tq), F32),
        compiler_params=_params(("arbitrary",)),
        name="bias_tables",
    )(e_left, e_right)


def kernel(x, c, ctx, c_ctx, ada_w, ada_b, norm1_g, norm2_g, w_in, pool_w, pool_scale,
           q_norm_g, k_norm_g, rpb, w_branch_pool, w_branch_na, w_out, mlp_w1, mlp_w2):
    b, l, d = x.shape
    depth = ada_w.shape[0]
    pw = pool_scale.shape[-1]
    nw = w_branch_na.shape[1]
    heads = nw // HEAD_DIM
    rows = l // GRID_W
    assert depth == 1, "the context-stream update is only needed when another layer follows"
    assert l % (TILE_ROWS * GRID_W) == 0 and rows >= KEY_ROWS
    assert w_in.shape[-1] == pw + 3 * nw + 2 * d and pw == nw
    ctx_row = b
    tm_proj = 512

    cos_t, sin_t, cos_tt, sin_tt = _rope_tables(l)
    mean_mat = jnp.asarray(np.kron(np.eye(heads), np.full((HEAD_DIM, HEAD_DIM), 1.0 / HEAD_DIM)), BF16)
    pad = (-(b + 1)) % SUBLANES
    cc = jnp.concatenate([c, c_ctx[None], jnp.zeros((pad, d), F32)], axis=0)

    layer = 0
    mod = _modulation(cc, ada_w[layer], ada_b[layer])
    mod3 = mod.reshape(mod.shape[0], N_MOD, d)
    wl = w_in[layer]
    q0, k0, v0, g0 = pw, pw + nw, pw + 2 * nw, pw + 3 * nw
    w_all = wl.astype(BF16)
    w_qvt = _qv_weights(wl, q0 // nw, v0 // nw, nw)
    qg_tt = jnp.broadcast_to(q_norm_g[layer][:, None], (HEAD_DIM, tm_proj))
    kg_t = jnp.tile(k_norm_g[layer], heads)[None]
    n1 = norm1_g[layer][None]

    k_ctx, vt_ctx = _ctx_proj(ctx, mod3, n1, w_all, w_qvt, kg_t, mean_mat, k0 // nw, ctx_row)
    later_weights = (w_branch_pool[layer], w_branch_na[layer], w_out[layer],
                     mlp_w1[layer], mlp_w2[layer])
    (u, qrt, qpt, kr, vt, gp, gn), (wbp, wbn, wo, w1, w2) = _in_proj(
        x, mod3, n1, w_all, w_qvt, cos_t, sin_t, cos_tt, sin_tt, qg_tt, kg_t, mean_mat,
        (0, k0, g0, g0 + d), pw, tm_proj, later_weights)
    bias = _bias_tables(rpb[layer], rows)
    na, pool = _mixer(u, qrt, qpt, kr, vt, k_ctx, vt_ctx, bias,
                      pool_w[layer].astype(BF16), pool_scale[layer][None])
    return _post(x, mod3, norm2_g[layer][None], na, pool, gp, gn, wbp, wbn, wo, w1, w2, tm=tm_proj)
```

```python
import functools

import numpy as np
import jax
import jax.numpy as jnp
from jax import lax
from jax.experimental import pallas as pl
from jax.experimental.pallas import tpu as pltpu

GRID_W = 64
N_MOD = 6
POOL_WINDOWS = (2, 4, 8, 16)
HEAD_DIM = 64
NA_ROWS = 8
NA_COLS = 16
ROPE_THETA = 10000.0
EPS = 1e-6
MASK_VALUE = -1e30
LOG2E = 1.4426950408889634

VMEM_LIMIT_BYTES = 56 * 1024 * 1024
SUBLANES = 8
BF16_ROWS = 16
POOL_HALO = 8
TILE_ROWS = 4
KEY_ROWS = TILE_ROWS + NA_ROWS

BF16 = jnp.bfloat16
F32 = jnp.float32


def _dot(a, b):
    return jnp.dot(a, b, preferred_element_type=F32)


def _dot_nt(a, b):
    return lax.dot_general(a, b, (((1,), (1,)), ((), ())), preferred_element_type=F32)


def _params(sem):
    return pltpu.CompilerParams(dimension_semantics=sem, vmem_limit_bytes=VMEM_LIMIT_BYTES)


def _rms(x):
    return x * lax.rsqrt(jnp.mean(x * x, axis=-1, keepdims=True) + EPS)


def _sigmoid_bf16(z):
    zb = z.astype(BF16)
    return 0.5 * jnp.tanh(0.5 * zb) + 0.5


def _mod_kernel(c_ref, w_ref, b_ref, o_ref):
    c = c_ref[...]
    s = (c * jax.nn.sigmoid(c)).astype(BF16)
    o_ref[...] = _dot(s, w_ref[...].astype(BF16)) + b_ref[...]


def _modulation(cc, ada_w, ada_b):
    m, d = cc.shape
    n = ada_w.shape[1]
    return pl.pallas_call(
        _mod_kernel,
        grid=(n // d,),
        in_specs=[pl.BlockSpec((m, d), lambda j: (0, 0)),
                  pl.BlockSpec((d, d), lambda j: (0, j)),
                  pl.BlockSpec((1, d), lambda j: (0, j))],
        out_specs=pl.BlockSpec((m, d), lambda j: (0, j)),
        out_shape=jax.ShapeDtypeStruct((m, n), F32),
        compiler_params=_params(("arbitrary",)),
        name="modulation",
    )(cc, ada_w, ada_b.reshape(1, n))


def _transpose_cast_kernel(w_ref, o_ref):
    o_ref[...] = w_ref[...].T.astype(BF16)


def _qv_weights(w, q_block, v_block, width):
    d = w.shape[0]
    return pl.pallas_call(
        _transpose_cast_kernel,
        grid=(2,),
        in_specs=[pl.BlockSpec((d, width), lambda s: (0, q_block + s * (v_block - q_block)))],
        out_specs=pl.BlockSpec((width, d), lambda s: (s, 0)),
        out_shape=jax.ShapeDtypeStruct((2 * width, d), BF16),
        compiler_params=_params(("arbitrary",)),
        name="qv_weights",
    )(w)


def _head_norm(a, mean_mat, g):
    ms = _dot((a * a).astype(BF16), mean_mat)
    return a * lax.rsqrt(ms + EPS) * g


def _rope(a, cos, sin_signed, first_half):
    width = a.shape[-1]
    quarter = HEAD_DIM // 4
    up = pltpu.roll(a, width - quarter, axis=1)
    down = pltpu.roll(a, quarter, axis=1)
    swapped = jnp.where(first_half, up, down)
    return a * cos + swapped * sin_signed


def _modulated_norm(x, mod_ref, g_ref, shift_row):
    return _rms(x) * g_ref[...] * (1.0 + mod_ref[0, shift_row + 1:shift_row + 2, :]) \
        + mod_ref[0, shift_row:shift_row + 1, :]


def _ctx_kernel(x_ref, mod_ref, g_ref, wk_ref, wvt_ref, kg_ref, mean_ref, k_ref, vt_ref):
    hb = _modulated_norm(x_ref[0], mod_ref, g_ref, 0).astype(BF16)
    k_ref[0] = _head_norm(_dot(hb, wk_ref[...]), mean_ref[...], kg_ref[...]).astype(BF16)
    vt_ref[0] = _dot_nt(wvt_ref[...], hb).astype(BF16)


def _ctx_proj(ctx, mod3, norm_g, w_all, w_qvt, kg_t, mean_mat, key_block, ctx_row):
    b, n, d = ctx.shape
    nw = kg_t.shape[-1]
    sd = jax.ShapeDtypeStruct
    return pl.pallas_call(
        _ctx_kernel,
        grid=(b,),
        in_specs=[pl.BlockSpec((1, n, d), lambda i: (i, 0, 0)),
                  pl.BlockSpec((1, N_MOD, d), lambda i: (ctx_row, 0, 0)),
                  pl.BlockSpec((1, d), lambda i: (0, 0)),
                  pl.BlockSpec((d, nw), lambda i: (0, key_block)),
                  pl.BlockSpec((nw, d), lambda i: (1, 0)),
                  pl.BlockSpec((1, nw), lambda i: (0, 0)),
                  pl.BlockSpec((nw, nw), lambda i: (0, 0))],
        out_specs=[pl.BlockSpec((1, n, nw), lambda i: (i, 0, 0)),
                   pl.BlockSpec((1, nw, n), lambda i: (i, 0, 0))],
        out_shape=[sd((b, n, nw), BF16), sd((b, nw, n), BF16)],
        compiler_params=_params(("arbitrary",)),
        name="ctx_proj",
    )(ctx, mod3, norm_g, w_all, w_qvt, kg_t, mean_mat)


def _in_proj_kernel(*refs, col_starts, n_cast):
    (x_ref, mod_ref, g_ref, w_ref, wt_ref, cos_ref, sin_ref, cost_ref, sint_ref,
     qgt_ref, kg_ref, mean_ref) = refs[:12]
    cast_in = refs[12:12 + n_cast]
    u_ref, qrt_ref, qpt_ref, kr_ref, vt_ref, gp_ref, gn_ref = refs[12 + n_cast:19 + n_cast]
    cast_out = refs[19 + n_cast:]
    for src, dst in zip(cast_in, cast_out):
        dst[...] = src[...].astype(BF16)

    pw = u_ref.shape[-1]
    nw = kr_ref.shape[-1]
    d = x_ref.shape[-1]
    heads = nw // HEAD_DIM
    quarter = HEAD_DIM // 4
    scale = HEAD_DIM ** -0.5 * LOG2E
    u0, k0, gp0, gn0 = col_starts

    hb = _modulated_norm(x_ref[0], mod_ref, g_ref, 0).astype(BF16)

    cost, sint, qgt = cost_ref[...], sint_ref[...], qgt_ref[...]

    def query_heads(first, last):
        qt = _dot_nt(wt_ref[first * HEAD_DIM:last * HEAD_DIM, :], hb)
        for h in range(first, last):
            rows = slice(h * HEAD_DIM, (h + 1) * HEAD_DIM)
            z = qt[(h - first) * HEAD_DIM:(h - first + 1) * HEAD_DIM, :]
            qn = z * lax.rsqrt(jnp.mean(z * z, axis=0, keepdims=True) + EPS) * (qgt * scale)
            swapped = jnp.concatenate([qn[quarter:2 * quarter], qn[0:quarter],
                                       qn[3 * quarter:], qn[2 * quarter:3 * quarter]], axis=0)
            qpt_ref[0, rows, :] = qn.astype(BF16)
            qrt_ref[0, rows, :] = (qn * cost + swapped * sint).astype(BF16)

    z_gp = _dot(hb, w_ref[:, gp0:gp0 + d])
    query_heads(0, heads // 2)
    z_gn = _dot(hb, w_ref[:, gn0:gn0 + d])
    gp_ref[0] = _sigmoid_bf16(z_gp)
    query_heads(heads // 2, heads)
    z_k = _dot(hb, w_ref[:, k0:k0 + nw])
    gn_ref[0] = _sigmoid_bf16(z_gn)
    z_vt = _dot_nt(wt_ref[nw:, :], hb)

    reps = nw // cos_ref.shape[-1]
    cos = jnp.concatenate([cos_ref[...]] * reps, axis=1)
    sin = jnp.concatenate([sin_ref[...]] * reps, axis=1)
    lane = lax.broadcasted_iota(jnp.int32, (1, nw), 1)
    first_half = (lane % (HEAD_DIM // 2)) < quarter
    k = _head_norm(z_k, mean_ref[...], kg_ref[...])
    kr_ref[0] = _rope(k, cos, sin, first_half).astype(BF16)
    vt_ref[0] = z_vt.astype(BF16)
    u_ref[0] = _dot(hb, w_ref[:, u0:u0 + pw])


def _in_proj(x, mod3, norm_g, w_all, w_qvt, cos_t, sin_t, cos_tt, sin_tt, qg_tt, kg_t, mean_mat,
             col_starts, pw, tm, later_weights):
    b, l, d = x.shape
    nw = kg_t.shape[-1]
    tw = cos_t.shape[-1]
    nt = l // tm
    tok = lambda bi, i: (bi, i, 0)
    chan = lambda bi, i: (bi, 0, i)
    const = lambda bi, i: (0, 0)
    sd = jax.ShapeDtypeStruct
    slab_specs = []
    for w in later_weights:
        assert w.shape[0] % (b * nt * BF16_ROWS) == 0
        slab_specs.append(pl.BlockSpec((w.shape[0] // (b * nt), w.shape[1]),
                                       lambda bi, i: (bi * nt + i, 0)))
    outs = pl.pallas_call(
        functools.partial(_in_proj_kernel, col_starts=col_starts, n_cast=len(later_weights)),
        grid=(b, nt),
        in_specs=[pl.BlockSpec((1, tm, d), tok),
                  pl.BlockSpec((1, N_MOD, d), lambda bi, i: (bi, 0, 0)),
                  pl.BlockSpec((1, d), const),
                  pl.BlockSpec(w_all.shape, const, pipeline_mode=pl.Buffered(1)),
                  pl.BlockSpec(w_qvt.shape, const, pipeline_mode=pl.Buffered(1)),
                  pl.BlockSpec((tm, tw), lambda bi, i: (i, 0)),
                  pl.BlockSpec((tm, tw), lambda bi, i: (i, 0)),
                  pl.BlockSpec((HEAD_DIM, tm), lambda bi, i: (0, i)),
                  pl.BlockSpec((HEAD_DIM, tm), lambda bi, i: (0, i)),
                  pl.BlockSpec((HEAD_DIM, tm), const),
                  pl.BlockSpec((1, nw), const),
                  pl.BlockSpec((nw, nw), const)] + slab_specs,
        out_specs=[pl.BlockSpec((1, tm, pw), tok),
                   pl.BlockSpec((1, nw, tm), chan), pl.BlockSpec((1, nw, tm), chan),
                   pl.BlockSpec((1, tm, nw), tok),
                   pl.BlockSpec((1, nw, tm), chan),
                   pl.BlockSpec((1, tm, d), tok), pl.BlockSpec((1, tm, d), tok)] + slab_specs,
        out_shape=[sd((b, l, pw), F32),
                   sd((b, nw, l), BF16), sd((b, nw, l), BF16),
                   sd((b, l, nw), BF16),
                   sd((b, nw, l), BF16),
                   sd((b, l, d), BF16), sd((b, l, d), BF16)]
                  + [sd(w.shape, BF16) for w in later_weights],
        compiler_params=_params(("arbitrary", "arbitrary")),
        name="in_proj",
    )(x, mod3, norm_g, w_all, w_qvt, cos_t, sin_t, cos_tt, sin_tt, qg_tt, kg_t, mean_mat,
      *later_weights)
    return outs[:7], outs[7:]


def _mixer_kernel(u_ref, up_ref, un_ref, qrt_ref, qpt_ref,
                  k0_ref, k1_ref, k2_ref, vt0_ref, vt1_ref, vt2_ref, kc_ref, vct_ref, bias_ref,
                  pw_ref, ps_ref,
                  na_ref, pool_ref, nat_scr, e_scr, a2_scr, a4_scr, a8_scr, *, seq_len):
    i = pl.program_id(1)
    tm = u_ref.shape[1]
    nw = qrt_ref.shape[1]
    heads = nw // HEAD_DIM
    pair = 2 * HEAD_DIM

    zeros = jnp.zeros((HEAD_DIM, tm), BF16)
    ones = jnp.ones((BF16_ROWS, tm), BF16)
    k_refs = (k0_ref, k1_ref, k2_ref)
    vt_refs = (vt0_ref, vt1_ref, vt2_ref)

    def head_rows(h):
        return slice(h * HEAD_DIM, (h + 1) * HEAD_DIM)

    def scores(h):
        rows = head_rows(h)
        lanes = slice((h // 2) * pair, (h // 2 + 1) * pair)
        q_rot, q_plain = qrt_ref[0, rows, :], qpt_ref[0, rows, :]
        if h % 2 == 0:
            w_rot = jnp.concatenate([q_rot, zeros], axis=0)
            w_plain = jnp.concatenate([q_plain, zeros], axis=0)
        else:
            w_rot = jnp.concatenate([zeros, q_rot], axis=0)
            w_plain = jnp.concatenate([zeros, q_plain], axis=0)
        s = [_dot(k_refs[j][0, :, lanes], w_rot) + bias_ref[0, h, j * tm:(j + 1) * tm, :]
             for j in range(3)]
        s.append(_dot(kc_ref[0, :, lanes], w_plain))
        m = s[0].max(axis=0, keepdims=True)
        for sj in s[1:]:
            m = jnp.maximum(m, sj.max(axis=0, keepdims=True))
        return s, m

    def probabilities(s_and_m):
        s, m = s_and_m
        return [jnp.exp2((sj - m).astype(BF16)) for sj in s]

    def weighted_values(h, p):
        rows = head_rows(h)
        ot = _dot(jnp.concatenate([vct_ref[0, rows, :], ones], axis=0), p[3])
        for j in range(3):
            ot = ot + _dot(jnp.concatenate([vt_refs[j][0, rows, :], ones], axis=0), p[j])
        nat_scr[rows, :] = ot[:HEAD_DIM] / ot[HEAD_DIM:HEAD_DIM + 1]

    gd = pw_ref.shape[-1]
    hi = tm + 2 * POOL_HALO
    first = POOL_HALO
    g0 = slice(0, gd)

    def rows_at(ref, off, cols):
        return ref[first + off:first + off + tm, cols]

    def pool_fill():
        for scr in (e_scr, a2_scr, a4_scr):
            scr[hi:hi + SUBLANES, :] = jnp.zeros((SUBLANES, scr.shape[1]), F32)
        e_scr[0:first, :] = jnp.where(i > 0, up_ref[0], 0.0)
        e_scr[first:first + tm, :] = u_ref[0]
        e_scr[first + tm:hi, :] = jnp.where((i + 1) * tm < seq_len, un_ref[0], 0.0)
        a2_scr[0:hi, :] = e_scr[0:hi, gd:] + e_scr[1:hi + 1, gd:]

    def pool_double():
        a4_scr[0:hi, :] = a2_scr[0:hi, gd:] + a2_scr[2:hi + 2, gd:]
        a8_scr[0:hi, :] = a4_scr[0:hi, gd:] + a4_scr[4:hi + 4, gd:]

    def pool_group(g):
        w = POOL_WINDOWS[g]
        if g == 0:
            total = rows_at(e_scr, -1, g0) + rows_at(e_scr, 0, g0)
        elif g == 1:
            total = rows_at(a2_scr, -2, g0) + rows_at(a2_scr, 0, g0)
        elif g == 2:
            total = rows_at(a4_scr, -4, g0) + rows_at(a4_scr, 0, g0)
        else:
            total = rows_at(a8_scr, -8, g0) + rows_at(a8_scr, 0, g0)
        pos = i * tm + lax.broadcasted_iota(jnp.int32, (tm, 1), 0)
        cnt = (jnp.minimum(pos + w // 2, seq_len) - jnp.maximum(pos - w // 2, 0)).astype(F32)
        dlt = total / cnt - u_ref[0, :, g * gd:(g + 1) * gd]
        cols = slice(g * gd, (g + 1) * gd)
        return (_dot(dlt.astype(BF16), pw_ref[g]) * ps_ref[:, cols]).astype(BF16)

    pooled = []
    pool_stages = [pool_fill, pool_double] + [functools.partial(pool_group, g)
                                               for g in range(len(POOL_WINDOWS))]

    p_next = probabilities(scores(0))
    for h in range(heads):
        p_cur = p_next
        if h + 1 < heads:
            p_next = probabilities(scores(h + 1))
        if h < len(pool_stages):
            out = pool_stages[h]()
            if out is not None:
                pooled.append(out)
        weighted_values(h, p_cur)

    pool_ref[0] = jnp.concatenate(pooled, axis=1)
    na_ref[0] = nat_scr[...].T.astype(BF16)


def _mixer(u, qrt, qpt, kr, vt, kc, vct, bias, pool_w_bf, pool_scale):
    b, l, pw = u.shape
    nw = kr.shape[-1]
    gd = pool_w_bf.shape[-1]
    tm = TILE_ROWS * GRID_W
    nt = l // tm
    hb = tm // POOL_HALO
    n_halo = l // POOL_HALO
    nc = kc.shape[1]
    assert pw == len(POOL_WINDOWS) * gd and POOL_HALO == max(POOL_WINDOWS) // 2

    tok = lambda bi, i: (bi, i, 0)
    chan = lambda bi, i: (bi, 0, i)
    const2 = lambda bi, i: (0, 0)
    first_block = lambda i: jnp.clip(i - 1, 0, nt - 3)

    def k_spec(j):
        return pl.BlockSpec((1, tm, nw), lambda bi, i: (bi, first_block(i) + j, 0))

    def vt_spec(j):
        return pl.BlockSpec((1, nw, tm), lambda bi, i: (bi, 0, first_block(i) + j))

    def bias_idx(bi, i):
        return (jnp.where(i == 0, 0, jnp.where(i == nt - 1, 2, 1)), 0, 0, 0)

    in_specs = [
        pl.BlockSpec((1, tm, pw), tok),
        pl.BlockSpec((1, POOL_HALO, pw), lambda bi, i: (bi, jnp.maximum(i * hb - 1, 0), 0)),
        pl.BlockSpec((1, POOL_HALO, pw), lambda bi, i: (bi, jnp.minimum((i + 1) * hb, n_halo - 1), 0)),
        pl.BlockSpec((1, nw, tm), chan),
        pl.BlockSpec((1, nw, tm), chan),
        k_spec(0), k_spec(1), k_spec(2),
        vt_spec(0), vt_spec(1), vt_spec(2),
        pl.BlockSpec((1, nc, nw), lambda bi, i: (bi, 0, 0)),
        pl.BlockSpec((1, nw, nc), lambda bi, i: (bi, 0, 0)),
        pl.BlockSpec((1,) + bias.shape[1:], bias_idx),
        pl.BlockSpec(pool_w_bf.shape, lambda bi, i: (0, 0, 0)),
        pl.BlockSpec((1, pw), const2),
    ]
    n_scr = tm + 2 * POOL_HALO + SUBLANES
    return pl.pallas_call(
        functools.partial(_mixer_kernel, seq_len=l),
        grid=(b, nt),
        in_specs=in_specs,
        out_specs=[pl.BlockSpec((1, tm, nw), tok), pl.BlockSpec((1, tm, pw), tok)],
        out_shape=[jax.ShapeDtypeStruct((b, l, nw), BF16), jax.ShapeDtypeStruct((b, l, pw), BF16)],
        scratch_shapes=[pltpu.VMEM((nw, tm), F32),
                        pltpu.VMEM((n_scr, pw), F32),
                        pltpu.VMEM((n_scr, pw - gd), F32),
                        pltpu.VMEM((n_scr, pw - 2 * gd), F32),
                        pltpu.VMEM((n_scr, pw - 3 * gd), F32)],
        compiler_params=_params(("arbitrary", "arbitrary")),
        name="mixer",
    )(u, u, u, qrt, qpt, kr, kr, kr, vt, vt, vt, kc, vct, bias, pool_w_bf, pool_scale)


def _post_kernel(x_ref, mod_ref, g_ref, na_ref, pool_ref, gp_ref, gn_ref,
                 wbp_ref, wbn_ref, wo_ref, w1_ref, w2_ref, o_ref, *, chunk):
    merged = (gp_ref[0].astype(F32) * _dot(pool_ref[0], wbp_ref[...])
              + gn_ref[0].astype(F32) * _dot(na_ref[0], wbn_ref[...]))
    x = x_ref[0] + mod_ref[0, 2:3, :] * _dot(merged.astype(BF16), wo_ref[...])
    hb = _modulated_norm(x, mod_ref, g_ref, 3).astype(BF16)
    acc = jnp.zeros(x.shape, F32)
    for c0 in range(0, w1_ref.shape[1], chunk):
        a = jnp.maximum(_dot(hb, w1_ref[:, c0:c0 + chunk]), 0.0)
        acc = acc + _dot((a * a).astype(BF16), w2_ref[c0:c0 + chunk, :])
    o_ref[0] = x + mod_ref[0, 5:6, :] * acc


def _post(x, mod3, norm_g, na, pool, gp, gn, wbp, wbn, wo, w1, w2, tm):
    b, l, d = x.shape
    tok = lambda bi, i: (bi, i, 0)
    const = lambda bi, i: (0, 0)
    resident = lambda w: pl.BlockSpec(w.shape, const, pipeline_mode=pl.Buffered(1))
    return pl.pallas_call(
        functools.partial(_post_kernel, chunk=d),
        grid=(b, l // tm),
        in_specs=[pl.BlockSpec((1, tm, d), tok),
                  pl.BlockSpec((1, N_MOD, d), lambda bi, i: (bi, 0, 0)),
                  pl.BlockSpec((1, d), const),
                  pl.BlockSpec((1, tm, na.shape[-1]), tok),
                  pl.BlockSpec((1, tm, pool.shape[-1]), tok),
                  pl.BlockSpec((1, tm, d), tok),
                  pl.BlockSpec((1, tm, d), tok),
                  resident(wbp), resident(wbn), resident(wo), resident(w1), resident(w2)],
        out_specs=pl.BlockSpec((1, tm, d), tok),
        out_shape=jax.ShapeDtypeStruct((b, l, d), F32),
        compiler_params=_params(("arbitrary", "arbitrary")),
        name="post",
    )(x, mod3, norm_g, na, pool, gp, gn, wbp, wbn, wo, w1, w2)


def _rope_tables(seq_len):
    n_freq = HEAD_DIM // 4
    t = np.arange(seq_len)
    inv = (ROPE_THETA ** (-np.arange(n_freq, dtype=np.float32) / n_freq)).astype(np.float32)
    ang_row = (t // GRID_W).astype(np.float32)[:, None] * inv
    ang_col = (t % GRID_W).astype(np.float32)[:, None] * inv
    cr, sr, cc, sc = np.cos(ang_row), np.sin(ang_row), np.cos(ang_col), np.sin(ang_col)
    cos = np.concatenate([cr, cr, cc, cc], axis=1).astype(np.float32)
    sin = np.concatenate([-sr, sr, -sc, sc], axis=1).astype(np.float32)
    return (jnp.asarray(np.tile(cos, (1, 2))), jnp.asarray(np.tile(sin, (1, 2))),
            jnp.asarray(np.ascontiguousarray(cos.T)), jnp.asarray(np.ascontiguousarray(sin.T)))


def _bias_kernel(el_ref, er_ref, o_ref, *, rows):
    lanes = 2 * GRID_W
    kc = lax.broadcasted_iota(jnp.int32, (GRID_W, lanes), 0)
    lane = lax.broadcasted_iota(jnp.int32, (GRID_W, lanes), 1)
    qc = lane % GRID_W
    c0 = jnp.clip(qc - NA_COLS // 2, 0, GRID_W - NA_COLS)
    col_ok = (kc >= c0) & (kc < c0 + NA_COLS)
    left = lane < GRID_W
    masked = jnp.full((GRID_W, lanes), MASK_VALUE, F32)

    cache = {}

    def toeplitz(side, dr):
        if (side, dr) not in cache:
            ref = er_ref if side else el_ref
            dd = dr + NA_ROWS - 1
            vec = jnp.broadcast_to(ref[0, dd:dd + 1, :], (GRID_W, lanes))
            cache[(side, dr)] = pltpu.roll(vec, 0, axis=1, stride=1, stride_axis=0) * LOG2E
        return cache[(side, dr)]

    nt = rows // TILE_ROWS
    for t, tile in enumerate((0, nt // 2, nt - 1)):
        r = tile * TILE_ROWS
        ks = min(max(r - TILE_ROWS, 0), rows - KEY_ROWS)
        for j in range(KEY_ROWS):
            kr = ks + j
            for ip in range(TILE_ROWS // 2):
                halves = []
                for side in range(2):
                    qr = r + 2 * ip + side
                    r0 = min(max(qr - NA_ROWS // 2, 0), rows - NA_ROWS)
                    ok = r0 <= kr < r0 + NA_ROWS
                    halves.append(toeplitz(side, kr - qr) if ok else masked)
                blk = jnp.where(col_ok, jnp.where(left, halves[0], halves[1]), MASK_VALUE)
                o_ref[t, 0, j * GRID_W:(j + 1) * GRID_W, ip * lanes:(ip + 1) * lanes] = blk


def _bias_tables(rpb, rows):
    heads, nr, ncol = rpb.shape
    lanes = 2 * GRID_W
    half = NA_COLS - 1
    zeros = lambda n: jnp.zeros((heads, nr, n), F32)
    flipped = rpb[..., ::-1]
    e_left = jnp.concatenate([flipped[..., half:], zeros(lanes - ncol), flipped[..., :half]], axis=-1)
    e_right = jnp.concatenate([zeros(GRID_W - half), flipped, zeros(lanes - GRID_W + half - ncol)], axis=-1)
    pad = (-nr) % SUBLANES
    e_left = jnp.pad(e_left, ((0, 0), (0, pad), (0, 0)))
    e_right = jnp.pad(e_right, ((0, 0), (0, pad), (0, 0)))
    tq, tk = TILE_ROWS * GRID_W, KEY_ROWS * GRID_W
    vec_spec = pl.BlockSpec((1, nr + pad, lanes), lambda h: (h, 0, 0))
    return pl.pallas_call(
        functools.partial(_bias_kernel, rows=rows),
        grid=(heads,),
        in_specs=[vec_spec, vec_spec],
        out_specs=pl.BlockSpec((3, 1, tk, tq), lambda h: (0, h, 0, 0)),
        out_shape=jax.ShapeDtypeStruct((3, heads, tk, tq), F32),
        compiler_params=_params(("arbitrary",)),
        name="bias_tables",
    )(e_left, e_right)


def kernel(x, c, ctx, c_ctx, ada_w, ada_b, norm1_g, norm2_g, w_in, pool_w, pool_scale,
           q_norm_g, k_norm_g, rpb, w_branch_pool, w_branch_na, w_out, mlp_w1, mlp_w2):
    b, l, d = x.shape
    depth = ada_w.shape[0]
    pw = pool_scale.shape[-1]
    nw = w_branch_na.shape[1]
    heads = nw // HEAD_DIM
    rows = l // GRID_W
    assert depth == 1, "the context-stream update is only needed when another layer follows"
    assert l % (TILE_ROWS * GRID_W) == 0 and rows >= KEY_ROWS
    assert w_in.shape[-1] == pw + 3 * nw + 2 * d and pw == nw
    ctx_row = b
    tm_proj = 512

    cos_t, sin_t, cos_tt, sin_tt = _rope_tables(l)
    mean_mat = jnp.asarray(np.kron(np.eye(heads), np.full((HEAD_DIM, HEAD_DIM), 1.0 / HEAD_DIM)), BF16)
    pad = (-(b + 1)) % SUBLANES
    cc = jnp.concatenate([c, c_ctx[None], jnp.zeros((pad, d), F32)], axis=0)

    layer = 0
    mod = _modulation(cc, ada_w[layer], ada_b[layer])
    mod3 = mod.reshape(mod.shape[0], N_MOD, d)
    wl = w_in[layer]
    q0, k0, v0, g0 = pw, pw + nw, pw + 2 * nw, pw + 3 * nw
    w_all = wl.astype(BF16)
    w_qvt = _qv_weights(wl, q0 // nw, v0 // nw, nw)
    qg_tt = jnp.broadcast_to(q_norm_g[layer][:, None], (HEAD_DIM, tm_proj))
    kg_t = jnp.tile(k_norm_g[layer], heads)[None]
    n1 = norm1_g[layer][None]

    k_ctx, vt_ctx = _ctx_proj(ctx, mod3, n1, w_all, w_qvt, kg_t, mean_mat, k0 // nw, ctx_row)
    later_weights = (w_branch_pool[layer], w_branch_na[layer], w_out[layer],
                     mlp_w1[layer], mlp_w2[layer])
    (u, qrt, qpt, kr, vt, gp, gn), (wbp, wbn, wo, w1, w2) = _in_proj(
        x, mod3, n1, w_all, w_qvt, cos_t, sin_t, cos_tt, sin_tt, qg_tt, kg_t, mean_mat,
        (0, k0, g0, g0 + d), pw, tm_proj, later_weights)
    bias = _bias_tables(rpb[layer], rows)
    na, pool = _mixer(u, qrt, qpt, kr, vt, k_ctx, vt_ctx, bias,
                      pool_w[layer].astype(BF16), pool_scale[layer][None])
    return _post(x, mod3, norm2_g[layer][None], na, pool, gp, gn, wbp, wbn, wo, w1, w2, tm=tm_proj)
```

```python
import functools

import numpy as np
import jax
import jax.numpy as jnp
from jax import lax
from jax.experimental import pallas as pl
from jax.experimental.pallas import tpu as pltpu

GRID_W = 64
N_MOD = 6
POOL_WINDOWS = (2, 4, 8, 16)
HEAD_DIM = 64
NA_ROWS = 8
NA_COLS = 16
ROPE_THETA = 10000.0
EPS = 1e-6
MASK_VALUE = -1e30
LOG2E = 1.4426950408889634

VMEM_LIMIT_BYTES = 56 * 1024 * 1024
SUBLANES = 8
BF16_ROWS = 16
POOL_HALO = 8
TILE_ROWS = 4
KEY_ROWS = TILE_ROWS + NA_ROWS

BF16 = jnp.bfloat16
F32 = jnp.float32


def _dot(a, b):
    return jnp.dot(a, b, preferred_element_type=F32)


def _dot_nt(a, b):
    return lax.dot_general(a, b, (((1,), (1,)), ((), ())), preferred_element_type=F32)


def _params(sem):
    return pltpu.CompilerParams(dimension_semantics=sem, vmem_limit_bytes=VMEM_LIMIT_BYTES)


def _rms(x):
    return x * lax.rsqrt(jnp.mean(x * x, axis=-1, keepdims=True) + EPS)


def _sigmoid_bf16(z):
    zb = z.astype(BF16)
    return 0.5 * jnp.tanh(0.5 * zb) + 0.5


def _mod_kernel(c_ref, w_ref, b_ref, o_ref):
    c = c_ref[...]
    s = (c * jax.nn.sigmoid(c)).astype(BF16)
    o_ref[...] = _dot(s, w_ref[...].astype(BF16)) + b_ref[...]


def _modulation(cc, ada_w, ada_b):
    m, d = cc.shape
    n = ada_w.shape[1]
    return pl.pallas_call(
        _mod_kernel,
        grid=(n // d,),
        in_specs=[pl.BlockSpec((m, d), lambda j: (0, 0)),
                  pl.BlockSpec((d, d), lambda j: (0, j)),
                  pl.BlockSpec((1, d), lambda j: (0, j))],
        out_specs=pl.BlockSpec((m, d), lambda j: (0, j)),
        out_shape=jax.ShapeDtypeStruct((m, n), F32),
        compiler_params=_params(("arbitrary",)),
        name="modulation",
    )(cc, ada_w, ada_b.reshape(1, n))


def _transpose_cast_kernel(w_ref, o_ref):
    o_ref[...] = w_ref[...].T.astype(BF16)


def _qv_weights(w, q_block, v_block, width):
    d = w.shape[0]
    return pl.pallas_call(
        _transpose_cast_kernel,
        grid=(2,),
        in_specs=[pl.BlockSpec((d, width), lambda s: (0, q_block + s * (v_block - q_block)))],
        out_specs=pl.BlockSpec((width, d), lambda s: (s, 0)),
        out_shape=jax.ShapeDtypeStruct((2 * width, d), BF16),
        compiler_params=_params(("arbitrary",)),
        name="qv_weights",
    )(w)


def _head_norm(a, mean_mat, g):
    ms = _dot((a * a).astype(BF16), mean_mat)
    return a * lax.rsqrt(ms + EPS) * g


def _rope(a, cos, sin_signed, first_half):
    width = a.shape[-1]
    quarter = HEAD_DIM // 4
    up = pltpu.roll(a, width - quarter, axis=1)
    down = pltpu.roll(a, quarter, axis=1)
    swapped = jnp.where(first_half, up, down)
    return a * cos + swapped * sin_signed


def _modulated_norm(x, mod_ref, g_ref, shift_row):
    return _rms(x) * g_ref[...] * (1.0 + mod_ref[0, shift_row + 1:shift_row + 2, :]) \
        + mod_ref[0, shift_row:shift_row + 1, :]


def _ctx_kernel(x_ref, mod_ref, g_ref, wk_ref, wvt_ref, kg_ref, mean_ref, k_ref, vt_ref):
    hb = _modulated_norm(x_ref[0], mod_ref, g_ref, 0).astype(BF16)
    k_ref[0] = _head_norm(_dot(hb, wk_ref[...]), mean_ref[...], kg_ref[...]).astype(BF16)
    vt_ref[0] = _dot_nt(wvt_ref[...], hb).astype(BF16)


def _ctx_proj(ctx, mod3, norm_g, w_all, w_qvt, kg_t, mean_mat, key_block, ctx_row):
    b, n, d = ctx.shape
    nw = kg_t.shape[-1]
    sd = jax.ShapeDtypeStruct
    return pl.pallas_call(
        _ctx_kernel,
        grid=(b,),
        in_specs=[pl.BlockSpec((1, n, d), lambda i: (i, 0, 0)),
                  pl.BlockSpec((1, N_MOD, d), lambda i: (ctx_row, 0, 0)),
                  pl.BlockSpec((1, d), lambda i: (0, 0)),
                  pl.BlockSpec((d, nw), lambda i: (0, key_block)),
                  pl.BlockSpec((nw, d), lambda i: (1, 0)),
                  pl.BlockSpec((1, nw), lambda i: (0, 0)),
                  pl.BlockSpec((nw, nw), lambda i: (0, 0))],
        out_specs=[pl.BlockSpec((1, n, nw), lambda i: (i, 0, 0)),
                   pl.BlockSpec((1, nw, n), lambda i: (i, 0, 0))],
        out_shape=[sd((b, n, nw), BF16), sd((b, nw, n), BF16)],
        compiler_params=_params(("arbitrary",)),
        name="ctx_proj",
    )(ctx, mod3, norm_g, w_all, w_qvt, kg_t, mean_mat)


def _in_proj_kernel(*refs, col_starts, n_cast):
    (x_ref, mod_ref, g_ref, w_ref, wt_ref, cos_ref, sin_ref, cost_ref, sint_ref,
     qgt_ref, kg_ref, mean_ref) = refs[:12]
    cast_in = refs[12:12 + n_cast]
    u_ref, qrt_ref, qpt_ref, kr_ref, vt_ref, gp_ref, gn_ref = refs[12 + n_cast:19 + n_cast]
    cast_out = refs[19 + n_cast:]
    for src, dst in zip(cast_in, cast_out):
        dst[...] = src[...].astype(BF16)

    pw = u_ref.shape[-1]
    nw = kr_ref.shape[-1]
    d = x_ref.shape[-1]
    heads = nw // HEAD_DIM
    quarter = HEAD_DIM // 4
    scale = HEAD_DIM ** -0.5 * LOG2E
    u0, k0, gp0, gn0 = col_starts

    hb = _modulated_norm(x_ref[0], mod_ref, g_ref, 0).astype(BF16)

    cost, sint, qgt = cost_ref[...], sint_ref[...], qgt_ref[...]

    def query_heads(first, last):
        qt = _dot_nt(wt_ref[first * HEAD_DIM:last * HEAD_DIM, :], hb)
        for h in range(first, last):
            rows = slice(h * HEAD_DIM, (h + 1) * HEAD_DIM)
            z = qt[(h - first) * HEAD_DIM:(h - first + 1) * HEAD_DIM, :]
            qn = z * lax.rsqrt(jnp.mean(z * z, axis=0, keepdims=True) + EPS) * (qgt * scale)
            swapped = jnp.concatenate([qn[quarter:2 * quarter], qn[0:quarter],
                                       qn[3 * quarter:], qn[2 * quarter:3 * quarter]], axis=0)
            qpt_ref[0, rows, :] = qn.astype(BF16)
            qrt_ref[0, rows, :] = (qn * cost + swapped * sint).astype(BF16)

    z_gp = _dot(hb, w_ref[:, gp0:gp0 + d])
    query_heads(0, heads // 2)
    z_gn = _dot(hb, w_ref[:, gn0:gn0 + d])
    gp_ref[0] = _sigmoid_bf16(z_gp)
    query_heads(heads // 2, heads)
    z_k = _dot(hb, w_ref[:, k0:k0 + nw])
    gn_ref[0] = _sigmoid_bf16(z_gn)
    z_vt = _dot_nt(wt_ref[nw:, :], hb)

    reps = nw // cos_ref.shape[-1]
    cos = jnp.concatenate([cos_ref[...]] * reps, axis=1)
    sin = jnp.concatenate([sin_ref[...]] * reps, axis=1)
    lane = lax.broadcasted_iota(jnp.int32, (1, nw), 1)
    first_half = (lane % (HEAD_DIM // 2)) < quarter
    k = _head_norm(z_k, mean_ref[...], kg_ref[...])
    kr_ref[0] = _rope(k, cos, sin, first_half).astype(BF16)
    vt_ref[0] = z_vt.astype(BF16)
    u_ref[0] = _dot(hb, w_ref[:, u0:u0 + pw])


def _in_proj(x, mod3, norm_g, w_all, w_qvt, cos_t, sin_t, cos_tt, sin_tt, qg_tt, kg_t, mean_mat,
             col_starts, pw, tm, later_weights):
    b, l, d = x.shape
    nw = kg_t.shape[-1]
    tw = cos_t.shape[-1]
    nt = l // tm
    tok = lambda bi, i: (bi, i, 0)
    chan = lambda bi, i: (bi, 0, i)
    const = lambda bi, i: (0, 0)
    sd = jax.ShapeDtypeStruct
    slab_specs = []
    for w in later_weights:
        assert w.shape[0] % (b * nt * BF16_ROWS) == 0
        slab_specs.append(pl.BlockSpec((w.shape[0] // (b * nt), w.shape[1]),
                                       lambda bi, i: (bi * nt + i, 0)))
    outs = pl.pallas_call(
        functools.partial(_in_proj_kernel, col_starts=col_starts, n_cast=len(later_weights)),
        grid=(b, nt),
        in_specs=[pl.BlockSpec((1, tm, d), tok),
                  pl.BlockSpec((1, N_MOD, d), lambda bi, i: (bi, 0, 0)),
                  pl.BlockSpec((1, d), const),
                  pl.BlockSpec(w_all.shape, const, pipeline_mode=pl.Buffered(1)),
                  pl.BlockSpec(w_qvt.shape, const, pipeline_mode=pl.Buffered(1)),
                  pl.BlockSpec((tm, tw), lambda bi, i: (i, 0)),
                  pl.BlockSpec((tm, tw), lambda bi, i: (i, 0)),
                  pl.BlockSpec((HEAD_DIM, tm), lambda bi, i: (0, i)),
                  pl.BlockSpec((HEAD_DIM, tm), lambda bi, i: (0, i)),
                  pl.BlockSpec((HEAD_DIM, tm), const),
                  pl.BlockSpec((1, nw), const),
                  pl.BlockSpec((nw, nw), const)] + slab_specs,
        out_specs=[pl.BlockSpec((1, tm, pw), tok),
                   pl.BlockSpec((1, nw, tm), chan), pl.BlockSpec((1, nw, tm), chan),
                   pl.BlockSpec((1, tm, nw), tok),
                   pl.BlockSpec((1, nw, tm), chan),
                   pl.BlockSpec((1, tm, d), tok), pl.BlockSpec((1, tm, d), tok)] + slab_specs,
        out_shape=[sd((b, l, pw), F32),
                   sd((b, nw, l), BF16), sd((b, nw, l), BF16),
                   sd((b, l, nw), BF16),
                   sd((b, nw, l), BF16),
                   sd((b, l, d), BF16), sd((b, l, d), BF16)]
                  + [sd(w.shape, BF16) for w in later_weights],
        compiler_params=_params(("arbitrary", "arbitrary")),
        name="in_proj",
    )(x, mod3, norm_g, w_all, w_qvt, cos_t, sin_t, cos_tt, sin_tt, qg_tt, kg_t, mean_mat,
      *later_weights)
    return outs[:7], outs[7:]


def _mixer_kernel(u_ref, up_ref, un_ref, qrt_ref, qpt_ref,
                  k0_ref, k1_ref, k2_ref, vt0_ref, vt1_ref, vt2_ref, kc_ref, vct_ref, bias_ref,
                  pw_ref, ps_ref,
                  na_ref, pool_ref, nat_scr, e_scr, a2_scr, a4_scr, a8_scr, *, seq_len):
    i = pl.program_id(1)
    tm = u_ref.shape[1]
    nw = qrt_ref.shape[1]
    heads = nw // HEAD_DIM
    pair = 2 * HEAD_DIM

    zeros = jnp.zeros((HEAD_DIM, tm), BF16)
    ones = jnp.ones((BF16_ROWS, tm), BF16)
    k_refs = (k0_ref, k1_ref, k2_ref)
    vt_refs = (vt0_ref, vt1_ref, vt2_ref)

    def head_rows(h):
        return slice(h * HEAD_DIM, (h + 1) * HEAD_DIM)

    def scores(h):
        rows = head_rows(h)
        lanes = slice((h // 2) * pair, (h // 2 + 1) * pair)
        q_rot, q_plain = qrt_ref[0, rows, :], qpt_ref[0, rows, :]
        if h % 2 == 0:
            w_rot = jnp.concatenate([q_rot, zeros], axis=0)
            w_plain = jnp.concatenate([q_plain, zeros], axis=0)
        else:
            w_rot = jnp.concatenate([zeros, q_rot], axis=0)
            w_plain = jnp.concatenate([zeros, q_plain], axis=0)
        s = [(_dot(k_refs[j][0, :, lanes], w_rot) + bias_ref[0, h, j * tm:(j + 1) * tm, :]).astype(BF16)
             for j in range(3)]
        s.append(_dot(kc_ref[0, :, lanes], w_plain).astype(BF16))
        m = s[0].max(axis=0, keepdims=True)
        for sj in s[1:]:
            m = jnp.maximum(m, sj.max(axis=0, keepdims=True))
        return s, m

    def probabilities(s_and_m):
        s, m = s_and_m
        return [jnp.exp2(sj - m) for sj in s]

    def weighted_values(h, p):
        rows = head_rows(h)
        ot = _dot(jnp.concatenate([vct_ref[0, rows, :], ones], axis=0), p[3])
        for j in range(3):
            ot = ot + _dot(jnp.concatenate([vt_refs[j][0, rows, :], ones], axis=0), p[j])
        nat_scr[rows, :] = ot[:HEAD_DIM] / ot[HEAD_DIM:HEAD_DIM + 1]

    gd = pw_ref.shape[-1]
    hi = tm + 2 * POOL_HALO
    first = POOL_HALO
    g0 = slice(0, gd)

    def rows_at(ref, off, cols):
        return ref[first + off:first + off + tm, cols]

    def pool_fill():
        for scr in (e_scr, a2_scr, a4_scr):
            scr[hi:hi + SUBLANES, :] = jnp.zeros((SUBLANES, scr.shape[1]), F32)
        e_scr[0:first, :] = jnp.where(i > 0, up_ref[0], 0.0)
        e_scr[first:first + tm, :] = u_ref[0]
        e_scr[first + tm:hi, :] = jnp.where((i + 1) * tm < seq_len, un_ref[0], 0.0)
        a2_scr[0:hi, :] = e_scr[0:hi, gd:] + e_scr[1:hi + 1, gd:]

    def pool_double():
        a4_scr[0:hi, :] = a2_scr[0:hi, gd:] + a2_scr[2:hi + 2, gd:]
        a8_scr[0:hi, :] = a4_scr[0:hi, gd:] + a4_scr[4:hi + 4, gd:]

    def pool_group(g):
        w = POOL_WINDOWS[g]
        if g == 0:
            total = rows_at(e_scr, -1, g0) + rows_at(e_scr, 0, g0)
        elif g == 1:
            total = rows_at(a2_scr, -2, g0) + rows_at(a2_scr, 0, g0)
        elif g == 2:
            total = rows_at(a4_scr, -4, g0) + rows_at(a4_scr, 0, g0)
        else:
            total = rows_at(a8_scr, -8, g0) + rows_at(a8_scr, 0, g0)
        pos = i * tm + lax.broadcasted_iota(jnp.int32, (tm, 1), 0)
        cnt = (jnp.minimum(pos + w // 2, seq_len) - jnp.maximum(pos - w // 2, 0)).astype(F32)
        dlt = total / cnt - u_ref[0, :, g * gd:(g + 1) * gd]
        cols = slice(g * gd, (g + 1) * gd)
        return (_dot(dlt.astype(BF16), pw_ref[g]) * ps_ref[:, cols]).astype(BF16)

    pooled = []
    pool_stages = [pool_fill, pool_double] + [functools.partial(pool_group, g)
                                               for g in range(len(POOL_WINDOWS))]

    p_next = probabilities(scores(0))
    for h in range(heads):
        p_cur = p_next
        if h + 1 < heads:
            p_next = probabilities(scores(h + 1))
        if h < len(pool_stages):
            out = pool_stages[h]()
            if out is not None:
                pooled.append(out)
        weighted_values(h, p_cur)

    pool_ref[0] = jnp.concatenate(pooled, axis=1)
    na_ref[0] = nat_scr[...].T.astype(BF16)


def _mixer(u, qrt, qpt, kr, vt, kc, vct, bias, pool_w_bf, pool_scale):
    b, l, pw = u.shape
    nw = kr.shape[-1]
    gd = pool_w_bf.shape[-1]
    tm = TILE_ROWS * GRID_W
    nt = l // tm
    hb = tm // POOL_HALO
    n_halo = l // POOL_HALO
    nc = kc.shape[1]
    assert pw == len(POOL_WINDOWS) * gd and POOL_HALO == max(POOL_WINDOWS) // 2

    tok = lambda bi, i: (bi, i, 0)
    chan = lambda bi, i: (bi, 0, i)
    const2 = lambda bi, i: (0, 0)
    first_block = lambda i: jnp.clip(i - 1, 0, nt - 3)

    def k_spec(j):
        return pl.BlockSpec((1, tm, nw), lambda bi, i: (bi, first_block(i) + j, 0))

    def vt_spec(j):
        return pl.BlockSpec((1, nw, tm), lambda bi, i: (bi, 0, first_block(i) + j))

    def bias_idx(bi, i):
        return (jnp.where(i == 0, 0, jnp.where(i == nt - 1, 2, 1)), 0, 0, 0)

    in_specs = [
        pl.BlockSpec((1, tm, pw), tok),
        pl.BlockSpec((1, POOL_HALO, pw), lambda bi, i: (bi, jnp.maximum(i * hb - 1, 0), 0)),
        pl.BlockSpec((1, POOL_HALO, pw), lambda bi, i: (bi, jnp.minimum((i + 1) * hb, n_halo - 1), 0)),
        pl.BlockSpec((1, nw, tm), chan),
        pl.BlockSpec((1, nw, tm), chan),
        k_spec(0), k_spec(1), k_spec(2),
        vt_spec(0), vt_spec(1), vt_spec(2),
        pl.BlockSpec((1, nc, nw), lambda bi, i: (bi, 0, 0)),
        pl.BlockSpec((1, nw, nc), lambda bi, i: (bi, 0, 0)),
        pl.BlockSpec((1,) + bias.shape[1:], bias_idx),
        pl.BlockSpec(pool_w_bf.shape, lambda bi, i: (0, 0, 0)),
        pl.BlockSpec((1, pw), const2),
    ]
    n_scr = tm + 2 * POOL_HALO + SUBLANES
    return pl.pallas_call(
        functools.partial(_mixer_kernel, seq_len=l),
        grid=(b, nt),
        in_specs=in_specs,
        out_specs=[pl.BlockSpec((1, tm, nw), tok), pl.BlockSpec((1, tm, pw), tok)],
        out_shape=[jax.ShapeDtypeStruct((b, l, nw), BF16), jax.ShapeDtypeStruct((b, l, pw), BF16)],
        scratch_shapes=[pltpu.VMEM((nw, tm), F32),
                        pltpu.VMEM((n_scr, pw), F32),
                        pltpu.VMEM((n_scr, pw - gd), F32),
                        pltpu.VMEM((n_scr, pw - 2 * gd), F32),
                        pltpu.VMEM((n_scr, pw - 3 * gd), F32)],
        compiler_params=_params(("arbitrary", "arbitrary")),
        name="mixer",
    )(u, u, u, qrt, qpt, kr, kr, kr, vt, vt, vt, kc, vct, bias, pool_w_bf, pool_scale)


def _post_kernel(x_ref, mod_ref, g_ref, na_ref, pool_ref, gp_ref, gn_ref,
                 wbp_ref, wbn_ref, wo_ref, w1_ref, w2_ref, o_ref, *, chunk):
    merged = (gp_ref[0].astype(F32) * _dot(pool_ref[0], wbp_ref[...])
              + gn_ref[0].astype(F32) * _dot(na_ref[0], wbn_ref[...]))
    x = x_ref[0] + mod_ref[0, 2:3, :] * _dot(merged.astype(BF16), wo_ref[...])
    hb = _modulated_norm(x, mod_ref, g_ref, 3).astype(BF16)
    acc = jnp.zeros(x.shape, F32)
    for c0 in range(0, w1_ref.shape[1], chunk):
        a = jnp.maximum(_dot(hb, w1_ref[:, c0:c0 + chunk]), 0.0)
        acc = acc + _dot((a * a).astype(BF16), w2_ref[c0:c0 + chunk, :])
    o_ref[0] = x + mod_ref[0, 5:6, :] * acc


def _post(x, mod3, norm_g, na, pool, gp, gn, wbp, wbn, wo, w1, w2, tm):
    b, l, d = x.shape
    tok = lambda bi, i: (bi, i, 0)
    const = lambda bi, i: (0, 0)
    resident = lambda w: pl.BlockSpec(w.shape, const, pipeline_mode=pl.Buffered(1))
    return pl.pallas_call(
        functools.partial(_post_kernel, chunk=d),
        grid=(b, l // tm),
        in_specs=[pl.BlockSpec((1, tm, d), tok),
                  pl.BlockSpec((1, N_MOD, d), lambda bi, i: (bi, 0, 0)),
                  pl.BlockSpec((1, d), const),
                  pl.BlockSpec((1, tm, na.shape[-1]), tok),
                  pl.BlockSpec((1, tm, pool.shape[-1]), tok),
                  pl.BlockSpec((1, tm, d), tok),
                  pl.BlockSpec((1, tm, d), tok),
                  resident(wbp), resident(wbn), resident(wo), resident(w1), resident(w2)],
        out_specs=pl.BlockSpec((1, tm, d), tok),
        out_shape=jax.ShapeDtypeStruct((b, l, d), F32),
        compiler_params=_params(("arbitrary", "arbitrary")),
        name="post",
    )(x, mod3, norm_g, na, pool, gp, gn, wbp, wbn, wo, w1, w2)


def _rope_tables(seq_len):
    n_freq = HEAD_DIM // 4
    t = np.arange(seq_len)
    inv = (ROPE_THETA ** (-np.arange(n_freq, dtype=np.float32) / n_freq)).astype(np.float32)
    ang_row = (t // GRID_W).astype(np.float32)[:, None] * inv
    ang_col = (t % GRID_W).astype(np.float32)[:, None] * inv
    cr, sr, cc, sc = np.cos(ang_row), np.sin(ang_row), np.cos(ang_col), np.sin(ang_col)
    cos = np.concatenate([cr, cr, cc, cc], axis=1).astype(np.float32)
    sin = np.concatenate([-sr, sr, -sc, sc], axis=1).astype(np.float32)
    return (jnp.asarray(np.tile(cos, (1, 2))), jnp.asarray(np.tile(sin, (1, 2))),
            jnp.asarray(np.ascontiguousarray(cos.T)), jnp.asarray(np.ascontiguousarray(sin.T)))


def _bias_kernel(el_ref, er_ref, o_ref, *, rows):
    lanes = 2 * GRID_W
    kc = lax.broadcasted_iota(jnp.int32, (GRID_W, lanes), 0)
    lane = lax.broadcasted_iota(jnp.int32, (GRID_W, lanes), 1)
    qc = lane % GRID_W
    c0 = jnp.clip(qc - NA_COLS // 2, 0, GRID_W - NA_COLS)
    col_ok = (kc >= c0) & (kc < c0 + NA_COLS)
    left = lane < GRID_W
    masked = jnp.full((GRID_W, lanes), MASK_VALUE, F32)

    cache = {}

    def toeplitz(side, dr):
        if (side, dr) not in cache:
            ref = er_ref if side else el_ref
            dd = dr + NA_ROWS - 1
            vec = jnp.broadcast_to(ref[0, dd:dd + 1, :], (GRID_W, lanes))
            cache[(side, dr)] = pltpu.roll(vec, 0, axis=1, stride=1, stride_axis=0) * LOG2E
        return cache[(side, dr)]

    nt = rows // TILE_ROWS
    for t, tile in enumerate((0, nt // 2, nt - 1)):
        r = tile * TILE_ROWS
        ks = min(max(r - TILE_ROWS, 0), rows - KEY_ROWS)
        for j in range(KEY_ROWS):
            kr = ks + j
            for ip in range(TILE_ROWS // 2):
                halves = []
                for side in range(2):
                    qr = r + 2 * ip + side
                    r0 = min(max(qr - NA_ROWS // 2, 0), rows - NA_ROWS)
                    ok = r0 <= kr < r0 + NA_ROWS
                    halves.append(toeplitz(side, kr - qr) if ok else masked)
                blk = jnp.where(col_ok, jnp.where(left, halves[0], halves[1]), MASK_VALUE)
                o_ref[t, 0, j * GRID_W:(j + 1) * GRID_W, ip * lanes:(ip + 1) * lanes] = blk


def _bias_tables(rpb, rows):
    heads, nr, ncol = rpb.shape
    lanes = 2 * GRID_W
    half = NA_COLS - 1
    zeros = lambda n: jnp.zeros((heads, nr, n), F32)
    flipped = rpb[..., ::-1]
    e_left = jnp.concatenate([flipped[..., half:], zeros(lanes - ncol), flipped[..., :half]], axis=-1)
    e_right = jnp.concatenate([zeros(GRID_W - half), flipped, zeros(lanes - GRID_W + half - ncol)], axis=-1)
    pad = (-nr) % SUBLANES
    e_left = jnp.pad(e_left, ((0, 0), (0, pad), (0, 0)))
    e_right = jnp.pad(e_right, ((0, 0), (0, pad), (0, 0)))
    tq, tk = TILE_ROWS * GRID_W, KEY_ROWS * GRID_W
    vec_spec = pl.BlockSpec((1, nr + pad, lanes), lambda h: (h, 0, 0))
    return pl.pallas_call(
        functools.partial(_bias_kernel, rows=rows),
        grid=(heads,),
        in_specs=[vec_spec, vec_spec],
        out_specs=pl.BlockSpec((3, 1, tk, tq), lambda h: (0, h, 0, 0)),
        out_shape=jax.ShapeDtypeStruct((3, heads, tk, tq), F32),
        compiler_params=_params(("arbitrary",)),
        name="bias_tables",
    )(e_left, e_right)


def kernel(x, c, ctx, c_ctx, ada_w, ada_b, norm1_g, norm2_g, w_in, pool_w, pool_scale,
           q_norm_g, k_norm_g, rpb, w_branch_pool, w_branch_na, w_out, mlp_w1, mlp_w2):
    b, l, d = x.shape
    depth = ada_w.shape[0]
    pw = pool_scale.shape[-1]
    nw = w_branch_na.shape[1]
    heads = nw // HEAD_DIM
    rows = l // GRID_W
    assert depth == 1, "the context-stream update is only needed when another layer follows"
    assert l % (TILE_ROWS * GRID_W) == 0 and rows >= KEY_ROWS
    assert w_in.shape[-1] == pw + 3 * nw + 2 * d and pw == nw
    ctx_row = b
    tm_proj = 512

    cos_t, sin_t, cos_tt, sin_tt = _rope_tables(l)
    mean_mat = jnp.asarray(np.kron(np.eye(heads), np.full((HEAD_DIM, HEAD_DIM), 1.0 / HEAD_DIM)), BF16)
    pad = (-(b + 1)) % SUBLANES
    cc = jnp.concatenate([c, c_ctx[None], jnp.zeros((pad, d), F32)], axis=0)

    layer = 0
    mod = _modulation(cc, ada_w[layer], ada_b[layer])
    mod3 = mod.reshape(mod.shape[0], N_MOD, d)
    wl = w_in[layer]
    q0, k0, v0, g0 = pw, pw + nw, pw + 2 * nw, pw + 3 * nw
    w_all = wl.astype(BF16)
    w_qvt = _qv_weights(wl, q0 // nw, v0 // nw, nw)
    qg_tt = jnp.broadcast_to(q_norm_g[layer][:, None], (HEAD_DIM, tm_proj))
    kg_t = jnp.tile(k_norm_g[layer], heads)[None]
    n1 = norm1_g[layer][None]

    k_ctx, vt_ctx = _ctx_proj(ctx, mod3, n1, w_all, w_qvt, kg_t, mean_mat, k0 // nw, ctx_row)
    later_weights = (w_branch_pool[layer], w_branch_na[layer], w_out[layer],
                     mlp_w1[layer], mlp_w2[layer])
    (u, qrt, qpt, kr, vt, gp, gn), (wbp, wbn, wo, w1, w2) = _in_proj(
        x, mod3, n1, w_all, w_qvt, cos_t, sin_t, cos_tt, sin_tt, qg_tt, kg_t, mean_mat,
        (0, k0, g0, g0 + d), pw, tm_proj, later_weights)
    bias = _bias_tables(rpb[layer], rows)
    na, pool = _mixer(u, qrt, qpt, kr, vt, k_ctx, vt_ctx, bias,
                      pool_w[layer].astype(BF16), pool_scale[layer][None])
    return _post(x, mod3, norm2_g[layer][None], na, pool, gp, gn, wbp, wbn, wo, w1, w2, tm=tm_proj)
```

```python
import functools

import numpy as np
import jax
import jax.numpy as jnp
from jax import lax
from jax.experimental import pallas as pl
from jax.experimental.pallas import tpu as pltpu

GRID_W = 64
N_MOD = 6
POOL_WINDOWS = (2, 4, 8, 16)
HEAD_DIM = 64
NA_ROWS = 8
NA_COLS = 16
ROPE_THETA = 10000.0
EPS = 1e-6
MASK_VALUE = -1e30
LOG2E = 1.4426950408889634

VMEM_LIMIT_BYTES = 56 * 1024 * 1024
SUBLANES = 8
BF16_ROWS = 16
POOL_HALO = 8
TILE_ROWS = 4
KEY_ROWS = TILE_ROWS + NA_ROWS

BF16 = jnp.bfloat16
F32 = jnp.float32


def _dot(a, b):
    return jnp.dot(a, b, preferred_element_type=F32)


def _dot_nt(a, b):
    return lax.dot_general(a, b, (((1,), (1,)), ((), ())), preferred_element_type=F32)


def _params(sem):
    return pltpu.CompilerParams(dimension_semantics=sem, vmem_limit_bytes=VMEM_LIMIT_BYTES)


def _rms(x):
    return x * lax.rsqrt(jnp.mean(x * x, axis=-1, keepdims=True) + EPS)


def _sigmoid_bf16(z):
    zb = z.astype(BF16)
    return 0.5 * jnp.tanh(0.5 * zb) + 0.5


def _mod_kernel(c_ref, w_ref, b_ref, o_ref):
    c = c_ref[...]
    s = (c * jax.nn.sigmoid(c)).astype(BF16)
    o_ref[...] = _dot(s, w_ref[...].astype(BF16)) + b_ref[...]


def _modulation(cc, ada_w, ada_b):
    m, d = cc.shape
    n = ada_w.shape[1]
    return pl.pallas_call(
        _mod_kernel,
        grid=(n // d,),
        in_specs=[pl.BlockSpec((m, d), lambda j: (0, 0)),
                  pl.BlockSpec((d, d), lambda j: (0, j)),
                  pl.BlockSpec((1, d), lambda j: (0, j))],
        out_specs=pl.BlockSpec((m, d), lambda j: (0, j)),
        out_shape=jax.ShapeDtypeStruct((m, n), F32),
        compiler_params=_params(("arbitrary",)),
        name="modulation",
    )(cc, ada_w, ada_b.reshape(1, n))


def _transpose_cast_kernel(w_ref, o_ref):
    o_ref[...] = w_ref[...].T.astype(BF16)


def _qv_weights(w, q_block, v_block, width):
    d = w.shape[0]
    return pl.pallas_call(
        _transpose_cast_kernel,
        grid=(2,),
        in_specs=[pl.BlockSpec((d, width), lambda s: (0, q_block + s * (v_block - q_block)))],
        out_specs=pl.BlockSpec((width, d), lambda s: (s, 0)),
        out_shape=jax.ShapeDtypeStruct((2 * width, d), BF16),
        compiler_params=_params(("arbitrary",)),
        name="qv_weights",
    )(w)


def _head_norm(a, mean_mat, g):
    ms = _dot((a * a).astype(BF16), mean_mat)
    return a * lax.rsqrt(ms + EPS) * g


def _rope(a, cos, sin_signed, first_half):
    width = a.shape[-1]
    quarter = HEAD_DIM // 4
    up = pltpu.roll(a, width - quarter, axis=1)
    down = pltpu.roll(a, quarter, axis=1)
    swapped = jnp.where(first_half, up, down)
    return a * cos + swapped * sin_signed


def _modulated_norm(x, mod_ref, g_ref, shift_row):
    return _rms(x) * g_ref[...] * (1.0 + mod_ref[0, shift_row + 1:shift_row + 2, :]) \
        + mod_ref[0, shift_row:shift_row + 1, :]


def _ctx_kernel(x_ref, mod_ref, g_ref, wk_ref, wvt_ref, kg_ref, mean_ref, k_ref, vt_ref):
    hb = _modulated_norm(x_ref[0], mod_ref, g_ref, 0).astype(BF16)
    k_ref[0] = _head_norm(_dot(hb, wk_ref[...]), mean_ref[...], kg_ref[...]).astype(BF16)
    vt_ref[0] = _dot_nt(wvt_ref[...], hb).astype(BF16)


def _ctx_proj(ctx, mod3, norm_g, w_all, w_qvt, kg_t, mean_mat, key_block, ctx_row):
    b, n, d = ctx.shape
    nw = kg_t.shape[-1]
    sd = jax.ShapeDtypeStruct
    return pl.pallas_call(
        _ctx_kernel,
        grid=(b,),
        in_specs=[pl.BlockSpec((1, n, d), lambda i: (i, 0, 0)),
                  pl.BlockSpec((1, N_MOD, d), lambda i: (ctx_row, 0, 0)),
                  pl.BlockSpec((1, d), lambda i: (0, 0)),
                  pl.BlockSpec((d, nw), lambda i: (0, key_block)),
                  pl.BlockSpec((nw, d), lambda i: (1, 0)),
                  pl.BlockSpec((1, nw), lambda i: (0, 0)),
                  pl.BlockSpec((nw, nw), lambda i: (0, 0))],
        out_specs=[pl.BlockSpec((1, n, nw), lambda i: (i, 0, 0)),
                   pl.BlockSpec((1, nw, n), lambda i: (i, 0, 0))],
        out_shape=[sd((b, n, nw), BF16), sd((b, nw, n), BF16)],
        compiler_params=_params(("arbitrary",)),
        name="ctx_proj",
    )(ctx, mod3, norm_g, w_all, w_qvt, kg_t, mean_mat)


def _in_proj_kernel(*refs, col_starts, n_cast):
    (x_ref, mod_ref, g_ref, w_ref, wt_ref, cos_ref, sin_ref, cost_ref, sint_ref,
     qgt_ref, kg_ref, mean_ref) = refs[:12]
    cast_in = refs[12:12 + n_cast]
    u_ref, qrt_ref, qpt_ref, kr_ref, vt_ref, gp_ref, gn_ref = refs[12 + n_cast:19 + n_cast]
    cast_out = refs[19 + n_cast:]
    for src, dst in zip(cast_in, cast_out):
        dst[...] = src[...].astype(BF16)

    pw = u_ref.shape[-1]
    nw = kr_ref.shape[-1]
    d = x_ref.shape[-1]
    heads = nw // HEAD_DIM
    quarter = HEAD_DIM // 4
    scale = HEAD_DIM ** -0.5 * LOG2E
    u0, k0, gp0, gn0 = col_starts

    hb = _modulated_norm(x_ref[0], mod_ref, g_ref, 0).astype(BF16)

    cost, sint, qgt = cost_ref[...], sint_ref[...], qgt_ref[...]

    def query_heads(first, last):
        qt = _dot_nt(wt_ref[first * HEAD_DIM:last * HEAD_DIM, :], hb)
        for h in range(first, last):
            rows = slice(h * HEAD_DIM, (h + 1) * HEAD_DIM)
            z = qt[(h - first) * HEAD_DIM:(h - first + 1) * HEAD_DIM, :]
            qn = z * lax.rsqrt(jnp.mean(z * z, axis=0, keepdims=True) + EPS) * (qgt * scale)
            swapped = jnp.concatenate([qn[quarter:2 * quarter], qn[0:quarter],
                                       qn[3 * quarter:], qn[2 * quarter:3 * quarter]], axis=0)
            qpt_ref[0, rows, :] = qn.astype(BF16)
            qrt_ref[0, rows, :] = (qn * cost + swapped * sint).astype(BF16)

    z_gp = _dot(hb, w_ref[:, gp0:gp0 + d])
    query_heads(0, heads // 2)
    z_gn = _dot(hb, w_ref[:, gn0:gn0 + d])
    gp_ref[0] = _sigmoid_bf16(z_gp)
    query_heads(heads // 2, heads)
    z_k = _dot(hb, w_ref[:, k0:k0 + nw])
    gn_ref[0] = _sigmoid_bf16(z_gn)
    z_vt = _dot_nt(wt_ref[nw:, :], hb)

    reps = nw // cos_ref.shape[-1]
    cos = jnp.concatenate([cos_ref[...]] * reps, axis=1)
    sin = jnp.concatenate([sin_ref[...]] * reps, axis=1)
    lane = lax.broadcasted_iota(jnp.int32, (1, nw), 1)
    first_half = (lane % (HEAD_DIM // 2)) < quarter
    k = _head_norm(z_k, mean_ref[...], kg_ref[...])
    kr_ref[0] = _rope(k, cos, sin, first_half).astype(BF16)
    vt_ref[0] = z_vt.astype(BF16)
    u_ref[0] = _dot(hb, w_ref[:, u0:u0 + pw])


def _in_proj(x, mod3, norm_g, w_all, w_qvt, cos_t, sin_t, cos_tt, sin_tt, qg_tt, kg_t, mean_mat,
             col_starts, pw, tm, later_weights):
    b, l, d = x.shape
    nw = kg_t.shape[-1]
    tw = cos_t.shape[-1]
    nt = l // tm
    tok = lambda bi, i: (bi, i, 0)
    chan = lambda bi, i: (bi, 0, i)
    const = lambda bi, i: (0, 0)
    sd = jax.ShapeDtypeStruct
    slab_specs = []
    for w in later_weights:
        assert w.shape[0] % (b * nt * BF16_ROWS) == 0
        slab_specs.append(pl.BlockSpec((w.shape[0] // (b * nt), w.shape[1]),
                                       lambda bi, i: (bi * nt + i, 0)))
    outs = pl.pallas_call(
        functools.partial(_in_proj_kernel, col_starts=col_starts, n_cast=len(later_weights)),
        grid=(b, nt),
        in_specs=[pl.BlockSpec((1, tm, d), tok),
                  pl.BlockSpec((1, N_MOD, d), lambda bi, i: (bi, 0, 0)),
                  pl.BlockSpec((1, d), const),
                  pl.BlockSpec(w_all.shape, const, pipeline_mode=pl.Buffered(1)),
                  pl.BlockSpec(w_qvt.shape, const, pipeline_mode=pl.Buffered(1)),
                  pl.BlockSpec((tm, tw), lambda bi, i: (i, 0)),
                  pl.BlockSpec((tm, tw), lambda bi, i: (i, 0)),
                  pl.BlockSpec((HEAD_DIM, tm), lambda bi, i: (0, i)),
                  pl.BlockSpec((HEAD_DIM, tm), lambda bi, i: (0, i)),
                  pl.BlockSpec((HEAD_DIM, tm), const),
                  pl.BlockSpec((1, nw), const),
                  pl.BlockSpec((nw, nw), const)] + slab_specs,
        out_specs=[pl.BlockSpec((1, tm, pw), tok),
                   pl.BlockSpec((1, nw, tm), chan), pl.BlockSpec((1, nw, tm), chan),
                   pl.BlockSpec((1, tm, nw), tok),
                   pl.BlockSpec((1, nw, tm), chan),
                   pl.BlockSpec((1, tm, d), tok), pl.BlockSpec((1, tm, d), tok)] + slab_specs,
        out_shape=[sd((b, l, pw), F32),
                   sd((b, nw, l), BF16), sd((b, nw, l), BF16),
                   sd((b, l, nw), BF16),
                   sd((b, nw, l), BF16),
                   sd((b, l, d), BF16), sd((b, l, d), BF16)]
                  + [sd(w.shape, BF16) for w in later_weights],
        compiler_params=_params(("arbitrary", "arbitrary")),
        name="in_proj",
    )(x, mod3, norm_g, w_all, w_qvt, cos_t, sin_t, cos_tt, sin_tt, qg_tt, kg_t, mean_mat,
      *later_weights)
    return outs[:7], outs[7:]


def _mixer_kernel(u_ref, up_ref, un_ref, qrt_ref, qpt_ref,
                  k0_ref, k1_ref, k2_ref, vt0_ref, vt1_ref, vt2_ref, kc_ref, vct_ref, bias_ref,
                  pw_ref, ps_ref,
                  na_ref, pool_ref, nat_scr, e_scr, a2_scr, a4_scr, a8_scr, *, seq_len):
    i = pl.program_id(1)
    tm = u_ref.shape[1]
    nw = qrt_ref.shape[1]
    heads = nw // HEAD_DIM
    pair = 2 * HEAD_DIM

    zeros = jnp.zeros((HEAD_DIM, tm), BF16)
    ones = jnp.ones((BF16_ROWS, tm), BF16)
    k_refs = (k0_ref, k1_ref, k2_ref)
    vt_refs = (vt0_ref, vt1_ref, vt2_ref)

    def head_rows(h):
        return slice(h * HEAD_DIM, (h + 1) * HEAD_DIM)

    def scores(h):
        rows = head_rows(h)
        lanes = slice((h // 2) * pair, (h // 2 + 1) * pair)
        q_rot, q_plain = qrt_ref[0, rows, :], qpt_ref[0, rows, :]
        if h % 2 == 0:
            w_rot = jnp.concatenate([q_rot, zeros], axis=0)
            w_plain = jnp.concatenate([q_plain, zeros], axis=0)
        else:
            w_rot = jnp.concatenate([zeros, q_rot], axis=0)
            w_plain = jnp.concatenate([zeros, q_plain], axis=0)
        s = [(_dot(k_refs[j][0, :, lanes], w_rot) + bias_ref[0, h, j * tm:(j + 1) * tm, :]).astype(BF16)
             for j in range(3)]
        s.append(_dot(kc_ref[0, :, lanes], w_plain).astype(BF16))
        m = s[0].max(axis=0, keepdims=True)
        for sj in s[1:]:
            m = jnp.maximum(m, sj.max(axis=0, keepdims=True))
        return s, m

    def probabilities(s_and_m):
        s, m = s_and_m
        return [jnp.exp2(sj - m) for sj in s]

    def weighted_values(h, p):
        rows = head_rows(h)
        ot = _dot(jnp.concatenate([vct_ref[0, rows, :], ones], axis=0), p[3])
        for j in range(3):
            ot = ot + _dot(jnp.concatenate([vt_refs[j][0, rows, :], ones], axis=0), p[j])
        nat_scr[rows, :] = ot[:HEAD_DIM] / ot[HEAD_DIM:HEAD_DIM + 1]

    gd = pw_ref.shape[-1]
    hi = tm + 2 * POOL_HALO
    first = POOL_HALO
    g0 = slice(0, gd)

    def rows_at(ref, off, cols):
        return ref[first + off:first + off + tm, cols]

    def pool_fill():
        for scr in (e_scr, a2_scr, a4_scr):
            scr[hi:hi + SUBLANES, :] = jnp.zeros((SUBLANES, scr.shape[1]), F32)
        e_scr[0:first, :] = jnp.where(i > 0, up_ref[0], 0.0)
        e_scr[first:first + tm, :] = u_ref[0]
        e_scr[first + tm:hi, :] = jnp.where((i + 1) * tm < seq_len, un_ref[0], 0.0)
        a2_scr[0:hi, :] = e_scr[0:hi, gd:] + e_scr[1:hi + 1, gd:]

    def pool_double():
        a4_scr[0:hi, :] = a2_scr[0:hi, gd:] + a2_scr[2:hi + 2, gd:]
        a8_scr[0:hi, :] = a4_scr[0:hi, gd:] + a4_scr[4:hi + 4, gd:]

    def pool_group(g):
        w = POOL_WINDOWS[g]
        if g == 0:
            total = rows_at(e_scr, -1, g0) + rows_at(e_scr, 0, g0)
        elif g == 1:
            total = rows_at(a2_scr, -2, g0) + rows_at(a2_scr, 0, g0)
        elif g == 2:
            total = rows_at(a4_scr, -4, g0) + rows_at(a4_scr, 0, g0)
        else:
            total = rows_at(a8_scr, -8, g0) + rows_at(a8_scr, 0, g0)
        pos = i * tm + lax.broadcasted_iota(jnp.int32, (tm, 1), 0)
        cnt = (jnp.minimum(pos + w // 2, seq_len) - jnp.maximum(pos - w // 2, 0)).astype(F32)
        dlt = total / cnt - u_ref[0, :, g * gd:(g + 1) * gd]
        cols = slice(g * gd, (g + 1) * gd)
        return (_dot(dlt.astype(BF16), pw_ref[g]) * ps_ref[:, cols]).astype(BF16)

    pooled = []
    pool_stages = [pool_fill, pool_double] + [functools.partial(pool_group, g)
                                               for g in range(len(POOL_WINDOWS))]

    p_next = probabilities(scores(0))
    for h in range(heads):
        p_cur = p_next
        if h + 1 < heads:
            p_next = probabilities(scores(h + 1))
        weighted_values(h, p_cur)
        stage = h - (heads - len(pool_stages))
        if stage >= 0:
            out = pool_stages[stage]()
            if out is not None:
                pooled.append(out)
        if h % 2 == 1:
            slab = slice((h - 1) * HEAD_DIM, (h + 1) * HEAD_DIM)
            na_ref[0, :, slab] = nat_scr[slab, :].T.astype(BF16)

    pool_ref[0] = jnp.concatenate(pooled, axis=1)


def _mixer(u, qrt, qpt, kr, vt, kc, vct, bias, pool_w_bf, pool_scale):
    b, l, pw = u.shape
    nw = kr.shape[-1]
    gd = pool_w_bf.shape[-1]
    tm = TILE_ROWS * GRID_W
    nt = l // tm
    hb = tm // POOL_HALO
    n_halo = l // POOL_HALO
    nc = kc.shape[1]
    assert pw == len(POOL_WINDOWS) * gd and POOL_HALO == max(POOL_WINDOWS) // 2

    tok = lambda bi, i: (bi, i, 0)
    chan = lambda bi, i: (bi, 0, i)
    const2 = lambda bi, i: (0, 0)
    first_block = lambda i: jnp.clip(i - 1, 0, nt - 3)

    def k_spec(j):
        return pl.BlockSpec((1, tm, nw), lambda bi, i: (bi, first_block(i) + j, 0))

    def vt_spec(j):
        return pl.BlockSpec((1, nw, tm), lambda bi, i: (bi, 0, first_block(i) + j))

    def bias_idx(bi, i):
        return (jnp.where(i == 0, 0, jnp.where(i == nt - 1, 2, 1)), 0, 0, 0)

    in_specs = [
        pl.BlockSpec((1, tm, pw), tok),
        pl.BlockSpec((1, POOL_HALO, pw), lambda bi, i: (bi, jnp.maximum(i * hb - 1, 0), 0)),
        pl.BlockSpec((1, POOL_HALO, pw), lambda bi, i: (bi, jnp.minimum((i + 1) * hb, n_halo - 1), 0)),
        pl.BlockSpec((1, nw, tm), chan),
        pl.BlockSpec((1, nw, tm), chan),
        k_spec(0), k_spec(1), k_spec(2),
        vt_spec(0), vt_spec(1), vt_spec(2),
        pl.BlockSpec((1, nc, nw), lambda bi, i: (bi, 0, 0)),
        pl.BlockSpec((1, nw, nc), lambda bi, i: (bi, 0, 0)),
        pl.BlockSpec((1,) + bias.shape[1:], bias_idx),
        pl.BlockSpec(pool_w_bf.shape, lambda bi, i: (0, 0, 0)),
        pl.BlockSpec((1, pw), const2),
    ]
    n_scr = tm + 2 * POOL_HALO + SUBLANES
    return pl.pallas_call(
        functools.partial(_mixer_kernel, seq_len=l),
        grid=(b, nt),
        in_specs=in_specs,
        out_specs=[pl.BlockSpec((1, tm, nw), tok), pl.BlockSpec((1, tm, pw), tok)],
        out_shape=[jax.ShapeDtypeStruct((b, l, nw), BF16), jax.ShapeDtypeStruct((b, l, pw), BF16)],
        scratch_shapes=[pltpu.VMEM((nw, tm), F32),
                        pltpu.VMEM((n_scr, pw), F32),
                        pltpu.VMEM((n_scr, pw - gd), F32),
                        pltpu.VMEM((n_scr, pw - 2 * gd), F32),
                        pltpu.VMEM((n_scr, pw - 3 * gd), F32)],
        compiler_params=_params(("arbitrary", "arbitrary")),
        name="mixer",
    )(u, u, u, qrt, qpt, kr, kr, kr, vt, vt, vt, kc, vct, bias, pool_w_bf, pool_scale)


def _post_kernel(x_ref, mod_ref, g_ref, na_ref, pool_ref, gp_ref, gn_ref,
                 wbp_ref, wbn_ref, wo_ref, w1_ref, w2_ref, o_ref, *, chunk):
    merged = (gp_ref[0].astype(F32) * _dot(pool_ref[0], wbp_ref[...])
              + gn_ref[0].astype(F32) * _dot(na_ref[0], wbn_ref[...]))
    x = x_ref[0] + mod_ref[0, 2:3, :] * _dot(merged.astype(BF16), wo_ref[...])
    hb = _modulated_norm(x, mod_ref, g_ref, 3).astype(BF16)
    acc = jnp.zeros(x.shape, F32)
    for c0 in range(0, w1_ref.shape[1], chunk):
        a = jnp.maximum(_dot(hb, w1_ref[:, c0:c0 + chunk]), 0.0)
        acc = acc + _dot((a * a).astype(BF16), w2_ref[c0:c0 + chunk, :])
    o_ref[0] = x + mod_ref[0, 5:6, :] * acc


def _post(x, mod3, norm_g, na, pool, gp, gn, wbp, wbn, wo, w1, w2, tm):
    b, l, d = x.shape
    tok = lambda bi, i: (bi, i, 0)
    const = lambda bi, i: (0, 0)
    resident = lambda w: pl.BlockSpec(w.shape, const, pipeline_mode=pl.Buffered(1))
    return pl.pallas_call(
        functools.partial(_post_kernel, chunk=d),
        grid=(b, l // tm),
        in_specs=[pl.BlockSpec((1, tm, d), tok),
                  pl.BlockSpec((1, N_MOD, d), lambda bi, i: (bi, 0, 0)),
                  pl.BlockSpec((1, d), const),
                  pl.BlockSpec((1, tm, na.shape[-1]), tok),
                  pl.BlockSpec((1, tm, pool.shape[-1]), tok),
                  pl.BlockSpec((1, tm, d), tok),
                  pl.BlockSpec((1, tm, d), tok),
                  resident(wbp), resident(wbn), resident(wo), resident(w1), resident(w2)],
        out_specs=pl.BlockSpec((1, tm, d), tok),
        out_shape=jax.ShapeDtypeStruct((b, l, d), F32),
        compiler_params=_params(("arbitrary", "arbitrary")),
        name="post",
    )(x, mod3, norm_g, na, pool, gp, gn, wbp, wbn, wo, w1, w2)


def _rope_tables(seq_len):
    n_freq = HEAD_DIM // 4
    t = np.arange(seq_len)
    inv = (ROPE_THETA ** (-np.arange(n_freq, dtype=np.float32) / n_freq)).astype(np.float32)
    ang_row = (t // GRID_W).astype(np.float32)[:, None] * inv
    ang_col = (t % GRID_W).astype(np.float32)[:, None] * inv
    cr, sr, cc, sc = np.cos(ang_row), np.sin(ang_row), np.cos(ang_col), np.sin(ang_col)
    cos = np.concatenate([cr, cr, cc, cc], axis=1).astype(np.float32)
    sin = np.concatenate([-sr, sr, -sc, sc], axis=1).astype(np.float32)
    return (jnp.asarray(np.tile(cos, (1, 2))), jnp.asarray(np.tile(sin, (1, 2))),
            jnp.asarray(np.ascontiguousarray(cos.T)), jnp.asarray(np.ascontiguousarray(sin.T)))


def _bias_kernel(el_ref, er_ref, o_ref, *, rows):
    lanes = 2 * GRID_W
    kc = lax.broadcasted_iota(jnp.int32, (GRID_W, lanes), 0)
    lane = lax.broadcasted_iota(jnp.int32, (GRID_W, lanes), 1)
    qc = lane % GRID_W
    c0 = jnp.clip(qc - NA_COLS // 2, 0, GRID_W - NA_COLS)
    col_ok = (kc >= c0) & (kc < c0 + NA_COLS)
    left = lane < GRID_W
    masked = jnp.full((GRID_W, lanes), MASK_VALUE, F32)

    cache = {}

    def toeplitz(side, dr):
        if (side, dr) not in cache:
            ref = er_ref if side else el_ref
            dd = dr + NA_ROWS - 1
            vec = jnp.broadcast_to(ref[0, dd:dd + 1, :], (GRID_W, lanes))
            cache[(side, dr)] = pltpu.roll(vec, 0, axis=1, stride=1, stride_axis=0) * LOG2E
        return cache[(side, dr)]

    nt = rows // TILE_ROWS
    for t, tile in enumerate((0, nt // 2, nt - 1)):
        r = tile * TILE_ROWS
        ks = min(max(r - TILE_ROWS, 0), rows - KEY_ROWS)
        for j in range(KEY_ROWS):
            kr = ks + j
            for ip in range(TILE_ROWS // 2):
                halves = []
                for side in range(2):
                    qr = r + 2 * ip + side
                    r0 = min(max(qr - NA_ROWS // 2, 0), rows - NA_ROWS)
                    ok = r0 <= kr < r0 + NA_ROWS
                    halves.append(toeplitz(side, kr - qr) if ok else masked)
                blk = jnp.where(col_ok, jnp.where(left, halves[0], halves[1]), MASK_VALUE)
                o_ref[t, 0, j * GRID_W:(j + 1) * GRID_W, ip * lanes:(ip + 1) * lanes] = blk


def _bias_tables(rpb, rows):
    heads, nr, ncol = rpb.shape
    lanes = 2 * GRID_W
    half = NA_COLS - 1
    zeros = lambda n: jnp.zeros((heads, nr, n), F32)
    flipped = rpb[..., ::-1]
    e_left = jnp.concatenate([flipped[..., half:], zeros(lanes - ncol), flipped[..., :half]], axis=-1)
    e_right = jnp.concatenate([zeros(GRID_W - half), flipped, zeros(lanes - GRID_W + half - ncol)], axis=-1)
    pad = (-nr) % SUBLANES
    e_left = jnp.pad(e_left, ((0, 0), (0, pad), (0, 0)))
    e_right = jnp.pad(e_right, ((0, 0), (0, pad), (0, 0)))
    tq, tk = TILE_ROWS * GRID_W, KEY_ROWS * GRID_W
    vec_spec = pl.BlockSpec((1, nr + pad, lanes), lambda h: (h, 0, 0))
    return pl.pallas_call(
        functools.partial(_bias_kernel, rows=rows),
        grid=(heads,),
        in_specs=[vec_spec, vec_spec],
        out_specs=pl.BlockSpec((3, 1, tk, tq), lambda h: (0, h, 0, 0)),
        out_shape=jax.ShapeDtypeStruct((3, heads, tk, tq), F32),
        compiler_params=_params(("arbitrary",)),
        name="bias_tables",
    )(e_left, e_right)


def kernel(x, c, ctx, c_ctx, ada_w, ada_b, norm1_g, norm2_g, w_in, pool_w, pool_scale,
           q_norm_g, k_norm_g, rpb, w_branch_pool, w_branch_na, w_out, mlp_w1, mlp_w2):
    b, l, d = x.shape
    depth = ada_w.shape[0]
    pw = pool_scale.shape[-1]
    nw = w_branch_na.shape[1]
    heads = nw // HEAD_DIM
    rows = l // GRID_W
    assert depth == 1, "the context-stream update is only needed when another layer follows"
    assert l % (TILE_ROWS * GRID_W) == 0 and rows >= KEY_ROWS
    assert w_in.shape[-1] == pw + 3 * nw + 2 * d and pw == nw
    ctx_row = b
    tm_proj = 512
    tm_in = 1024

    cos_t, sin_t, cos_tt, sin_tt = _rope_tables(l)
    mean_mat = jnp.asarray(np.kron(np.eye(heads), np.full((HEAD_DIM, HEAD_DIM), 1.0 / HEAD_DIM)), BF16)
    pad = (-(b + 1)) % SUBLANES
    cc = jnp.concatenate([c, c_ctx[None], jnp.zeros((pad, d), F32)], axis=0)

    layer = 0
    mod = _modulation(cc, ada_w[layer], ada_b[layer])
    mod3 = mod.reshape(mod.shape[0], N_MOD, d)
    wl = w_in[layer]
    q0, k0, v0, g0 = pw, pw + nw, pw + 2 * nw, pw + 3 * nw
    w_all = wl.astype(BF16)
    w_qvt = _qv_weights(wl, q0 // nw, v0 // nw, nw)
    qg_tt = jnp.broadcast_to(q_norm_g[layer][:, None], (HEAD_DIM, tm_in))
    kg_t = jnp.tile(k_norm_g[layer], heads)[None]
    n1 = norm1_g[layer][None]

    k_ctx, vt_ctx = _ctx_proj(ctx, mod3, n1, w_all, w_qvt, kg_t, mean_mat, k0 // nw, ctx_row)
    later_weights = (w_branch_pool[layer], w_branch_na[layer], w_out[layer],
                     mlp_w1[layer], mlp_w2[layer])
    (u, qrt, qpt, kr, vt, gp, gn), (wbp, wbn, wo, w1, w2) = _in_proj(
        x, mod3, n1, w_all, w_qvt, cos_t, sin_t, cos_tt, sin_tt, qg_tt, kg_t, mean_mat,
        (0, k0, g0, g0 + d), pw, tm_in, later_weights)
    bias = _bias_tables(rpb[layer], rows)
    na, pool = _mixer(u, qrt, qpt, kr, vt, k_ctx, vt_ctx, bias,
                      pool_w[layer].astype(BF16), pool_scale[layer][None])
    return _post(x, mod3, norm2_g[layer][None], na, pool, gp, gn, wbp, wbn, wo, w1, w2, tm=tm_proj)
```

```python
import functools

import numpy as np
import jax
import jax.numpy as jnp
from jax import lax
from jax.experimental import pallas as pl
from jax.experimental.pallas import tpu as pltpu

GRID_W = 64
N_MOD = 6
POOL_WINDOWS = (2, 4, 8, 16)
HEAD_DIM = 64
NA_ROWS = 8
NA_COLS = 16
ROPE_THETA = 10000.0
EPS = 1e-6
MASK_VALUE = -1e30
LOG2E = 1.4426950408889634

VMEM_LIMIT_BYTES = 56 * 1024 * 1024
SUBLANES = 8
BF16_ROWS = 16
POOL_HALO = 8
TILE_ROWS = 4
KEY_ROWS = TILE_ROWS + NA_ROWS

BF16 = jnp.bfloat16
F32 = jnp.float32


def _dot(a, b):
    return jnp.dot(a, b, preferred_element_type=F32)


def _dot_nt(a, b):
    return lax.dot_general(a, b, (((1,), (1,)), ((), ())), preferred_element_type=F32)


def _params(sem):
    return pltpu.CompilerParams(dimension_semantics=sem, vmem_limit_bytes=VMEM_LIMIT_BYTES)


def _rms(x):
    return x * lax.rsqrt(jnp.mean(x * x, axis=-1, keepdims=True) + EPS)


def _sigmoid_bf16(z):
    zb = z.astype(BF16)
    return 0.5 * jnp.tanh(0.5 * zb) + 0.5


def _prep_kernel(c_ref, aw_ref, ab_ref, w_rows_ref, w_cols_ref, mod_ref, w_all_ref, w_qvt_ref):
    c = c_ref[...]
    s = (c * jax.nn.sigmoid(c)).astype(BF16)
    mod_ref[...] = _dot(s, aw_ref[...].astype(BF16)) + ab_ref[...]
    w_all_ref[...] = w_rows_ref[...].astype(BF16)
    w_qvt_ref[...] = w_cols_ref[...].T.astype(BF16)


def _prep(cc, ada_w, ada_b, w, q0, v0, width):
    m, d = cc.shape
    n = ada_w.shape[1]
    lanes = 128
    steps = 2 * width // lanes
    assert n % (steps * lanes) == 0 and w.shape[0] % (steps * BF16_ROWS) == 0
    nb, rows = n // steps, w.shape[0] // steps
    per_seg = width // lanes
    col_block = lambda j: jnp.where(j < per_seg, q0 // lanes + j, v0 // lanes + j - per_seg)
    return pl.pallas_call(
        _prep_kernel,
        grid=(steps,),
        in_specs=[pl.BlockSpec((m, d), lambda j: (0, 0)),
                  pl.BlockSpec((d, nb), lambda j: (0, j)),
                  pl.BlockSpec((1, nb), lambda j: (0, j)),
                  pl.BlockSpec((rows, w.shape[1]), lambda j: (j, 0)),
                  pl.BlockSpec((w.shape[0], lanes), lambda j: (0, col_block(j)))],
        out_specs=[pl.BlockSpec((m, nb), lambda j: (0, j)),
                   pl.BlockSpec((rows, w.shape[1]), lambda j: (j, 0)),
                   pl.BlockSpec((lanes, w.shape[0]), lambda j: (j, 0))],
        out_shape=[jax.ShapeDtypeStruct((m, n), F32),
                   jax.ShapeDtypeStruct(w.shape, BF16),
                   jax.ShapeDtypeStruct((2 * width, w.shape[0]), BF16)],
        compiler_params=_params(("arbitrary",)),
        name="prep",
    )(cc, ada_w, ada_b.reshape(1, n), w, w)


def _head_norm(a, mean_mat, g):
    ms = _dot((a * a).astype(BF16), mean_mat)
    return a * lax.rsqrt(ms + EPS) * g


def _rope(a, cos, sin_signed, first_half):
    width = a.shape[-1]
    quarter = HEAD_DIM // 4
    up = pltpu.roll(a, width - quarter, axis=1)
    down = pltpu.roll(a, quarter, axis=1)
    swapped = jnp.where(first_half, up, down)
    return a * cos + swapped * sin_signed


def _modulated_norm(x, mod_ref, g_ref, shift_row):
    return _rms(x) * g_ref[...] * (1.0 + mod_ref[0, shift_row + 1:shift_row + 2, :]) \
        + mod_ref[0, shift_row:shift_row + 1, :]


def _ctx_kernel(x_ref, mod_ref, g_ref, wk_ref, wvt_ref, kg_ref, mean_ref, k_ref, vt_ref):
    hb = _modulated_norm(x_ref[0], mod_ref, g_ref, 0).astype(BF16)
    k_ref[0] = _head_norm(_dot(hb, wk_ref[...]), mean_ref[...], kg_ref[...]).astype(BF16)
    vt_ref[0] = _dot_nt(wvt_ref[...], hb).astype(BF16)


def _ctx_proj(ctx, mod3, norm_g, w_all, w_qvt, kg_t, mean_mat, key_block, ctx_row):
    b, n, d = ctx.shape
    nw = kg_t.shape[-1]
    sd = jax.ShapeDtypeStruct
    return pl.pallas_call(
        _ctx_kernel,
        grid=(b,),
        in_specs=[pl.BlockSpec((1, n, d), lambda i: (i, 0, 0)),
                  pl.BlockSpec((1, N_MOD, d), lambda i: (ctx_row, 0, 0)),
                  pl.BlockSpec((1, d), lambda i: (0, 0)),
                  pl.BlockSpec((d, nw), lambda i: (0, key_block)),
                  pl.BlockSpec((nw, d), lambda i: (1, 0)),
                  pl.BlockSpec((1, nw), lambda i: (0, 0)),
                  pl.BlockSpec((nw, nw), lambda i: (0, 0))],
        out_specs=[pl.BlockSpec((1, n, nw), lambda i: (i, 0, 0)),
                   pl.BlockSpec((1, nw, n), lambda i: (i, 0, 0))],
        out_shape=[sd((b, n, nw), BF16), sd((b, nw, n), BF16)],
        compiler_params=_params(("arbitrary",)),
        name="ctx_proj",
    )(ctx, mod3, norm_g, w_all, w_qvt, kg_t, mean_mat)


def _in_proj_kernel(*refs, col_starts, n_cast):
    (x_ref, mod_ref, g_ref, w_ref, wt_ref, cos_ref, sin_ref, cost_ref, sint_ref,
     qgt_ref, kg_ref, mean_ref) = refs[:12]
    cast_in = refs[12:12 + n_cast]
    u_ref, qrt_ref, qpt_ref, kr_ref, vt_ref, gp_ref, gn_ref = refs[12 + n_cast:19 + n_cast]
    cast_out = refs[19 + n_cast:]
    for src, dst in zip(cast_in, cast_out):
        dst[...] = src[...].astype(BF16)

    pw = u_ref.shape[-1]
    nw = kr_ref.shape[-1]
    d = x_ref.shape[-1]
    heads = nw // HEAD_DIM
    quarter = HEAD_DIM // 4
    scale = HEAD_DIM ** -0.5 * LOG2E
    u0, k0, gp0, gn0 = col_starts

    hb = _modulated_norm(x_ref[0], mod_ref, g_ref, 0).astype(BF16)

    cost, sint, qgt = cost_ref[...], sint_ref[...], qgt_ref[...]

    def query_heads(first, last):
        qt = _dot_nt(wt_ref[first * HEAD_DIM:last * HEAD_DIM, :], hb)
        for h in range(first, last):
            rows = slice(h * HEAD_DIM, (h + 1) * HEAD_DIM)
            z = qt[(h - first) * HEAD_DIM:(h - first + 1) * HEAD_DIM, :]
            qn = z * lax.rsqrt(jnp.mean(z * z, axis=0, keepdims=True) + EPS) * (qgt * scale)
            swapped = jnp.concatenate([qn[quarter:2 * quarter], qn[0:quarter],
                                       qn[3 * quarter:], qn[2 * quarter:3 * quarter]], axis=0)
            qpt_ref[0, rows, :] = qn.astype(BF16)
            qrt_ref[0, rows, :] = (qn * cost + swapped * sint).astype(BF16)

    z_gp = _dot(hb, w_ref[:, gp0:gp0 + d])
    query_heads(0, heads // 2)
    z_gn = _dot(hb, w_ref[:, gn0:gn0 + d])
    gp_ref[0] = _sigmoid_bf16(z_gp)
    query_heads(heads // 2, heads)
    z_k = _dot(hb, w_ref[:, k0:k0 + nw])
    gn_ref[0] = _sigmoid_bf16(z_gn)
    z_vt = _dot_nt(wt_ref[nw:, :], hb)

    reps = nw // cos_ref.shape[-1]
    cos = jnp.concatenate([cos_ref[...]] * reps, axis=1)
    sin = jnp.concatenate([sin_ref[...]] * reps, axis=1)
    lane = lax.broadcasted_iota(jnp.int32, (1, nw), 1)
    first_half = (lane % (HEAD_DIM // 2)) < quarter
    k = _head_norm(z_k, mean_ref[...], kg_ref[...])
    kr_ref[0] = _rope(k, cos, sin, first_half).astype(BF16)
    vt_ref[0] = z_vt.astype(BF16)
    u_ref[0] = _dot(hb, w_ref[:, u0:u0 + pw])


def _in_proj(x, mod3, norm_g, w_all, w_qvt, cos_t, sin_t, cos_tt, sin_tt, qg_tt, kg_t, mean_mat,
             col_starts, pw, tm, later_weights):
    b, l, d = x.shape
    nw = kg_t.shape[-1]
    tw = cos_t.shape[-1]
    nt = l // tm
    tok = lambda bi, i: (bi, i, 0)
    chan = lambda bi, i: (bi, 0, i)
    const = lambda bi, i: (0, 0)
    sd = jax.ShapeDtypeStruct
    slab_specs = []
    for w in later_weights:
        assert w.shape[0] % (b * nt * BF16_ROWS) == 0
        slab_specs.append(pl.BlockSpec((w.shape[0] // (b * nt), w.shape[1]),
                                       lambda bi, i: (bi * nt + i, 0)))
    outs = pl.pallas_call(
        functools.partial(_in_proj_kernel, col_starts=col_starts, n_cast=len(later_weights)),
        grid=(b, nt),
        in_specs=[pl.BlockSpec((1, tm, d), tok),
                  pl.BlockSpec((1, N_MOD, d), lambda bi, i: (bi, 0, 0)),
                  pl.BlockSpec((1, d), const),
                  pl.BlockSpec(w_all.shape, const, pipeline_mode=pl.Buffered(1)),
                  pl.BlockSpec(w_qvt.shape, const, pipeline_mode=pl.Buffered(1)),
                  pl.BlockSpec((tm, tw), lambda bi, i: (i, 0)),
                  pl.BlockSpec((tm, tw), lambda bi, i: (i, 0)),
                  pl.BlockSpec((HEAD_DIM, tm), lambda bi, i: (0, i)),
                  pl.BlockSpec((HEAD_DIM, tm), lambda bi, i: (0, i)),
                  pl.BlockSpec((HEAD_DIM, tm), const),
                  pl.BlockSpec((1, nw), const),
                  pl.BlockSpec((nw, nw), const)] + slab_specs,
        out_specs=[pl.BlockSpec((1, tm, pw), tok),
                   pl.BlockSpec((1, nw, tm), chan), pl.BlockSpec((1, nw, tm), chan),
                   pl.BlockSpec((1, tm, nw), tok),
                   pl.BlockSpec((1, nw, tm), chan),
                   pl.BlockSpec((1, tm, d), tok), pl.BlockSpec((1, tm, d), tok)] + slab_specs,
        out_shape=[sd((b, l, pw), F32),
                   sd((b, nw, l), BF16), sd((b, nw, l), BF16),
                   sd((b, l, nw), BF16),
                   sd((b, nw, l), BF16),
                   sd((b, l, d), BF16), sd((b, l, d), BF16)]
                  + [sd(w.shape, BF16) for w in later_weights],
        compiler_params=_params(("arbitrary", "arbitrary")),
        name="in_proj",
    )(x, mod3, norm_g, w_all, w_qvt, cos_t, sin_t, cos_tt, sin_tt, qg_tt, kg_t, mean_mat,
      *later_weights)
    return outs[:7], outs[7:]


def _mixer_kernel(u_ref, up_ref, un_ref, qrt_ref, qpt_ref,
                  k0_ref, k1_ref, k2_ref, vt0_ref, vt1_ref, vt2_ref, kc_ref, vct_ref, bias_ref,
                  pw_ref, ps_ref,
                  na_ref, pool_ref, nat_scr, e_scr, a2_scr, a4_scr, a8_scr, *, seq_len):
    i = pl.program_id(1)
    tm = u_ref.shape[1]
    nw = qrt_ref.shape[1]
    heads = nw // HEAD_DIM
    pair = 2 * HEAD_DIM

    zeros = jnp.zeros((HEAD_DIM, tm), BF16)
    ones = jnp.ones((BF16_ROWS, tm), BF16)
    k_refs = (k0_ref, k1_ref, k2_ref)
    vt_refs = (vt0_ref, vt1_ref, vt2_ref)

    def head_rows(h):
        return slice(h * HEAD_DIM, (h + 1) * HEAD_DIM)

    def scores(h):
        rows = head_rows(h)
        lanes = slice((h // 2) * pair, (h // 2 + 1) * pair)
        q_rot, q_plain = qrt_ref[0, rows, :], qpt_ref[0, rows, :]
        if h % 2 == 0:
            w_rot = jnp.concatenate([q_rot, zeros], axis=0)
            w_plain = jnp.concatenate([q_plain, zeros], axis=0)
        else:
            w_rot = jnp.concatenate([zeros, q_rot], axis=0)
            w_plain = jnp.concatenate([zeros, q_plain], axis=0)
        s = [(_dot(k_refs[j][0, :, lanes], w_rot) + bias_ref[0, h, j * tm:(j + 1) * tm, :]).astype(BF16)
             for j in range(3)]
        s.append(_dot(kc_ref[0, :, lanes], w_plain).astype(BF16))
        m = s[0].max(axis=0, keepdims=True)
        for sj in s[1:]:
            m = jnp.maximum(m, sj.max(axis=0, keepdims=True))
        return s, m

    def probabilities(s_and_m):
        s, m = s_and_m
        return [jnp.exp2(sj - m) for sj in s]

    def weighted_values(h, p):
        rows = head_rows(h)
        ot = _dot(jnp.concatenate([vct_ref[0, rows, :], ones], axis=0), p[3])
        for j in range(3):
            ot = ot + _dot(jnp.concatenate([vt_refs[j][0, rows, :], ones], axis=0), p[j])
        nat_scr[rows, :] = ot[:HEAD_DIM] / ot[HEAD_DIM:HEAD_DIM + 1]

    gd = pw_ref.shape[-1]
    hi = tm + 2 * POOL_HALO
    first = POOL_HALO
    g0 = slice(0, gd)

    def rows_at(ref, off, cols):
        return ref[first + off:first + off + tm, cols]

    def pool_fill():
        for scr in (e_scr, a2_scr, a4_scr):
            scr[hi:hi + SUBLANES, :] = jnp.zeros((SUBLANES, scr.shape[1]), F32)
        e_scr[0:first, :] = jnp.where(i > 0, up_ref[0], 0.0)
        e_scr[first:first + tm, :] = u_ref[0]
        e_scr[first + tm:hi, :] = jnp.where((i + 1) * tm < seq_len, un_ref[0], 0.0)
        a2_scr[0:hi, :] = e_scr[0:hi, gd:] + e_scr[1:hi + 1, gd:]

    def pool_double():
        a4_scr[0:hi, :] = a2_scr[0:hi, gd:] + a2_scr[2:hi + 2, gd:]
        a8_scr[0:hi, :] = a4_scr[0:hi, gd:] + a4_scr[4:hi + 4, gd:]

    def pool_group(g):
        w = POOL_WINDOWS[g]
        if g == 0:
            total = rows_at(e_scr, -1, g0) + rows_at(e_scr, 0, g0)
        elif g == 1:
            total = rows_at(a2_scr, -2, g0) + rows_at(a2_scr, 0, g0)
        elif g == 2:
            total = rows_at(a4_scr, -4, g0) + rows_at(a4_scr, 0, g0)
        else:
            total = rows_at(a8_scr, -8, g0) + rows_at(a8_scr, 0, g0)
        pos = i * tm + lax.broadcasted_iota(jnp.int32, (tm, 1), 0)
        cnt = (jnp.minimum(pos + w // 2, seq_len) - jnp.maximum(pos - w // 2, 0)).astype(F32)
        dlt = total / cnt - u_ref[0, :, g * gd:(g + 1) * gd]
        cols = slice(g * gd, (g + 1) * gd)
        return (_dot(dlt.astype(BF16), pw_ref[g]) * ps_ref[:, cols]).astype(BF16)

    pooled = []
    pool_stages = [pool_fill, pool_double] + [functools.partial(pool_group, g)
                                               for g in range(len(POOL_WINDOWS))]

    p_next = probabilities(scores(0))
    for h in range(heads):
        p_cur = p_next
        if h + 1 < heads:
            p_next = probabilities(scores(h + 1))
        weighted_values(h, p_cur)
        stage = h - (heads - len(pool_stages))
        if stage >= 0:
            out = pool_stages[stage]()
            if out is not None:
                pooled.append(out)
        if h % 2 == 1:
            slab = slice((h - 1) * HEAD_DIM, (h + 1) * HEAD_DIM)
            na_ref[0, :, slab] = nat_scr[slab, :].T.astype(BF16)

    pool_ref[0] = jnp.concatenate(pooled, axis=1)


def _mixer(u, qrt, qpt, kr, vt, kc, vct, bias, pool_w_bf, pool_scale):
    b, l, pw = u.shape
    nw = kr.shape[-1]
    gd = pool_w_bf.shape[-1]
    tm = TILE_ROWS * GRID_W
    nt = l // tm
    hb = tm // POOL_HALO
    n_halo = l // POOL_HALO
    nc = kc.shape[1]
    assert pw == len(POOL_WINDOWS) * gd and POOL_HALO == max(POOL_WINDOWS) // 2

    tok = lambda bi, i: (bi, i, 0)
    chan = lambda bi, i: (bi, 0, i)
    const2 = lambda bi, i: (0, 0)
    first_block = lambda i: jnp.clip(i - 1, 0, nt - 3)

    def k_spec(j):
        return pl.BlockSpec((1, tm, nw), lambda bi, i: (bi, first_block(i) + j, 0))

    def vt_spec(j):
        return pl.BlockSpec((1, nw, tm), lambda bi, i: (bi, 0, first_block(i) + j))

    def bias_idx(bi, i):
        return (jnp.where(i == 0, 0, jnp.where(i == nt - 1, 2, 1)), 0, 0, 0)

    in_specs = [
        pl.BlockSpec((1, tm, pw), tok),
        pl.BlockSpec((1, POOL_HALO, pw), lambda bi, i: (bi, jnp.maximum(i * hb - 1, 0), 0)),
        pl.BlockSpec((1, POOL_HALO, pw), lambda bi, i: (bi, jnp.minimum((i + 1) * hb, n_halo - 1), 0)),
        pl.BlockSpec((1, nw, tm), chan),
        pl.BlockSpec((1, nw, tm), chan),
        k_spec(0), k_spec(1), k_spec(2),
        vt_spec(0), vt_spec(1), vt_spec(2),
        pl.BlockSpec((1, nc, nw), lambda bi, i: (bi, 0, 0)),
        pl.BlockSpec((1, nw, nc), lambda bi, i: (bi, 0, 0)),
        pl.BlockSpec((1,) + bias.shape[1:], bias_idx),
        pl.BlockSpec(pool_w_bf.shape, lambda bi, i: (0, 0, 0)),
        pl.BlockSpec((1, pw), const2),
    ]
    n_scr = tm + 2 * POOL_HALO + SUBLANES
    return pl.pallas_call(
        functools.partial(_mixer_kernel, seq_len=l),
        grid=(b, nt),
        in_specs=in_specs,
        out_specs=[pl.BlockSpec((1, tm, nw), tok), pl.BlockSpec((1, tm, pw), tok)],
        out_shape=[jax.ShapeDtypeStruct((b, l, nw), BF16), jax.ShapeDtypeStruct((b, l, pw), BF16)],
        scratch_shapes=[pltpu.VMEM((nw, tm), F32),
                        pltpu.VMEM((n_scr, pw), F32),
                        pltpu.VMEM((n_scr, pw - gd), F32),
                        pltpu.VMEM((n_scr, pw - 2 * gd), F32),
                        pltpu.VMEM((n_scr, pw - 3 * gd), F32)],
        compiler_params=_params(("arbitrary", "arbitrary")),
        name="mixer",
    )(u, u, u, qrt, qpt, kr, kr, kr, vt, vt, vt, kc, vct, bias, pool_w_bf, pool_scale)


def _post_kernel(x_ref, mod_ref, g_ref, na_ref, pool_ref, gp_ref, gn_ref,
                 wbp_ref, wbn_ref, wo_ref, w1_ref, w2_ref, o_ref, *, chunk):
    merged = (gp_ref[0].astype(F32) * _dot(pool_ref[0], wbp_ref[...])
              + gn_ref[0].astype(F32) * _dot(na_ref[0], wbn_ref[...]))
    x = x_ref[0] + mod_ref[0, 2:3, :] * _dot(merged.astype(BF16), wo_ref[...])
    hb = _modulated_norm(x, mod_ref, g_ref, 3).astype(BF16)
    acc = jnp.zeros(x.shape, F32)
    for c0 in range(0, w1_ref.shape[1], chunk):
        a = jnp.maximum(_dot(hb, w1_ref[:, c0:c0 + chunk]), 0.0)
        acc = acc + _dot((a * a).astype(BF16), w2_ref[c0:c0 + chunk, :])
    o_ref[0] = x + mod_ref[0, 5:6, :] * acc


def _post(x, mod3, norm_g, na, pool, gp, gn, wbp, wbn, wo, w1, w2, tm):
    b, l, d = x.shape
    tok = lambda bi, i: (bi, i, 0)
    const = lambda bi, i: (0, 0)
    resident = lambda w: pl.BlockSpec(w.shape, const, pipeline_mode=pl.Buffered(1))
    return pl.pallas_call(
        functools.partial(_post_kernel, chunk=d),
        grid=(b, l // tm),
        in_specs=[pl.BlockSpec((1, tm, d), tok),
                  pl.BlockSpec((1, N_MOD, d), lambda bi, i: (bi, 0, 0)),
                  pl.BlockSpec((1, d), const),
                  pl.BlockSpec((1, tm, na.shape[-1]), tok),
                  pl.BlockSpec((1, tm, pool.shape[-1]), tok),
                  pl.BlockSpec((1, tm, d), tok),
                  pl.BlockSpec((1, tm, d), tok),
                  resident(wbp), resident(wbn), resident(wo), resident(w1), resident(w2)],
        out_specs=pl.BlockSpec((1, tm, d), tok),
        out_shape=jax.ShapeDtypeStruct((b, l, d), F32),
        compiler_params=_params(("arbitrary", "arbitrary")),
        name="post",
    )(x, mod3, norm_g, na, pool, gp, gn, wbp, wbn, wo, w1, w2)


def _rope_tables(seq_len):
    n_freq = HEAD_DIM // 4
    t = np.arange(seq_len)
    inv = (ROPE_THETA ** (-np.arange(n_freq, dtype=np.float32) / n_freq)).astype(np.float32)
    ang_row = (t // GRID_W).astype(np.float32)[:, None] * inv
    ang_col = (t % GRID_W).astype(np.float32)[:, None] * inv
    cr, sr, cc, sc = np.cos(ang_row), np.sin(ang_row), np.cos(ang_col), np.sin(ang_col)
    cos = np.concatenate([cr, cr, cc, cc], axis=1).astype(np.float32)
    sin = np.concatenate([-sr, sr, -sc, sc], axis=1).astype(np.float32)
    return (jnp.asarray(np.tile(cos, (1, 2))), jnp.asarray(np.tile(sin, (1, 2))),
            jnp.asarray(np.ascontiguousarray(cos.T)), jnp.asarray(np.ascontiguousarray(sin.T)))


def _bias_kernel(el_ref, er_ref, o_ref, *, rows):
    lanes = 2 * GRID_W
    kc = lax.broadcasted_iota(jnp.int32, (GRID_W, lanes), 0)
    lane = lax.broadcasted_iota(jnp.int32, (GRID_W, lanes), 1)
    qc = lane % GRID_W
    c0 = jnp.clip(qc - NA_COLS // 2, 0, GRID_W - NA_COLS)
    col_ok = (kc >= c0) & (kc < c0 + NA_COLS)
    left = lane < GRID_W
    masked = jnp.full((GRID_W, lanes), MASK_VALUE, F32)

    cache = {}

    def toeplitz(side, dr):
        if (side, dr) not in cache:
            ref = er_ref if side else el_ref
            dd = dr + NA_ROWS - 1
            vec = jnp.broadcast_to(ref[0, dd:dd + 1, :], (GRID_W, lanes))
            cache[(side, dr)] = pltpu.roll(vec, 0, axis=1, stride=1, stride_axis=0) * LOG2E
        return cache[(side, dr)]

    nt = rows // TILE_ROWS
    for t, tile in enumerate((0, nt // 2, nt - 1)):
        r = tile * TILE_ROWS
        ks = min(max(r - TILE_ROWS, 0), rows - KEY_ROWS)
        for j in range(KEY_ROWS):
            kr = ks + j
            for ip in range(TILE_ROWS // 2):
                halves = []
                for side in range(2):
                    qr = r + 2 * ip + side
                    r0 = min(max(qr - NA_ROWS // 2, 0), rows - NA_ROWS)
                    ok = r0 <= kr < r0 + NA_ROWS
                    halves.append(toeplitz(side, kr - qr) if ok else masked)
                blk = jnp.where(col_ok, jnp.where(left, halves[0], halves[1]), MASK_VALUE)
                o_ref[t, 0, j * GRID_W:(j + 1) * GRID_W, ip * lanes:(ip + 1) * lanes] = blk


def _bias_tables(rpb, rows):
    heads, nr, ncol = rpb.shape
    lanes = 2 * GRID_W
    half = NA_COLS - 1
    zeros = lambda n: jnp.zeros((heads, nr, n), F32)
    flipped = rpb[..., ::-1]
    e_left = jnp.concatenate([flipped[..., half:], zeros(lanes - ncol), flipped[..., :half]], axis=-1)
    e_right = jnp.concatenate([zeros(GRID_W - half), flipped, zeros(lanes - GRID_W + half - ncol)], axis=-1)
    pad = (-nr) % SUBLANES
    e_left = jnp.pad(e_left, ((0, 0), (0, pad), (0, 0)))
    e_right = jnp.pad(e_right, ((0, 0), (0, pad), (0, 0)))
    tq, tk = TILE_ROWS * GRID_W, KEY_ROWS * GRID_W
    vec_spec = pl.BlockSpec((1, nr + pad, lanes), lambda h: (h, 0, 0))
    return pl.pallas_call(
        functools.partial(_bias_kernel, rows=rows),
        grid=(heads,),
        in_specs=[vec_spec, vec_spec],
        out_specs=pl.BlockSpec((3, 1, tk, tq), lambda h: (0, h, 0, 0)),
        out_shape=jax.ShapeDtypeStruct((3, heads, tk, tq), F32),
        compiler_params=_params(("arbitrary",)),
        name="bias_tables",
    )(e_left, e_right)


def kernel(x, c, ctx, c_ctx, ada_w, ada_b, norm1_g, norm2_g, w_in, pool_w, pool_scale,
           q_norm_g, k_norm_g, rpb, w_branch_pool, w_branch_na, w_out, mlp_w1, mlp_w2):
    b, l, d = x.shape
    depth = ada_w.shape[0]
    pw = pool_scale.shape[-1]
    nw = w_branch_na.shape[1]
    heads = nw // HEAD_DIM
    rows = l // GRID_W
    assert depth == 1, "the context-stream update is only needed when another layer follows"
    assert l % (TILE_ROWS * GRID_W) == 0 and rows >= KEY_ROWS
    assert w_in.shape[-1] == pw + 3 * nw + 2 * d and pw == nw
    ctx_row = b
    tm_proj = 512
    tm_in = 1024

    cos_t, sin_t, cos_tt, sin_tt = _rope_tables(l)
    mean_mat = jnp.asarray(np.kron(np.eye(heads), np.full((HEAD_DIM, HEAD_DIM), 1.0 / HEAD_DIM)), BF16)
    pad = (-(b + 1)) % SUBLANES
    cc = jnp.concatenate([c, c_ctx[None], jnp.zeros((pad, d), F32)], axis=0)

    layer = 0
    q0, k0, v0, g0 = pw, pw + nw, pw + 2 * nw, pw + 3 * nw
    mod, w_all, w_qvt = _prep(cc, ada_w[layer], ada_b[layer], w_in[layer], q0, v0, nw)
    mod3 = mod.reshape(mod.shape[0], N_MOD, d)
    qg_tt = jnp.broadcast_to(q_norm_g[layer][:, None], (HEAD_DIM, tm_in))
    kg_t = jnp.tile(k_norm_g[layer], heads)[None]
    n1 = norm1_g[layer][None]

    k_ctx, vt_ctx = _ctx_proj(ctx, mod3, n1, w_all, w_qvt, kg_t, mean_mat, k0 // nw, ctx_row)
    later_weights = (w_branch_pool[layer], w_branch_na[layer], w_out[layer],
                     mlp_w1[layer], mlp_w2[layer])
    (u, qrt, qpt, kr, vt, gp, gn), (wbp, wbn, wo, w1, w2) = _in_proj(
        x, mod3, n1, w_all, w_qvt, cos_t, sin_t, cos_tt, sin_tt, qg_tt, kg_t, mean_mat,
        (0, k0, g0, g0 + d), pw, tm_in, later_weights)
    bias = _bias_tables(rpb[layer], rows)
    na, pool = _mixer(u, qrt, qpt, kr, vt, k_ctx, vt_ctx, bias,
                      pool_w[layer].astype(BF16), pool_scale[layer][None])
    return _post(x, mod3, norm2_g[layer][None], na, pool, gp, gn, wbp, wbn, wo, w1, w2, tm=tm_proj)
```

```python
import functools

import numpy as np
import jax
import jax.numpy as jnp
from jax import lax
from jax.experimental import pallas as pl
from jax.experimental.pallas import tpu as pltpu

GRID_W = 64
N_MOD = 6
POOL_WINDOWS = (2, 4, 8, 16)
HEAD_DIM = 64
NA_ROWS = 8
NA_COLS = 16
ROPE_THETA = 10000.0
EPS = 1e-6
MASK_VALUE = -1e30
LOG2E = 1.4426950408889634

VMEM_LIMIT_BYTES = 56 * 1024 * 1024
SUBLANES = 8
BF16_ROWS = 16
POOL_HALO = 8
TILE_ROWS = 4
KEY_ROWS = TILE_ROWS + NA_ROWS

BF16 = jnp.bfloat16
F32 = jnp.float32


def _dot(a, b):
    return jnp.dot(a, b, preferred_element_type=F32)


def _dot_nt(a, b):
    return lax.dot_general(a, b, (((1,), (1,)), ((), ())), preferred_element_type=F32)


def _params(sem):
    return pltpu.CompilerParams(dimension_semantics=sem, vmem_limit_bytes=VMEM_LIMIT_BYTES)


def _rms(x):
    return x * lax.rsqrt(jnp.mean(x * x, axis=-1, keepdims=True) + EPS)


def _sigmoid_bf16(z):
    zb = z.astype(BF16)
    return 0.5 * jnp.tanh(0.5 * zb) + 0.5


def _prep_kernel(c_ref, aw_ref, ab_ref, w_rows_ref, w_cols_ref, el_ref, er_ref,
                 mod_ref, w_all_ref, w_qvt_ref, bias_ref, *, grid_rows):
    c = c_ref[...]
    s = (c * jax.nn.sigmoid(c)).astype(BF16)
    mod_ref[...] = _dot(s, aw_ref[...].astype(BF16)) + ab_ref[...]
    w_all_ref[...] = w_rows_ref[...].astype(BF16)
    w_qvt_ref[...] = w_cols_ref[...].T.astype(BF16)
    _bias_kernel(el_ref, er_ref, bias_ref, rows=grid_rows)


def _prep(cc, ada_w, ada_b, w, q0, v0, width, rpb, grid_rows):
    m, d = cc.shape
    n = ada_w.shape[1]
    lanes = 128
    heads = rpb.shape[0]
    steps = 2 * width // lanes
    assert steps == heads and n % (steps * lanes) == 0 and w.shape[0] % (steps * BF16_ROWS) == 0
    nb, rows = n // steps, w.shape[0] // steps
    per_seg = width // lanes
    col_block = lambda j: jnp.where(j < per_seg, q0 // lanes + j, v0 // lanes + j - per_seg)
    e_left, e_right = _bias_vectors(rpb)
    vec_spec = pl.BlockSpec((1,) + e_left.shape[1:], lambda j: (j, 0, 0))
    tq, tk = TILE_ROWS * GRID_W, KEY_ROWS * GRID_W
    return pl.pallas_call(
        functools.partial(_prep_kernel, grid_rows=grid_rows),
        grid=(steps,),
        in_specs=[pl.BlockSpec((m, d), lambda j: (0, 0)),
                  pl.BlockSpec((d, nb), lambda j: (0, j)),
                  pl.BlockSpec((1, nb), lambda j: (0, j)),
                  pl.BlockSpec((rows, w.shape[1]), lambda j: (j, 0)),
                  pl.BlockSpec((w.shape[0], lanes), lambda j: (0, col_block(j))),
                  vec_spec, vec_spec],
        out_specs=[pl.BlockSpec((m, nb), lambda j: (0, j)),
                   pl.BlockSpec((rows, w.shape[1]), lambda j: (j, 0)),
                   pl.BlockSpec((lanes, w.shape[0]), lambda j: (j, 0)),
                   pl.BlockSpec((3, 1, tk, tq), lambda j: (0, j, 0, 0))],
        out_shape=[jax.ShapeDtypeStruct((m, n), F32),
                   jax.ShapeDtypeStruct(w.shape, BF16),
                   jax.ShapeDtypeStruct((2 * width, w.shape[0]), BF16),
                   jax.ShapeDtypeStruct((3, heads, tk, tq), F32)],
        compiler_params=_params(("arbitrary",)),
        name="prep",
    )(cc, ada_w, ada_b.reshape(1, n), w, w, e_left, e_right)


def _head_norm(a, mean_mat, g):
    ms = _dot((a * a).astype(BF16), mean_mat)
    return a * lax.rsqrt(ms + EPS) * g


def _rope(a, cos, sin_signed, first_half):
    width = a.shape[-1]
    quarter = HEAD_DIM // 4
    up = pltpu.roll(a, width - quarter, axis=1)
    down = pltpu.roll(a, quarter, axis=1)
    swapped = jnp.where(first_half, up, down)
    return a * cos + swapped * sin_signed


def _modulated_norm(x, mod_ref, g_ref, shift_row):
    return _rms(x) * g_ref[...] * (1.0 + mod_ref[0, shift_row + 1:shift_row + 2, :]) \
        + mod_ref[0, shift_row:shift_row + 1, :]


def _ctx_kernel(x_ref, mod_ref, g_ref, wk_ref, wvt_ref, kg_ref, mean_ref, k_ref, vt_ref):
    hb = _modulated_norm(x_ref[0], mod_ref, g_ref, 0).astype(BF16)
    k_ref[0] = _head_norm(_dot(hb, wk_ref[...]), mean_ref[...], kg_ref[...]).astype(BF16)
    vt_ref[0] = _dot_nt(wvt_ref[...], hb).astype(BF16)


def _ctx_proj(ctx, mod3, norm_g, w_all, w_qvt, kg_t, mean_mat, key_block, ctx_row):
    b, n, d = ctx.shape
    nw = kg_t.shape[-1]
    sd = jax.ShapeDtypeStruct
    return pl.pallas_call(
        _ctx_kernel,
        grid=(b,),
        in_specs=[pl.BlockSpec((1, n, d), lambda i: (i, 0, 0)),
                  pl.BlockSpec((1, N_MOD, d), lambda i: (ctx_row, 0, 0)),
                  pl.BlockSpec((1, d), lambda i: (0, 0)),
                  pl.BlockSpec((d, nw), lambda i: (0, key_block)),
                  pl.BlockSpec((nw, d), lambda i: (1, 0)),
                  pl.BlockSpec((1, nw), lambda i: (0, 0)),
                  pl.BlockSpec((nw, nw), lambda i: (0, 0))],
        out_specs=[pl.BlockSpec((1, n, nw), lambda i: (i, 0, 0)),
                   pl.BlockSpec((1, nw, n), lambda i: (i, 0, 0))],
        out_shape=[sd((b, n, nw), BF16), sd((b, nw, n), BF16)],
        compiler_params=_params(("arbitrary",)),
        name="ctx_proj",
    )(ctx, mod3, norm_g, w_all, w_qvt, kg_t, mean_mat)


def _in_proj_kernel(*refs, col_starts, n_cast):
    (x_ref, mod_ref, g_ref, w_ref, wt_ref, cos_ref, sin_ref, cost_ref, sint_ref,
     qgt_ref, kg_ref, mean_ref) = refs[:12]
    cast_in = refs[12:12 + n_cast]
    u_ref, qrt_ref, qpt_ref, kr_ref, vt_ref, gp_ref, gn_ref = refs[12 + n_cast:19 + n_cast]
    cast_out = refs[19 + n_cast:]
    for src, dst in zip(cast_in, cast_out):
        dst[...] = src[...].astype(BF16)

    pw = u_ref.shape[-1]
    nw = kr_ref.shape[-1]
    d = x_ref.shape[-1]
    heads = nw // HEAD_DIM
    quarter = HEAD_DIM // 4
    scale = HEAD_DIM ** -0.5 * LOG2E
    u0, k0, gp0, gn0 = col_starts

    hb = _modulated_norm(x_ref[0], mod_ref, g_ref, 0).astype(BF16)

    cost, sint, qgt = cost_ref[...], sint_ref[...], qgt_ref[...]

    def query_heads(first, last):
        qt = _dot_nt(wt_ref[first * HEAD_DIM:last * HEAD_DIM, :], hb)
        for h in range(first, last):
            rows = slice(h * HEAD_DIM, (h + 1) * HEAD_DIM)
            z = qt[(h - first) * HEAD_DIM:(h - first + 1) * HEAD_DIM, :]
            qn = z * lax.rsqrt(jnp.mean(z * z, axis=0, keepdims=True) + EPS) * (qgt * scale)
            swapped = jnp.concatenate([qn[quarter:2 * quarter], qn[0:quarter],
                                       qn[3 * quarter:], qn[2 * quarter:3 * quarter]], axis=0)
            qpt_ref[0, rows, :] = qn.astype(BF16)
            qrt_ref[0, rows, :] = (qn * cost + swapped * sint).astype(BF16)

    z_gp = _dot(hb, w_ref[:, gp0:gp0 + d])
    query_heads(0, heads // 2)
    z_gn = _dot(hb, w_ref[:, gn0:gn0 + d])
    gp_ref[0] = _sigmoid_bf16(z_gp)
    query_heads(heads // 2, heads)
    z_k = _dot(hb, w_ref[:, k0:k0 + nw])
    gn_ref[0] = _sigmoid_bf16(z_gn)
    z_vt = _dot_nt(wt_ref[nw:, :], hb)

    reps = nw // cos_ref.shape[-1]
    cos = jnp.concatenate([cos_ref[...]] * reps, axis=1)
    sin = jnp.concatenate([sin_ref[...]] * reps, axis=1)
    lane = lax.broadcasted_iota(jnp.int32, (1, nw), 1)
    first_half = (lane % (HEAD_DIM // 2)) < quarter
    k = _head_norm(z_k, mean_ref[...], kg_ref[...])
    kr_ref[0] = _rope(k, cos, sin, first_half).astype(BF16)
    vt_ref[0] = z_vt.astype(BF16)
    u_ref[0] = _dot(hb, w_ref[:, u0:u0 + pw])


def _in_proj(x, mod3, norm_g, w_all, w_qvt, cos_t, sin_t, cos_tt, sin_tt, qg_tt, kg_t, mean_mat,
             col_starts, pw, tm, later_weights):
    b, l, d = x.shape
    nw = kg_t.shape[-1]
    tw = cos_t.shape[-1]
    nt = l // tm
    tok = lambda bi, i: (bi, i, 0)
    chan = lambda bi, i: (bi, 0, i)
    const = lambda bi, i: (0, 0)
    sd = jax.ShapeDtypeStruct
    slab_specs = []
    for w in later_weights:
        assert w.shape[0] % (b * nt * BF16_ROWS) == 0
        slab_specs.append(pl.BlockSpec((w.shape[0] // (b * nt), w.shape[1]),
                                       lambda bi, i: (bi * nt + i, 0)))
    outs = pl.pallas_call(
        functools.partial(_in_proj_kernel, col_starts=col_starts, n_cast=len(later_weights)),
        grid=(b, nt),
        in_specs=[pl.BlockSpec((1, tm, d), tok),
                  pl.BlockSpec((1, N_MOD, d), lambda bi, i: (bi, 0, 0)),
                  pl.BlockSpec((1, d), const),
                  pl.BlockSpec(w_all.shape, const, pipeline_mode=pl.Buffered(1)),
                  pl.BlockSpec(w_qvt.shape, const, pipeline_mode=pl.Buffered(1)),
                  pl.BlockSpec((tm, tw), lambda bi, i: (i, 0)),
                  pl.BlockSpec((tm, tw), lambda bi, i: (i, 0)),
                  pl.BlockSpec((HEAD_DIM, tm), lambda bi, i: (0, i)),
                  pl.BlockSpec((HEAD_DIM, tm), lambda bi, i: (0, i)),
                  pl.BlockSpec((HEAD_DIM, tm), const),
                  pl.BlockSpec((1, nw), const),
                  pl.BlockSpec((nw, nw), const)] + slab_specs,
        out_specs=[pl.BlockSpec((1, tm, pw), tok),
                   pl.BlockSpec((1, nw, tm), chan), pl.BlockSpec((1, nw, tm), chan),
                   pl.BlockSpec((1, tm, nw), tok),
                   pl.BlockSpec((1, nw, tm), chan),
                   pl.BlockSpec((1, tm, d), tok), pl.BlockSpec((1, tm, d), tok)] + slab_specs,
        out_shape=[sd((b, l, pw), F32),
                   sd((b, nw, l), BF16), sd((b, nw, l), BF16),
                   sd((b, l, nw), BF16),
                   sd((b, nw, l), BF16),
                   sd((b, l, d), BF16), sd((b, l, d), BF16)]
                  + [sd(w.shape, BF16) for w in later_weights],
        compiler_params=_params(("arbitrary", "arbitrary")),
        name="in_proj",
    )(x, mod3, norm_g, w_all, w_qvt, cos_t, sin_t, cos_tt, sin_tt, qg_tt, kg_t, mean_mat,
      *later_weights)
    return outs[:7], outs[7:]


def _mixer_kernel(u_ref, up_ref, un_ref, qrt_ref, qpt_ref,
                  k0_ref, k1_ref, k2_ref, vt0_ref, vt1_ref, vt2_ref, kc_ref, vct_ref, bias_ref,
                  pw_ref, ps_ref,
                  na_ref, pool_ref, nat_scr, e_scr, a2_scr, a4_scr, a8_scr, *, seq_len):
    i = pl.program_id(1)
    tm = u_ref.shape[1]
    nw = qrt_ref.shape[1]
    heads = nw // HEAD_DIM
    pair = 2 * HEAD_DIM

    zeros = jnp.zeros((HEAD_DIM, tm), BF16)
    ones = jnp.ones((BF16_ROWS, tm), BF16)
    k_refs = (k0_ref, k1_ref, k2_ref)
    vt_refs = (vt0_ref, vt1_ref, vt2_ref)

    def head_rows(h):
        return slice(h * HEAD_DIM, (h + 1) * HEAD_DIM)

    def scores(h):
        rows = head_rows(h)
        lanes = slice((h // 2) * pair, (h // 2 + 1) * pair)
        q_rot, q_plain = qrt_ref[0, rows, :], qpt_ref[0, rows, :]
        if h % 2 == 0:
            w_rot = jnp.concatenate([q_rot, zeros], axis=0)
            w_plain = jnp.concatenate([q_plain, zeros], axis=0)
        else:
            w_rot = jnp.concatenate([zeros, q_rot], axis=0)
            w_plain = jnp.concatenate([zeros, q_plain], axis=0)
        s = [(_dot(k_refs[j][0, :, lanes], w_rot) + bias_ref[0, h, j * tm:(j + 1) * tm, :]).astype(BF16)
             for j in range(3)]
        s.append(_dot(kc_ref[0, :, lanes], w_plain).astype(BF16))
        m = s[0].max(axis=0, keepdims=True)
        for sj in s[1:]:
            m = jnp.maximum(m, sj.max(axis=0, keepdims=True))
        return s, m

    def probabilities(s_and_m):
        s, m = s_and_m
        return [jnp.exp2(sj - m) for sj in s]

    def weighted_values(h, p):
        rows = head_rows(h)
        ot = _dot(jnp.concatenate([vct_ref[0, rows, :], ones], axis=0), p[3])
        for j in range(3):
            ot = ot + _dot(jnp.concatenate([vt_refs[j][0, rows, :], ones], axis=0), p[j])
        nat_scr[rows, :] = ot[:HEAD_DIM] / ot[HEAD_DIM:HEAD_DIM + 1]

    gd = pw_ref.shape[-1]
    hi = tm + 2 * POOL_HALO
    first = POOL_HALO
    g0 = slice(0, gd)

    def rows_at(ref, off, cols):
        return ref[first + off:first + off + tm, cols]

    def pool_fill():
        for scr in (e_scr, a2_scr, a4_scr):
            scr[hi:hi + SUBLANES, :] = jnp.zeros((SUBLANES, scr.shape[1]), F32)
        e_scr[0:first, :] = jnp.where(i > 0, up_ref[0], 0.0)
        e_scr[first:first + tm, :] = u_ref[0]
        e_scr[first + tm:hi, :] = jnp.where((i + 1) * tm < seq_len, un_ref[0], 0.0)
        a2_scr[0:hi, :] = e_scr[0:hi, gd:] + e_scr[1:hi + 1, gd:]

    def pool_double():
        a4_scr[0:hi, :] = a2_scr[0:hi, gd:] + a2_scr[2:hi + 2, gd:]
        a8_scr[0:hi, :] = a4_scr[0:hi, gd:] + a4_scr[4:hi + 4, gd:]

    def pool_group(g):
        w = POOL_WINDOWS[g]
        if g == 0:
            total = rows_at(e_scr, -1, g0) + rows_at(e_scr, 0, g0)
        elif g == 1:
            total = rows_at(a2_scr, -2, g0) + rows_at(a2_scr, 0, g0)
        elif g == 2:
            total = rows_at(a4_scr, -4, g0) + rows_at(a4_scr, 0, g0)
        else:
            total = rows_at(a8_scr, -8, g0) + rows_at(a8_scr, 0, g0)
        pos = i * tm + lax.broadcasted_iota(jnp.int32, (tm, 1), 0)
        cnt = (jnp.minimum(pos + w // 2, seq_len) - jnp.maximum(pos - w // 2, 0)).astype(F32)
        dlt = total / cnt - u_ref[0, :, g * gd:(g + 1) * gd]
        cols = slice(g * gd, (g + 1) * gd)
        return (_dot(dlt.astype(BF16), pw_ref[g]) * ps_ref[:, cols]).astype(BF16)

    pooled = []
    pool_stages = [pool_fill, pool_double] + [functools.partial(pool_group, g)
                                               for g in range(len(POOL_WINDOWS))]

    p_next = probabilities(scores(0))
    for h in range(heads):
        p_cur = p_next
        if h + 1 < heads:
            p_next = probabilities(scores(h + 1))
        weighted_values(h, p_cur)
        stage = h - (heads - len(pool_stages))
        if stage >= 0:
            out = pool_stages[stage]()
            if out is not None:
                pooled.append(out)
        if h % 2 == 1:
            slab = slice((h - 1) * HEAD_DIM, (h + 1) * HEAD_DIM)
            na_ref[0, :, slab] = nat_scr[slab, :].T.astype(BF16)

    pool_ref[0] = jnp.concatenate(pooled, axis=1)


def _mixer(u, qrt, qpt, kr, vt, kc, vct, bias, pool_w_bf, pool_scale):
    b, l, pw = u.shape
    nw = kr.shape[-1]
    gd = pool_w_bf.shape[-1]
    tm = TILE_ROWS * GRID_W
    nt = l // tm
    hb = tm // POOL_HALO
    n_halo = l // POOL_HALO
    nc = kc.shape[1]
    assert pw == len(POOL_WINDOWS) * gd and POOL_HALO == max(POOL_WINDOWS) // 2

    tok = lambda bi, i: (bi, i, 0)
    chan = lambda bi, i: (bi, 0, i)
    const2 = lambda bi, i: (0, 0)
    first_block = lambda i: jnp.clip(i - 1, 0, nt - 3)

    def k_spec(j):
        return pl.BlockSpec((1, tm, nw), lambda bi, i: (bi, first_block(i) + j, 0))

    def vt_spec(j):
        return pl.BlockSpec((1, nw, tm), lambda bi, i: (bi, 0, first_block(i) + j))

    def bias_idx(bi, i):
        return (jnp.where(i == 0, 0, jnp.where(i == nt - 1, 2, 1)), 0, 0, 0)

    in_specs = [
        pl.BlockSpec((1, tm, pw), tok),
        pl.BlockSpec((1, POOL_HALO, pw), lambda bi, i: (bi, jnp.maximum(i * hb - 1, 0), 0)),
        pl.BlockSpec((1, POOL_HALO, pw), lambda bi, i: (bi, jnp.minimum((i + 1) * hb, n_halo - 1), 0)),
        pl.BlockSpec((1, nw, tm), chan),
        pl.BlockSpec((1, nw, tm), chan),
        k_spec(0), k_spec(1), k_spec(2),
        vt_spec(0), vt_spec(1), vt_spec(2),
        pl.BlockSpec((1, nc, nw), lambda bi, i: (bi, 0, 0)),
        pl.BlockSpec((1, nw, nc), lambda bi, i: (bi, 0, 0)),
        pl.BlockSpec((1,) + bias.shape[1:], bias_idx),
        pl.BlockSpec(pool_w_bf.shape, lambda bi, i: (0, 0, 0)),
        pl.BlockSpec((1, pw), const2),
    ]
    n_scr = tm + 2 * POOL_HALO + SUBLANES
    return pl.pallas_call(
        functools.partial(_mixer_kernel, seq_len=l),
        grid=(b, nt),
        in_specs=in_specs,
        out_specs=[pl.BlockSpec((1, tm, nw), tok), pl.BlockSpec((1, tm, pw), tok)],
        out_shape=[jax.ShapeDtypeStruct((b, l, nw), BF16), jax.ShapeDtypeStruct((b, l, pw), BF16)],
        scratch_shapes=[pltpu.VMEM((nw, tm), F32),
                        pltpu.VMEM((n_scr, pw), F32),
                        pltpu.VMEM((n_scr, pw - gd), F32),
                        pltpu.VMEM((n_scr, pw - 2 * gd), F32),
                        pltpu.VMEM((n_scr, pw - 3 * gd), F32)],
        compiler_params=_params(("arbitrary", "arbitrary")),
        name="mixer",
    )(u, u, u, qrt, qpt, kr, kr, kr, vt, vt, vt, kc, vct, bias, pool_w_bf, pool_scale)


def _post_kernel(x_ref, mod_ref, g_ref, na_ref, pool_ref, gp_ref, gn_ref,
                 wbp_ref, wbn_ref, wo_ref, w1_ref, w2_ref, o_ref, *, chunk):
    merged = (gp_ref[0].astype(F32) * _dot(pool_ref[0], wbp_ref[...])
              + gn_ref[0].astype(F32) * _dot(na_ref[0], wbn_ref[...]))
    x = x_ref[0] + mod_ref[0, 2:3, :] * _dot(merged.astype(BF16), wo_ref[...])
    hb = _modulated_norm(x, mod_ref, g_ref, 3).astype(BF16)
    acc = jnp.zeros(x.shape, F32)
    for c0 in range(0, w1_ref.shape[1], chunk):
        a = jnp.maximum(_dot(hb, w1_ref[:, c0:c0 + chunk]), 0.0)
        acc = acc + _dot((a * a).astype(BF16), w2_ref[c0:c0 + chunk, :])
    o_ref[0] = x + mod_ref[0, 5:6, :] * acc


def _post(x, mod3, norm_g, na, pool, gp, gn, wbp, wbn, wo, w1, w2, tm):
    b, l, d = x.shape
    tok = lambda bi, i: (bi, i, 0)
    const = lambda bi, i: (0, 0)
    resident = lambda w: pl.BlockSpec(w.shape, const, pipeline_mode=pl.Buffered(1))
    return pl.pallas_call(
        functools.partial(_post_kernel, chunk=d),
        grid=(b, l // tm),
        in_specs=[pl.BlockSpec((1, tm, d), tok),
                  pl.BlockSpec((1, N_MOD, d), lambda bi, i: (bi, 0, 0)),
                  pl.BlockSpec((1, d), const),
                  pl.BlockSpec((1, tm, na.shape[-1]), tok),
                  pl.BlockSpec((1, tm, pool.shape[-1]), tok),
                  pl.BlockSpec((1, tm, d), tok),
                  pl.BlockSpec((1, tm, d), tok),
                  resident(wbp), resident(wbn), resident(wo), resident(w1), resident(w2)],
        out_specs=pl.BlockSpec((1, tm, d), tok),
        out_shape=jax.ShapeDtypeStruct((b, l, d), F32),
        compiler_params=_params(("arbitrary", "arbitrary")),
        name="post",
    )(x, mod3, norm_g, na, pool, gp, gn, wbp, wbn, wo, w1, w2)


def _rope_tables(seq_len):
    n_freq = HEAD_DIM // 4
    t = np.arange(seq_len)
    inv = (ROPE_THETA ** (-np.arange(n_freq, dtype=np.float32) / n_freq)).astype(np.float32)
    ang_row = (t // GRID_W).astype(np.float32)[:, None] * inv
    ang_col = (t % GRID_W).astype(np.float32)[:, None] * inv
    cr, sr, cc, sc = np.cos(ang_row), np.sin(ang_row), np.cos(ang_col), np.sin(ang_col)
    cos = np.concatenate([cr, cr, cc, cc], axis=1).astype(np.float32)
    sin = np.concatenate([-sr, sr, -sc, sc], axis=1).astype(np.float32)
    return (jnp.asarray(np.tile(cos, (1, 2))), jnp.asarray(np.tile(sin, (1, 2))),
            jnp.asarray(np.ascontiguousarray(cos.T)), jnp.asarray(np.ascontiguousarray(sin.T)))


def _bias_kernel(el_ref, er_ref, o_ref, *, rows):
    lanes = 2 * GRID_W
    kc = lax.broadcasted_iota(jnp.int32, (GRID_W, lanes), 0)
    lane = lax.broadcasted_iota(jnp.int32, (GRID_W, lanes), 1)
    qc = lane % GRID_W
    c0 = jnp.clip(qc - NA_COLS // 2, 0, GRID_W - NA_COLS)
    col_ok = (kc >= c0) & (kc < c0 + NA_COLS)
    left = lane < GRID_W
    masked = jnp.full((GRID_W, lanes), MASK_VALUE, F32)

    cache = {}

    def toeplitz(side, dr):
        if (side, dr) not in cache:
            ref = er_ref if side else el_ref
            dd = dr + NA_ROWS - 1
            vec = jnp.broadcast_to(ref[0, dd:dd + 1, :], (GRID_W, lanes))
            cache[(side, dr)] = pltpu.roll(vec, 0, axis=1, stride=1, stride_axis=0) * LOG2E
        return cache[(side, dr)]

    nt = rows // TILE_ROWS
    for t, tile in enumerate((0, nt // 2, nt - 1)):
        r = tile * TILE_ROWS
        ks = min(max(r - TILE_ROWS, 0), rows - KEY_ROWS)
        for j in range(KEY_ROWS):
            kr = ks + j
            for ip in range(TILE_ROWS // 2):
                halves = []
                for side in range(2):
                    qr = r + 2 * ip + side
                    r0 = min(max(qr - NA_ROWS // 2, 0), rows - NA_ROWS)
                    ok = r0 <= kr < r0 + NA_ROWS
                    halves.append(toeplitz(side, kr - qr) if ok else masked)
                blk = jnp.where(col_ok, jnp.where(left, halves[0], halves[1]), MASK_VALUE)
                o_ref[t, 0, j * GRID_W:(j + 1) * GRID_W, ip * lanes:(ip + 1) * lanes] = blk


def _bias_vectors(rpb):
    heads, nr, ncol = rpb.shape
    lanes = 2 * GRID_W
    half = NA_COLS - 1
    zeros = lambda n: jnp.zeros((heads, nr, n), F32)
    flipped = rpb[..., ::-1]
    e_left = jnp.concatenate([flipped[..., half:], zeros(lanes - ncol), flipped[..., :half]], axis=-1)
    e_right = jnp.concatenate([zeros(GRID_W - half), flipped, zeros(lanes - GRID_W + half - ncol)], axis=-1)
    pad = (-nr) % SUBLANES
    e_left = jnp.pad(e_left, ((0, 0), (0, pad), (0, 0)))
    e_right = jnp.pad(e_right, ((0, 0), (0, pad), (0, 0)))
    return e_left, e_right


def kernel(x, c, ctx, c_ctx, ada_w, ada_b, norm1_g, norm2_g, w_in, pool_w, pool_scale,
           q_norm_g, k_norm_g, rpb, w_branch_pool, w_branch_na, w_out, mlp_w1, mlp_w2):
    b, l, d = x.shape
    depth = ada_w.shape[0]
    pw = pool_scale.shape[-1]
    nw = w_branch_na.shape[1]
    heads = nw // HEAD_DIM
    rows = l // GRID_W
    assert depth == 1, "the context-stream update is only needed when another layer follows"
    assert l % (TILE_ROWS * GRID_W) == 0 and rows >= KEY_ROWS
    assert w_in.shape[-1] == pw + 3 * nw + 2 * d and pw == nw
    ctx_row = b
    tm_proj = 512
    tm_in = 1024

    cos_t, sin_t, cos_tt, sin_tt = _rope_tables(l)
    mean_mat = jnp.asarray(np.kron(np.eye(heads), np.full((HEAD_DIM, HEAD_DIM), 1.0 / HEAD_DIM)), BF16)
    pad = (-(b + 1)) % SUBLANES
    cc = jnp.concatenate([c, c_ctx[None], jnp.zeros((pad, d), F32)], axis=0)

    layer = 0
    q0, k0, v0, g0 = pw, pw + nw, pw + 2 * nw, pw + 3 * nw
    mod, w_all, w_qvt, bias = _prep(cc, ada_w[layer], ada_b[layer], w_in[layer], q0, v0, nw,
                                    rpb[layer], rows)
    mod3 = mod.reshape(mod.shape[0], N_MOD, d)
    qg_tt = jnp.broadcast_to(q_norm_g[layer][:, None], (HEAD_DIM, tm_in))
    kg_t = jnp.tile(k_norm_g[layer], heads)[None]
    n1 = norm1_g[layer][None]

    k_ctx, vt_ctx = _ctx_proj(ctx, mod3, n1, w_all, w_qvt, kg_t, mean_mat, k0 // nw, ctx_row)
    later_weights = (w_branch_pool[layer], w_branch_na[layer], w_out[layer],
                     mlp_w1[layer], mlp_w2[layer])
    (u, qrt, qpt, kr, vt, gp, gn), (wbp, wbn, wo, w1, w2) = _in_proj(
        x, mod3, n1, w_all, w_qvt, cos_t, sin_t, cos_tt, sin_tt, qg_tt, kg_t, mean_mat,
        (0, k0, g0, g0 + d), pw, tm_in, later_weights)
    na, pool = _mixer(u, qrt, qpt, kr, vt, k_ctx, vt_ctx, bias,
                      pool_w[layer].astype(BF16), pool_scale[layer][None])
    return _post(x, mod3, norm2_g[layer][None], na, pool, gp, gn, wbp, wbn, wo, w1, w2, tm=tm_proj)
```

```python
import functools

import numpy as np
import jax
import jax.numpy as jnp
from jax import lax
from jax.experimental import pallas as pl
from jax.experimental.pallas import tpu as pltpu

GRID_W = 64
N_MOD = 6
POOL_WINDOWS = (2, 4, 8, 16)
HEAD_DIM = 64
NA_ROWS = 8
NA_COLS = 16
ROPE_THETA = 10000.0
EPS = 1e-6
MASK_VALUE = -1e30
LOG2E = 1.4426950408889634

VMEM_LIMIT_BYTES = 56 * 1024 * 1024
SUBLANES = 8
BF16_ROWS = 16
POOL_HALO = 8
TILE_ROWS = 4
KEY_ROWS = TILE_ROWS + NA_ROWS
SUB_TILES = 2
KV_BLOCKS = KEY_ROWS // TILE_ROWS + SUB_TILES - 1

BF16 = jnp.bfloat16
F32 = jnp.float32


def _dot(a, b):
    return jnp.dot(a, b, preferred_element_type=F32)


def _dot_nt(a, b):
    return lax.dot_general(a, b, (((1,), (1,)), ((), ())), preferred_element_type=F32)


def _params(sem):
    return pltpu.CompilerParams(dimension_semantics=sem, vmem_limit_bytes=VMEM_LIMIT_BYTES)


def _rms(x):
    return x * lax.rsqrt(jnp.mean(x * x, axis=-1, keepdims=True) + EPS)


def _sigmoid_bf16(z):
    zb = z.astype(BF16)
    return 0.5 * jnp.tanh(0.5 * zb) + 0.5


def _prep_kernel(c_ref, aw_ref, ab_ref, w_rows_ref, w_cols_ref, el_ref, er_ref,
                 mod_ref, w_all_ref, w_qvt_ref, bias_ref, *, grid_rows):
    c = c_ref[...]
    s = (c * jax.nn.sigmoid(c)).astype(BF16)
    mod_ref[...] = _dot(s, aw_ref[...].astype(BF16)) + ab_ref[...]
    w_all_ref[...] = w_rows_ref[...].astype(BF16)
    w_qvt_ref[...] = w_cols_ref[...].T.astype(BF16)
    _bias_kernel(el_ref, er_ref, bias_ref, rows=grid_rows)


def _prep(cc, ada_w, ada_b, w, q0, v0, width, rpb, grid_rows):
    m, d = cc.shape
    n = ada_w.shape[1]
    lanes = 128
    heads = rpb.shape[0]
    steps = 2 * width // lanes
    assert steps == heads and n % (steps * lanes) == 0 and w.shape[0] % (steps * BF16_ROWS) == 0
    nb, rows = n // steps, w.shape[0] // steps
    per_seg = width // lanes
    col_block = lambda j: jnp.where(j < per_seg, q0 // lanes + j, v0 // lanes + j - per_seg)
    e_left, e_right = _bias_vectors(rpb)
    vec_spec = pl.BlockSpec((1,) + e_left.shape[1:], lambda j: (j, 0, 0))
    tq, tk = TILE_ROWS * GRID_W, KEY_ROWS * GRID_W
    return pl.pallas_call(
        functools.partial(_prep_kernel, grid_rows=grid_rows),
        grid=(steps,),
        in_specs=[pl.BlockSpec((m, d), lambda j: (0, 0)),
                  pl.BlockSpec((d, nb), lambda j: (0, j)),
                  pl.BlockSpec((1, nb), lambda j: (0, j)),
                  pl.BlockSpec((rows, w.shape[1]), lambda j: (j, 0)),
                  pl.BlockSpec((w.shape[0], lanes), lambda j: (0, col_block(j))),
                  vec_spec, vec_spec],
        out_specs=[pl.BlockSpec((m, nb), lambda j: (0, j)),
                   pl.BlockSpec((rows, w.shape[1]), lambda j: (j, 0)),
                   pl.BlockSpec((lanes, w.shape[0]), lambda j: (j, 0)),
                   pl.BlockSpec((3, 1, tk, tq), lambda j: (0, j, 0, 0))],
        out_shape=[jax.ShapeDtypeStruct((m, n), F32),
                   jax.ShapeDtypeStruct(w.shape, BF16),
                   jax.ShapeDtypeStruct((2 * width, w.shape[0]), BF16),
                   jax.ShapeDtypeStruct((3, heads, tk, tq), F32)],
        compiler_params=_params(("arbitrary",)),
        name="prep",
    )(cc, ada_w, ada_b.reshape(1, n), w, w, e_left, e_right)


def _head_norm(a, mean_mat, g):
    ms = _dot((a * a).astype(BF16), mean_mat)
    return a * lax.rsqrt(ms + EPS) * g


def _rope(a, cos, sin_signed, first_half):
    width = a.shape[-1]
    quarter = HEAD_DIM // 4
    up = pltpu.roll(a, width - quarter, axis=1)
    down = pltpu.roll(a, quarter, axis=1)
    swapped = jnp.where(first_half, up, down)
    return a * cos + swapped * sin_signed


def _modulated_norm(x, mod_ref, g_ref, shift_row):
    return _rms(x) * g_ref[...] * (1.0 + mod_ref[0, shift_row + 1:shift_row + 2, :]) \
        + mod_ref[0, shift_row:shift_row + 1, :]


def _ctx_kernel(x_ref, mod_ref, g_ref, wk_ref, wvt_ref, kg_ref, mean_ref, k_ref, vt_ref):
    hb = _modulated_norm(x_ref[0], mod_ref, g_ref, 0).astype(BF16)
    k_ref[0] = _head_norm(_dot(hb, wk_ref[...]), mean_ref[...], kg_ref[...]).astype(BF16)
    vt_ref[0] = _dot_nt(wvt_ref[...], hb).astype(BF16)


def _ctx_proj(ctx, mod3, norm_g, w_all, w_qvt, kg_t, mean_mat, key_block, ctx_row):
    b, n, d = ctx.shape
    nw = kg_t.shape[-1]
    sd = jax.ShapeDtypeStruct
    return pl.pallas_call(
        _ctx_kernel,
        grid=(b,),
        in_specs=[pl.BlockSpec((1, n, d), lambda i: (i, 0, 0)),
                  pl.BlockSpec((1, N_MOD, d), lambda i: (ctx_row, 0, 0)),
                  pl.BlockSpec((1, d), lambda i: (0, 0)),
                  pl.BlockSpec((d, nw), lambda i: (0, key_block)),
                  pl.BlockSpec((nw, d), lambda i: (1, 0)),
                  pl.BlockSpec((1, nw), lambda i: (0, 0)),
                  pl.BlockSpec((nw, nw), lambda i: (0, 0))],
        out_specs=[pl.BlockSpec((1, n, nw), lambda i: (i, 0, 0)),
                   pl.BlockSpec((1, nw, n), lambda i: (i, 0, 0))],
        out_shape=[sd((b, n, nw), BF16), sd((b, nw, n), BF16)],
        compiler_params=_params(("arbitrary",)),
        name="ctx_proj",
    )(ctx, mod3, norm_g, w_all, w_qvt, kg_t, mean_mat)


def _in_proj_kernel(*refs, col_starts, n_cast):
    (x_ref, mod_ref, g_ref, w_ref, wt_ref, cos_ref, sin_ref, cost_ref, sint_ref,
     qgt_ref, kg_ref, mean_ref) = refs[:12]
    cast_in = refs[12:12 + n_cast]
    u_ref, qrt_ref, qpt_ref, kr_ref, vt_ref, gp_ref, gn_ref = refs[12 + n_cast:19 + n_cast]
    cast_out = refs[19 + n_cast:]
    for src, dst in zip(cast_in, cast_out):
        dst[...] = src[...].astype(BF16)

    pw = u_ref.shape[-1]
    nw = kr_ref.shape[-1]
    d = x_ref.shape[-1]
    heads = nw // HEAD_DIM
    quarter = HEAD_DIM // 4
    scale = HEAD_DIM ** -0.5 * LOG2E
    u0, k0, gp0, gn0 = col_starts

    hb = _modulated_norm(x_ref[0], mod_ref, g_ref, 0).astype(BF16)

    cost, sint, qgt = cost_ref[...], sint_ref[...], qgt_ref[...]

    def query_heads(first, last):
        qt = _dot_nt(wt_ref[first * HEAD_DIM:last * HEAD_DIM, :], hb)
        for h in range(first, last):
            rows = slice(h * HEAD_DIM, (h + 1) * HEAD_DIM)
            z = qt[(h - first) * HEAD_DIM:(h - first + 1) * HEAD_DIM, :]
            qn = z * lax.rsqrt(jnp.mean(z * z, axis=0, keepdims=True) + EPS) * (qgt * scale)
            swapped = jnp.concatenate([qn[quarter:2 * quarter], qn[0:quarter],
                                       qn[3 * quarter:], qn[2 * quarter:3 * quarter]], axis=0)
            qpt_ref[0, rows, :] = qn.astype(BF16)
            qrt_ref[0, rows, :] = (qn * cost + swapped * sint).astype(BF16)

    z_gp = _dot(hb, w_ref[:, gp0:gp0 + d])
    query_heads(0, heads // 2)
    z_gn = _dot(hb, w_ref[:, gn0:gn0 + d])
    gp_ref[0] = _sigmoid_bf16(z_gp)
    query_heads(heads // 2, heads)
    z_k = _dot(hb, w_ref[:, k0:k0 + nw])
    gn_ref[0] = _sigmoid_bf16(z_gn)
    z_vt = _dot_nt(wt_ref[nw:, :], hb)

    reps = nw // cos_ref.shape[-1]
    cos = jnp.concatenate([cos_ref[...]] * reps, axis=1)
    sin = jnp.concatenate([sin_ref[...]] * reps, axis=1)
    lane = lax.broadcasted_iota(jnp.int32, (1, nw), 1)
    first_half = (lane % (HEAD_DIM // 2)) < quarter
    k = _head_norm(z_k, mean_ref[...], kg_ref[...])
    kr_ref[0] = _rope(k, cos, sin, first_half).astype(BF16)
    vt_ref[0] = z_vt.astype(BF16)
    u_ref[0] = _dot(hb, w_ref[:, u0:u0 + pw])


def _in_proj(x, mod3, norm_g, w_all, w_qvt, cos_t, sin_t, cos_tt, sin_tt, qg_tt, kg_t, mean_mat,
             col_starts, pw, tm, later_weights):
    b, l, d = x.shape
    nw = kg_t.shape[-1]
    tw = cos_t.shape[-1]
    nt = l // tm
    tok = lambda bi, i: (bi, i, 0)
    chan = lambda bi, i: (bi, 0, i)
    const = lambda bi, i: (0, 0)
    sd = jax.ShapeDtypeStruct
    slab_specs = []
    for w in later_weights:
        assert w.shape[0] % (b * nt * BF16_ROWS) == 0
        slab_specs.append(pl.BlockSpec((w.shape[0] // (b * nt), w.shape[1]),
                                       lambda bi, i: (bi * nt + i, 0)))
    outs = pl.pallas_call(
        functools.partial(_in_proj_kernel, col_starts=col_starts, n_cast=len(later_weights)),
        grid=(b, nt),
        in_specs=[pl.BlockSpec((1, tm, d), tok),
                  pl.BlockSpec((1, N_MOD, d), lambda bi, i: (bi, 0, 0)),
                  pl.BlockSpec((1, d), const),
                  pl.BlockSpec(w_all.shape, const, pipeline_mode=pl.Buffered(1)),
                  pl.BlockSpec(w_qvt.shape, const, pipeline_mode=pl.Buffered(1)),
                  pl.BlockSpec((tm, tw), lambda bi, i: (i, 0)),
                  pl.BlockSpec((tm, tw), lambda bi, i: (i, 0)),
                  pl.BlockSpec((HEAD_DIM, tm), lambda bi, i: (0, i)),
                  pl.BlockSpec((HEAD_DIM, tm), lambda bi, i: (0, i)),
                  pl.BlockSpec((HEAD_DIM, tm), const),
                  pl.BlockSpec((1, nw), const),
                  pl.BlockSpec((nw, nw), const)] + slab_specs,
        out_specs=[pl.BlockSpec((1, tm, pw), tok),
                   pl.BlockSpec((1, nw, tm), chan), pl.BlockSpec((1, nw, tm), chan),
                   pl.BlockSpec((1, tm, nw), tok),
                   pl.BlockSpec((1, nw, tm), chan),
                   pl.BlockSpec((1, tm, d), tok), pl.BlockSpec((1, tm, d), tok)] + slab_specs,
        out_shape=[sd((b, l, pw), F32),
                   sd((b, nw, l), BF16), sd((b, nw, l), BF16),
                   sd((b, l, nw), BF16),
                   sd((b, nw, l), BF16),
                   sd((b, l, d), BF16), sd((b, l, d), BF16)]
                  + [sd(w.shape, BF16) for w in later_weights],
        compiler_params=_params(("arbitrary", "arbitrary")),
        name="in_proj",
    )(x, mod3, norm_g, w_all, w_qvt, cos_t, sin_t, cos_tt, sin_tt, qg_tt, kg_t, mean_mat,
      *later_weights)
    return outs[:7], outs[7:]


def _mixer_kernel(u_ref, up_ref, un_ref, qrt_ref, qpt_ref, *rest, seq_len):
    k_refs, vt_refs = rest[:KV_BLOCKS], rest[KV_BLOCKS:2 * KV_BLOCKS]
    (kc_ref, vct_ref, bias_ref, pw_ref, ps_ref,
     na_ref, pool_ref, nat_scr, e_scr, a2_scr, a4_scr, a8_scr) = rest[2 * KV_BLOCKS:]
    i = pl.program_id(1)
    n_steps = pl.num_programs(1)
    tm = u_ref.shape[1]
    sub = tm // SUB_TILES
    nw = qrt_ref.shape[1]
    heads = nw // HEAD_DIM
    pair = 2 * HEAD_DIM
    win_blocks = KEY_ROWS // TILE_ROWS

    def head_rows(h):
        return slice(h * HEAD_DIM, (h + 1) * HEAD_DIM)

    def step_body(first_blocks, bias_types):
        zeros = jnp.zeros((HEAD_DIM, sub), BF16)
        ones = jnp.ones((BF16_ROWS, sub), BF16)

        def scores(t, h):
            rows = head_rows(h)
            tok = slice(t * sub, (t + 1) * sub)
            lanes = slice((h // 2) * pair, (h // 2 + 1) * pair)
            q_rot, q_plain = qrt_ref[0, rows, tok], qpt_ref[0, rows, tok]
            if h % 2 == 0:
                w_rot = jnp.concatenate([q_rot, zeros], axis=0)
                w_plain = jnp.concatenate([q_plain, zeros], axis=0)
            else:
                w_rot = jnp.concatenate([zeros, q_rot], axis=0)
                w_plain = jnp.concatenate([zeros, q_plain], axis=0)
            s = [(_dot(k_refs[first_blocks[t] + j][0, :, lanes], w_rot)
                  + bias_ref[bias_types[t], h, j * sub:(j + 1) * sub, :]).astype(BF16)
                 for j in range(win_blocks)]
            s.append(_dot(kc_ref[0, :, lanes], w_plain).astype(BF16))
            m = s[0].max(axis=0, keepdims=True)
            for sj in s[1:]:
                m = jnp.maximum(m, sj.max(axis=0, keepdims=True))
            return [jnp.exp2(sj - m) for sj in s]

        def weighted_values(t, h, p):
            rows = head_rows(h)
            ot = _dot(jnp.concatenate([vct_ref[0, rows, :], ones], axis=0), p[win_blocks])
            for j in range(win_blocks):
                vt = vt_refs[first_blocks[t] + j][0, rows, :]
                ot = ot + _dot(jnp.concatenate([vt, ones], axis=0), p[j])
            nat_scr[rows, t * sub:(t + 1) * sub] = ot[:HEAD_DIM] / ot[HEAD_DIM:HEAD_DIM + 1]

        gd = pw_ref.shape[-1]
        hi = tm + 2 * POOL_HALO
        first = POOL_HALO
        g0 = slice(0, gd)

        def rows_at(ref, off, cols):
            return ref[first + off:first + off + tm, cols]

        def pool_fill():
            for scr in (e_scr, a2_scr, a4_scr):
                scr[hi:hi + SUBLANES, :] = jnp.zeros((SUBLANES, scr.shape[1]), F32)
            e_scr[0:first, :] = jnp.where(i > 0, up_ref[0], 0.0)
            e_scr[first:first + tm, :] = u_ref[0]
            e_scr[first + tm:hi, :] = jnp.where((i + 1) * tm < seq_len, un_ref[0], 0.0)
            a2_scr[0:hi, :] = e_scr[0:hi, gd:] + e_scr[1:hi + 1, gd:]

        def pool_double():
            a4_scr[0:hi, :] = a2_scr[0:hi, gd:] + a2_scr[2:hi + 2, gd:]
            a8_scr[0:hi, :] = a4_scr[0:hi, gd:] + a4_scr[4:hi + 4, gd:]

        def pool_group(g):
            w = POOL_WINDOWS[g]
            src = (e_scr, a2_scr, a4_scr, a8_scr)[g]
            total = rows_at(src, -(w // 2), g0) + rows_at(src, 0, g0)
            pos = i * tm + lax.broadcasted_iota(jnp.int32, (tm, 1), 0)
            cnt = (jnp.minimum(pos + w // 2, seq_len) - jnp.maximum(pos - w // 2, 0)).astype(F32)
            dlt = total / cnt - u_ref[0, :, g * gd:(g + 1) * gd]
            cols = slice(g * gd, (g + 1) * gd)
            return (_dot(dlt.astype(BF16), pw_ref[g]) * ps_ref[:, cols]).astype(BF16)

        pooled = []
        pool_stages = [pool_fill, pool_double] + [functools.partial(pool_group, g)
                                                   for g in range(len(POOL_WINDOWS))]
        tiles = range(SUB_TILES)

        p_next = [scores(t, 0) for t in tiles]
        for h in range(heads):
            p_cur = p_next
            if h + 1 < heads:
                p_next = [scores(t, h + 1) for t in tiles]
            for t in tiles:
                weighted_values(t, h, p_cur[t])
            stage = h - (heads - len(pool_stages))
            if stage >= 0:
                out = pool_stages[stage]()
                if out is not None:
                    pooled.append(out)
            if h % 2 == 1:
                slab = slice((h - 1) * HEAD_DIM, (h + 1) * HEAD_DIM)
                na_ref[0, :, slab] = nat_scr[slab, :].T.astype(BF16)

        pool_ref[0] = jnp.concatenate(pooled, axis=1)

    top, interior, bottom = 0, 1, 2
    spare = KV_BLOCKS - win_blocks
    assert SUB_TILES == 2 and spare == 1

    @pl.when(i == 0)
    def _():
        step_body((0, 0), (top, interior))

    @pl.when(i == n_steps - 1)
    def _():
        step_body((spare, spare), (interior, bottom))

    @pl.when((i > 0) & (i < n_steps - 1))
    def _():
        step_body((0, spare), (interior, interior))


def _mixer(u, qrt, qpt, kr, vt, kc, vct, bias, pool_w_bf, pool_scale):
    b, l, pw = u.shape
    nw = kr.shape[-1]
    gd = pool_w_bf.shape[-1]
    sub = TILE_ROWS * GRID_W
    tm = SUB_TILES * sub
    n_sub = l // sub
    hb = tm // POOL_HALO
    n_halo = l // POOL_HALO
    nc = kc.shape[1]
    assert pw == len(POOL_WINDOWS) * gd and POOL_HALO == max(POOL_WINDOWS) // 2
    assert l % tm == 0 and n_sub >= KV_BLOCKS

    tok = lambda bi, i: (bi, i, 0)
    chan = lambda bi, i: (bi, 0, i)
    const2 = lambda bi, i: (0, 0)
    first_block = lambda i: jnp.clip(SUB_TILES * i - 1, 0, n_sub - KV_BLOCKS)

    def k_spec(j):
        return pl.BlockSpec((1, sub, nw), lambda bi, i: (bi, first_block(i) + j, 0))

    def vt_spec(j):
        return pl.BlockSpec((1, nw, sub), lambda bi, i: (bi, 0, first_block(i) + j))

    in_specs = [
        pl.BlockSpec((1, tm, pw), tok),
        pl.BlockSpec((1, POOL_HALO, pw), lambda bi, i: (bi, jnp.maximum(i * hb - 1, 0), 0)),
        pl.BlockSpec((1, POOL_HALO, pw), lambda bi, i: (bi, jnp.minimum((i + 1) * hb, n_halo - 1), 0)),
        pl.BlockSpec((1, nw, tm), chan),
        pl.BlockSpec((1, nw, tm), chan),
        *[k_spec(j) for j in range(KV_BLOCKS)],
        *[vt_spec(j) for j in range(KV_BLOCKS)],
        pl.BlockSpec((1, nc, nw), lambda bi, i: (bi, 0, 0)),
        pl.BlockSpec((1, nw, nc), lambda bi, i: (bi, 0, 0)),
        pl.BlockSpec(bias.shape, lambda bi, i: (0, 0, 0, 0), pipeline_mode=pl.Buffered(1)),
        pl.BlockSpec(pool_w_bf.shape, lambda bi, i: (0, 0, 0)),
        pl.BlockSpec((1, pw), const2),
    ]
    n_scr = tm + 2 * POOL_HALO + SUBLANES
    return pl.pallas_call(
        functools.partial(_mixer_kernel, seq_len=l),
        grid=(b, l // tm),
        in_specs=in_specs,
        out_specs=[pl.BlockSpec((1, tm, nw), tok), pl.BlockSpec((1, tm, pw), tok)],
        out_shape=[jax.ShapeDtypeStruct((b, l, nw), BF16), jax.ShapeDtypeStruct((b, l, pw), BF16)],
        scratch_shapes=[pltpu.VMEM((nw, tm), F32),
                        pltpu.VMEM((n_scr, pw), F32),
                        pltpu.VMEM((n_scr, pw - gd), F32),
                        pltpu.VMEM((n_scr, pw - 2 * gd), F32),
                        pltpu.VMEM((n_scr, pw - 3 * gd), F32)],
        compiler_params=_params(("arbitrary", "arbitrary")),
        name="mixer",
    )(u, u, u, qrt, qpt, *([kr] * KV_BLOCKS), *([vt] * KV_BLOCKS), kc, vct, bias,
      pool_w_bf, pool_scale)


def _post_kernel(x_ref, mod_ref, g_ref, na_ref, pool_ref, gp_ref, gn_ref,
                 wbp_ref, wbn_ref, wo_ref, w1_ref, w2_ref, o_ref, *, chunk):
    merged = (gp_ref[0].astype(F32) * _dot(pool_ref[0], wbp_ref[...])
              + gn_ref[0].astype(F32) * _dot(na_ref[0], wbn_ref[...]))
    x = x_ref[0] + mod_ref[0, 2:3, :] * _dot(merged.astype(BF16), wo_ref[...])
    hb = _modulated_norm(x, mod_ref, g_ref, 3).astype(BF16)
    acc = jnp.zeros(x.shape, F32)
    for c0 in range(0, w1_ref.shape[1], chunk):
        a = jnp.maximum(_dot(hb, w1_ref[:, c0:c0 + chunk]), 0.0)
        acc = acc + _dot((a * a).astype(BF16), w2_ref[c0:c0 + chunk, :])
    o_ref[0] = x + mod_ref[0, 5:6, :] * acc


def _post(x, mod3, norm_g, na, pool, gp, gn, wbp, wbn, wo, w1, w2, tm):
    b, l, d = x.shape
    tok = lambda bi, i: (bi, i, 0)
    const = lambda bi, i: (0, 0)
    resident = lambda w: pl.BlockSpec(w.shape, const, pipeline_mode=pl.Buffered(1))
    return pl.pallas_call(
        functools.partial(_post_kernel, chunk=d),
        grid=(b, l // tm),
        in_specs=[pl.BlockSpec((1, tm, d), tok),
                  pl.BlockSpec((1, N_MOD, d), lambda bi, i: (bi, 0, 0)),
                  pl.BlockSpec((1, d), const),
                  pl.BlockSpec((1, tm, na.shape[-1]), tok),
                  pl.BlockSpec((1, tm, pool.shape[-1]), tok),
                  pl.BlockSpec((1, tm, d), tok),
                  pl.BlockSpec((1, tm, d), tok),
                  resident(wbp), resident(wbn), resident(wo), resident(w1), resident(w2)],
        out_specs=pl.BlockSpec((1, tm, d), tok),
        out_shape=jax.ShapeDtypeStruct((b, l, d), F32),
        compiler_params=_params(("arbitrary", "arbitrary")),
        name="post",
    )(x, mod3, norm_g, na, pool, gp, gn, wbp, wbn, wo, w1, w2)


def _rope_tables(seq_len):
    n_freq = HEAD_DIM // 4
    t = np.arange(seq_len)
    inv = (ROPE_THETA ** (-np.arange(n_freq, dtype=np.float32) / n_freq)).astype(np.float32)
    ang_row = (t // GRID_W).astype(np.float32)[:, None] * inv
    ang_col = (t % GRID_W).astype(np.float32)[:, None] * inv
    cr, sr, cc, sc = np.cos(ang_row), np.sin(ang_row), np.cos(ang_col), np.sin(ang_col)
    cos = np.concatenate([cr, cr, cc, cc], axis=1).astype(np.float32)
    sin = np.concatenate([-sr, sr, -sc, sc], axis=1).astype(np.float32)
    return (jnp.asarray(np.tile(cos, (1, 2))), jnp.asarray(np.tile(sin, (1, 2))),
            jnp.asarray(np.ascontiguousarray(cos.T)), jnp.asarray(np.ascontiguousarray(sin.T)))


def _bias_kernel(el_ref, er_ref, o_ref, *, rows):
    lanes = 2 * GRID_W
    kc = lax.broadcasted_iota(jnp.int32, (GRID_W, lanes), 0)
    lane = lax.broadcasted_iota(jnp.int32, (GRID_W, lanes), 1)
    qc = lane % GRID_W
    c0 = jnp.clip(qc - NA_COLS // 2, 0, GRID_W - NA_COLS)
    col_ok = (kc >= c0) & (kc < c0 + NA_COLS)
    left = lane < GRID_W
    masked = jnp.full((GRID_W, lanes), MASK_VALUE, F32)

    cache = {}

    def toeplitz(side, dr):
        if (side, dr) not in cache:
            ref = er_ref if side else el_ref
            dd = dr + NA_ROWS - 1
            vec = jnp.broadcast_to(ref[0, dd:dd + 1, :], (GRID_W, lanes))
            cache[(side, dr)] = pltpu.roll(vec, 0, axis=1, stride=1, stride_axis=0) * LOG2E
        return cache[(side, dr)]

    nt = rows // TILE_ROWS
    for t, tile in enumerate((0, nt // 2, nt - 1)):
        r = tile * TILE_ROWS
        ks = min(max(r - TILE_ROWS, 0), rows - KEY_ROWS)
        for j in range(KEY_ROWS):
            kr = ks + j
            for ip in range(TILE_ROWS // 2):
                halves = []
                for side in range(2):
                    qr = r + 2 * ip + side
                    r0 = min(max(qr - NA_ROWS // 2, 0), rows - NA_ROWS)
                    ok = r0 <= kr < r0 + NA_ROWS
                    halves.append(toeplitz(side, kr - qr) if ok else masked)
                blk = jnp.where(col_ok, jnp.where(left, halves[0], halves[1]), MASK_VALUE)
                o_ref[t, 0, j * GRID_W:(j + 1) * GRID_W, ip * lanes:(ip + 1) * lanes] = blk


def _bias_vectors(rpb):
    heads, nr, ncol = rpb.shape
    lanes = 2 * GRID_W
    half = NA_COLS - 1
    zeros = lambda n: jnp.zeros((heads, nr, n), F32)
    flipped = rpb[..., ::-1]
    e_left = jnp.concatenate([flipped[..., half:], zeros(lanes - ncol), flipped[..., :half]], axis=-1)
    e_right = jnp.concatenate([zeros(GRID_W - half), flipped, zeros(lanes - GRID_W + half - ncol)], axis=-1)
    pad = (-nr) % SUBLANES
    e_left = jnp.pad(e_left, ((0, 0), (0, pad), (0, 0)))
    e_right = jnp.pad(e_right, ((0, 0), (0, pad), (0, 0)))
    return e_left, e_right


def kernel(x, c, ctx, c_ctx, ada_w, ada_b, norm1_g, norm2_g, w_in, pool_w, pool_scale,
           q_norm_g, k_norm_g, rpb, w_branch_pool, w_branch_na, w_out, mlp_w1, mlp_w2):
    b, l, d = x.shape
    depth = ada_w.shape[0]
    pw = pool_scale.shape[-1]
    nw = w_branch_na.shape[1]
    heads = nw // HEAD_DIM
    rows = l // GRID_W
    assert depth == 1, "the context-stream update is only needed when another layer follows"
    assert l % (TILE_ROWS * GRID_W) == 0 and rows >= KEY_ROWS
    assert w_in.shape[-1] == pw + 3 * nw + 2 * d and pw == nw
    ctx_row = b
    tm_proj = 512
    tm_in = 1024

    cos_t, sin_t, cos_tt, sin_tt = _rope_tables(l)
    mean_mat = jnp.asarray(np.kron(np.eye(heads), np.full((HEAD_DIM, HEAD_DIM), 1.0 / HEAD_DIM)), BF16)
    pad = (-(b + 1)) % SUBLANES
    cc = jnp.concatenate([c, c_ctx[None], jnp.zeros((pad, d), F32)], axis=0)

    layer = 0
    q0, k0, v0, g0 = pw, pw + nw, pw + 2 * nw, pw + 3 * nw
    mod, w_all, w_qvt, bias = _prep(cc, ada_w[layer], ada_b[layer], w_in[layer], q0, v0, nw,
                                    rpb[layer], rows)
    mod3 = mod.reshape(mod.shape[0], N_MOD, d)
    qg_tt = jnp.broadcast_to(q_norm_g[layer][:, None], (HEAD_DIM, tm_in))
    kg_t = jnp.tile(k_norm_g[layer], heads)[None]
    n1 = norm1_g[layer][None]

    k_ctx, vt_ctx = _ctx_proj(ctx, mod3, n1, w_all, w_qvt, kg_t, mean_mat, k0 // nw, ctx_row)
    later_weights = (w_branch_pool[layer], w_branch_na[layer], w_out[layer],
                     mlp_w1[layer], mlp_w2[layer])
    (u, qrt, qpt, kr, vt, gp, gn), (wbp, wbn, wo, w1, w2) = _in_proj(
        x, mod3, n1, w_all, w_qvt, cos_t, sin_t, cos_tt, sin_tt, qg_tt, kg_t, mean_mat,
        (0, k0, g0, g0 + d), pw, tm_in, later_weights)
    na, pool = _mixer(u, qrt, qpt, kr, vt, k_ctx, vt_ctx, bias,
                      pool_w[layer].astype(BF16), pool_scale[layer][None])
    return _post(x, mod3, norm2_g[layer][None], na, pool, gp, gn, wbp, wbn, wo, w1, w2, tm=tm_proj)
```

```python
import functools

import numpy as np
import jax
import jax.numpy as jnp
from jax import lax
from jax.experimental import pallas as pl
from jax.experimental.pallas import tpu as pltpu

GRID_W = 64
N_MOD = 6
POOL_WINDOWS = (2, 4, 8, 16)
HEAD_DIM = 64
NA_ROWS = 8
NA_COLS = 16
ROPE_THETA = 10000.0
EPS = 1e-6
MASK_VALUE = -1e30
LOG2E = 1.4426950408889634

VMEM_LIMIT_BYTES = 62 * 1024 * 1024
SUBLANES = 8
BF16_ROWS = 16
POOL_HALO = 8
TILE_ROWS = 4
KEY_ROWS = TILE_ROWS + NA_ROWS

BF16 = jnp.bfloat16
F32 = jnp.float32


def _dot(a, b):
    return jnp.dot(a, b, preferred_element_type=F32)


def _dot_nt(a, b):
    return lax.dot_general(a, b, (((1,), (1,)), ((), ())), preferred_element_type=F32)


def _params(sem):
    return pltpu.CompilerParams(dimension_semantics=sem, vmem_limit_bytes=VMEM_LIMIT_BYTES)


def _rms(x):
    return x * lax.rsqrt(jnp.mean(x * x, axis=-1, keepdims=True) + EPS)


def _sigmoid_bf16(z):
    zb = z.astype(BF16)
    return 0.5 * jnp.tanh(0.5 * zb) + 0.5


def _prep_kernel(c_ref, aw_ref, ab_ref, w_rows_ref, w_cols_ref, el_ref, er_ref,
                 mod_ref, w_all_ref, w_qvt_ref, bias_ref, *, grid_rows):
    c = c_ref[...]
    s = (c * jax.nn.sigmoid(c)).astype(BF16)
    mod_ref[...] = _dot(s, aw_ref[...].astype(BF16)) + ab_ref[...]
    w_all_ref[...] = w_rows_ref[...].astype(BF16)
    w_qvt_ref[...] = w_cols_ref[...].T.astype(BF16)
    _bias_kernel(el_ref, er_ref, bias_ref, rows=grid_rows)


def _prep(cc, ada_w, ada_b, w, q0, v0, width, rpb, grid_rows):
    m, d = cc.shape
    n = ada_w.shape[1]
    lanes = 128
    heads = rpb.shape[0]
    steps = 2 * width // lanes
    assert steps == heads and n % (steps * lanes) == 0 and w.shape[0] % (steps * BF16_ROWS) == 0
    nb, rows = n // steps, w.shape[0] // steps
    per_seg = width // lanes
    col_block = lambda j: jnp.where(j < per_seg, q0 // lanes + j, v0 // lanes + j - per_seg)
    e_left, e_right = _bias_vectors(rpb)
    vec_spec = pl.BlockSpec((1,) + e_left.shape[1:], lambda j: (j, 0, 0))
    tq, tk = TILE_ROWS * GRID_W, KEY_ROWS * GRID_W
    return pl.pallas_call(
        functools.partial(_prep_kernel, grid_rows=grid_rows),
        grid=(steps,),
        in_specs=[pl.BlockSpec((m, d), lambda j: (0, 0)),
                  pl.BlockSpec((d, nb), lambda j: (0, j)),
                  pl.BlockSpec((1, nb), lambda j: (0, j)),
                  pl.BlockSpec((rows, w.shape[1]), lambda j: (j, 0)),
                  pl.BlockSpec((w.shape[0], lanes), lambda j: (0, col_block(j))),
                  vec_spec, vec_spec],
        out_specs=[pl.BlockSpec((m, nb), lambda j: (0, j)),
                   pl.BlockSpec((rows, w.shape[1]), lambda j: (j, 0)),
                   pl.BlockSpec((lanes, w.shape[0]), lambda j: (j, 0)),
                   pl.BlockSpec((3, 1, tk, tq), lambda j: (0, j, 0, 0))],
        out_shape=[jax.ShapeDtypeStruct((m, n), F32),
                   jax.ShapeDtypeStruct(w.shape, BF16),
                   jax.ShapeDtypeStruct((2 * width, w.shape[0]), BF16),
                   jax.ShapeDtypeStruct((3, heads, tk, tq), F32)],
        compiler_params=_params(("arbitrary",)),
        name="prep",
    )(cc, ada_w, ada_b.reshape(1, n), w, w, e_left, e_right)


def _head_norm(a, mean_mat, g):
    ms = _dot((a * a).astype(BF16), mean_mat)
    return a * lax.rsqrt(ms + EPS) * g


def _rope(a, cos, sin_signed, first_half):
    width = a.shape[-1]
    quarter = HEAD_DIM // 4
    up = pltpu.roll(a, width - quarter, axis=1)
    down = pltpu.roll(a, quarter, axis=1)
    swapped = jnp.where(first_half, up, down)
    return a * cos + swapped * sin_signed


def _modulated_norm(x, mod_ref, g_ref, shift_row):
    return _rms(x) * g_ref[...] * (1.0 + mod_ref[0, shift_row + 1:shift_row + 2, :]) \
        + mod_ref[0, shift_row:shift_row + 1, :]


def _ctx_kernel(x_ref, mod_ref, g_ref, wk_ref, wvt_ref, kg_ref, mean_ref, k_ref, vt_ref):
    hb = _modulated_norm(x_ref[0], mod_ref, g_ref, 0).astype(BF16)
    k_ref[0] = _head_norm(_dot(hb, wk_ref[...]), mean_ref[...], kg_ref[...]).astype(BF16)
    vt_ref[0] = _dot_nt(wvt_ref[...], hb).astype(BF16)


def _ctx_proj(ctx, mod3, norm_g, w_all, w_qvt, kg_t, mean_mat, key_block, ctx_row):
    b, n, d = ctx.shape
    nw = kg_t.shape[-1]
    sd = jax.ShapeDtypeStruct
    return pl.pallas_call(
        _ctx_kernel,
        grid=(b,),
        in_specs=[pl.BlockSpec((1, n, d), lambda i: (i, 0, 0)),
                  pl.BlockSpec((1, N_MOD, d), lambda i: (ctx_row, 0, 0)),
                  pl.BlockSpec((1, d), lambda i: (0, 0)),
                  pl.BlockSpec((d, nw), lambda i: (0, key_block)),
                  pl.BlockSpec((nw, d), lambda i: (1, 0)),
                  pl.BlockSpec((1, nw), lambda i: (0, 0)),
                  pl.BlockSpec((nw, nw), lambda i: (0, 0))],
        out_specs=[pl.BlockSpec((1, n, nw), lambda i: (i, 0, 0)),
                   pl.BlockSpec((1, nw, n), lambda i: (i, 0, 0))],
        out_shape=[sd((b, n, nw), BF16), sd((b, nw, n), BF16)],
        compiler_params=_params(("arbitrary",)),
        name="ctx_proj",
    )(ctx, mod3, norm_g, w_all, w_qvt, kg_t, mean_mat)


def _in_proj_kernel(*refs, col_starts, n_cast):
    (x_ref, mod_ref, g_ref, w_ref, wt_ref, cos_ref, sin_ref, cost_ref, sint_ref,
     qgt_ref, kg_ref, mean_ref) = refs[:12]
    cast_in = refs[12:12 + n_cast]
    u_ref, qrt_ref, qpt_ref, kr_ref, vt_ref, gp_ref, gn_ref = refs[12 + n_cast:19 + n_cast]
    cast_out = refs[19 + n_cast:]
    for src, dst in zip(cast_in, cast_out):
        dst[...] = src[...].astype(BF16)

    pw = u_ref.shape[-1]
    nw = kr_ref.shape[-1]
    d = x_ref.shape[-1]
    heads = nw // HEAD_DIM
    quarter = HEAD_DIM // 4
    scale = HEAD_DIM ** -0.5 * LOG2E
    u0, k0, gp0, gn0 = col_starts

    hb = _modulated_norm(x_ref[0], mod_ref, g_ref, 0).astype(BF16)

    cost, sint, qgt = cost_ref[...], sint_ref[...], qgt_ref[...]

    def query_heads(first, last):
        qt = _dot_nt(wt_ref[first * HEAD_DIM:last * HEAD_DIM, :], hb)
        for h in range(first, last):
            rows = slice(h * HEAD_DIM, (h + 1) * HEAD_DIM)
            z = qt[(h - first) * HEAD_DIM:(h - first + 1) * HEAD_DIM, :]
            qn = z * lax.rsqrt(jnp.mean(z * z, axis=0, keepdims=True) + EPS) * (qgt * scale)
            swapped = jnp.concatenate([qn[quarter:2 * quarter], qn[0:quarter],
                                       qn[3 * quarter:], qn[2 * quarter:3 * quarter]], axis=0)
            qpt_ref[0, rows, :] = qn.astype(BF16)
            qrt_ref[0, rows, :] = (qn * cost + swapped * sint).astype(BF16)

    z_gp = _dot(hb, w_ref[:, gp0:gp0 + d])
    query_heads(0, heads // 2)
    z_gn = _dot(hb, w_ref[:, gn0:gn0 + d])
    gp_ref[0] = _sigmoid_bf16(z_gp)
    query_heads(heads // 2, heads)
    z_k = _dot(hb, w_ref[:, k0:k0 + nw])
    gn_ref[0] = _sigmoid_bf16(z_gn)
    z_vt = _dot_nt(wt_ref[nw:, :], hb)

    reps = nw // cos_ref.shape[-1]
    cos = jnp.concatenate([cos_ref[...]] * reps, axis=1)
    sin = jnp.concatenate([sin_ref[...]] * reps, axis=1)
    lane = lax.broadcasted_iota(jnp.int32, (1, nw), 1)
    first_half = (lane % (HEAD_DIM // 2)) < quarter
    k = _head_norm(z_k, mean_ref[...], kg_ref[...])
    kr_ref[0] = _rope(k, cos, sin, first_half).astype(BF16)
    vt_ref[0] = z_vt.astype(BF16)
    u_ref[0] = _dot(hb, w_ref[:, u0:u0 + pw])


def _in_proj(x, mod3, norm_g, w_all, w_qvt, cos_t, sin_t, cos_tt, sin_tt, qg_tt, kg_t, mean_mat,
             col_starts, pw, tm, later_weights):
    b, l, d = x.shape
    nw = kg_t.shape[-1]
    tw = cos_t.shape[-1]
    nt = l // tm
    tok = lambda bi, i: (bi, i, 0)
    chan = lambda bi, i: (bi, 0, i)
    const = lambda bi, i: (0, 0)
    sd = jax.ShapeDtypeStruct
    slab_specs = []
    for w in later_weights:
        assert w.shape[0] % (b * nt * BF16_ROWS) == 0
        slab_specs.append(pl.BlockSpec((w.shape[0] // (b * nt), w.shape[1]),
                                       lambda bi, i: (bi * nt + i, 0)))
    outs = pl.pallas_call(
        functools.partial(_in_proj_kernel, col_starts=col_starts, n_cast=len(later_weights)),
        grid=(b, nt),
        in_specs=[pl.BlockSpec((1, tm, d), tok),
                  pl.BlockSpec((1, N_MOD, d), lambda bi, i: (bi, 0, 0)),
                  pl.BlockSpec((1, d), const),
                  pl.BlockSpec(w_all.shape, const, pipeline_mode=pl.Buffered(1)),
                  pl.BlockSpec(w_qvt.shape, const, pipeline_mode=pl.Buffered(1)),
                  pl.BlockSpec((tm, tw), lambda bi, i: (i, 0)),
                  pl.BlockSpec((tm, tw), lambda bi, i: (i, 0)),
                  pl.BlockSpec((HEAD_DIM, tm), lambda bi, i: (0, i)),
                  pl.BlockSpec((HEAD_DIM, tm), lambda bi, i: (0, i)),
                  pl.BlockSpec((HEAD_DIM, tm), const),
                  pl.BlockSpec((1, nw), const),
                  pl.BlockSpec((nw, nw), const)] + slab_specs,
        out_specs=[pl.BlockSpec((1, tm, pw), tok),
                   pl.BlockSpec((1, nw, tm), chan), pl.BlockSpec((1, nw, tm), chan),
                   pl.BlockSpec((1, tm, nw), tok),
                   pl.BlockSpec((1, nw, tm), chan),
                   pl.BlockSpec((1, tm, d), tok), pl.BlockSpec((1, tm, d), tok)] + slab_specs,
        out_shape=[sd((b, l, pw), F32),
                   sd((b, nw, l), BF16), sd((b, nw, l), BF16),
                   sd((b, l, nw), BF16),
                   sd((b, nw, l), BF16),
                   sd((b, l, d), BF16), sd((b, l, d), BF16)]
                  + [sd(w.shape, BF16) for w in later_weights],
        compiler_params=_params(("arbitrary", "arbitrary")),
        name="in_proj",
    )(x, mod3, norm_g, w_all, w_qvt, cos_t, sin_t, cos_tt, sin_tt, qg_tt, kg_t, mean_mat,
      *later_weights)
    return outs[:7], outs[7:]


def _mixer_kernel(u_ref, up_ref, un_ref, qrt_ref, qpt_ref,
                  k0_ref, k1_ref, k2_ref, vt0_ref, vt1_ref, vt2_ref, kc_ref, vct_ref, bias_ref,
                  pw_ref, ps_ref,
                  na_ref, pool_ref, nat_scr, e_scr, a2_scr, a4_scr, a8_scr, *, seq_len):
    i = pl.program_id(1)
    tm = u_ref.shape[1]
    nw = qrt_ref.shape[1]
    heads = nw // HEAD_DIM
    pair = 2 * HEAD_DIM

    zeros = jnp.zeros((HEAD_DIM, tm), BF16)
    ones = jnp.ones((BF16_ROWS, tm), BF16)
    k_refs = (k0_ref, k1_ref, k2_ref)
    vt_refs = (vt0_ref, vt1_ref, vt2_ref)

    def head_rows(h):
        return slice(h * HEAD_DIM, (h + 1) * HEAD_DIM)

    def scores(h):
        rows = head_rows(h)
        lanes = slice((h // 2) * pair, (h // 2 + 1) * pair)
        q_rot, q_plain = qrt_ref[0, rows, :], qpt_ref[0, rows, :]
        if h % 2 == 0:
            w_rot = jnp.concatenate([q_rot, zeros], axis=0)
            w_plain = jnp.concatenate([q_plain, zeros], axis=0)
        else:
            w_rot = jnp.concatenate([zeros, q_rot], axis=0)
            w_plain = jnp.concatenate([zeros, q_plain], axis=0)
        s = [(_dot(k_refs[j][0, :, lanes], w_rot) + bias_ref[0, h, j * tm:(j + 1) * tm, :]).astype(BF16)
             for j in range(3)]
        s.append(_dot(kc_ref[0, :, lanes], w_plain).astype(BF16))
        m = s[0].max(axis=0, keepdims=True)
        for sj in s[1:]:
            m = jnp.maximum(m, sj.max(axis=0, keepdims=True))
        return s, m

    def probabilities(s_and_m):
        s, m = s_and_m
        return [jnp.exp2(sj - m) for sj in s]

    def weighted_values(h, p):
        rows = head_rows(h)
        ot = _dot(jnp.concatenate([vct_ref[0, rows, :], ones], axis=0), p[3])
        for j in range(3):
            ot = ot + _dot(jnp.concatenate([vt_refs[j][0, rows, :], ones], axis=0), p[j])
        nat_scr[rows, :] = ot[:HEAD_DIM] / ot[HEAD_DIM:HEAD_DIM + 1]

    gd = pw_ref.shape[-1]
    hi = tm + 2 * POOL_HALO
    first = POOL_HALO
    g0 = slice(0, gd)

    def rows_at(ref, off, cols):
        return ref[first + off:first + off + tm, cols]

    def pool_fill():
        for scr in (e_scr, a2_scr, a4_scr):
            scr[hi:hi + SUBLANES, :] = jnp.zeros((SUBLANES, scr.shape[1]), F32)
        e_scr[0:first, :] = jnp.where(i > 0, up_ref[0], 0.0)
        e_scr[first:first + tm, :] = u_ref[0]
        e_scr[first + tm:hi, :] = jnp.where((i + 1) * tm < seq_len, un_ref[0], 0.0)
        a2_scr[0:hi, :] = e_scr[0:hi, gd:] + e_scr[1:hi + 1, gd:]

    def pool_double():
        a4_scr[0:hi, :] = a2_scr[0:hi, gd:] + a2_scr[2:hi + 2, gd:]
        a8_scr[0:hi, :] = a4_scr[0:hi, gd:] + a4_scr[4:hi + 4, gd:]

    def pool_group(g):
        w = POOL_WINDOWS[g]
        if g == 0:
            total = rows_at(e_scr, -1, g0) + rows_at(e_scr, 0, g0)
        elif g == 1:
            total = rows_at(a2_scr, -2, g0) + rows_at(a2_scr, 0, g0)
        elif g == 2:
            total = rows_at(a4_scr, -4, g0) + rows_at(a4_scr, 0, g0)
        else:
            total = rows_at(a8_scr, -8, g0) + rows_at(a8_scr, 0, g0)
        pos = i * tm + lax.broadcasted_iota(jnp.int32, (tm, 1), 0)
        cnt = (jnp.minimum(pos + w // 2, seq_len) - jnp.maximum(pos - w // 2, 0)).astype(F32)
        dlt = total / cnt - u_ref[0, :, g * gd:(g + 1) * gd]
        cols = slice(g * gd, (g + 1) * gd)
        return (_dot(dlt.astype(BF16), pw_ref[g]) * ps_ref[:, cols]).astype(BF16)

    pooled = []
    pool_stages = [pool_fill, pool_double] + [functools.partial(pool_group, g)
                                               for g in range(len(POOL_WINDOWS))]

    p_next = probabilities(scores(0))
    for h in range(heads):
        p_cur = p_next
        if h + 1 < heads:
            p_next = probabilities(scores(h + 1))
        weighted_values(h, p_cur)
        stage = h - (heads - len(pool_stages))
        if stage >= 0:
            out = pool_stages[stage]()
            if out is not None:
                pooled.append(out)
        if h % 2 == 1:
            slab = slice((h - 1) * HEAD_DIM, (h + 1) * HEAD_DIM)
            na_ref[0, :, slab] = nat_scr[slab, :].T.astype(BF16)

    pool_ref[0] = jnp.concatenate(pooled, axis=1)


def _mixer(u, qrt, qpt, kr, vt, kc, vct, bias, pool_w_bf, pool_scale):
    b, l, pw = u.shape
    nw = kr.shape[-1]
    gd = pool_w_bf.shape[-1]
    tm = TILE_ROWS * GRID_W
    nt = l // tm
    hb = tm // POOL_HALO
    n_halo = l // POOL_HALO
    nc = kc.shape[1]
    assert pw == len(POOL_WINDOWS) * gd and POOL_HALO == max(POOL_WINDOWS) // 2

    tok = lambda bi, i: (bi, i, 0)
    chan = lambda bi, i: (bi, 0, i)
    const2 = lambda bi, i: (0, 0)
    first_block = lambda i: jnp.clip(i - 1, 0, nt - 3)

    def k_spec(j):
        return pl.BlockSpec((1, tm, nw), lambda bi, i: (bi, first_block(i) + j, 0))

    def vt_spec(j):
        return pl.BlockSpec((1, nw, tm), lambda bi, i: (bi, 0, first_block(i) + j))

    def bias_idx(bi, i):
        return (jnp.where(i == 0, 0, jnp.where(i == nt - 1, 2, 1)), 0, 0, 0)

    in_specs = [
        pl.BlockSpec((1, tm, pw), tok),
        pl.BlockSpec((1, POOL_HALO, pw), lambda bi, i: (bi, jnp.maximum(i * hb - 1, 0), 0)),
        pl.BlockSpec((1, POOL_HALO, pw), lambda bi, i: (bi, jnp.minimum((i + 1) * hb, n_halo - 1), 0)),
        pl.BlockSpec((1, nw, tm), chan),
        pl.BlockSpec((1, nw, tm), chan),
        k_spec(0), k_spec(1), k_spec(2),
        vt_spec(0), vt_spec(1), vt_spec(2),
        pl.BlockSpec((1, nc, nw), lambda bi, i: (bi, 0, 0)),
        pl.BlockSpec((1, nw, nc), lambda bi, i: (bi, 0, 0)),
        pl.BlockSpec((1,) + bias.shape[1:], bias_idx),
        pl.BlockSpec(pool_w_bf.shape, lambda bi, i: (0, 0, 0)),
        pl.BlockSpec((1, pw), const2),
    ]
    n_scr = tm + 2 * POOL_HALO + SUBLANES
    return pl.pallas_call(
        functools.partial(_mixer_kernel, seq_len=l),
        grid=(b, nt),
        in_specs=in_specs,
        out_specs=[pl.BlockSpec((1, tm, nw), tok), pl.BlockSpec((1, tm, pw), tok)],
        out_shape=[jax.ShapeDtypeStruct((b, l, nw), BF16), jax.ShapeDtypeStruct((b, l, pw), BF16)],
        scratch_shapes=[pltpu.VMEM((nw, tm), F32),
                        pltpu.VMEM((n_scr, pw), F32),
                        pltpu.VMEM((n_scr, pw - gd), F32),
                        pltpu.VMEM((n_scr, pw - 2 * gd), F32),
                        pltpu.VMEM((n_scr, pw - 3 * gd), F32)],
        compiler_params=_params(("arbitrary", "arbitrary")),
        name="mixer",
    )(u, u, u, qrt, qpt, kr, kr, kr, vt, vt, vt, kc, vct, bias, pool_w_bf, pool_scale)


def _post_kernel(x_ref, mod_ref, g_ref, na_ref, pool_ref, gp_ref, gn_ref,
                 wbp_ref, wbn_ref, wo_ref, w1_ref, w2_ref, o_ref, *, chunk):
    merged = (gp_ref[0].astype(F32) * _dot(pool_ref[0], wbp_ref[...])
              + gn_ref[0].astype(F32) * _dot(na_ref[0], wbn_ref[...]))
    x = x_ref[0] + mod_ref[0, 2:3, :] * _dot(merged.astype(BF16), wo_ref[...])
    hb = _modulated_norm(x, mod_ref, g_ref, 3).astype(BF16)
    acc = jnp.zeros(x.shape, F32)
    for c0 in range(0, w1_ref.shape[1], chunk):
        a = jnp.maximum(_dot(hb, w1_ref[:, c0:c0 + chunk]), 0.0)
        acc = acc + _dot((a * a).astype(BF16), w2_ref[c0:c0 + chunk, :])
    o_ref[0] = x + mod_ref[0, 5:6, :] * acc


def _post(x, mod3, norm_g, na, pool, gp, gn, wbp, wbn, wo, w1, w2, tm):
    b, l, d = x.shape
    tok = lambda bi, i: (bi, i, 0)
    const = lambda bi, i: (0, 0)
    resident = lambda w: pl.BlockSpec(w.shape, const, pipeline_mode=pl.Buffered(1))
    return pl.pallas_call(
        functools.partial(_post_kernel, chunk=d),
        grid=(b, l // tm),
        in_specs=[pl.BlockSpec((1, tm, d), tok),
                  pl.BlockSpec((1, N_MOD, d), lambda bi, i: (bi, 0, 0)),
                  pl.BlockSpec((1, d), const),
                  pl.BlockSpec((1, tm, na.shape[-1]), tok),
                  pl.BlockSpec((1, tm, pool.shape[-1]), tok),
                  pl.BlockSpec((1, tm, d), tok),
                  pl.BlockSpec((1, tm, d), tok),
                  resident(wbp), resident(wbn), resident(wo), resident(w1), resident(w2)],
        out_specs=pl.BlockSpec((1, tm, d), tok),
        out_shape=jax.ShapeDtypeStruct((b, l, d), F32),
        compiler_params=_params(("arbitrary", "arbitrary")),
        name="post",
    )(x, mod3, norm_g, na, pool, gp, gn, wbp, wbn, wo, w1, w2)


def _rope_tables(seq_len):
    n_freq = HEAD_DIM // 4
    t = np.arange(seq_len)
    inv = (ROPE_THETA ** (-np.arange(n_freq, dtype=np.float32) / n_freq)).astype(np.float32)
    ang_row = (t // GRID_W).astype(np.float32)[:, None] * inv
    ang_col = (t % GRID_W).astype(np.float32)[:, None] * inv
    cr, sr, cc, sc = np.cos(ang_row), np.sin(ang_row), np.cos(ang_col), np.sin(ang_col)
    cos = np.concatenate([cr, cr, cc, cc], axis=1).astype(np.float32)
    sin = np.concatenate([-sr, sr, -sc, sc], axis=1).astype(np.float32)
    return (jnp.asarray(np.tile(cos, (1, 2))), jnp.asarray(np.tile(sin, (1, 2))),
            jnp.asarray(np.ascontiguousarray(cos.T)), jnp.asarray(np.ascontiguousarray(sin.T)))


def _bias_kernel(el_ref, er_ref, o_ref, *, rows):
    lanes = 2 * GRID_W
    kc = lax.broadcasted_iota(jnp.int32, (GRID_W, lanes), 0)
    lane = lax.broadcasted_iota(jnp.int32, (GRID_W, lanes), 1)
    qc = lane % GRID_W
    c0 = jnp.clip(qc - NA_COLS // 2, 0, GRID_W - NA_COLS)
    col_ok = (kc >= c0) & (kc < c0 + NA_COLS)
    left = lane < GRID_W
    masked = jnp.full((GRID_W, lanes), MASK_VALUE, F32)

    cache = {}

    def toeplitz(side, dr):
        if (side, dr) not in cache:
            ref = er_ref if side else el_ref
            dd = dr + NA_ROWS - 1
            vec = jnp.broadcast_to(ref[0, dd:dd + 1, :], (GRID_W, lanes))
            cache[(side, dr)] = pltpu.roll(vec, 0, axis=1, stride=1, stride_axis=0) * LOG2E
        return cache[(side, dr)]

    nt = rows // TILE_ROWS
    for t, tile in enumerate((0, nt // 2, nt - 1)):
        r = tile * TILE_ROWS
        ks = min(max(r - TILE_ROWS, 0), rows - KEY_ROWS)
        for j in range(KEY_ROWS):
            kr = ks + j
            for ip in range(TILE_ROWS // 2):
                halves = []
                for side in range(2):
                    qr = r + 2 * ip + side
                    r0 = min(max(qr - NA_ROWS // 2, 0), rows - NA_ROWS)
                    ok = r0 <= kr < r0 + NA_ROWS
                    halves.append(toeplitz(side, kr - qr) if ok else masked)
                blk = jnp.where(col_ok, jnp.where(left, halves[0], halves[1]), MASK_VALUE)
                o_ref[t, 0, j * GRID_W:(j + 1) * GRID_W, ip * lanes:(ip + 1) * lanes] = blk


def _bias_vectors(rpb):
    heads, nr, ncol = rpb.shape
    lanes = 2 * GRID_W
    half = NA_COLS - 1
    zeros = lambda n: jnp.zeros((heads, nr, n), F32)
    flipped = rpb[..., ::-1]
    e_left = jnp.concatenate([flipped[..., half:], zeros(lanes - ncol), flipped[..., :half]], axis=-1)
    e_right = jnp.concatenate([zeros(GRID_W - half), flipped, zeros(lanes - GRID_W + half - ncol)], axis=-1)
    pad = (-nr) % SUBLANES
    e_left = jnp.pad(e_left, ((0, 0), (0, pad), (0, 0)))
    e_right = jnp.pad(e_right, ((0, 0), (0, pad), (0, 0)))
    return e_left, e_right


def kernel(x, c, ctx, c_ctx, ada_w, ada_b, norm1_g, norm2_g, w_in, pool_w, pool_scale,
           q_norm_g, k_norm_g, rpb, w_branch_pool, w_branch_na, w_out, mlp_w1, mlp_w2):
    b, l, d = x.shape
    depth = ada_w.shape[0]
    pw = pool_scale.shape[-1]
    nw = w_branch_na.shape[1]
    heads = nw // HEAD_DIM
    rows = l // GRID_W
    assert depth == 1, "the context-stream update is only needed when another layer follows"
    assert l % (TILE_ROWS * GRID_W) == 0 and rows >= KEY_ROWS
    assert w_in.shape[-1] == pw + 3 * nw + 2 * d and pw == nw
    ctx_row = b
    tm_proj = 512
    tm_in = 1024

    cos_t, sin_t, cos_tt, sin_tt = _rope_tables(l)
    mean_mat = jnp.asarray(np.kron(np.eye(heads), np.full((HEAD_DIM, HEAD_DIM), 1.0 / HEAD_DIM)), BF16)
    pad = (-(b + 1)) % SUBLANES
    cc = jnp.concatenate([c, c_ctx[None], jnp.zeros((pad, d), F32)], axis=0)

    layer = 0
    q0, k0, v0, g0 = pw, pw + nw, pw + 2 * nw, pw + 3 * nw
    mod, w_all, w_qvt, bias = _prep(cc, ada_w[layer], ada_b[layer], w_in[layer], q0, v0, nw,
                                    rpb[layer], rows)
    mod3 = mod.reshape(mod.shape[0], N_MOD, d)
    qg_tt = jnp.broadcast_to(q_norm_g[layer][:, None], (HEAD_DIM, tm_in))
    kg_t = jnp.tile(k_norm_g[layer], heads)[None]
    n1 = norm1_g[layer][None]

    k_ctx, vt_ctx = _ctx_proj(ctx, mod3, n1, w_all, w_qvt, kg_t, mean_mat, k0 // nw, ctx_row)
    later_weights = (w_branch_pool[layer], w_branch_na[layer], w_out[layer],
                     mlp_w1[layer], mlp_w2[layer])
    (u, qrt, qpt, kr, vt, gp, gn), (wbp, wbn, wo, w1, w2) = _in_proj(
        x, mod3, n1, w_all, w_qvt, cos_t, sin_t, cos_tt, sin_tt, qg_tt, kg_t, mean_mat,
        (0, k0, g0, g0 + d), pw, tm_in, later_weights)
    na, pool = _mixer(u, qrt, qpt, kr, vt, k_ctx, vt_ctx, bias,
                      pool_w[layer].astype(BF16), pool_scale[layer][None])
    return _post(x, mod3, norm2_g[layer][None], na, pool, gp, gn, wbp, wbn, wo, w1, w2, tm=1024)
```

```python
import functools

import numpy as np
import jax
import jax.numpy as jnp
from jax import lax
from jax.experimental import pallas as pl
from jax.experimental.pallas import tpu as pltpu

GRID_W = 64
N_MOD = 6
POOL_WINDOWS = (2, 4, 8, 16)
HEAD_DIM = 64
NA_ROWS = 8
NA_COLS = 16
ROPE_THETA = 10000.0
EPS = 1e-6
MASK_VALUE = -1e30
LOG2E = 1.4426950408889634

V7X_VMEM_BYTES = 64 * 1024 * 1024
VMEM_LIMIT_BYTES = V7X_VMEM_BYTES - 2 * 1024 * 1024
SUBLANES = 8
BF16_ROWS = 16
POOL_HALO = 8
TILE_ROWS = 4
KEY_ROWS = TILE_ROWS + NA_ROWS

BF16 = jnp.bfloat16
F32 = jnp.float32


def _dot(a, b):
    return jnp.dot(a, b, preferred_element_type=F32)


def _dot_nt(a, b):
    return lax.dot_general(a, b, (((1,), (1,)), ((), ())), preferred_element_type=F32)


def _params(sem):
    return pltpu.CompilerParams(dimension_semantics=sem, vmem_limit_bytes=VMEM_LIMIT_BYTES)


def _rms(x):
    return x * lax.rsqrt(jnp.mean(x * x, axis=-1, keepdims=True) + EPS)


def _sigmoid_bf16(z):
    zb = z.astype(BF16)
    return 0.5 * jnp.tanh(0.5 * zb) + 0.5


def _prep_kernel(c_ref, aw_ref, ab_ref, w_rows_ref, w_cols_ref, el_ref, er_ref,
                 mod_ref, w_all_ref, w_qvt_ref, bias_ref, *, grid_rows):
    c = c_ref[...]
    s = (c * jax.nn.sigmoid(c)).astype(BF16)
    mod_ref[...] = _dot(s, aw_ref[...].astype(BF16)) + ab_ref[...]
    w_all_ref[...] = w_rows_ref[...].astype(BF16)
    w_qvt_ref[...] = w_cols_ref[...].T.astype(BF16)
    _bias_kernel(el_ref, er_ref, bias_ref, rows=grid_rows)


def _prep(cc, ada_w, ada_b, w, q0, v0, width, rpb, grid_rows):
    m, d = cc.shape
    n = ada_w.shape[1]
    lanes = 128
    heads = rpb.shape[0]
    steps = 2 * width // lanes
    assert steps == heads and n % (steps * lanes) == 0 and w.shape[0] % (steps * BF16_ROWS) == 0
    nb, rows = n // steps, w.shape[0] // steps
    per_seg = width // lanes
    col_block = lambda j: jnp.where(j < per_seg, q0 // lanes + j, v0 // lanes + j - per_seg)
    e_left, e_right = _bias_vectors(rpb)
    vec_spec = pl.BlockSpec((1,) + e_left.shape[1:], lambda j: (j, 0, 0))
    tq, tk = TILE_ROWS * GRID_W, KEY_ROWS * GRID_W
    return pl.pallas_call(
        functools.partial(_prep_kernel, grid_rows=grid_rows),
        grid=(steps,),
        in_specs=[pl.BlockSpec((m, d), lambda j: (0, 0)),
                  pl.BlockSpec((d, nb), lambda j: (0, j)),
                  pl.BlockSpec((1, nb), lambda j: (0, j)),
                  pl.BlockSpec((rows, w.shape[1]), lambda j: (j, 0)),
                  pl.BlockSpec((w.shape[0], lanes), lambda j: (0, col_block(j))),
                  vec_spec, vec_spec],
        out_specs=[pl.BlockSpec((m, nb), lambda j: (0, j)),
                   pl.BlockSpec((rows, w.shape[1]), lambda j: (j, 0)),
                   pl.BlockSpec((lanes, w.shape[0]), lambda j: (j, 0)),
                   pl.BlockSpec((3, 1, tk, tq), lambda j: (0, j, 0, 0))],
        out_shape=[jax.ShapeDtypeStruct((m, n), F32),
                   jax.ShapeDtypeStruct(w.shape, BF16),
                   jax.ShapeDtypeStruct((2 * width, w.shape[0]), BF16),
                   jax.ShapeDtypeStruct((3, heads, tk, tq), F32)],
        compiler_params=_params(("arbitrary",)),
        name="prep",
    )(cc, ada_w, ada_b.reshape(1, n), w, w, e_left, e_right)


def _head_norm(a, mean_mat, g):
    ms = _dot((a * a).astype(BF16), mean_mat)
    return a * lax.rsqrt(ms + EPS) * g


def _rope(a, cos, sin_signed, first_half):
    width = a.shape[-1]
    quarter = HEAD_DIM // 4
    up = pltpu.roll(a, width - quarter, axis=1)
    down = pltpu.roll(a, quarter, axis=1)
    swapped = jnp.where(first_half, up, down)
    return a * cos + swapped * sin_signed


def _modulated_norm(x, mod_ref, g_ref, shift_row):
    return _rms(x) * g_ref[...] * (1.0 + mod_ref[0, shift_row + 1:shift_row + 2, :]) \
        + mod_ref[0, shift_row:shift_row + 1, :]


def _ctx_kernel(x_ref, mod_ref, g_ref, wk_ref, wvt_ref, kg_ref, mean_ref, k_ref, vt_ref):
    hb = _modulated_norm(x_ref[0], mod_ref, g_ref, 0).astype(BF16)
    k_ref[0] = _head_norm(_dot(hb, wk_ref[...]), mean_ref[...], kg_ref[...]).astype(BF16)
    vt_ref[0] = _dot_nt(wvt_ref[...], hb).astype(BF16)


def _ctx_proj(ctx, mod3, norm_g, w_all, w_qvt, kg_t, mean_mat, key_block, ctx_row):
    b, n, d = ctx.shape
    nw = kg_t.shape[-1]
    sd = jax.ShapeDtypeStruct
    return pl.pallas_call(
        _ctx_kernel,
        grid=(b,),
        in_specs=[pl.BlockSpec((1, n, d), lambda i: (i, 0, 0)),
                  pl.BlockSpec((1, N_MOD, d), lambda i: (ctx_row, 0, 0)),
                  pl.BlockSpec((1, d), lambda i: (0, 0)),
                  pl.BlockSpec((d, nw), lambda i: (0, key_block)),
                  pl.BlockSpec((nw, d), lambda i: (1, 0)),
                  pl.BlockSpec((1, nw), lambda i: (0, 0)),
                  pl.BlockSpec((nw, nw), lambda i: (0, 0))],
        out_specs=[pl.BlockSpec((1, n, nw), lambda i: (i, 0, 0)),
                   pl.BlockSpec((1, nw, n), lambda i: (i, 0, 0))],
        out_shape=[sd((b, n, nw), BF16), sd((b, nw, n), BF16)],
        compiler_params=_params(("arbitrary",)),
        name="ctx_proj",
    )(ctx, mod3, norm_g, w_all, w_qvt, kg_t, mean_mat)


def _in_proj_kernel(*refs, col_starts, n_cast):
    (x_ref, mod_ref, g_ref, w_ref, wt_ref, cos_ref, sin_ref, cost_ref, sint_ref,
     qgt_ref, kg_ref, mean_ref) = refs[:12]
    cast_in = refs[12:12 + n_cast]
    u_ref, qrt_ref, qpt_ref, kr_ref, vt_ref, gp_ref, gn_ref = refs[12 + n_cast:19 + n_cast]
    cast_out = refs[19 + n_cast:]
    for src, dst in zip(cast_in, cast_out):
        dst[...] = src[...].astype(BF16)

    pw = u_ref.shape[-1]
    nw = kr_ref.shape[-1]
    d = x_ref.shape[-1]
    heads = nw // HEAD_DIM
    quarter = HEAD_DIM // 4
    scale = HEAD_DIM ** -0.5 * LOG2E
    u0, k0, gp0, gn0 = col_starts

    hb = _modulated_norm(x_ref[0], mod_ref, g_ref, 0).astype(BF16)

    cost, sint, qgt = cost_ref[...], sint_ref[...], qgt_ref[...]

    def query_heads(first, last):
        qt = _dot_nt(wt_ref[first * HEAD_DIM:last * HEAD_DIM, :], hb)
        for h in range(first, last):
            rows = slice(h * HEAD_DIM, (h + 1) * HEAD_DIM)
            z = qt[(h - first) * HEAD_DIM:(h - first + 1) * HEAD_DIM, :]
            qn = z * lax.rsqrt(jnp.mean(z * z, axis=0, keepdims=True) + EPS) * (qgt * scale)
            swapped = jnp.concatenate([qn[quarter:2 * quarter], qn[0:quarter],
                                       qn[3 * quarter:], qn[2 * quarter:3 * quarter]], axis=0)
            qpt_ref[0, rows, :] = qn.astype(BF16)
            qrt_ref[0, rows, :] = (qn * cost + swapped * sint).astype(BF16)

    z_gp = _dot(hb, w_ref[:, gp0:gp0 + d])
    query_heads(0, heads // 2)
    z_gn = _dot(hb, w_ref[:, gn0:gn0 + d])
    gp_ref[0] = _sigmoid_bf16(z_gp)
    query_heads(heads // 2, heads)
    z_k = _dot(hb, w_ref[:, k0:k0 + nw])
    gn_ref[0] = _sigmoid_bf16(z_gn)
    z_vt = _dot_nt(wt_ref[nw:, :], hb)

    reps = nw // cos_ref.shape[-1]
    cos = jnp.concatenate([cos_ref[...]] * reps, axis=1)
    sin = jnp.concatenate([sin_ref[...]] * reps, axis=1)
    lane = lax.broadcasted_iota(jnp.int32, (1, nw), 1)
    first_half = (lane % (HEAD_DIM // 2)) < quarter
    k = _head_norm(z_k, mean_ref[...], kg_ref[...])
    kr_ref[0] = _rope(k, cos, sin, first_half).astype(BF16)
    vt_ref[0] = z_vt.astype(BF16)
    u_ref[0] = _dot(hb, w_ref[:, u0:u0 + pw])


def _in_proj(x, mod3, norm_g, w_all, w_qvt, cos_t, sin_t, cos_tt, sin_tt, qg_tt, kg_t, mean_mat,
             col_starts, pw, tm, later_weights):
    b, l, d = x.shape
    nw = kg_t.shape[-1]
    tw = cos_t.shape[-1]
    nt = l // tm
    tok = lambda bi, i: (bi, i, 0)
    chan = lambda bi, i: (bi, 0, i)
    const = lambda bi, i: (0, 0)
    sd = jax.ShapeDtypeStruct
    slab_specs = []
    for w in later_weights:
        assert w.shape[0] % (b * nt * BF16_ROWS) == 0
        slab_specs.append(pl.BlockSpec((w.shape[0] // (b * nt), w.shape[1]),
                                       lambda bi, i: (bi * nt + i, 0)))
    outs = pl.pallas_call(
        functools.partial(_in_proj_kernel, col_starts=col_starts, n_cast=len(later_weights)),
        grid=(b, nt),
        in_specs=[pl.BlockSpec((1, tm, d), tok),
                  pl.BlockSpec((1, N_MOD, d), lambda bi, i: (bi, 0, 0)),
                  pl.BlockSpec((1, d), const),
                  pl.BlockSpec(w_all.shape, const, pipeline_mode=pl.Buffered(1)),
                  pl.BlockSpec(w_qvt.shape, const, pipeline_mode=pl.Buffered(1)),
                  pl.BlockSpec((tm, tw), lambda bi, i: (i, 0)),
                  pl.BlockSpec((tm, tw), lambda bi, i: (i, 0)),
                  pl.BlockSpec((HEAD_DIM, tm), lambda bi, i: (0, i)),
                  pl.BlockSpec((HEAD_DIM, tm), lambda bi, i: (0, i)),
                  pl.BlockSpec((HEAD_DIM, tm), const),
                  pl.BlockSpec((1, nw), const),
                  pl.BlockSpec((nw, nw), const)] + slab_specs,
        out_specs=[pl.BlockSpec((1, tm, pw), tok),
                   pl.BlockSpec((1, nw, tm), chan), pl.BlockSpec((1, nw, tm), chan),
                   pl.BlockSpec((1, tm, nw), tok),
                   pl.BlockSpec((1, nw, tm), chan),
                   pl.BlockSpec((1, tm, d), tok), pl.BlockSpec((1, tm, d), tok)] + slab_specs,
        out_shape=[sd((b, l, pw), F32),
                   sd((b, nw, l), BF16), sd((b, nw, l), BF16),
                   sd((b, l, nw), BF16),
                   sd((b, nw, l), BF16),
                   sd((b, l, d), BF16), sd((b, l, d), BF16)]
                  + [sd(w.shape, BF16) for w in later_weights],
        compiler_params=_params(("arbitrary", "arbitrary")),
        name="in_proj",
    )(x, mod3, norm_g, w_all, w_qvt, cos_t, sin_t, cos_tt, sin_tt, qg_tt, kg_t, mean_mat,
      *later_weights)
    return outs[:7], outs[7:]


def _mixer_kernel(u_ref, up_ref, un_ref, qrt_ref, qpt_ref,
                  k0_ref, k1_ref, k2_ref, vt0_ref, vt1_ref, vt2_ref, kc_ref, vct_ref, bias_ref,
                  pw_ref, ps_ref,
                  na_ref, pool_ref, nat_scr, e_scr, a2_scr, a4_scr, a8_scr, *, seq_len):
    i = pl.program_id(1)
    tm = u_ref.shape[1]
    nw = qrt_ref.shape[1]
    heads = nw // HEAD_DIM
    pair = 2 * HEAD_DIM

    zeros = jnp.zeros((HEAD_DIM, tm), BF16)
    ones = jnp.ones((BF16_ROWS, tm), BF16)
    k_refs = (k0_ref, k1_ref, k2_ref)
    vt_refs = (vt0_ref, vt1_ref, vt2_ref)

    def head_rows(h):
        return slice(h * HEAD_DIM, (h + 1) * HEAD_DIM)

    def scores(h):
        rows = head_rows(h)
        lanes = slice((h // 2) * pair, (h // 2 + 1) * pair)
        q_rot, q_plain = qrt_ref[0, rows, :], qpt_ref[0, rows, :]
        if h % 2 == 0:
            w_rot = jnp.concatenate([q_rot, zeros], axis=0)
            w_plain = jnp.concatenate([q_plain, zeros], axis=0)
        else:
            w_rot = jnp.concatenate([zeros, q_rot], axis=0)
            w_plain = jnp.concatenate([zeros, q_plain], axis=0)
        s = [(_dot(k_refs[j][0, :, lanes], w_rot) + bias_ref[0, h, j * tm:(j + 1) * tm, :]).astype(BF16)
             for j in range(3)]
        s.append(_dot(kc_ref[0, :, lanes], w_plain).astype(BF16))
        m = s[0].max(axis=0, keepdims=True)
        for sj in s[1:]:
            m = jnp.maximum(m, sj.max(axis=0, keepdims=True))
        return s, m

    def probabilities(s_and_m):
        s, m = s_and_m
        return [jnp.exp2(sj - m) for sj in s]

    def weighted_values(h, p):
        rows = head_rows(h)
        ot = _dot(jnp.concatenate([vct_ref[0, rows, :], ones], axis=0), p[3])
        for j in range(3):
            ot = ot + _dot(jnp.concatenate([vt_refs[j][0, rows, :], ones], axis=0), p[j])
        nat_scr[rows, :] = ot[:HEAD_DIM] / ot[HEAD_DIM:HEAD_DIM + 1]

    gd = pw_ref.shape[-1]
    hi = tm + 2 * POOL_HALO
    first = POOL_HALO
    g0 = slice(0, gd)

    def rows_at(ref, off, cols):
        return ref[first + off:first + off + tm, cols]

    def pool_fill():
        for scr in (e_scr, a2_scr, a4_scr):
            scr[hi:hi + SUBLANES, :] = jnp.zeros((SUBLANES, scr.shape[1]), F32)
        e_scr[0:first, :] = jnp.where(i > 0, up_ref[0], 0.0)
        e_scr[first:first + tm, :] = u_ref[0]
        e_scr[first + tm:hi, :] = jnp.where((i + 1) * tm < seq_len, un_ref[0], 0.0)
        a2_scr[0:hi, :] = e_scr[0:hi, gd:] + e_scr[1:hi + 1, gd:]

    def pool_double():
        a4_scr[0:hi, :] = a2_scr[0:hi, gd:] + a2_scr[2:hi + 2, gd:]
        a8_scr[0:hi, :] = a4_scr[0:hi, gd:] + a4_scr[4:hi + 4, gd:]

    def pool_group(g):
        w = POOL_WINDOWS[g]
        if g == 0:
            total = rows_at(e_scr, -1, g0) + rows_at(e_scr, 0, g0)
        elif g == 1:
            total = rows_at(a2_scr, -2, g0) + rows_at(a2_scr, 0, g0)
        elif g == 2:
            total = rows_at(a4_scr, -4, g0) + rows_at(a4_scr, 0, g0)
        else:
            total = rows_at(a8_scr, -8, g0) + rows_at(a8_scr, 0, g0)
        pos = i * tm + lax.broadcasted_iota(jnp.int32, (tm, 1), 0)
        cnt = (jnp.minimum(pos + w // 2, seq_len) - jnp.maximum(pos - w // 2, 0)).astype(F32)
        dlt = total / cnt - u_ref[0, :, g * gd:(g + 1) * gd]
        cols = slice(g * gd, (g + 1) * gd)
        return (_dot(dlt.astype(BF16), pw_ref[g]) * ps_ref[:, cols]).astype(BF16)

    pooled = []
    pool_stages = [pool_fill, pool_double] + [functools.partial(pool_group, g)
                                               for g in range(len(POOL_WINDOWS))]

    p_next = probabilities(scores(0))
    for h in range(heads):
        p_cur = p_next
        if h + 1 < heads:
            p_next = probabilities(scores(h + 1))
        weighted_values(h, p_cur)
        stage = h - (heads - len(pool_stages))
        if stage >= 0:
            out = pool_stages[stage]()
            if out is not None:
                pooled.append(out)
        if h % 2 == 1:
            slab = slice((h - 1) * HEAD_DIM, (h + 1) * HEAD_DIM)
            na_ref[0, :, slab] = nat_scr[slab, :].T.astype(BF16)

    pool_ref[0] = jnp.concatenate(pooled, axis=1)


def _mixer(u, qrt, qpt, kr, vt, kc, vct, bias, pool_w_bf, pool_scale):
    b, l, pw = u.shape
    nw = kr.shape[-1]
    gd = pool_w_bf.shape[-1]
    tm = TILE_ROWS * GRID_W
    nt = l // tm
    hb = tm // POOL_HALO
    n_halo = l // POOL_HALO
    nc = kc.shape[1]
    assert pw == len(POOL_WINDOWS) * gd and POOL_HALO == max(POOL_WINDOWS) // 2

    tok = lambda bi, i: (bi, i, 0)
    chan = lambda bi, i: (bi, 0, i)
    const2 = lambda bi, i: (0, 0)
    first_block = lambda i: jnp.clip(i - 1, 0, nt - 3)

    def k_spec(j):
        return pl.BlockSpec((1, tm, nw), lambda bi, i: (bi, first_block(i) + j, 0))

    def vt_spec(j):
        return pl.BlockSpec((1, nw, tm), lambda bi, i: (bi, 0, first_block(i) + j))

    def bias_idx(bi, i):
        return (jnp.where(i == 0, 0, jnp.where(i == nt - 1, 2, 1)), 0, 0, 0)

    in_specs = [
        pl.BlockSpec((1, tm, pw), tok),
        pl.BlockSpec((1, POOL_HALO, pw), lambda bi, i: (bi, jnp.maximum(i * hb - 1, 0), 0)),
        pl.BlockSpec((1, POOL_HALO, pw), lambda bi, i: (bi, jnp.minimum((i + 1) * hb, n_halo - 1), 0)),
        pl.BlockSpec((1, nw, tm), chan),
        pl.BlockSpec((1, nw, tm), chan),
        k_spec(0), k_spec(1), k_spec(2),
        vt_spec(0), vt_spec(1), vt_spec(2),
        pl.BlockSpec((1, nc, nw), lambda bi, i: (bi, 0, 0)),
        pl.BlockSpec((1, nw, nc), lambda bi, i: (bi, 0, 0)),
        pl.BlockSpec((1,) + bias.shape[1:], bias_idx),
        pl.BlockSpec(pool_w_bf.shape, lambda bi, i: (0, 0, 0)),
        pl.BlockSpec((1, pw), const2),
    ]
    n_scr = tm + 2 * POOL_HALO + SUBLANES
    return pl.pallas_call(
        functools.partial(_mixer_kernel, seq_len=l),
        grid=(b, nt),
        in_specs=in_specs,
        out_specs=[pl.BlockSpec((1, tm, nw), tok), pl.BlockSpec((1, tm, pw), tok)],
        out_shape=[jax.ShapeDtypeStruct((b, l, nw), BF16), jax.ShapeDtypeStruct((b, l, pw), BF16)],
        scratch_shapes=[pltpu.VMEM((nw, tm), F32),
                        pltpu.VMEM((n_scr, pw), F32),
                        pltpu.VMEM((n_scr, pw - gd), F32),
                        pltpu.VMEM((n_scr, pw - 2 * gd), F32),
                        pltpu.VMEM((n_scr, pw - 3 * gd), F32)],
        compiler_params=_params(("arbitrary", "arbitrary")),
        name="mixer",
    )(u, u, u, qrt, qpt, kr, kr, kr, vt, vt, vt, kc, vct, bias, pool_w_bf, pool_scale)


def _post_kernel(x_ref, mod_ref, g_ref, na_ref, pool_ref, gp_ref, gn_ref,
                 wbp_ref, wbn_ref, wo_ref, w1_ref, w2_ref, o_ref, *, chunk):
    merged = (gp_ref[0].astype(F32) * _dot(pool_ref[0], wbp_ref[...])
              + gn_ref[0].astype(F32) * _dot(na_ref[0], wbn_ref[...]))
    x = x_ref[0] + mod_ref[0, 2:3, :] * _dot(merged.astype(BF16), wo_ref[...])
    hb = _modulated_norm(x, mod_ref, g_ref, 3).astype(BF16)
    acc = jnp.zeros(x.shape, F32)
    for c0 in range(0, w1_ref.shape[1], chunk):
        a = jnp.maximum(_dot(hb, w1_ref[:, c0:c0 + chunk]), 0.0)
        acc = acc + _dot((a * a).astype(BF16), w2_ref[c0:c0 + chunk, :])
    o_ref[0] = x + mod_ref[0, 5:6, :] * acc


def _post(x, mod3, norm_g, na, pool, gp, gn, wbp, wbn, wo, w1, w2, tm):
    b, l, d = x.shape
    tok = lambda bi, i: (bi, i, 0)
    const = lambda bi, i: (0, 0)
    resident = lambda w: pl.BlockSpec(w.shape, const, pipeline_mode=pl.Buffered(1))
    return pl.pallas_call(
        functools.partial(_post_kernel, chunk=d),
        grid=(b, l // tm),
        in_specs=[pl.BlockSpec((1, tm, d), tok),
                  pl.BlockSpec((1, N_MOD, d), lambda bi, i: (bi, 0, 0)),
                  pl.BlockSpec((1, d), const),
                  pl.BlockSpec((1, tm, na.shape[-1]), tok),
                  pl.BlockSpec((1, tm, pool.shape[-1]), tok),
                  pl.BlockSpec((1, tm, d), tok),
                  pl.BlockSpec((1, tm, d), tok),
                  resident(wbp), resident(wbn), resident(wo), resident(w1), resident(w2)],
        out_specs=pl.BlockSpec((1, tm, d), tok),
        out_shape=jax.ShapeDtypeStruct((b, l, d), F32),
        compiler_params=_params(("arbitrary", "arbitrary")),
        name="post",
    )(x, mod3, norm_g, na, pool, gp, gn, wbp, wbn, wo, w1, w2)


def _rope_tables(seq_len):
    n_freq = HEAD_DIM // 4
    t = np.arange(seq_len)
    inv = (ROPE_THETA ** (-np.arange(n_freq, dtype=np.float32) / n_freq)).astype(np.float32)
    ang_row = (t // GRID_W).astype(np.float32)[:, None] * inv
    ang_col = (t % GRID_W).astype(np.float32)[:, None] * inv
    cr, sr, cc, sc = np.cos(ang_row), np.sin(ang_row), np.cos(ang_col), np.sin(ang_col)
    cos = np.concatenate([cr, cr, cc, cc], axis=1).astype(np.float32)
    sin = np.concatenate([-sr, sr, -sc, sc], axis=1).astype(np.float32)
    return (jnp.asarray(np.tile(cos, (1, 2))), jnp.asarray(np.tile(sin, (1, 2))),
            jnp.asarray(np.ascontiguousarray(cos.T)), jnp.asarray(np.ascontiguousarray(sin.T)))


def _bias_kernel(el_ref, er_ref, o_ref, *, rows):
    lanes = 2 * GRID_W
    kc = lax.broadcasted_iota(jnp.int32, (GRID_W, lanes), 0)
    lane = lax.broadcasted_iota(jnp.int32, (GRID_W, lanes), 1)
    qc = lane % GRID_W
    c0 = jnp.clip(qc - NA_COLS // 2, 0, GRID_W - NA_COLS)
    col_ok = (kc >= c0) & (kc < c0 + NA_COLS)
    left = lane < GRID_W
    masked = jnp.full((GRID_W, lanes), MASK_VALUE, F32)

    cache = {}

    def toeplitz(side, dr):
        if (side, dr) not in cache:
            ref = er_ref if side else el_ref
            dd = dr + NA_ROWS - 1
            vec = jnp.broadcast_to(ref[0, dd:dd + 1, :], (GRID_W, lanes))
            cache[(side, dr)] = pltpu.roll(vec, 0, axis=1, stride=1, stride_axis=0) * LOG2E
        return cache[(side, dr)]

    nt = rows // TILE_ROWS
    for t, tile in enumerate((0, nt // 2, nt - 1)):
        r = tile * TILE_ROWS
        ks = min(max(r - TILE_ROWS, 0), rows - KEY_ROWS)
        for j in range(KEY_ROWS):
            kr = ks + j
            for ip in range(TILE_ROWS // 2):
                halves = []
                for side in range(2):
                    qr = r + 2 * ip + side
                    r0 = min(max(qr - NA_ROWS // 2, 0), rows - NA_ROWS)
                    ok = r0 <= kr < r0 + NA_ROWS
                    halves.append(toeplitz(side, kr - qr) if ok else masked)
                blk = jnp.where(col_ok, jnp.where(left, halves[0], halves[1]), MASK_VALUE)
                o_ref[t, 0, j * GRID_W:(j + 1) * GRID_W, ip * lanes:(ip + 1) * lanes] = blk


def _bias_vectors(rpb):
    heads, nr, ncol = rpb.shape
    lanes = 2 * GRID_W
    half = NA_COLS - 1
    zeros = lambda n: jnp.zeros((heads, nr, n), F32)
    flipped = rpb[..., ::-1]
    e_left = jnp.concatenate([flipped[..., half:], zeros(lanes - ncol), flipped[..., :half]], axis=-1)
    e_right = jnp.concatenate([zeros(GRID_W - half), flipped, zeros(lanes - GRID_W + half - ncol)], axis=-1)
    pad = (-nr) % SUBLANES
    e_left = jnp.pad(e_left, ((0, 0), (0, pad), (0, 0)))
    e_right = jnp.pad(e_right, ((0, 0), (0, pad), (0, 0)))
    return e_left, e_right


def kernel(x, c, ctx, c_ctx, ada_w, ada_b, norm1_g, norm2_g, w_in, pool_w, pool_scale,
           q_norm_g, k_norm_g, rpb, w_branch_pool, w_branch_na, w_out, mlp_w1, mlp_w2):
    b, l, d = x.shape
    depth = ada_w.shape[0]
    pw = pool_scale.shape[-1]
    nw = w_branch_na.shape[1]
    heads = nw // HEAD_DIM
    rows = l // GRID_W
    assert depth == 1, "the context-stream update is only needed when another layer follows"
    assert l % (TILE_ROWS * GRID_W) == 0 and rows >= KEY_ROWS
    assert w_in.shape[-1] == pw + 3 * nw + 2 * d and pw == nw
    ctx_row = b
    tm_proj = 1024

    cos_t, sin_t, cos_tt, sin_tt = _rope_tables(l)
    mean_mat = jnp.asarray(np.kron(np.eye(heads), np.full((HEAD_DIM, HEAD_DIM), 1.0 / HEAD_DIM)), BF16)
    pad = (-(b + 1)) % SUBLANES
    cc = jnp.concatenate([c, c_ctx[None], jnp.zeros((pad, d), F32)], axis=0)

    layer = 0
    q0, k0, v0, g0 = pw, pw + nw, pw + 2 * nw, pw + 3 * nw
    mod, w_all, w_qvt, bias = _prep(cc, ada_w[layer], ada_b[layer], w_in[layer], q0, v0, nw,
                                    rpb[layer], rows)
    mod3 = mod.reshape(mod.shape[0], N_MOD, d)
    qg_tt = jnp.broadcast_to(q_norm_g[layer][:, None], (HEAD_DIM, tm_proj))
    kg_t = jnp.tile(k_norm_g[layer], heads)[None]
    n1 = norm1_g[layer][None]

    k_ctx, vt_ctx = _ctx_proj(ctx, mod3, n1, w_all, w_qvt, kg_t, mean_mat, k0 // nw, ctx_row)
    later_weights = (w_branch_pool[layer], w_branch_na[layer], w_out[layer],
                     mlp_w1[layer], mlp_w2[layer])
    (u, qrt, qpt, kr, vt, gp, gn), (wbp, wbn, wo, w1, w2) = _in_proj(
        x, mod3, n1, w_all, w_qvt, cos_t, sin_t, cos_tt, sin_tt, qg_tt, kg_t, mean_mat,
        (0, k0, g0, g0 + d), pw, tm_proj, later_weights)
    na, pool = _mixer(u, qrt, qpt, kr, vt, k_ctx, vt_ctx, bias,
                      pool_w[layer].astype(BF16), pool_scale[layer][None])
    return _post(x, mod3, norm2_g[layer][None], na, pool, gp, gn, wbp, wbn, wo, w1, w2, tm=tm_proj)
```

```python
import functools

import numpy as np
import jax
import jax.numpy as jnp
from jax import lax
from jax.experimental import pallas as pl
from jax.experimental.pallas import tpu as pltpu

GRID_W = 64
N_MOD = 6
POOL_WINDOWS = (2, 4, 8, 16)
HEAD_DIM = 64
NA_ROWS = 8
NA_COLS = 16
ROPE_THETA = 10000.0
EPS = 1e-6
MASK_VALUE = -1e30
LOG2E = 1.4426950408889634

V7X_VMEM_BYTES = 64 * 1024 * 1024
VMEM_LIMIT_BYTES = V7X_VMEM_BYTES - 2 * 1024 * 1024
SUBLANES = 8
BF16_ROWS = 16
POOL_HALO = 8
TILE_ROWS = 4
KEY_ROWS = TILE_ROWS + NA_ROWS

BF16 = jnp.bfloat16
F32 = jnp.float32


def _dot(a, b):
    return jnp.dot(a, b, preferred_element_type=F32)


def _dot_nt(a, b):
    return lax.dot_general(a, b, (((1,), (1,)), ((), ())), preferred_element_type=F32)


def _params(sem):
    return pltpu.CompilerParams(dimension_semantics=sem, vmem_limit_bytes=VMEM_LIMIT_BYTES)


def _rms(x):
    return x * lax.rsqrt(jnp.mean(x * x, axis=-1, keepdims=True) + EPS)


def _sigmoid_bf16(z):
    zb = z.astype(BF16)
    return 0.5 * jnp.tanh(0.5 * zb) + 0.5


def _prep_kernel(c_ref, aw_ref, ab_ref, w_rows_ref, w_cols_ref, el_ref, er_ref,
                 mod_ref, w_all_ref, w_qvt_ref, bias_ref, *, grid_rows):
    c = c_ref[...]
    s = (c * jax.nn.sigmoid(c)).astype(BF16)
    mod_ref[0] = _dot(s, aw_ref[...].astype(BF16)) + ab_ref[...]
    w_all_ref[...] = w_rows_ref[...].astype(BF16)
    w_qvt_ref[...] = w_cols_ref[...].T.astype(BF16)
    _bias_kernel(el_ref, er_ref, bias_ref, rows=grid_rows)


def _prep(cc, ada_w, ada_b, w, q0, v0, width, rpb, grid_rows):
    m, d = cc.shape
    n = ada_w.shape[1]
    lanes = 128
    heads = rpb.shape[0]
    steps = 2 * width // lanes
    assert steps == heads and steps >= N_MOD and n == N_MOD * d
    assert w.shape[0] % (steps * BF16_ROWS) == 0
    rows = w.shape[0] // steps
    mod_block = lambda j: jnp.minimum(j, N_MOD - 1)
    per_seg = width // lanes
    col_block = lambda j: jnp.where(j < per_seg, q0 // lanes + j, v0 // lanes + j - per_seg)
    e_left, e_right = _bias_vectors(rpb)
    vec_spec = pl.BlockSpec((1,) + e_left.shape[1:], lambda j: (j, 0, 0))
    tq, tk = TILE_ROWS * GRID_W, KEY_ROWS * GRID_W
    return pl.pallas_call(
        functools.partial(_prep_kernel, grid_rows=grid_rows),
        grid=(steps,),
        in_specs=[pl.BlockSpec((m, d), lambda j: (0, 0)),
                  pl.BlockSpec((d, d), lambda j: (0, mod_block(j))),
                  pl.BlockSpec((1, d), lambda j: (0, mod_block(j))),
                  pl.BlockSpec((rows, w.shape[1]), lambda j: (j, 0)),
                  pl.BlockSpec((w.shape[0], lanes), lambda j: (0, col_block(j))),
                  vec_spec, vec_spec],
        out_specs=[pl.BlockSpec((1, m, d), lambda j: (mod_block(j), 0, 0)),
                   pl.BlockSpec((rows, w.shape[1]), lambda j: (j, 0)),
                   pl.BlockSpec((lanes, w.shape[0]), lambda j: (j, 0)),
                   pl.BlockSpec((3, 1, tk, tq), lambda j: (0, j, 0, 0))],
        out_shape=[jax.ShapeDtypeStruct((N_MOD, m, d), F32),
                   jax.ShapeDtypeStruct(w.shape, BF16),
                   jax.ShapeDtypeStruct((2 * width, w.shape[0]), BF16),
                   jax.ShapeDtypeStruct((3, heads, tk, tq), F32)],
        compiler_params=_params(("arbitrary",)),
        name="prep",
    )(cc, ada_w, ada_b.reshape(1, n), w, w, e_left, e_right)


def _head_norm(a, mean_mat, g):
    ms = _dot((a * a).astype(BF16), mean_mat)
    return a * lax.rsqrt(ms + EPS) * g


def _rope(a, cos, sin_signed, first_half):
    width = a.shape[-1]
    quarter = HEAD_DIM // 4
    up = pltpu.roll(a, width - quarter, axis=1)
    down = pltpu.roll(a, quarter, axis=1)
    swapped = jnp.where(first_half, up, down)
    return a * cos + swapped * sin_signed


def _mod(mod_ref, k, row):
    return mod_ref[k, pl.ds(row, 1), :]


def _modulated_norm(x, mod_ref, g_ref, shift_k, row):
    return _rms(x) * g_ref[...] * (1.0 + _mod(mod_ref, shift_k + 1, row)) + _mod(mod_ref, shift_k, row)


def _ctx_kernel(x_ref, mod_ref, g_ref, wk_ref, wvt_ref, kg_ref, mean_ref, k_ref, vt_ref, *, ctx_row):
    hb = _modulated_norm(x_ref[0], mod_ref, g_ref, 0, ctx_row).astype(BF16)
    k_ref[0] = _head_norm(_dot(hb, wk_ref[...]), mean_ref[...], kg_ref[...]).astype(BF16)
    vt_ref[0] = _dot_nt(wvt_ref[...], hb).astype(BF16)


def _ctx_proj(ctx, mod3, norm_g, w_all, w_qvt, kg_t, mean_mat, key_block, ctx_row):
    b, n, d = ctx.shape
    nw = kg_t.shape[-1]
    sd = jax.ShapeDtypeStruct
    return pl.pallas_call(
        functools.partial(_ctx_kernel, ctx_row=ctx_row),
        grid=(b,),
        in_specs=[pl.BlockSpec((1, n, d), lambda i: (i, 0, 0)),
                  pl.BlockSpec(mod3.shape, lambda i: (0, 0, 0)),
                  pl.BlockSpec((1, d), lambda i: (0, 0)),
                  pl.BlockSpec((d, nw), lambda i: (0, key_block)),
                  pl.BlockSpec((nw, d), lambda i: (1, 0)),
                  pl.BlockSpec((1, nw), lambda i: (0, 0)),
                  pl.BlockSpec((nw, nw), lambda i: (0, 0))],
        out_specs=[pl.BlockSpec((1, n, nw), lambda i: (i, 0, 0)),
                   pl.BlockSpec((1, nw, n), lambda i: (i, 0, 0))],
        out_shape=[sd((b, n, nw), BF16), sd((b, nw, n), BF16)],
        compiler_params=_params(("arbitrary",)),
        name="ctx_proj",
    )(ctx, mod3, norm_g, w_all, w_qvt, kg_t, mean_mat)


def _in_proj_kernel(*refs, col_starts, n_cast):
    (x_ref, mod_ref, g_ref, w_ref, wt_ref, cos_ref, sin_ref, cost_ref, sint_ref,
     qgt_ref, kg_ref, mean_ref) = refs[:12]
    cast_in = refs[12:12 + n_cast]
    u_ref, qrt_ref, qpt_ref, kr_ref, vt_ref, gp_ref, gn_ref = refs[12 + n_cast:19 + n_cast]
    cast_out = refs[19 + n_cast:]
    for src, dst in zip(cast_in, cast_out):
        dst[...] = src[...].astype(BF16)

    pw = u_ref.shape[-1]
    nw = kr_ref.shape[-1]
    d = x_ref.shape[-1]
    heads = nw // HEAD_DIM
    quarter = HEAD_DIM // 4
    scale = HEAD_DIM ** -0.5 * LOG2E
    u0, k0, gp0, gn0 = col_starts

    hb = _modulated_norm(x_ref[0], mod_ref, g_ref, 0, pl.program_id(0)).astype(BF16)

    cost, sint, qgt = cost_ref[...], sint_ref[...], qgt_ref[...]

    def query_heads(first, last):
        qt = _dot_nt(wt_ref[first * HEAD_DIM:last * HEAD_DIM, :], hb)
        for h in range(first, last):
            rows = slice(h * HEAD_DIM, (h + 1) * HEAD_DIM)
            z = qt[(h - first) * HEAD_DIM:(h - first + 1) * HEAD_DIM, :]
            qn = z * lax.rsqrt(jnp.mean(z * z, axis=0, keepdims=True) + EPS) * (qgt * scale)
            swapped = jnp.concatenate([qn[quarter:2 * quarter], qn[0:quarter],
                                       qn[3 * quarter:], qn[2 * quarter:3 * quarter]], axis=0)
            qpt_ref[0, rows, :] = qn.astype(BF16)
            qrt_ref[0, rows, :] = (qn * cost + swapped * sint).astype(BF16)

    z_gp = _dot(hb, w_ref[:, gp0:gp0 + d])
    query_heads(0, heads // 2)
    z_gn = _dot(hb, w_ref[:, gn0:gn0 + d])
    gp_ref[0] = _sigmoid_bf16(z_gp)
    query_heads(heads // 2, heads)
    z_k = _dot(hb, w_ref[:, k0:k0 + nw])
    gn_ref[0] = _sigmoid_bf16(z_gn)
    z_vt = _dot_nt(wt_ref[nw:, :], hb)

    reps = nw // cos_ref.shape[-1]
    cos = jnp.concatenate([cos_ref[...]] * reps, axis=1)
    sin = jnp.concatenate([sin_ref[...]] * reps, axis=1)
    lane = lax.broadcasted_iota(jnp.int32, (1, nw), 1)
    first_half = (lane % (HEAD_DIM // 2)) < quarter
    k = _head_norm(z_k, mean_ref[...], kg_ref[...])
    kr_ref[0] = _rope(k, cos, sin, first_half).astype(BF16)
    vt_ref[0] = z_vt.astype(BF16)
    u_ref[0] = _dot(hb, w_ref[:, u0:u0 + pw])


def _in_proj(x, mod3, norm_g, w_all, w_qvt, cos_t, sin_t, cos_tt, sin_tt, qg_tt, kg_t, mean_mat,
             col_starts, pw, tm, later_weights):
    b, l, d = x.shape
    nw = kg_t.shape[-1]
    tw = cos_t.shape[-1]
    nt = l // tm
    tok = lambda bi, i: (bi, i, 0)
    chan = lambda bi, i: (bi, 0, i)
    const = lambda bi, i: (0, 0)
    sd = jax.ShapeDtypeStruct
    slab_specs = []
    for w in later_weights:
        assert w.shape[0] % (b * nt * BF16_ROWS) == 0
        slab_specs.append(pl.BlockSpec((w.shape[0] // (b * nt), w.shape[1]),
                                       lambda bi, i: (bi * nt + i, 0)))
    outs = pl.pallas_call(
        functools.partial(_in_proj_kernel, col_starts=col_starts, n_cast=len(later_weights)),
        grid=(b, nt),
        in_specs=[pl.BlockSpec((1, tm, d), tok),
                  pl.BlockSpec(mod3.shape, lambda bi, i: (0, 0, 0)),
                  pl.BlockSpec((1, d), const),
                  pl.BlockSpec(w_all.shape, const, pipeline_mode=pl.Buffered(1)),
                  pl.BlockSpec(w_qvt.shape, const, pipeline_mode=pl.Buffered(1)),
                  pl.BlockSpec((tm, tw), lambda bi, i: (i, 0)),
                  pl.BlockSpec((tm, tw), lambda bi, i: (i, 0)),
                  pl.BlockSpec((HEAD_DIM, tm), lambda bi, i: (0, i)),
                  pl.BlockSpec((HEAD_DIM, tm), lambda bi, i: (0, i)),
                  pl.BlockSpec((HEAD_DIM, 1), const),
                  pl.BlockSpec((1, nw), const),
                  pl.BlockSpec((nw, nw), const)] + slab_specs,
        out_specs=[pl.BlockSpec((1, tm, pw), tok),
                   pl.BlockSpec((1, nw, tm), chan), pl.BlockSpec((1, nw, tm), chan),
                   pl.BlockSpec((1, tm, nw), tok),
                   pl.BlockSpec((1, nw, tm), chan),
                   pl.BlockSpec((1, tm, d), tok), pl.BlockSpec((1, tm, d), tok)] + slab_specs,
        out_shape=[sd((b, l, pw), F32),
                   sd((b, nw, l), BF16), sd((b, nw, l), BF16),
                   sd((b, l, nw), BF16),
                   sd((b, nw, l), BF16),
                   sd((b, l, d), BF16), sd((b, l, d), BF16)]
                  + [sd(w.shape, BF16) for w in later_weights],
        compiler_params=_params(("arbitrary", "arbitrary")),
        name="in_proj",
    )(x, mod3, norm_g, w_all, w_qvt, cos_t, sin_t, cos_tt, sin_tt, qg_tt, kg_t, mean_mat,
      *later_weights)
    return outs[:7], outs[7:]


def _mixer_kernel(u_ref, up_ref, un_ref, qrt_ref, qpt_ref,
                  k0_ref, k1_ref, k2_ref, vt0_ref, vt1_ref, vt2_ref, kc_ref, vct_ref, bias_ref,
                  pw_ref, ps_ref,
                  na_ref, pool_ref, nat_scr, e_scr, a2_scr, a4_scr, a8_scr, *, seq_len):
    i = pl.program_id(1)
    tm = u_ref.shape[1]
    nw = qrt_ref.shape[1]
    heads = nw // HEAD_DIM
    pair = 2 * HEAD_DIM

    zeros = jnp.zeros((HEAD_DIM, tm), BF16)
    ones = jnp.ones((BF16_ROWS, tm), BF16)
    k_refs = (k0_ref, k1_ref, k2_ref)
    vt_refs = (vt0_ref, vt1_ref, vt2_ref)

    def head_rows(h):
        return slice(h * HEAD_DIM, (h + 1) * HEAD_DIM)

    def scores(h):
        rows = head_rows(h)
        lanes = slice((h // 2) * pair, (h // 2 + 1) * pair)
        q_rot, q_plain = qrt_ref[0, rows, :], qpt_ref[0, rows, :]
        if h % 2 == 0:
            w_rot = jnp.concatenate([q_rot, zeros], axis=0)
            w_plain = jnp.concatenate([q_plain, zeros], axis=0)
        else:
            w_rot = jnp.concatenate([zeros, q_rot], axis=0)
            w_plain = jnp.concatenate([zeros, q_plain], axis=0)
        s = [(_dot(k_refs[j][0, :, lanes], w_rot) + bias_ref[0, h, j * tm:(j + 1) * tm, :]).astype(BF16)
             for j in range(3)]
        s.append(_dot(kc_ref[0, :, lanes], w_plain).astype(BF16))
        m = s[0].max(axis=0, keepdims=True)
        for sj in s[1:]:
            m = jnp.maximum(m, sj.max(axis=0, keepdims=True))
        return s, m

    def probabilities(s_and_m):
        s, m = s_and_m
        return [jnp.exp2(sj - m) for sj in s]

    def weighted_values(h, p):
        rows = head_rows(h)
        ot = _dot(jnp.concatenate([vct_ref[0, rows, :], ones], axis=0), p[3])
        for j in range(3):
            ot = ot + _dot(jnp.concatenate([vt_refs[j][0, rows, :], ones], axis=0), p[j])
        nat_scr[rows, :] = ot[:HEAD_DIM] / ot[HEAD_DIM:HEAD_DIM + 1]

    gd = pw_ref.shape[-1]
    hi = tm + 2 * POOL_HALO
    first = POOL_HALO
    g0 = slice(0, gd)

    def rows_at(ref, off, cols):
        return ref[first + off:first + off + tm, cols]

    def pool_fill():
        for scr in (e_scr, a2_scr, a4_scr):
            scr[hi:hi + SUBLANES, :] = jnp.zeros((SUBLANES, scr.shape[1]), F32)
        e_scr[0:first, :] = jnp.where(i > 0, up_ref[0], 0.0)
        e_scr[first:first + tm, :] = u_ref[0]
        e_scr[first + tm:hi, :] = jnp.where((i + 1) * tm < seq_len, un_ref[0], 0.0)
        a2_scr[0:hi, :] = e_scr[0:hi, gd:] + e_scr[1:hi + 1, gd:]

    def pool_double():
        a4_scr[0:hi, :] = a2_scr[0:hi, gd:] + a2_scr[2:hi + 2, gd:]
        a8_scr[0:hi, :] = a4_scr[0:hi, gd:] + a4_scr[4:hi + 4, gd:]

    def pool_group(g):
        w = POOL_WINDOWS[g]
        if g == 0:
            total = rows_at(e_scr, -1, g0) + rows_at(e_scr, 0, g0)
        elif g == 1:
            total = rows_at(a2_scr, -2, g0) + rows_at(a2_scr, 0, g0)
        elif g == 2:
            total = rows_at(a4_scr, -4, g0) + rows_at(a4_scr, 0, g0)
        else:
            total = rows_at(a8_scr, -8, g0) + rows_at(a8_scr, 0, g0)
        pos = i * tm + lax.broadcasted_iota(jnp.int32, (tm, 1), 0)
        cnt = (jnp.minimum(pos + w // 2, seq_len) - jnp.maximum(pos - w // 2, 0)).astype(F32)
        dlt = total / cnt - u_ref[0, :, g * gd:(g + 1) * gd]
        cols = slice(g * gd, (g + 1) * gd)
        return (_dot(dlt.astype(BF16), pw_ref[g].astype(BF16)) * ps_ref[:, cols]).astype(BF16)

    pooled = []
    pool_stages = [pool_fill, pool_double] + [functools.partial(pool_group, g)
                                               for g in range(len(POOL_WINDOWS))]

    p_next = probabilities(scores(0))
    for h in range(heads):
        p_cur = p_next
        if h + 1 < heads:
            p_next = probabilities(scores(h + 1))
        weighted_values(h, p_cur)
        stage = h - (heads - len(pool_stages))
        if stage >= 0:
            out = pool_stages[stage]()
            if out is not None:
                pooled.append(out)
        if h % 2 == 1:
            slab = slice((h - 1) * HEAD_DIM, (h + 1) * HEAD_DIM)
            na_ref[0, :, slab] = nat_scr[slab, :].T.astype(BF16)

    pool_ref[0] = jnp.concatenate(pooled, axis=1)


def _mixer(u, qrt, qpt, kr, vt, kc, vct, bias, pool_w_bf, pool_scale):
    b, l, pw = u.shape
    nw = kr.shape[-1]
    gd = pool_w_bf.shape[-1]
    tm = TILE_ROWS * GRID_W
    nt = l // tm
    hb = tm // POOL_HALO
    n_halo = l // POOL_HALO
    nc = kc.shape[1]
    assert pw == len(POOL_WINDOWS) * gd and POOL_HALO == max(POOL_WINDOWS) // 2

    tok = lambda bi, i: (bi, i, 0)
    chan = lambda bi, i: (bi, 0, i)
    const2 = lambda bi, i: (0, 0)
    first_block = lambda i: jnp.clip(i - 1, 0, nt - 3)

    def k_spec(j):
        return pl.BlockSpec((1, tm, nw), lambda bi, i: (bi, first_block(i) + j, 0))

    def vt_spec(j):
        return pl.BlockSpec((1, nw, tm), lambda bi, i: (bi, 0, first_block(i) + j))

    def bias_idx(bi, i):
        return (jnp.where(i == 0, 0, jnp.where(i == nt - 1, 2, 1)), 0, 0, 0)

    in_specs = [
        pl.BlockSpec((1, tm, pw), tok),
        pl.BlockSpec((1, POOL_HALO, pw), lambda bi, i: (bi, jnp.maximum(i * hb - 1, 0), 0)),
        pl.BlockSpec((1, POOL_HALO, pw), lambda bi, i: (bi, jnp.minimum((i + 1) * hb, n_halo - 1), 0)),
        pl.BlockSpec((1, nw, tm), chan),
        pl.BlockSpec((1, nw, tm), chan),
        k_spec(0), k_spec(1), k_spec(2),
        vt_spec(0), vt_spec(1), vt_spec(2),
        pl.BlockSpec((1, nc, nw), lambda bi, i: (bi, 0, 0)),
        pl.BlockSpec((1, nw, nc), lambda bi, i: (bi, 0, 0)),
        pl.BlockSpec((1,) + bias.shape[1:], bias_idx),
        pl.BlockSpec(pool_w_bf.shape, lambda bi, i: (0, 0, 0)),
        pl.BlockSpec((1, pw), const2),
    ]
    n_scr = tm + 2 * POOL_HALO + SUBLANES
    return pl.pallas_call(
        functools.partial(_mixer_kernel, seq_len=l),
        grid=(b, nt),
        in_specs=in_specs,
        out_specs=[pl.BlockSpec((1, tm, nw), tok), pl.BlockSpec((1, tm, pw), tok)],
        out_shape=[jax.ShapeDtypeStruct((b, l, nw), BF16), jax.ShapeDtypeStruct((b, l, pw), BF16)],
        scratch_shapes=[pltpu.VMEM((nw, tm), F32),
                        pltpu.VMEM((n_scr, pw), F32),
                        pltpu.VMEM((n_scr, pw - gd), F32),
                        pltpu.VMEM((n_scr, pw - 2 * gd), F32),
                        pltpu.VMEM((n_scr, pw - 3 * gd), F32)],
        compiler_params=_params(("arbitrary", "arbitrary")),
        name="mixer",
    )(u, u, u, qrt, qpt, kr, kr, kr, vt, vt, vt, kc, vct, bias, pool_w_bf, pool_scale)


def _post_kernel(x_ref, mod_ref, g_ref, na_ref, pool_ref, gp_ref, gn_ref,
                 wbp_ref, wbn_ref, wo_ref, w1_ref, w2_ref, o_ref, *, chunk):
    merged = (gp_ref[0].astype(F32) * _dot(pool_ref[0], wbp_ref[...])
              + gn_ref[0].astype(F32) * _dot(na_ref[0], wbn_ref[...]))
    row = pl.program_id(0)
    x = x_ref[0] + _mod(mod_ref, 2, row) * _dot(merged.astype(BF16), wo_ref[...])
    hb = _modulated_norm(x, mod_ref, g_ref, 3, row).astype(BF16)
    acc = jnp.zeros(x.shape, F32)
    for c0 in range(0, w1_ref.shape[1], chunk):
        a = jnp.maximum(_dot(hb, w1_ref[:, c0:c0 + chunk]), 0.0)
        acc = acc + _dot((a * a).astype(BF16), w2_ref[c0:c0 + chunk, :])
    o_ref[0] = x + _mod(mod_ref, 5, row) * acc


def _post(x, mod3, norm_g, na, pool, gp, gn, wbp, wbn, wo, w1, w2, tm):
    b, l, d = x.shape
    tok = lambda bi, i: (bi, i, 0)
    const = lambda bi, i: (0, 0)
    resident = lambda w: pl.BlockSpec(w.shape, const, pipeline_mode=pl.Buffered(1))
    return pl.pallas_call(
        functools.partial(_post_kernel, chunk=d),
        grid=(b, l // tm),
        in_specs=[pl.BlockSpec((1, tm, d), tok),
                  pl.BlockSpec(mod3.shape, lambda bi, i: (0, 0, 0)),
                  pl.BlockSpec((1, d), const),
                  pl.BlockSpec((1, tm, na.shape[-1]), tok),
                  pl.BlockSpec((1, tm, pool.shape[-1]), tok),
                  pl.BlockSpec((1, tm, d), tok),
                  pl.BlockSpec((1, tm, d), tok),
                  resident(wbp), resident(wbn), resident(wo), resident(w1), resident(w2)],
        out_specs=pl.BlockSpec((1, tm, d), tok),
        out_shape=jax.ShapeDtypeStruct((b, l, d), F32),
        compiler_params=_params(("arbitrary", "arbitrary")),
        name="post",
    )(x, mod3, norm_g, na, pool, gp, gn, wbp, wbn, wo, w1, w2)


def _rope_tables(seq_len):
    n_freq = HEAD_DIM // 4
    t = np.arange(seq_len)
    inv = (ROPE_THETA ** (-np.arange(n_freq, dtype=np.float32) / n_freq)).astype(np.float32)
    ang_row = (t // GRID_W).astype(np.float32)[:, None] * inv
    ang_col = (t % GRID_W).astype(np.float32)[:, None] * inv
    cr, sr, cc, sc = np.cos(ang_row), np.sin(ang_row), np.cos(ang_col), np.sin(ang_col)
    cos = np.concatenate([cr, cr, cc, cc], axis=1).astype(np.float32)
    sin = np.concatenate([-sr, sr, -sc, sc], axis=1).astype(np.float32)
    return (jnp.asarray(np.tile(cos, (1, 2))), jnp.asarray(np.tile(sin, (1, 2))),
            jnp.asarray(np.ascontiguousarray(cos.T)), jnp.asarray(np.ascontiguousarray(sin.T)))


def _bias_kernel(el_ref, er_ref, o_ref, *, rows):
    lanes = 2 * GRID_W
    kc = lax.broadcasted_iota(jnp.int32, (GRID_W, lanes), 0)
    lane = lax.broadcasted_iota(jnp.int32, (GRID_W, lanes), 1)
    qc = lane % GRID_W
    c0 = jnp.clip(qc - NA_COLS // 2, 0, GRID_W - NA_COLS)
    col_ok = (kc >= c0) & (kc < c0 + NA_COLS)
    left = lane < GRID_W
    masked = jnp.full((GRID_W, lanes), MASK_VALUE, F32)

    cache = {}

    def toeplitz(side, dr):
        if (side, dr) not in cache:
            ref = er_ref if side else el_ref
            dd = dr + NA_ROWS - 1
            vec = jnp.broadcast_to(ref[0, dd:dd + 1, :], (GRID_W, lanes))
            cache[(side, dr)] = pltpu.roll(vec, 0, axis=1, stride=1, stride_axis=0) * LOG2E
        return cache[(side, dr)]

    nt = rows // TILE_ROWS
    for t, tile in enumerate((0, nt // 2, nt - 1)):
        r = tile * TILE_ROWS
        ks = min(max(r - TILE_ROWS, 0), rows - KEY_ROWS)
        for j in range(KEY_ROWS):
            kr = ks + j
            for ip in range(TILE_ROWS // 2):
                halves = []
                for side in range(2):
                    qr = r + 2 * ip + side
                    r0 = min(max(qr - NA_ROWS // 2, 0), rows - NA_ROWS)
                    ok = r0 <= kr < r0 + NA_ROWS
                    halves.append(toeplitz(side, kr - qr) if ok else masked)
                blk = jnp.where(col_ok, jnp.where(left, halves[0], halves[1]), MASK_VALUE)
                o_ref[t, 0, j * GRID_W:(j + 1) * GRID_W, ip * lanes:(ip + 1) * lanes] = blk


def _bias_vectors(rpb):
    heads, nr, ncol = rpb.shape
    lanes = 2 * GRID_W
    half = NA_COLS - 1
    zeros = lambda n: jnp.zeros((heads, nr, n), F32)
    flipped = rpb[..., ::-1]
    e_left = jnp.concatenate([flipped[..., half:], zeros(lanes - ncol), flipped[..., :half]], axis=-1)
    e_right = jnp.concatenate([zeros(GRID_W - half), flipped, zeros(lanes - GRID_W + half - ncol)], axis=-1)
    pad = (-nr) % SUBLANES
    e_left = jnp.pad(e_left, ((0, 0), (0, pad), (0, 0)))
    e_right = jnp.pad(e_right, ((0, 0), (0, pad), (0, 0)))
    return e_left, e_right


def kernel(x, c, ctx, c_ctx, ada_w, ada_b, norm1_g, norm2_g, w_in, pool_w, pool_scale,
           q_norm_g, k_norm_g, rpb, w_branch_pool, w_branch_na, w_out, mlp_w1, mlp_w2):
    b, l, d = x.shape
    depth = ada_w.shape[0]
    pw = pool_scale.shape[-1]
    nw = w_branch_na.shape[1]
    heads = nw // HEAD_DIM
    rows = l // GRID_W
    assert depth == 1, "the context-stream update is only needed when another layer follows"
    assert l % (TILE_ROWS * GRID_W) == 0 and rows >= KEY_ROWS
    assert w_in.shape[-1] == pw + 3 * nw + 2 * d and pw == nw
    ctx_row = b
    tm_proj = 1024

    cos_t, sin_t, cos_tt, sin_tt = _rope_tables(l)
    mean_mat = jnp.asarray(np.kron(np.eye(heads), np.full((HEAD_DIM, HEAD_DIM), 1.0 / HEAD_DIM)), BF16)
    pad = (-(b + 1)) % SUBLANES
    cc = jnp.concatenate([c, c_ctx[None], jnp.zeros((pad, d), F32)], axis=0)

    layer = 0
    q0, k0, v0, g0 = pw, pw + nw, pw + 2 * nw, pw + 3 * nw
    mod3, w_all, w_qvt, bias = _prep(cc, ada_w[layer], ada_b[layer], w_in[layer], q0, v0, nw,
                                     rpb[layer], rows)
    qg_tt = q_norm_g[layer][:, None]
    kg_t = jnp.tile(k_norm_g[layer], heads)[None]
    n1 = norm1_g[layer][None]

    k_ctx, vt_ctx = _ctx_proj(ctx, mod3, n1, w_all, w_qvt, kg_t, mean_mat, k0 // nw, ctx_row)
    later_weights = (w_branch_pool[layer], w_branch_na[layer], w_out[layer],
                     mlp_w1[layer], mlp_w2[layer])
    (u, qrt, qpt, kr, vt, gp, gn), (wbp, wbn, wo, w1, w2) = _in_proj(
        x, mod3, n1, w_all, w_qvt, cos_t, sin_t, cos_tt, sin_tt, qg_tt, kg_t, mean_mat,
        (0, k0, g0, g0 + d), pw, tm_proj, later_weights)
    na, pool = _mixer(u, qrt, qpt, kr, vt, k_ctx, vt_ctx, bias,
                      pool_w[layer], pool_scale[layer][None])
    return _post(x, mod3, norm2_g[layer][None], na, pool, gp, gn, wbp, wbn, wo, w1, w2, tm=tm_proj)
```

```python
import functools

import numpy as np
import jax
import jax.numpy as jnp
from jax import lax
from jax.experimental import pallas as pl
from jax.experimental.pallas import tpu as pltpu

GRID_W = 64
N_MOD = 6
POOL_WINDOWS = (2, 4, 8, 16)
HEAD_DIM = 64
NA_ROWS = 8
NA_COLS = 16
ROPE_THETA = 10000.0
EPS = 1e-6
MASK_VALUE = -1e30
LOG2E = 1.4426950408889634

V7X_VMEM_BYTES = 64 * 1024 * 1024
VMEM_LIMIT_BYTES = V7X_VMEM_BYTES - 2 * 1024 * 1024
SUBLANES = 8
BF16_ROWS = 16
POOL_HALO = 8
TILE_ROWS = 4
KEY_ROWS = TILE_ROWS + NA_ROWS

BF16 = jnp.bfloat16
F32 = jnp.float32


def _dot(a, b):
    return jnp.dot(a, b, preferred_element_type=F32)


def _dot_nt(a, b):
    return lax.dot_general(a, b, (((1,), (1,)), ((), ())), preferred_element_type=F32)


def _params(sem):
    return pltpu.CompilerParams(dimension_semantics=sem, vmem_limit_bytes=VMEM_LIMIT_BYTES)


def _rms(x):
    return x * lax.rsqrt(jnp.mean(x * x, axis=-1, keepdims=True) + EPS)


def _sigmoid_bf16(z):
    zb = z.astype(BF16)
    return 0.5 * jnp.tanh(0.5 * zb) + 0.5


def _prep_kernel(c_ref, aw_ref, ab_ref, w_rows_ref, w_cols_ref, el_ref, er_ref,
                 mod_ref, w_all_ref, w_qvt_ref, bias_ref, *, grid_rows):
    c = c_ref[...]
    s = (c * jax.nn.sigmoid(c)).astype(BF16)
    mod_ref[0] = _dot(s, aw_ref[...].astype(BF16)) + ab_ref[...]
    w_all_ref[...] = w_rows_ref[...].astype(BF16)
    w_qvt_ref[...] = w_cols_ref[...].T.astype(BF16)
    _bias_kernel(el_ref, er_ref, bias_ref, rows=grid_rows)


def _prep(cc, ada_w, ada_b, w, q0, v0, width, rpb, grid_rows):
    m, d = cc.shape
    n = ada_w.shape[1]
    lanes = 128
    heads = rpb.shape[0]
    steps = 2 * width // lanes
    assert steps == heads and steps >= N_MOD and n == N_MOD * d
    assert w.shape[0] % (steps * BF16_ROWS) == 0
    rows = w.shape[0] // steps
    mod_block = lambda j: jnp.minimum(j, N_MOD - 1)
    per_seg = width // lanes
    col_block = lambda j: jnp.where(j < per_seg, q0 // lanes + j, v0 // lanes + j - per_seg)
    e_left, e_right = _bias_vectors(rpb)
    vec_spec = pl.BlockSpec((1,) + e_left.shape[1:], lambda j: (j, 0, 0))
    tq, tk = TILE_ROWS * GRID_W, KEY_ROWS * GRID_W
    return pl.pallas_call(
        functools.partial(_prep_kernel, grid_rows=grid_rows),
        grid=(steps,),
        in_specs=[pl.BlockSpec((m, d), lambda j: (0, 0)),
                  pl.BlockSpec((d, d), lambda j: (0, mod_block(j))),
                  pl.BlockSpec((1, d), lambda j: (0, mod_block(j))),
                  pl.BlockSpec((rows, w.shape[1]), lambda j: (j, 0)),
                  pl.BlockSpec((w.shape[0], lanes), lambda j: (0, col_block(j))),
                  vec_spec, vec_spec],
        out_specs=[pl.BlockSpec((1, m, d), lambda j: (mod_block(j), 0, 0)),
                   pl.BlockSpec((rows, w.shape[1]), lambda j: (j, 0)),
                   pl.BlockSpec((lanes, w.shape[0]), lambda j: (j, 0)),
                   pl.BlockSpec((3, 1, tk, tq), lambda j: (0, j, 0, 0))],
        out_shape=[jax.ShapeDtypeStruct((N_MOD, m, d), F32),
                   jax.ShapeDtypeStruct(w.shape, BF16),
                   jax.ShapeDtypeStruct((2 * width, w.shape[0]), BF16),
                   jax.ShapeDtypeStruct((3, heads, tk, tq), F32)],
        compiler_params=_params(("arbitrary",)),
        name="prep",
    )(cc, ada_w, ada_b.reshape(1, n), w, w, e_left, e_right)


def _head_norm(a, mean_mat, g):
    ms = _dot((a * a).astype(BF16), mean_mat)
    return a * lax.rsqrt(ms + EPS) * g


def _rope(a, cos, sin_signed, first_half):
    width = a.shape[-1]
    quarter = HEAD_DIM // 4
    up = pltpu.roll(a, width - quarter, axis=1)
    down = pltpu.roll(a, quarter, axis=1)
    swapped = jnp.where(first_half, up, down)
    return a * cos + swapped * sin_signed


def _mod(mod_ref, k, row):
    return mod_ref[k, pl.ds(row, 1), :]


def _modulated_norm(x, mod_ref, g_ref, shift_k, row):
    return _rms(x) * g_ref[...] * (1.0 + _mod(mod_ref, shift_k + 1, row)) + _mod(mod_ref, shift_k, row)


def _ctx_kernel(x_ref, mod_ref, g_ref, wk_ref, wvt_ref, kg_ref, mean_ref, k_ref, vt_ref, *, ctx_row):
    hb = _modulated_norm(x_ref[0], mod_ref, g_ref, 0, ctx_row).astype(BF16)
    k_ref[0] = _head_norm(_dot(hb, wk_ref[...]), mean_ref[...], kg_ref[...]).astype(BF16)
    vt_ref[0] = _dot_nt(wvt_ref[...], hb).astype(BF16)


def _ctx_proj(ctx, mod3, norm_g, w_all, w_qvt, kg_t, mean_mat, key_block, ctx_row):
    b, n, d = ctx.shape
    nw = kg_t.shape[-1]
    sd = jax.ShapeDtypeStruct
    return pl.pallas_call(
        functools.partial(_ctx_kernel, ctx_row=ctx_row),
        grid=(b,),
        in_specs=[pl.BlockSpec((1, n, d), lambda i: (i, 0, 0)),
                  pl.BlockSpec(mod3.shape, lambda i: (0, 0, 0)),
                  pl.BlockSpec((1, d), lambda i: (0, 0)),
                  pl.BlockSpec((d, nw), lambda i: (0, key_block)),
                  pl.BlockSpec((nw, d), lambda i: (1, 0)),
                  pl.BlockSpec((1, nw), lambda i: (0, 0)),
                  pl.BlockSpec((nw, nw), lambda i: (0, 0))],
        out_specs=[pl.BlockSpec((1, n, nw), lambda i: (i, 0, 0)),
                   pl.BlockSpec((1, nw, n), lambda i: (i, 0, 0))],
        out_shape=[sd((b, n, nw), BF16), sd((b, nw, n), BF16)],
        compiler_params=_params(("arbitrary",)),
        name="ctx_proj",
    )(ctx, mod3, norm_g, w_all, w_qvt, kg_t, mean_mat)


def _in_proj_kernel(*refs, col_starts, n_cast):
    (x_ref, mod_ref, g_ref, w_ref, wt_ref, cos_ref, sin_ref, cost_ref, sint_ref,
     qgt_ref, kg_ref, mean_ref) = refs[:12]
    cast_in = refs[12:12 + n_cast]
    u_ref, qrt_ref, qpt_ref, kr_ref, vt_ref, gp_ref, gn_ref = refs[12 + n_cast:19 + n_cast]
    cast_out = refs[19 + n_cast:]
    for src, dst in zip(cast_in, cast_out):
        dst[...] = src[...].astype(BF16)

    pw = u_ref.shape[-1]
    nw = kr_ref.shape[-1]
    d = x_ref.shape[-1]
    heads = nw // HEAD_DIM
    quarter = HEAD_DIM // 4
    scale = HEAD_DIM ** -0.5 * LOG2E
    u0, k0, gp0, gn0 = col_starts

    hb = _modulated_norm(x_ref[0], mod_ref, g_ref, 0, pl.program_id(0)).astype(BF16)

    cost, sint, qgt = cost_ref[...], sint_ref[...], qgt_ref[...]

    def query_heads(first, last):
        qt = _dot_nt(wt_ref[first * HEAD_DIM:last * HEAD_DIM, :], hb)
        for h in range(first, last):
            rows = slice(h * HEAD_DIM, (h + 1) * HEAD_DIM)
            z = qt[(h - first) * HEAD_DIM:(h - first + 1) * HEAD_DIM, :]
            qn = z * lax.rsqrt(jnp.mean(z * z, axis=0, keepdims=True) + EPS) * (qgt * scale)
            swapped = jnp.concatenate([qn[quarter:2 * quarter], qn[0:quarter],
                                       qn[3 * quarter:], qn[2 * quarter:3 * quarter]], axis=0)
            qpt_ref[0, rows, :] = qn.astype(BF16)
            qrt_ref[0, rows, :] = (qn * cost + swapped * sint).astype(BF16)

    z_gp = _dot(hb, w_ref[:, gp0:gp0 + d])
    query_heads(0, heads // 2)
    z_gn = _dot(hb, w_ref[:, gn0:gn0 + d])
    gp_ref[0] = _sigmoid_bf16(z_gp)
    query_heads(heads // 2, heads)
    z_k = _dot(hb, w_ref[:, k0:k0 + nw])
    gn_ref[0] = _sigmoid_bf16(z_gn)
    z_vt = _dot_nt(wt_ref[nw:, :], hb)

    reps = nw // cos_ref.shape[-1]
    cos = jnp.concatenate([cos_ref[...]] * reps, axis=1)
    sin = jnp.concatenate([sin_ref[...]] * reps, axis=1)
    lane = lax.broadcasted_iota(jnp.int32, (1, nw), 1)
    first_half = (lane % (HEAD_DIM // 2)) < quarter
    k = _head_norm(z_k, mean_ref[...], kg_ref[...])
    kr_ref[0] = _rope(k, cos, sin, first_half).astype(BF16)
    vt_ref[0] = z_vt.astype(BF16)
    u_ref[0] = _dot(hb, w_ref[:, u0:u0 + pw])


def _in_proj(x, mod3, norm_g, w_all, w_qvt, cos_t, sin_t, cos_tt, sin_tt, qg_tt, kg_t, mean_mat,
             col_starts, pw, tm, later_weights):
    b, l, d = x.shape
    nw = kg_t.shape[-1]
    tw = cos_t.shape[-1]
    nt = l // tm
    tok = lambda bi, i: (bi, i, 0)
    chan = lambda bi, i: (bi, 0, i)
    const = lambda bi, i: (0, 0)
    sd = jax.ShapeDtypeStruct
    slab_specs = []
    for w in later_weights:
        assert w.shape[0] % (b * nt * BF16_ROWS) == 0
        slab_specs.append(pl.BlockSpec((w.shape[0] // (b * nt), w.shape[1]),
                                       lambda bi, i: (bi * nt + i, 0)))
    outs = pl.pallas_call(
        functools.partial(_in_proj_kernel, col_starts=col_starts, n_cast=len(later_weights)),
        grid=(b, nt),
        in_specs=[pl.BlockSpec((1, tm, d), tok),
                  pl.BlockSpec(mod3.shape, lambda bi, i: (0, 0, 0)),
                  pl.BlockSpec((1, d), const),
                  pl.BlockSpec(w_all.shape, const, pipeline_mode=pl.Buffered(1)),
                  pl.BlockSpec(w_qvt.shape, const, pipeline_mode=pl.Buffered(1)),
                  pl.BlockSpec((tm, tw), lambda bi, i: (i, 0)),
                  pl.BlockSpec((tm, tw), lambda bi, i: (i, 0)),
                  pl.BlockSpec((HEAD_DIM, tm), lambda bi, i: (0, i)),
                  pl.BlockSpec((HEAD_DIM, tm), lambda bi, i: (0, i)),
                  pl.BlockSpec((HEAD_DIM, 1), const),
                  pl.BlockSpec((1, nw), const),
                  pl.BlockSpec((nw, nw), const)] + slab_specs,
        out_specs=[pl.BlockSpec((1, tm, pw), tok),
                   pl.BlockSpec((1, nw, tm), chan), pl.BlockSpec((1, nw, tm), chan),
                   pl.BlockSpec((1, tm, nw), tok),
                   pl.BlockSpec((1, nw, tm), chan),
                   pl.BlockSpec((1, tm, d), tok), pl.BlockSpec((1, tm, d), tok)] + slab_specs,
        out_shape=[sd((b, l, pw), F32),
                   sd((b, nw, l), BF16), sd((b, nw, l), BF16),
                   sd((b, l, nw), BF16),
                   sd((b, nw, l), BF16),
                   sd((b, l, d), BF16), sd((b, l, d), BF16)]
                  + [sd(w.shape, BF16) for w in later_weights],
        compiler_params=_params(("arbitrary", "arbitrary")),
        name="in_proj",
    )(x, mod3, norm_g, w_all, w_qvt, cos_t, sin_t, cos_tt, sin_tt, qg_tt, kg_t, mean_mat,
      *later_weights)
    return outs[:7], outs[7:]


def _mixer_kernel(u_ref, up_ref, un_ref, qrt_ref, qpt_ref,
                  k0_ref, k1_ref, k2_ref, vt0_ref, vt1_ref, vt2_ref, kc_ref, vct_ref, bias_ref,
                  pw_ref, ps_ref,
                  na_ref, pool_ref, nat_scr, e_scr, a2_scr, a4_scr, a8_scr, *, seq_len):
    i = pl.program_id(1)
    tm = u_ref.shape[1]
    nw = qrt_ref.shape[1]
    heads = nw // HEAD_DIM
    pair = 2 * HEAD_DIM

    zeros = jnp.zeros((HEAD_DIM, tm), BF16)
    ones = jnp.ones((BF16_ROWS, tm), BF16)
    k_refs = (k0_ref, k1_ref, k2_ref)
    vt_refs = (vt0_ref, vt1_ref, vt2_ref)

    def head_rows(h):
        return slice(h * HEAD_DIM, (h + 1) * HEAD_DIM)

    def scores(h):
        rows = head_rows(h)
        lanes = slice((h // 2) * pair, (h // 2 + 1) * pair)
        q_rot, q_plain = qrt_ref[0, rows, :], qpt_ref[0, rows, :]
        if h % 2 == 0:
            w_rot = jnp.concatenate([q_rot, zeros], axis=0)
            w_plain = jnp.concatenate([q_plain, zeros], axis=0)
        else:
            w_rot = jnp.concatenate([zeros, q_rot], axis=0)
            w_plain = jnp.concatenate([zeros, q_plain], axis=0)
        s = [(_dot(k_refs[j][0, :, lanes], w_rot) + bias_ref[0, h, j * tm:(j + 1) * tm, :]).astype(BF16)
             for j in range(3)]
        s.append(_dot(kc_ref[0, :, lanes], w_plain).astype(BF16))
        m = s[0].max(axis=0, keepdims=True)
        for sj in s[1:]:
            m = jnp.maximum(m, sj.max(axis=0, keepdims=True))
        return s, m

    def probabilities(s_and_m):
        s, m = s_and_m
        return [jnp.exp2(sj - m) for sj in s]

    def weighted_values(h, p):
        rows = head_rows(h)
        ot = _dot(jnp.concatenate([vct_ref[0, rows, :], ones], axis=0), p[3])
        for j in range(3):
            ot = ot + _dot(jnp.concatenate([vt_refs[j][0, rows, :], ones], axis=0), p[j])
        nat_scr[rows, :] = ot[:HEAD_DIM] / ot[HEAD_DIM:HEAD_DIM + 1]

    gd = pw_ref.shape[-1]
    hi = tm + 2 * POOL_HALO
    first = POOL_HALO
    g0 = slice(0, gd)

    def rows_at(ref, off, cols):
        return ref[first + off:first + off + tm, cols]

    def pool_fill():
        for scr in (e_scr, a2_scr, a4_scr):
            scr[hi:hi + SUBLANES, :] = jnp.zeros((SUBLANES, scr.shape[1]), F32)
        e_scr[0:first, :] = jnp.where(i > 0, up_ref[0], 0.0)
        e_scr[first:first + tm, :] = u_ref[0]
        e_scr[first + tm:hi, :] = jnp.where((i + 1) * tm < seq_len, un_ref[0], 0.0)
        a2_scr[0:hi, :] = e_scr[0:hi, gd:] + e_scr[1:hi + 1, gd:]

    def pool_double():
        a4_scr[0:hi, :] = a2_scr[0:hi, gd:] + a2_scr[2:hi + 2, gd:]
        a8_scr[0:hi, :] = a4_scr[0:hi, gd:] + a4_scr[4:hi + 4, gd:]

    def pool_group(g):
        w = POOL_WINDOWS[g]
        if g == 0:
            total = rows_at(e_scr, -1, g0) + rows_at(e_scr, 0, g0)
        elif g == 1:
            total = rows_at(a2_scr, -2, g0) + rows_at(a2_scr, 0, g0)
        elif g == 2:
            total = rows_at(a4_scr, -4, g0) + rows_at(a4_scr, 0, g0)
        else:
            total = rows_at(a8_scr, -8, g0) + rows_at(a8_scr, 0, g0)
        pos = i * tm + lax.broadcasted_iota(jnp.int32, (tm, 1), 0)
        cnt = (jnp.minimum(pos + w // 2, seq_len) - jnp.maximum(pos - w // 2, 0)).astype(F32)
        dlt = total / cnt - u_ref[0, :, g * gd:(g + 1) * gd]
        cols = slice(g * gd, (g + 1) * gd)
        return (_dot(dlt.astype(BF16), pw_ref[g].astype(BF16)) * ps_ref[:, cols]).astype(BF16)

    pooled = []
    pool_stages = [pool_fill, pool_double] + [functools.partial(pool_group, g)
                                               for g in range(len(POOL_WINDOWS))]

    p_next = probabilities(scores(0))
    for h in range(heads):
        p_cur = p_next
        if h + 1 < heads:
            p_next = probabilities(scores(h + 1))
        weighted_values(h, p_cur)
        stage = h - (heads - len(pool_stages))
        if stage >= 0:
            out = pool_stages[stage]()
            if out is not None:
                pooled.append(out)
        if h % 2 == 1:
            slab = slice((h - 1) * HEAD_DIM, (h + 1) * HEAD_DIM)
            na_ref[0, :, slab] = nat_scr[slab, :].T.astype(BF16)

    pool_ref[0] = jnp.concatenate(pooled, axis=1)


def _mixer(u, qrt, qpt, kr, vt, kc, vct, bias, pool_w, pool_scale):
    b, l, pw = u.shape
    nw = kr.shape[-1]
    gd = pool_w.shape[-1]
    tm = TILE_ROWS * GRID_W
    nt = l // tm
    hb = tm // POOL_HALO
    n_halo = l // POOL_HALO
    nc = kc.shape[1]
    assert pw == len(POOL_WINDOWS) * gd and POOL_HALO == max(POOL_WINDOWS) // 2

    tok = lambda bi, i: (bi, i, 0)
    chan = lambda bi, i: (bi, 0, i)
    const2 = lambda bi, i: (0, 0)
    first_block = lambda i: jnp.clip(i - 1, 0, nt - 3)

    def k_spec(j):
        return pl.BlockSpec((1, tm, nw), lambda bi, i: (bi, first_block(i) + j, 0))

    def vt_spec(j):
        return pl.BlockSpec((1, nw, tm), lambda bi, i: (bi, 0, first_block(i) + j))

    def bias_idx(bi, i):
        return (jnp.where(i == 0, 0, jnp.where(i == nt - 1, 2, 1)), 0, 0, 0)

    in_specs = [
        pl.BlockSpec((1, tm, pw), tok),
        pl.BlockSpec((1, POOL_HALO, pw), lambda bi, i: (bi, jnp.maximum(i * hb - 1, 0), 0)),
        pl.BlockSpec((1, POOL_HALO, pw), lambda bi, i: (bi, jnp.minimum((i + 1) * hb, n_halo - 1), 0)),
        pl.BlockSpec((1, nw, tm), chan),
        pl.BlockSpec((1, nw, tm), chan),
        k_spec(0), k_spec(1), k_spec(2),
        vt_spec(0), vt_spec(1), vt_spec(2),
        pl.BlockSpec((1, nc, nw), lambda bi, i: (bi, 0, 0)),
        pl.BlockSpec((1, nw, nc), lambda bi, i: (bi, 0, 0)),
        pl.BlockSpec((1,) + bias.shape[1:], bias_idx),
        pl.BlockSpec(pool_w.shape, lambda bi, i: (0, 0, 0)),
        pl.BlockSpec((1, pw), const2),
    ]
    n_scr = tm + 2 * POOL_HALO + SUBLANES
    return pl.pallas_call(
        functools.partial(_mixer_kernel, seq_len=l),
        grid=(b, nt),
        in_specs=in_specs,
        out_specs=[pl.BlockSpec((1, tm, nw), tok), pl.BlockSpec((1, tm, pw), tok)],
        out_shape=[jax.ShapeDtypeStruct((b, l, nw), BF16), jax.ShapeDtypeStruct((b, l, pw), BF16)],
        scratch_shapes=[pltpu.VMEM((nw, tm), F32),
                        pltpu.VMEM((n_scr, pw), F32),
                        pltpu.VMEM((n_scr, pw - gd), F32),
                        pltpu.VMEM((n_scr, pw - 2 * gd), F32),
                        pltpu.VMEM((n_scr, pw - 3 * gd), F32)],
        compiler_params=_params(("arbitrary", "arbitrary")),
        name="mixer",
    )(u, u, u, qrt, qpt, kr, kr, kr, vt, vt, vt, kc, vct, bias, pool_w, pool_scale)


def _post_kernel(x_ref, mod_ref, g_ref, na_ref, pool_ref, gp_ref, gn_ref,
                 wbp_ref, wbn_ref, wo_ref, w1_ref, w2_ref, o_ref, *, chunk):
    merged = (gp_ref[0].astype(F32) * _dot(pool_ref[0], wbp_ref[...])
              + gn_ref[0].astype(F32) * _dot(na_ref[0], wbn_ref[...]))
    row = pl.program_id(0)
    x = x_ref[0] + _mod(mod_ref, 2, row) * _dot(merged.astype(BF16), wo_ref[...])
    hb = _modulated_norm(x, mod_ref, g_ref, 3, row).astype(BF16)
    acc = jnp.zeros(x.shape, F32)
    for c0 in range(0, w1_ref.shape[1], chunk):
        a = jnp.maximum(_dot(hb, w1_ref[:, c0:c0 + chunk]), 0.0)
        acc = acc + _dot((a * a).astype(BF16), w2_ref[c0:c0 + chunk, :])
    o_ref[0] = x + _mod(mod_ref, 5, row) * acc


def _post(x, mod3, norm_g, na, pool, gp, gn, wbp, wbn, wo, w1, w2, tm):
    b, l, d = x.shape
    tok = lambda bi, i: (bi, i, 0)
    const = lambda bi, i: (0, 0)
    resident = lambda w: pl.BlockSpec(w.shape, const, pipeline_mode=pl.Buffered(1))
    return pl.pallas_call(
        functools.partial(_post_kernel, chunk=d),
        grid=(b, l // tm),
        in_specs=[pl.BlockSpec((1, tm, d), tok),
                  pl.BlockSpec(mod3.shape, lambda bi, i: (0, 0, 0)),
                  pl.BlockSpec((1, d), const),
                  pl.BlockSpec((1, tm, na.shape[-1]), tok),
                  pl.BlockSpec((1, tm, pool.shape[-1]), tok),
                  pl.BlockSpec((1, tm, d), tok),
                  pl.BlockSpec((1, tm, d), tok),
                  resident(wbp), resident(wbn), resident(wo), resident(w1), resident(w2)],
        out_specs=pl.BlockSpec((1, tm, d), tok),
        out_shape=jax.ShapeDtypeStruct((b, l, d), F32),
        compiler_params=_params(("arbitrary", "arbitrary")),
        name="post",
    )(x, mod3, norm_g, na, pool, gp, gn, wbp, wbn, wo, w1, w2)


def _rope_tables(seq_len):
    n_freq = HEAD_DIM // 4
    t = np.arange(seq_len)
    inv = (ROPE_THETA ** (-np.arange(n_freq, dtype=np.float32) / n_freq)).astype(np.float32)
    ang_row = (t // GRID_W).astype(np.float32)[:, None] * inv
    ang_col = (t % GRID_W).astype(np.float32)[:, None] * inv
    cr, sr, cc, sc = np.cos(ang_row), np.sin(ang_row), np.cos(ang_col), np.sin(ang_col)
    cos = np.concatenate([cr, cr, cc, cc], axis=1).astype(np.float32)
    sin = np.concatenate([-sr, sr, -sc, sc], axis=1).astype(np.float32)
    return (jnp.asarray(np.tile(cos, (1, 2))), jnp.asarray(np.tile(sin, (1, 2))),
            jnp.asarray(np.ascontiguousarray(cos.T)), jnp.asarray(np.ascontiguousarray(sin.T)))


def _bias_kernel(el_ref, er_ref, o_ref, *, rows):
    lanes = 2 * GRID_W
    kc = lax.broadcasted_iota(jnp.int32, (GRID_W, lanes), 0)
    lane = lax.broadcasted_iota(jnp.int32, (GRID_W, lanes), 1)
    qc = lane % GRID_W
    c0 = jnp.clip(qc - NA_COLS // 2, 0, GRID_W - NA_COLS)
    col_ok = (kc >= c0) & (kc < c0 + NA_COLS)
    left = lane < GRID_W
    masked = jnp.full((GRID_W, lanes), MASK_VALUE, F32)

    cache = {}

    def toeplitz(side, dr):
        if (side, dr) not in cache:
            ref = er_ref if side else el_ref
            dd = dr + NA_ROWS - 1
            vec = jnp.broadcast_to(ref[0, dd:dd + 1, :], (GRID_W, lanes))
            cache[(side, dr)] = pltpu.roll(vec, 0, axis=1, stride=1, stride_axis=0) * LOG2E
        return cache[(side, dr)]

    nt = rows // TILE_ROWS
    for t, tile in enumerate((0, nt // 2, nt - 1)):
        r = tile * TILE_ROWS
        ks = min(max(r - TILE_ROWS, 0), rows - KEY_ROWS)
        for j in range(KEY_ROWS):
            kr = ks + j
            for ip in range(TILE_ROWS // 2):
                halves = []
                for side in range(2):
                    qr = r + 2 * ip + side
                    r0 = min(max(qr - NA_ROWS // 2, 0), rows - NA_ROWS)
                    ok = r0 <= kr < r0 + NA_ROWS
                    halves.append(toeplitz(side, kr - qr) if ok else masked)
                blk = jnp.where(col_ok, jnp.where(left, halves[0], halves[1]), MASK_VALUE)
                o_ref[t, 0, j * GRID_W:(j + 1) * GRID_W, ip * lanes:(ip + 1) * lanes] = blk


def _bias_vectors(rpb):
    heads, nr, ncol = rpb.shape
    lanes = 2 * GRID_W
    half = NA_COLS - 1
    zeros = lambda n: jnp.zeros((heads, nr, n), F32)
    flipped = rpb[..., ::-1]
    e_left = jnp.concatenate([flipped[..., half:], zeros(lanes - ncol), flipped[..., :half]], axis=-1)
    e_right = jnp.concatenate([zeros(GRID_W - half), flipped, zeros(lanes - GRID_W + half - ncol)], axis=-1)
    pad = (-nr) % SUBLANES
    e_left = jnp.pad(e_left, ((0, 0), (0, pad), (0, 0)))
    e_right = jnp.pad(e_right, ((0, 0), (0, pad), (0, 0)))
    return e_left, e_right


def kernel(x, c, ctx, c_ctx, ada_w, ada_b, norm1_g, norm2_g, w_in, pool_w, pool_scale,
           q_norm_g, k_norm_g, rpb, w_branch_pool, w_branch_na, w_out, mlp_w1, mlp_w2):
    b, l, d = x.shape
    depth = ada_w.shape[0]
    pw = pool_scale.shape[-1]
    nw = w_branch_na.shape[1]
    heads = nw // HEAD_DIM
    rows = l // GRID_W
    assert depth == 1, "the context-stream update is only needed when another layer follows"
    assert l % (TILE_ROWS * GRID_W) == 0 and rows >= KEY_ROWS
    assert w_in.shape[-1] == pw + 3 * nw + 2 * d and pw == nw
    ctx_row = b
    tm_proj = 1024

    cos_t, sin_t, cos_tt, sin_tt = _rope_tables(l)
    mean_mat = jnp.asarray(np.kron(np.eye(heads), np.full((HEAD_DIM, HEAD_DIM), 1.0 / HEAD_DIM)), BF16)
    pad = (-(b + 1)) % SUBLANES
    cc = jnp.concatenate([c, c_ctx[None], jnp.zeros((pad, d), F32)], axis=0)

    layer = 0
    q0, k0, v0, g0 = pw, pw + nw, pw + 2 * nw, pw + 3 * nw
    mod3, w_all, w_qvt, bias = _prep(cc, ada_w[layer], ada_b[layer], w_in[layer], q0, v0, nw,
                                     rpb[layer], rows)
    qg_tt = q_norm_g[layer][:, None]
    kg_t = jnp.tile(k_norm_g[layer], heads)[None]
    n1 = norm1_g[layer][None]

    k_ctx, vt_ctx = _ctx_proj(ctx, mod3, n1, w_all, w_qvt, kg_t, mean_mat, k0 // nw, ctx_row)
    later_weights = (w_branch_pool[layer], w_branch_na[layer], w_out[layer],
                     mlp_w1[layer], mlp_w2[layer])
    (u, qrt, qpt, kr, vt, gp, gn), (wbp, wbn, wo, w1, w2) = _in_proj(
        x, mod3, n1, w_all, w_qvt, cos_t, sin_t, cos_tt, sin_tt, qg_tt, kg_t, mean_mat,
        (0, k0, g0, g0 + d), pw, tm_proj, later_weights)
    na, pool = _mixer(u, qrt, qpt, kr, vt, k_ctx, vt_ctx, bias,
                      pool_w[layer], pool_scale[layer][None])
    return _post(x, mod3, norm2_g[layer][None], na, pool, gp, gn, wbp, wbn, wo, w1, w2, tm=tm_proj)
```

```python
import functools

import numpy as np
import jax
import jax.numpy as jnp
from jax import lax
from jax.experimental import pallas as pl
from jax.experimental.pallas import tpu as pltpu

GRID_W = 64
N_MOD = 6
POOL_WINDOWS = (2, 4, 8, 16)
HEAD_DIM = 64
NA_ROWS = 8
NA_COLS = 16
ROPE_THETA = 10000.0
EPS = 1e-6
MASK_VALUE = -1e30
LOG2E = 1.4426950408889634

V7X_VMEM_BYTES = 64 * 1024 * 1024
VMEM_LIMIT_BYTES = V7X_VMEM_BYTES - 2 * 1024 * 1024
SUBLANES = 8
BF16_ROWS = 16
POOL_HALO = 8
TILE_ROWS = 4
KEY_ROWS = TILE_ROWS + NA_ROWS

BF16 = jnp.bfloat16
F32 = jnp.float32


def _dot(a, b):
    return jnp.dot(a, b, preferred_element_type=F32)


def _dot_nt(a, b):
    return lax.dot_general(a, b, (((1,), (1,)), ((), ())), preferred_element_type=F32)


def _params(sem):
    return pltpu.CompilerParams(dimension_semantics=sem, vmem_limit_bytes=VMEM_LIMIT_BYTES)


def _rms(x):
    return x * lax.rsqrt(jnp.mean(x * x, axis=-1, keepdims=True) + EPS)


def _sigmoid_bf16(z):
    zb = z.astype(BF16)
    return 0.5 * jnp.tanh(0.5 * zb) + 0.5


def _prep_kernel(c_ref, aw_ref, ab_ref, w_rows_ref, w_cols_ref, el_ref, er_ref,
                 mod_ref, w_all_ref, w_qvt_ref, bias_ref, *, grid_rows):
    c = c_ref[...]
    s = (c * jax.nn.sigmoid(c)).astype(BF16)
    mod_ref[0] = _dot(s, aw_ref[...].astype(BF16)) + ab_ref[...]
    w_all_ref[...] = w_rows_ref[...].astype(BF16)
    w_qvt_ref[...] = w_cols_ref[...].T.astype(BF16)
    _bias_kernel(el_ref, er_ref, bias_ref, rows=grid_rows)


def _prep(cc, ada_w, ada_b, w, q0, v0, width, rpb, grid_rows):
    m, d = cc.shape
    n = ada_w.shape[1]
    lanes = 128
    heads = rpb.shape[0]
    steps = 2 * width // lanes
    assert steps == heads and steps >= N_MOD and n == N_MOD * d
    assert w.shape[0] % (steps * BF16_ROWS) == 0
    rows = w.shape[0] // steps
    mod_block = lambda j: jnp.minimum(j, N_MOD - 1)
    per_seg = width // lanes
    col_block = lambda j: jnp.where(j < per_seg, q0 // lanes + j, v0 // lanes + j - per_seg)
    e_left, e_right = _bias_vectors(rpb)
    vec_spec = pl.BlockSpec((1,) + e_left.shape[1:], lambda j: (j, 0, 0))
    tq, tk = TILE_ROWS * GRID_W, KEY_ROWS * GRID_W
    return pl.pallas_call(
        functools.partial(_prep_kernel, grid_rows=grid_rows),
        grid=(steps,),
        in_specs=[pl.BlockSpec((m, d), lambda j: (0, 0)),
                  pl.BlockSpec((d, d), lambda j: (0, mod_block(j))),
                  pl.BlockSpec((1, d), lambda j: (0, mod_block(j))),
                  pl.BlockSpec((rows, w.shape[1]), lambda j: (j, 0)),
                  pl.BlockSpec((w.shape[0], lanes), lambda j: (0, col_block(j))),
                  vec_spec, vec_spec],
        out_specs=[pl.BlockSpec((1, m, d), lambda j: (mod_block(j), 0, 0)),
                   pl.BlockSpec((rows, w.shape[1]), lambda j: (j, 0)),
                   pl.BlockSpec((lanes, w.shape[0]), lambda j: (j, 0)),
                   pl.BlockSpec((3, 1, tk, tq), lambda j: (0, j, 0, 0))],
        out_shape=[jax.ShapeDtypeStruct((N_MOD, m, d), F32),
                   jax.ShapeDtypeStruct(w.shape, BF16),
                   jax.ShapeDtypeStruct((2 * width, w.shape[0]), BF16),
                   jax.ShapeDtypeStruct((3, heads, tk, tq), F32)],
        compiler_params=_params(("arbitrary",)),
        name="prep",
    )(cc, ada_w, ada_b.reshape(1, n), w, w, e_left, e_right)


def _head_norm(a, mean_mat, g):
    ms = _dot((a * a).astype(BF16), mean_mat)
    return a * lax.rsqrt(ms + EPS) * g


def _rope(a, cos, sin_signed, first_half):
    width = a.shape[-1]
    quarter = HEAD_DIM // 4
    up = pltpu.roll(a, width - quarter, axis=1)
    down = pltpu.roll(a, quarter, axis=1)
    swapped = jnp.where(first_half, up, down)
    return a * cos + swapped * sin_signed


def _mod(mod_ref, k, row):
    return mod_ref[k, pl.ds(row, 1), :]


def _modulated_norm(x, mod_ref, g_ref, shift_k, row):
    return _rms(x) * g_ref[...] * (1.0 + _mod(mod_ref, shift_k + 1, row)) + _mod(mod_ref, shift_k, row)


def _ctx_kernel(x_ref, mod_ref, g_ref, wk_ref, wvt_ref, kg_ref, mean_ref, k_ref, vt_ref, *, ctx_row):
    hb = _modulated_norm(x_ref[0], mod_ref, g_ref, 0, ctx_row).astype(BF16)
    k_ref[0] = _head_norm(_dot(hb, wk_ref[...]), mean_ref[...], kg_ref[...]).astype(BF16)
    vt_ref[0] = _dot_nt(wvt_ref[...], hb).astype(BF16)


def _ctx_proj(ctx, mod3, norm_g, w_all, w_qvt, kg_t, mean_mat, key_block, ctx_row):
    b, n, d = ctx.shape
    nw = kg_t.shape[-1]
    sd = jax.ShapeDtypeStruct
    return pl.pallas_call(
        functools.partial(_ctx_kernel, ctx_row=ctx_row),
        grid=(b,),
        in_specs=[pl.BlockSpec((1, n, d), lambda i: (i, 0, 0)),
                  pl.BlockSpec(mod3.shape, lambda i: (0, 0, 0)),
                  pl.BlockSpec((1, d), lambda i: (0, 0)),
                  pl.BlockSpec((d, nw), lambda i: (0, key_block)),
                  pl.BlockSpec((nw, d), lambda i: (1, 0)),
                  pl.BlockSpec((1, nw), lambda i: (0, 0)),
                  pl.BlockSpec((nw, nw), lambda i: (0, 0))],
        out_specs=[pl.BlockSpec((1, n, nw), lambda i: (i, 0, 0)),
                   pl.BlockSpec((1, nw, n), lambda i: (i, 0, 0))],
        out_shape=[sd((b, n, nw), BF16), sd((b, nw, n), BF16)],
        compiler_params=_params(("arbitrary",)),
        name="ctx_proj",
    )(ctx, mod3, norm_g, w_all, w_qvt, kg_t, mean_mat)


def _in_proj_kernel(*refs, col_starts, n_cast):
    (x_ref, mod_ref, g_ref, w_ref, wt_ref, cos_ref, sin_ref, cost_ref, sint_ref,
     qgt_ref, kg_ref, mean_ref) = refs[:12]
    cast_in = refs[12:12 + n_cast]
    u_ref, qrt_ref, qpt_ref, kr_ref, vt_ref, gp_ref, gn_ref = refs[12 + n_cast:19 + n_cast]
    cast_out = refs[19 + n_cast:]
    for src, dst in zip(cast_in, cast_out):
        dst[...] = src[...].astype(BF16)

    pw = u_ref.shape[-1]
    nw = kr_ref.shape[-1]
    d = x_ref.shape[-1]
    heads = nw // HEAD_DIM
    quarter = HEAD_DIM // 4
    scale = HEAD_DIM ** -0.5 * LOG2E
    u0, k0, gp0, gn0 = col_starts

    hb = _modulated_norm(x_ref[0], mod_ref, g_ref, 0, pl.program_id(0)).astype(BF16)

    cost, sint, qgt = cost_ref[...], sint_ref[...], qgt_ref[...]

    def query_heads(first, last):
        qt = _dot_nt(wt_ref[first * HEAD_DIM:last * HEAD_DIM, :], hb)
        for h in range(first, last):
            rows = slice(h * HEAD_DIM, (h + 1) * HEAD_DIM)
            z = qt[(h - first) * HEAD_DIM:(h - first + 1) * HEAD_DIM, :]
            qn = z * lax.rsqrt(jnp.mean(z * z, axis=0, keepdims=True) + EPS) * (qgt * scale)
            swapped = jnp.concatenate([qn[quarter:2 * quarter], qn[0:quarter],
                                       qn[3 * quarter:], qn[2 * quarter:3 * quarter]], axis=0)
            qpt_ref[0, rows, :] = qn.astype(BF16)
            qrt_ref[0, rows, :] = (qn * cost + swapped * sint).astype(BF16)

    z_gp = _dot(hb, w_ref[:, gp0:gp0 + d])
    query_heads(0, heads // 2)
    z_gn = _dot(hb, w_ref[:, gn0:gn0 + d])
    gp_ref[0] = _sigmoid_bf16(z_gp)
    query_heads(heads // 2, heads)
    z_k = _dot(hb, w_ref[:, k0:k0 + nw])
    gn_ref[0] = _sigmoid_bf16(z_gn)
    z_vt = _dot_nt(wt_ref[nw:, :], hb)

    reps = nw // cos_ref.shape[-1]
    cos = jnp.concatenate([cos_ref[...]] * reps, axis=1)
    sin = jnp.concatenate([sin_ref[...]] * reps, axis=1)
    lane = lax.broadcasted_iota(jnp.int32, (1, nw), 1)
    first_half = (lane % (HEAD_DIM // 2)) < quarter
    k = _head_norm(z_k, mean_ref[...], kg_ref[...])
    kr_ref[0] = _rope(k, cos, sin, first_half).astype(BF16)
    vt_ref[0] = z_vt.astype(BF16)
    u_ref[0] = _dot(hb, w_ref[:, u0:u0 + pw])


def _in_proj(x, mod3, norm_g, w_all, w_qvt, cos_t, sin_t, cos_tt, sin_tt, qg_tt, kg_t, mean_mat,
             col_starts, pw, tm, later_weights):
    b, l, d = x.shape
    nw = kg_t.shape[-1]
    tw = cos_t.shape[-1]
    nt = l // tm
    tok = lambda bi, i: (bi, i, 0)
    chan = lambda bi, i: (bi, 0, i)
    const = lambda bi, i: (0, 0)
    sd = jax.ShapeDtypeStruct
    slab_specs = []
    for w in later_weights:
        assert w.shape[0] % (b * nt * BF16_ROWS) == 0
        slab_specs.append(pl.BlockSpec((w.shape[0] // (b * nt), w.shape[1]),
                                       lambda bi, i: (bi * nt + i, 0)))
    outs = pl.pallas_call(
        functools.partial(_in_proj_kernel, col_starts=col_starts, n_cast=len(later_weights)),
        grid=(b, nt),
        in_specs=[pl.BlockSpec((1, tm, d), tok),
                  pl.BlockSpec(mod3.shape, lambda bi, i: (0, 0, 0)),
                  pl.BlockSpec((1, d), const),
                  pl.BlockSpec(w_all.shape, const, pipeline_mode=pl.Buffered(1)),
                  pl.BlockSpec(w_qvt.shape, const, pipeline_mode=pl.Buffered(1)),
                  pl.BlockSpec((tm, tw), lambda bi, i: (i, 0)),
                  pl.BlockSpec((tm, tw), lambda bi, i: (i, 0)),
                  pl.BlockSpec((HEAD_DIM, tm), lambda bi, i: (0, i)),
                  pl.BlockSpec((HEAD_DIM, tm), lambda bi, i: (0, i)),
                  pl.BlockSpec((HEAD_DIM, 1), const),
                  pl.BlockSpec((1, nw), const),
                  pl.BlockSpec((nw, nw), const)] + slab_specs,
        out_specs=[pl.BlockSpec((1, tm, pw), tok),
                   pl.BlockSpec((1, nw, tm), chan), pl.BlockSpec((1, nw, tm), chan),
                   pl.BlockSpec((1, tm, nw), tok),
                   pl.BlockSpec((1, nw, tm), chan),
                   pl.BlockSpec((1, tm, d), tok), pl.BlockSpec((1, tm, d), tok)] + slab_specs,
        out_shape=[sd((b, l, pw), F32),
                   sd((b, nw, l), BF16), sd((b, nw, l), BF16),
                   sd((b, l, nw), BF16),
                   sd((b, nw, l), BF16),
                   sd((b, l, d), BF16), sd((b, l, d), BF16)]
                  + [sd(w.shape, BF16) for w in later_weights],
        compiler_params=_params(("arbitrary", "arbitrary")),
        name="in_proj",
    )(x, mod3, norm_g, w_all, w_qvt, cos_t, sin_t, cos_tt, sin_tt, qg_tt, kg_t, mean_mat,
      *later_weights)
    return outs[:7], outs[7:]


def _mixer_kernel(u_ref, up_ref, un_ref, qrt_ref, qpt_ref,
                  k0_ref, k1_ref, k2_ref, vt0_ref, vt1_ref, vt2_ref, kc_ref, vct_ref, bias_ref,
                  pw_ref, ps_ref,
                  na_ref, pool_ref, nat_scr, e_scr, a2_scr, a4_scr, a8_scr, *, seq_len):
    i = pl.program_id(1)
    tm = u_ref.shape[1]
    nw = qrt_ref.shape[1]
    heads = nw // HEAD_DIM
    pair = 2 * HEAD_DIM

    zeros = jnp.zeros((HEAD_DIM, tm), BF16)
    ones = jnp.ones((BF16_ROWS, tm), BF16)
    k_refs = (k0_ref, k1_ref, k2_ref)
    vt_refs = (vt0_ref, vt1_ref, vt2_ref)

    def head_rows(h):
        return slice(h * HEAD_DIM, (h + 1) * HEAD_DIM)

    def scores(h):
        rows = head_rows(h)
        lanes = slice((h // 2) * pair, (h // 2 + 1) * pair)
        q_rot, q_plain = qrt_ref[0, rows, :], qpt_ref[0, rows, :]
        if h % 2 == 0:
            w_rot = jnp.concatenate([q_rot, zeros], axis=0)
            w_plain = jnp.concatenate([q_plain, zeros], axis=0)
        else:
            w_rot = jnp.concatenate([zeros, q_rot], axis=0)
            w_plain = jnp.concatenate([zeros, q_plain], axis=0)
        s = [(_dot(k_refs[j][0, :, lanes], w_rot) + bias_ref[0, h, j * tm:(j + 1) * tm, :]).astype(BF16)
             for j in range(3)]
        s.append(_dot(kc_ref[0, :, lanes], w_plain).astype(BF16))
        m = s[0].max(axis=0, keepdims=True)
        for sj in s[1:]:
            m = jnp.maximum(m, sj.max(axis=0, keepdims=True))
        return s, m

    def probabilities(s_and_m):
        s, m = s_and_m
        return [jnp.exp2(sj - m) for sj in s]

    def weighted_values(h, p):
        rows = head_rows(h)
        ot = _dot(jnp.concatenate([vct_ref[0, rows, :], ones], axis=0), p[3])
        for j in range(3):
            ot = ot + _dot(jnp.concatenate([vt_refs[j][0, rows, :], ones], axis=0), p[j])
        nat_scr[rows, :] = ot[:HEAD_DIM] / ot[HEAD_DIM:HEAD_DIM + 1]

    gd = pw_ref.shape[-1]
    hi = tm + 2 * POOL_HALO
    first = POOL_HALO
    g0 = slice(0, gd)

    def rows_at(ref, off, cols):
        return ref[first + off:first + off + tm, cols]

    def pool_fill():
        for scr in (e_scr, a2_scr, a4_scr):
            scr[hi:hi + SUBLANES, :] = jnp.zeros((SUBLANES, scr.shape[1]), F32)
        e_scr[0:first, :] = jnp.where(i > 0, up_ref[0], 0.0)
        e_scr[first:first + tm, :] = u_ref[0]
        e_scr[first + tm:hi, :] = jnp.where((i + 1) * tm < seq_len, un_ref[0], 0.0)
        a2_scr[0:hi, :] = e_scr[0:hi, gd:] + e_scr[1:hi + 1, gd:]

    def pool_double():
        a4_scr[0:hi, :] = a2_scr[0:hi, gd:] + a2_scr[2:hi + 2, gd:]
        a8_scr[0:hi, :] = a4_scr[0:hi, gd:] + a4_scr[4:hi + 4, gd:]

    def pool_group(g):
        w = POOL_WINDOWS[g]
        if g == 0:
            total = rows_at(e_scr, -1, g0) + rows_at(e_scr, 0, g0)
        elif g == 1:
            total = rows_at(a2_scr, -2, g0) + rows_at(a2_scr, 0, g0)
        elif g == 2:
            total = rows_at(a4_scr, -4, g0) + rows_at(a4_scr, 0, g0)
        else:
            total = rows_at(a8_scr, -8, g0) + rows_at(a8_scr, 0, g0)
        pos = i * tm + lax.broadcasted_iota(jnp.int32, (tm, 1), 0)
        cnt = (jnp.minimum(pos + w // 2, seq_len) - jnp.maximum(pos - w // 2, 0)).astype(F32)
        dlt = total / cnt - u_ref[0, :, g * gd:(g + 1) * gd]
        cols = slice(g * gd, (g + 1) * gd)
        return (_dot(dlt.astype(BF16), pw_ref[g].astype(BF16)) * ps_ref[:, cols]).astype(BF16)

    pooled = []
    pool_stages = [pool_fill, pool_double] + [functools.partial(pool_group, g)
                                               for g in range(len(POOL_WINDOWS))]

    p_next = probabilities(scores(0))
    for h in range(heads):
        p_cur = p_next
        if h + 1 < heads:
            p_next = probabilities(scores(h + 1))
        weighted_values(h, p_cur)
        stage = h - (heads - len(pool_stages))
        if stage >= 0:
            out = pool_stages[stage]()
            if out is not None:
                pooled.append(out)
        if h % 2 == 1:
            slab = slice((h - 1) * HEAD_DIM, (h + 1) * HEAD_DIM)
            na_ref[0, :, slab] = nat_scr[slab, :].T.astype(BF16)

    pool_ref[0] = jnp.concatenate(pooled, axis=1)


def _mixer(u, qrt, qpt, kr, vt, kc, vct, bias, pool_w, pool_scale):
    b, l, pw = u.shape
    nw = kr.shape[-1]
    gd = pool_w.shape[-1]
    tm = TILE_ROWS * GRID_W
    nt = l // tm
    hb = tm // POOL_HALO
    n_halo = l // POOL_HALO
    nc = kc.shape[1]
    assert pw == len(POOL_WINDOWS) * gd and POOL_HALO == max(POOL_WINDOWS) // 2

    tok = lambda bi, i: (bi, i, 0)
    chan = lambda bi, i: (bi, 0, i)
    const2 = lambda bi, i: (0, 0)
    first_block = lambda i: jnp.clip(i - 1, 0, nt - 3)

    def k_spec(j):
        return pl.BlockSpec((1, tm, nw), lambda bi, i: (bi, first_block(i) + j, 0))

    def vt_spec(j):
        return pl.BlockSpec((1, nw, tm), lambda bi, i: (bi, 0, first_block(i) + j))

    def bias_idx(bi, i):
        return (jnp.where(i == 0, 0, jnp.where(i == nt - 1, 2, 1)), 0, 0, 0)

    in_specs = [
        pl.BlockSpec((1, tm, pw), tok),
        pl.BlockSpec((1, POOL_HALO, pw), lambda bi, i: (bi, jnp.maximum(i * hb - 1, 0), 0)),
        pl.BlockSpec((1, POOL_HALO, pw), lambda bi, i: (bi, jnp.minimum((i + 1) * hb, n_halo - 1), 0)),
        pl.BlockSpec((1, nw, tm), chan),
        pl.BlockSpec((1, nw, tm), chan),
        k_spec(0), k_spec(1), k_spec(2),
        vt_spec(0), vt_spec(1), vt_spec(2),
        pl.BlockSpec((1, nc, nw), lambda bi, i: (bi, 0, 0)),
        pl.BlockSpec((1, nw, nc), lambda bi, i: (bi, 0, 0)),
        pl.BlockSpec((1,) + bias.shape[1:], bias_idx),
        pl.BlockSpec(pool_w.shape, lambda bi, i: (0, 0, 0)),
        pl.BlockSpec((1, pw), const2),
    ]
    n_scr = tm + 2 * POOL_HALO + SUBLANES
    return pl.pallas_call(
        functools.partial(_mixer_kernel, seq_len=l),
        grid=(b, nt),
        in_specs=in_specs,
        out_specs=[pl.BlockSpec((1, tm, nw), tok), pl.BlockSpec((1, tm, pw), tok)],
        out_shape=[jax.ShapeDtypeStruct((b, l, nw), BF16), jax.ShapeDtypeStruct((b, l, pw), BF16)],
        scratch_shapes=[pltpu.VMEM((nw, tm), F32),
                        pltpu.VMEM((n_scr, pw), F32),
                        pltpu.VMEM((n_scr, pw - gd), F32),
                        pltpu.VMEM((n_scr, pw - 2 * gd), F32),
                        pltpu.VMEM((n_scr, pw - 3 * gd), F32)],
        compiler_params=_params(("arbitrary", "arbitrary")),
        name="mixer",
    )(u, u, u, qrt, qpt, kr, kr, kr, vt, vt, vt, kc, vct, bias, pool_w, pool_scale)


def _post_kernel(x_ref, mod_ref, g_ref, na_ref, pool_ref, gp_ref, gn_ref,
                 wbp_ref, wbn_ref, wo_ref, w1_ref, w2_ref, o_ref, *, chunk):
    merged = (gp_ref[0] * _dot(pool_ref[0], wbp_ref[...]).astype(BF16)
              + gn_ref[0] * _dot(na_ref[0], wbn_ref[...]).astype(BF16))
    row = pl.program_id(0)
    x = x_ref[0] + _mod(mod_ref, 2, row) * _dot(merged, wo_ref[...])
    hb = _modulated_norm(x, mod_ref, g_ref, 3, row).astype(BF16)
    acc = jnp.zeros(x.shape, F32)
    for c0 in range(0, w1_ref.shape[1], chunk):
        a = jnp.maximum(_dot(hb, w1_ref[:, c0:c0 + chunk]), 0.0)
        acc = acc + _dot((a * a).astype(BF16), w2_ref[c0:c0 + chunk, :])
    o_ref[0] = x + _mod(mod_ref, 5, row) * acc


def _post(x, mod3, norm_g, na, pool, gp, gn, wbp, wbn, wo, w1, w2, tm):
    b, l, d = x.shape
    tok = lambda bi, i: (bi, i, 0)
    const = lambda bi, i: (0, 0)
    resident = lambda w: pl.BlockSpec(w.shape, const, pipeline_mode=pl.Buffered(1))
    return pl.pallas_call(
        functools.partial(_post_kernel, chunk=d),
        grid=(b, l // tm),
        in_specs=[pl.BlockSpec((1, tm, d), tok),
                  pl.BlockSpec(mod3.shape, lambda bi, i: (0, 0, 0)),
                  pl.BlockSpec((1, d), const),
                  pl.BlockSpec((1, tm, na.shape[-1]), tok),
                  pl.BlockSpec((1, tm, pool.shape[-1]), tok),
                  pl.BlockSpec((1, tm, d), tok),
                  pl.BlockSpec((1, tm, d), tok),
                  resident(wbp), resident(wbn), resident(wo), resident(w1), resident(w2)],
        out_specs=pl.BlockSpec((1, tm, d), tok),
        out_shape=jax.ShapeDtypeStruct((b, l, d), F32),
        compiler_params=_params(("arbitrary", "arbitrary")),
        name="post",
    )(x, mod3, norm_g, na, pool, gp, gn, wbp, wbn, wo, w1, w2)


def _rope_tables(seq_len):
    n_freq = HEAD_DIM // 4
    t = np.arange(seq_len)
    inv = (ROPE_THETA ** (-np.arange(n_freq, dtype=np.float32) / n_freq)).astype(np.float32)
    ang_row = (t // GRID_W).astype(np.float32)[:, None] * inv
    ang_col = (t % GRID_W).astype(np.float32)[:, None] * inv
    cr, sr, cc, sc = np.cos(ang_row), np.sin(ang_row), np.cos(ang_col), np.sin(ang_col)
    cos = np.concatenate([cr, cr, cc, cc], axis=1).astype(np.float32)
    sin = np.concatenate([-sr, sr, -sc, sc], axis=1).astype(np.float32)
    return (jnp.asarray(np.tile(cos, (1, 2))), jnp.asarray(np.tile(sin, (1, 2))),
            jnp.asarray(np.ascontiguousarray(cos.T)), jnp.asarray(np.ascontiguousarray(sin.T)))


def _bias_kernel(el_ref, er_ref, o_ref, *, rows):
    lanes = 2 * GRID_W
    kc = lax.broadcasted_iota(jnp.int32, (GRID_W, lanes), 0)
    lane = lax.broadcasted_iota(jnp.int32, (GRID_W, lanes), 1)
    qc = lane % GRID_W
    c0 = jnp.clip(qc - NA_COLS // 2, 0, GRID_W - NA_COLS)
    col_ok = (kc >= c0) & (kc < c0 + NA_COLS)
    left = lane < GRID_W
    masked = jnp.full((GRID_W, lanes), MASK_VALUE, F32)

    cache = {}

    def toeplitz(side, dr):
        if (side, dr) not in cache:
            ref = er_ref if side else el_ref
            dd = dr + NA_ROWS - 1
            vec = jnp.broadcast_to(ref[0, dd:dd + 1, :], (GRID_W, lanes))
            cache[(side, dr)] = pltpu.roll(vec, 0, axis=1, stride=1, stride_axis=0) * LOG2E
        return cache[(side, dr)]

    nt = rows // TILE_ROWS
    for t, tile in enumerate((0, nt // 2, nt - 1)):
        r = tile * TILE_ROWS
        ks = min(max(r - TILE_ROWS, 0), rows - KEY_ROWS)
        for j in range(KEY_ROWS):
            kr = ks + j
            for ip in range(TILE_ROWS // 2):
                halves = []
                for side in range(2):
                    qr = r + 2 * ip + side
                    r0 = min(max(qr - NA_ROWS // 2, 0), rows - NA_ROWS)
                    ok = r0 <= kr < r0 + NA_ROWS
                    halves.append(toeplitz(side, kr - qr) if ok else masked)
                blk = jnp.where(col_ok, jnp.where(left, halves[0], halves[1]), MASK_VALUE)
                o_ref[t, 0, j * GRID_W:(j + 1) * GRID_W, ip * lanes:(ip + 1) * lanes] = blk


def _bias_vectors(rpb):
    heads, nr, ncol = rpb.shape
    lanes = 2 * GRID_W
    half = NA_COLS - 1
    zeros = lambda n: jnp.zeros((heads, nr, n), F32)
    flipped = rpb[..., ::-1]
    e_left = jnp.concatenate([flipped[..., half:], zeros(lanes - ncol), flipped[..., :half]], axis=-1)
    e_right = jnp.concatenate([zeros(GRID_W - half), flipped, zeros(lanes - GRID_W + half - ncol)], axis=-1)
    pad = (-nr) % SUBLANES
    e_left = jnp.pad(e_left, ((0, 0), (0, pad), (0, 0)))
    e_right = jnp.pad(e_right, ((0, 0), (0, pad), (0, 0)))
    return e_left, e_right


def kernel(x, c, ctx, c_ctx, ada_w, ada_b, norm1_g, norm2_g, w_in, pool_w, pool_scale,
           q_norm_g, k_norm_g, rpb, w_branch_pool, w_branch_na, w_out, mlp_w1, mlp_w2):
    b, l, d = x.shape
    depth = ada_w.shape[0]
    pw = pool_scale.shape[-1]
    nw = w_branch_na.shape[1]
    heads = nw // HEAD_DIM
    rows = l // GRID_W
    assert depth == 1, "the context-stream update is only needed when another layer follows"
    assert l % (TILE_ROWS * GRID_W) == 0 and rows >= KEY_ROWS
    assert w_in.shape[-1] == pw + 3 * nw + 2 * d and pw == nw
    ctx_row = b
    tm_proj = 1024

    cos_t, sin_t, cos_tt, sin_tt = _rope_tables(l)
    mean_mat = jnp.asarray(np.kron(np.eye(heads), np.full((HEAD_DIM, HEAD_DIM), 1.0 / HEAD_DIM)), BF16)
    pad = (-(b + 1)) % SUBLANES
    cc = jnp.concatenate([c, c_ctx[None], jnp.zeros((pad, d), F32)], axis=0)

    layer = 0
    q0, k0, v0, g0 = pw, pw + nw, pw + 2 * nw, pw + 3 * nw
    mod3, w_all, w_qvt, bias = _prep(cc, ada_w[layer], ada_b[layer], w_in[layer], q0, v0, nw,
                                     rpb[layer], rows)
    qg_tt = q_norm_g[layer][:, None]
    kg_t = jnp.tile(k_norm_g[layer], heads)[None]
    n1 = norm1_g[layer][None]

    k_ctx, vt_ctx = _ctx_proj(ctx, mod3, n1, w_all, w_qvt, kg_t, mean_mat, k0 // nw, ctx_row)
    later_weights = (w_branch_pool[layer], w_branch_na[layer], w_out[layer],
                     mlp_w1[layer], mlp_w2[layer])
    (u, qrt, qpt, kr, vt, gp, gn), (wbp, wbn, wo, w1, w2) = _in_proj(
        x, mod3, n1, w_all, w_qvt, cos_t, sin_t, cos_tt, sin_tt, qg_tt, kg_t, mean_mat,
        (0, k0, g0, g0 + d), pw, tm_proj, later_weights)
    na, pool = _mixer(u, qrt, qpt, kr, vt, k_ctx, vt_ctx, bias,
                      pool_w[layer], pool_scale[layer][None])
    return _post(x, mod3, norm2_g[layer][None], na, pool, gp, gn, wbp, wbn, wo, w1, w2, tm=tm_proj)
```

```python
import functools

import numpy as np
import jax
import jax.numpy as jnp
from jax import lax
from jax.experimental import pallas as pl
from jax.experimental.pallas import tpu as pltpu

GRID_W = 64
N_MOD = 6
POOL_WINDOWS = (2, 4, 8, 16)
HEAD_DIM = 64
NA_ROWS = 8
NA_COLS = 16
ROPE_THETA = 10000.0
EPS = 1e-6
MASK_VALUE = -1e30
LOG2E = 1.4426950408889634

V7X_VMEM_BYTES = 64 * 1024 * 1024
VMEM_LIMIT_BYTES = V7X_VMEM_BYTES - 2 * 1024 * 1024
SUBLANES = 8
BF16_ROWS = 16
POOL_HALO = 8
TILE_ROWS = 4
KEY_ROWS = TILE_ROWS + NA_ROWS

BF16 = jnp.bfloat16
F32 = jnp.float32


def _dot(a, b):
    return jnp.dot(a, b, preferred_element_type=F32)


def _dot_nt(a, b):
    return lax.dot_general(a, b, (((1,), (1,)), ((), ())), preferred_element_type=F32)


def _params(sem):
    return pltpu.CompilerParams(dimension_semantics=sem, vmem_limit_bytes=VMEM_LIMIT_BYTES)


def _rms(x):
    return x * lax.rsqrt(jnp.mean(x * x, axis=-1, keepdims=True) + EPS)


def _sigmoid_bf16(z):
    zb = z.astype(BF16)
    return 0.5 * jnp.tanh(0.5 * zb) + 0.5


def _prep_kernel(c_ref, aw_ref, ab_ref, w_rows_ref, w_cols_ref, wk_cols_ref, el_ref, er_ref,
                 ctx_ref, g_ref, kg_ref, mean_ref,
                 mod_ref, w_all_ref, w_qvt_ref, bias_ref, kctx_ref, vtctx_ref, ss_scr, hb_scr,
                 *, grid_rows, per_seg, ctx_row):
    j = pl.program_id(0)
    c = c_ref[...]
    s = (c * jax.nn.sigmoid(c)).astype(BF16)
    mod = _dot(s, aw_ref[...].astype(BF16)) + ab_ref[...]
    mod_ref[0] = mod

    @pl.when(j < 2)
    def _():
        ss_scr[j] = mod

    w_all_ref[...] = w_rows_ref[...].astype(BF16)
    wt = w_cols_ref[...].T.astype(BF16)
    w_qvt_ref[...] = wt
    _bias_kernel(el_ref, er_ref, bias_ref, rows=grid_rows)

    @pl.when(j == per_seg)
    def _():
        for i in range(ctx_ref.shape[0]):
            hb_scr[i] = _modulated_norm(ctx_ref[i], ss_scr, g_ref, 0, ctx_row).astype(BF16)

    @pl.when(j >= per_seg)
    def _():
        wk = wk_cols_ref[...].astype(BF16)
        for i in range(ctx_ref.shape[0]):
            hb = hb_scr[i]
            kctx_ref[i] = _head_norm(_dot(hb, wk), mean_ref[...], kg_ref[...]).astype(BF16)
            vtctx_ref[i] = _dot_nt(wt, hb).astype(BF16)


def _prep(cc, ada_w, ada_b, w, q0, k0, v0, width, rpb, grid_rows, ctx, norm_g, kg_t, mean_mat, ctx_row):
    m, d = cc.shape
    nb, nc, _ = ctx.shape
    n = ada_w.shape[1]
    lanes = 128
    heads = rpb.shape[0]
    steps = 2 * width // lanes
    assert steps == heads and steps >= N_MOD and n == N_MOD * d and lanes == 2 * HEAD_DIM
    assert w.shape[0] % (steps * BF16_ROWS) == 0
    rows = w.shape[0] // steps
    mod_block = lambda j: jnp.minimum(j, N_MOD - 1)
    per_seg = width // lanes
    col_block = lambda j: jnp.where(j < per_seg, q0 // lanes + j, v0 // lanes + j - per_seg)
    ctx_block = lambda j: jnp.maximum(j - per_seg, 0)
    e_left, e_right = _bias_vectors(rpb)
    vec_spec = pl.BlockSpec((1,) + e_left.shape[1:], lambda j: (j, 0, 0))
    tq, tk = TILE_ROWS * GRID_W, KEY_ROWS * GRID_W
    return pl.pallas_call(
        functools.partial(_prep_kernel, grid_rows=grid_rows, per_seg=per_seg, ctx_row=ctx_row),
        grid=(steps,),
        in_specs=[pl.BlockSpec((m, d), lambda j: (0, 0)),
                  pl.BlockSpec((d, d), lambda j: (0, mod_block(j))),
                  pl.BlockSpec((1, d), lambda j: (0, mod_block(j))),
                  pl.BlockSpec((rows, w.shape[1]), lambda j: (j, 0)),
                  pl.BlockSpec((w.shape[0], lanes), lambda j: (0, col_block(j))),
                  pl.BlockSpec((w.shape[0], lanes), lambda j: (0, k0 // lanes + ctx_block(j))),
                  vec_spec, vec_spec,
                  pl.BlockSpec(ctx.shape, lambda j: (0, 0, 0)),
                  pl.BlockSpec((1, d), lambda j: (0, 0)),
                  pl.BlockSpec((1, lanes), lambda j: (0, 0)),
                  pl.BlockSpec((lanes, lanes), lambda j: (0, 0))],
        out_specs=[pl.BlockSpec((1, m, d), lambda j: (mod_block(j), 0, 0)),
                   pl.BlockSpec((rows, w.shape[1]), lambda j: (j, 0)),
                   pl.BlockSpec((lanes, w.shape[0]), lambda j: (j, 0)),
                   pl.BlockSpec((3, 1, tk, tq), lambda j: (0, j, 0, 0)),
                   pl.BlockSpec((nb, nc, lanes), lambda j: (0, 0, ctx_block(j))),
                   pl.BlockSpec((nb, lanes, nc), lambda j: (0, ctx_block(j), 0))],
        out_shape=[jax.ShapeDtypeStruct((N_MOD, m, d), F32),
                   jax.ShapeDtypeStruct(w.shape, BF16),
                   jax.ShapeDtypeStruct((2 * width, w.shape[0]), BF16),
                   jax.ShapeDtypeStruct((3, heads, tk, tq), F32),
                   jax.ShapeDtypeStruct((nb, nc, width), BF16),
                   jax.ShapeDtypeStruct((nb, width, nc), BF16)],
        scratch_shapes=[pltpu.VMEM((2, m, d), F32), pltpu.VMEM((nb, nc, d), BF16)],
        compiler_params=_params(("arbitrary",)),
        name="prep",
    )(cc, ada_w, ada_b.reshape(1, n), w, w, w, e_left, e_right, ctx, norm_g, kg_t, mean_mat)


def _head_norm(a, mean_mat, g):
    ms = _dot((a * a).astype(BF16), mean_mat)
    return a * lax.rsqrt(ms + EPS) * g


def _rope(a, cos, sin_signed, first_half):
    width = a.shape[-1]
    quarter = HEAD_DIM // 4
    up = pltpu.roll(a, width - quarter, axis=1)
    down = pltpu.roll(a, quarter, axis=1)
    swapped = jnp.where(first_half, up, down)
    return a * cos + swapped * sin_signed


def _mod(mod_ref, k, row):
    return mod_ref[k, pl.ds(row, 1), :]


def _modulated_norm(x, mod_ref, g_ref, shift_k, row):
    return _rms(x) * g_ref[...] * (1.0 + _mod(mod_ref, shift_k + 1, row)) + _mod(mod_ref, shift_k, row)


def _in_proj_kernel(*refs, col_starts, n_cast):
    (x_ref, mod_ref, g_ref, w_ref, wt_ref, cos_ref, sin_ref, cost_ref, sint_ref,
     qgt_ref, kg_ref, mean_ref) = refs[:12]
    cast_in = refs[12:12 + n_cast]
    u_ref, qrt_ref, qpt_ref, kr_ref, vt_ref, gp_ref, gn_ref = refs[12 + n_cast:19 + n_cast]
    cast_out = refs[19 + n_cast:]
    for src, dst in zip(cast_in, cast_out):
        dst[...] = src[...].astype(BF16)

    pw = u_ref.shape[-1]
    nw = kr_ref.shape[-1]
    d = x_ref.shape[-1]
    heads = nw // HEAD_DIM
    quarter = HEAD_DIM // 4
    scale = HEAD_DIM ** -0.5 * LOG2E
    u0, k0, gp0, gn0 = col_starts

    hb = _modulated_norm(x_ref[0], mod_ref, g_ref, 0, pl.program_id(0)).astype(BF16)

    cost, sint, qgt = cost_ref[...], sint_ref[...], qgt_ref[...]

    def query_heads(first, last):
        qt = _dot_nt(wt_ref[first * HEAD_DIM:last * HEAD_DIM, :], hb)
        for h in range(first, last):
            rows = slice(h * HEAD_DIM, (h + 1) * HEAD_DIM)
            z = qt[(h - first) * HEAD_DIM:(h - first + 1) * HEAD_DIM, :]
            qn = z * lax.rsqrt(jnp.mean(z * z, axis=0, keepdims=True) + EPS) * (qgt * scale)
            swapped = jnp.concatenate([qn[quarter:2 * quarter], qn[0:quarter],
                                       qn[3 * quarter:], qn[2 * quarter:3 * quarter]], axis=0)
            qpt_ref[0, rows, :] = qn.astype(BF16)
            qrt_ref[0, rows, :] = (qn * cost + swapped * sint).astype(BF16)

    z_gp = _dot(hb, w_ref[:, gp0:gp0 + d])
    query_heads(0, heads // 2)
    z_gn = _dot(hb, w_ref[:, gn0:gn0 + d])
    gp_ref[0] = _sigmoid_bf16(z_gp)
    query_heads(heads // 2, heads)
    z_k = _dot(hb, w_ref[:, k0:k0 + nw])
    gn_ref[0] = _sigmoid_bf16(z_gn)
    z_vt = _dot_nt(wt_ref[nw:, :], hb)

    reps = nw // cos_ref.shape[-1]
    cos = jnp.concatenate([cos_ref[...]] * reps, axis=1)
    sin = jnp.concatenate([sin_ref[...]] * reps, axis=1)
    lane = lax.broadcasted_iota(jnp.int32, (1, nw), 1)
    first_half = (lane % (HEAD_DIM // 2)) < quarter
    k = _head_norm(z_k, mean_ref[...], kg_ref[...])
    kr_ref[0] = _rope(k, cos, sin, first_half).astype(BF16)
    vt_ref[0] = z_vt.astype(BF16)
    u_ref[0] = _dot(hb, w_ref[:, u0:u0 + pw])


def _in_proj(x, mod3, norm_g, w_all, w_qvt, cos_t, sin_t, cos_tt, sin_tt, qg_tt, kg_t, mean_mat,
             col_starts, pw, tm, later_weights):
    b, l, d = x.shape
    nw = kg_t.shape[-1]
    tw = cos_t.shape[-1]
    nt = l // tm
    tok = lambda bi, i: (bi, i, 0)
    chan = lambda bi, i: (bi, 0, i)
    const = lambda bi, i: (0, 0)
    sd = jax.ShapeDtypeStruct
    slab_specs = []
    for w in later_weights:
        assert w.shape[0] % (b * nt * BF16_ROWS) == 0
        slab_specs.append(pl.BlockSpec((w.shape[0] // (b * nt), w.shape[1]),
                                       lambda bi, i: (bi * nt + i, 0)))
    outs = pl.pallas_call(
        functools.partial(_in_proj_kernel, col_starts=col_starts, n_cast=len(later_weights)),
        grid=(b, nt),
        in_specs=[pl.BlockSpec((1, tm, d), tok),
                  pl.BlockSpec(mod3.shape, lambda bi, i: (0, 0, 0)),
                  pl.BlockSpec((1, d), const),
                  pl.BlockSpec(w_all.shape, const, pipeline_mode=pl.Buffered(1)),
                  pl.BlockSpec(w_qvt.shape, const, pipeline_mode=pl.Buffered(1)),
                  pl.BlockSpec((tm, tw), lambda bi, i: (i, 0)),
                  pl.BlockSpec((tm, tw), lambda bi, i: (i, 0)),
                  pl.BlockSpec((HEAD_DIM, tm), lambda bi, i: (0, i)),
                  pl.BlockSpec((HEAD_DIM, tm), lambda bi, i: (0, i)),
                  pl.BlockSpec((HEAD_DIM, 1), const),
                  pl.BlockSpec((1, nw), const),
                  pl.BlockSpec((nw, nw), const)] + slab_specs,
        out_specs=[pl.BlockSpec((1, tm, pw), tok),
                   pl.BlockSpec((1, nw, tm), chan), pl.BlockSpec((1, nw, tm), chan),
                   pl.BlockSpec((1, tm, nw), tok),
                   pl.BlockSpec((1, nw, tm), chan),
                   pl.BlockSpec((1, tm, d), tok), pl.BlockSpec((1, tm, d), tok)] + slab_specs,
        out_shape=[sd((b, l, pw), F32),
                   sd((b, nw, l), BF16), sd((b, nw, l), BF16),
                   sd((b, l, nw), BF16),
                   sd((b, nw, l), BF16),
                   sd((b, l, d), BF16), sd((b, l, d), BF16)]
                  + [sd(w.shape, BF16) for w in later_weights],
        compiler_params=_params(("arbitrary", "arbitrary")),
        name="in_proj",
    )(x, mod3, norm_g, w_all, w_qvt, cos_t, sin_t, cos_tt, sin_tt, qg_tt, kg_t, mean_mat,
      *later_weights)
    return outs[:7], outs[7:]


def _mixer_kernel(u_ref, up_ref, un_ref, qrt_ref, qpt_ref,
                  k0_ref, k1_ref, k2_ref, vt0_ref, vt1_ref, vt2_ref, kc_ref, vct_ref, bias_ref,
                  pw_ref, ps_ref,
                  na_ref, pool_ref, nat_scr, e_scr, a2_scr, a4_scr, a8_scr, *, seq_len):
    i = pl.program_id(1)
    tm = u_ref.shape[1]
    nw = qrt_ref.shape[1]
    heads = nw // HEAD_DIM
    pair = 2 * HEAD_DIM

    zeros = jnp.zeros((HEAD_DIM, tm), BF16)
    ones = jnp.ones((BF16_ROWS, tm), BF16)
    k_refs = (k0_ref, k1_ref, k2_ref)
    vt_refs = (vt0_ref, vt1_ref, vt2_ref)

    def head_rows(h):
        return slice(h * HEAD_DIM, (h + 1) * HEAD_DIM)

    def scores(h):
        rows = head_rows(h)
        lanes = slice((h // 2) * pair, (h // 2 + 1) * pair)
        q_rot, q_plain = qrt_ref[0, rows, :], qpt_ref[0, rows, :]
        if h % 2 == 0:
            w_rot = jnp.concatenate([q_rot, zeros], axis=0)
            w_plain = jnp.concatenate([q_plain, zeros], axis=0)
        else:
            w_rot = jnp.concatenate([zeros, q_rot], axis=0)
            w_plain = jnp.concatenate([zeros, q_plain], axis=0)
        s = [(_dot(k_refs[j][0, :, lanes], w_rot) + bias_ref[0, h, j * tm:(j + 1) * tm, :]).astype(BF16)
             for j in range(3)]
        s.append(_dot(kc_ref[0, :, lanes], w_plain).astype(BF16))
        m = s[0].max(axis=0, keepdims=True)
        for sj in s[1:]:
            m = jnp.maximum(m, sj.max(axis=0, keepdims=True))
        return s, m

    def probabilities(s_and_m):
        s, m = s_and_m
        return [jnp.exp2(sj - m) for sj in s]

    def weighted_values(h, p):
        rows = head_rows(h)
        ot = _dot(jnp.concatenate([vct_ref[0, rows, :], ones], axis=0), p[3])
        for j in range(3):
            ot = ot + _dot(jnp.concatenate([vt_refs[j][0, rows, :], ones], axis=0), p[j])
        nat_scr[rows, :] = ot[:HEAD_DIM] / ot[HEAD_DIM:HEAD_DIM + 1]

    gd = pw_ref.shape[-1]
    hi = tm + 2 * POOL_HALO
    first = POOL_HALO
    g0 = slice(0, gd)

    def rows_at(ref, off, cols):
        return ref[first + off:first + off + tm, cols]

    def pool_fill():
        for scr in (e_scr, a2_scr, a4_scr):
            scr[hi:hi + SUBLANES, :] = jnp.zeros((SUBLANES, scr.shape[1]), F32)
        e_scr[0:first, :] = jnp.where(i > 0, up_ref[0], 0.0)
        e_scr[first:first + tm, :] = u_ref[0]
        e_scr[first + tm:hi, :] = jnp.where((i + 1) * tm < seq_len, un_ref[0], 0.0)
        a2_scr[0:hi, :] = e_scr[0:hi, gd:] + e_scr[1:hi + 1, gd:]

    def pool_double():
        a4_scr[0:hi, :] = a2_scr[0:hi, gd:] + a2_scr[2:hi + 2, gd:]
        a8_scr[0:hi, :] = a4_scr[0:hi, gd:] + a4_scr[4:hi + 4, gd:]

    def pool_group(g):
        w = POOL_WINDOWS[g]
        if g == 0:
            total = rows_at(e_scr, -1, g0) + rows_at(e_scr, 0, g0)
        elif g == 1:
            total = rows_at(a2_scr, -2, g0) + rows_at(a2_scr, 0, g0)
        elif g == 2:
            total = rows_at(a4_scr, -4, g0) + rows_at(a4_scr, 0, g0)
        else:
            total = rows_at(a8_scr, -8, g0) + rows_at(a8_scr, 0, g0)
        pos = i * tm + lax.broadcasted_iota(jnp.int32, (tm, 1), 0)
        cnt = (jnp.minimum(pos + w // 2, seq_len) - jnp.maximum(pos - w // 2, 0)).astype(F32)
        dlt = total / cnt - u_ref[0, :, g * gd:(g + 1) * gd]
        cols = slice(g * gd, (g + 1) * gd)
        return (_dot(dlt.astype(BF16), pw_ref[g].astype(BF16)) * ps_ref[:, cols]).astype(BF16)

    pooled = []
    pool_stages = [pool_fill, pool_double] + [functools.partial(pool_group, g)
                                               for g in range(len(POOL_WINDOWS))]

    p_next = probabilities(scores(0))
    for h in range(heads):
        p_cur = p_next
        if h + 1 < heads:
            p_next = probabilities(scores(h + 1))
        weighted_values(h, p_cur)
        stage = h - (heads - len(pool_stages))
        if stage >= 0:
            out = pool_stages[stage]()
            if out is not None:
                pooled.append(out)
        if h % 2 == 1:
            slab = slice((h - 1) * HEAD_DIM, (h + 1) * HEAD_DIM)
            na_ref[0, :, slab] = nat_scr[slab, :].T.astype(BF16)

    pool_ref[0] = jnp.concatenate(pooled, axis=1)


def _mixer(u, qrt, qpt, kr, vt, kc, vct, bias, pool_w, pool_scale):
    b, l, pw = u.shape
    nw = kr.shape[-1]
    gd = pool_w.shape[-1]
    tm = TILE_ROWS * GRID_W
    nt = l // tm
    hb = tm // POOL_HALO
    n_halo = l // POOL_HALO
    nc = kc.shape[1]
    assert pw == len(POOL_WINDOWS) * gd and POOL_HALO == max(POOL_WINDOWS) // 2

    tok = lambda bi, i: (bi, i, 0)
    chan = lambda bi, i: (bi, 0, i)
    const2 = lambda bi, i: (0, 0)
    first_block = lambda i: jnp.clip(i - 1, 0, nt - 3)

    def k_spec(j):
        return pl.BlockSpec((1, tm, nw), lambda bi, i: (bi, first_block(i) + j, 0))

    def vt_spec(j):
        return pl.BlockSpec((1, nw, tm), lambda bi, i: (bi, 0, first_block(i) + j))

    def bias_idx(bi, i):
        return (jnp.where(i == 0, 0, jnp.where(i == nt - 1, 2, 1)), 0, 0, 0)

    in_specs = [
        pl.BlockSpec((1, tm, pw), tok),
        pl.BlockSpec((1, POOL_HALO, pw), lambda bi, i: (bi, jnp.maximum(i * hb - 1, 0), 0)),
        pl.BlockSpec((1, POOL_HALO, pw), lambda bi, i: (bi, jnp.minimum((i + 1) * hb, n_halo - 1), 0)),
        pl.BlockSpec((1, nw, tm), chan),
        pl.BlockSpec((1, nw, tm), chan),
        k_spec(0), k_spec(1), k_spec(2),
        vt_spec(0), vt_spec(1), vt_spec(2),
        pl.BlockSpec((1, nc, nw), lambda bi, i: (bi, 0, 0)),
        pl.BlockSpec((1, nw, nc), lambda bi, i: (bi, 0, 0)),
        pl.BlockSpec((1,) + bias.shape[1:], bias_idx),
        pl.BlockSpec(pool_w.shape, lambda bi, i: (0, 0, 0)),
        pl.BlockSpec((1, pw), const2),
    ]
    n_scr = tm + 2 * POOL_HALO + SUBLANES
    return pl.pallas_call(
        functools.partial(_mixer_kernel, seq_len=l),
        grid=(b, nt),
        in_specs=in_specs,
        out_specs=[pl.BlockSpec((1, tm, nw), tok), pl.BlockSpec((1, tm, pw), tok)],
        out_shape=[jax.ShapeDtypeStruct((b, l, nw), BF16), jax.ShapeDtypeStruct((b, l, pw), BF16)],
        scratch_shapes=[pltpu.VMEM((nw, tm), F32),
                        pltpu.VMEM((n_scr, pw), F32),
                        pltpu.VMEM((n_scr, pw - gd), F32),
                        pltpu.VMEM((n_scr, pw - 2 * gd), F32),
                        pltpu.VMEM((n_scr, pw - 3 * gd), F32)],
        compiler_params=_params(("arbitrary", "arbitrary")),
        name="mixer",
    )(u, u, u, qrt, qpt, kr, kr, kr, vt, vt, vt, kc, vct, bias, pool_w, pool_scale)


def _post_kernel(x_ref, mod_ref, g_ref, na_ref, pool_ref, gp_ref, gn_ref,
                 wbp_ref, wbn_ref, wo_ref, w1_ref, w2_ref, o_ref, *, chunk):
    merged = (gp_ref[0].astype(F32) * _dot(pool_ref[0], wbp_ref[...])
              + gn_ref[0].astype(F32) * _dot(na_ref[0], wbn_ref[...]))
    row = pl.program_id(0)
    x = x_ref[0] + _mod(mod_ref, 2, row) * _dot(merged.astype(BF16), wo_ref[...])
    hb = _modulated_norm(x, mod_ref, g_ref, 3, row).astype(BF16)
    acc = jnp.zeros(x.shape, F32)
    for c0 in range(0, w1_ref.shape[1], chunk):
        a = jnp.maximum(_dot(hb, w1_ref[:, c0:c0 + chunk]), 0.0)
        acc = acc + _dot((a * a).astype(BF16), w2_ref[c0:c0 + chunk, :])
    o_ref[0] = x + _mod(mod_ref, 5, row) * acc


def _post(x, mod3, norm_g, na, pool, gp, gn, wbp, wbn, wo, w1, w2, tm):
    b, l, d = x.shape
    tok = lambda bi, i: (bi, i, 0)
    const = lambda bi, i: (0, 0)
    resident = lambda w: pl.BlockSpec(w.shape, const, pipeline_mode=pl.Buffered(1))
    return pl.pallas_call(
        functools.partial(_post_kernel, chunk=d),
        grid=(b, l // tm),
        in_specs=[pl.BlockSpec((1, tm, d), tok),
                  pl.BlockSpec(mod3.shape, lambda bi, i: (0, 0, 0)),
                  pl.BlockSpec((1, d), const),
                  pl.BlockSpec((1, tm, na.shape[-1]), tok),
                  pl.BlockSpec((1, tm, pool.shape[-1]), tok),
                  pl.BlockSpec((1, tm, d), tok),
                  pl.BlockSpec((1, tm, d), tok),
                  resident(wbp), resident(wbn), resident(wo), resident(w1), resident(w2)],
        out_specs=pl.BlockSpec((1, tm, d), tok),
        out_shape=jax.ShapeDtypeStruct((b, l, d), F32),
        compiler_params=_params(("arbitrary", "arbitrary")),
        name="post",
    )(x, mod3, norm_g, na, pool, gp, gn, wbp, wbn, wo, w1, w2)


def _rope_tables(seq_len):
    n_freq = HEAD_DIM // 4
    t = np.arange(seq_len)
    inv = (ROPE_THETA ** (-np.arange(n_freq, dtype=np.float32) / n_freq)).astype(np.float32)
    ang_row = (t // GRID_W).astype(np.float32)[:, None] * inv
    ang_col = (t % GRID_W).astype(np.float32)[:, None] * inv
    cr, sr, cc, sc = np.cos(ang_row), np.sin(ang_row), np.cos(ang_col), np.sin(ang_col)
    cos = np.concatenate([cr, cr, cc, cc], axis=1).astype(np.float32)
    sin = np.concatenate([-sr, sr, -sc, sc], axis=1).astype(np.float32)
    return (jnp.asarray(np.tile(cos, (1, 2))), jnp.asarray(np.tile(sin, (1, 2))),
            jnp.asarray(np.ascontiguousarray(cos.T)), jnp.asarray(np.ascontiguousarray(sin.T)))


def _bias_kernel(el_ref, er_ref, o_ref, *, rows):
    lanes = 2 * GRID_W
    kc = lax.broadcasted_iota(jnp.int32, (GRID_W, lanes), 0)
    lane = lax.broadcasted_iota(jnp.int32, (GRID_W, lanes), 1)
    qc = lane % GRID_W
    c0 = jnp.clip(qc - NA_COLS // 2, 0, GRID_W - NA_COLS)
    col_ok = (kc >= c0) & (kc < c0 + NA_COLS)
    left = lane < GRID_W
    masked = jnp.full((GRID_W, lanes), MASK_VALUE, F32)

    cache = {}

    def toeplitz(side, dr):
        if (side, dr) not in cache:
            ref = er_ref if side else el_ref
            dd = dr + NA_ROWS - 1
            vec = jnp.broadcast_to(ref[0, dd:dd + 1, :], (GRID_W, lanes))
            cache[(side, dr)] = pltpu.roll(vec, 0, axis=1, stride=1, stride_axis=0) * LOG2E
        return cache[(side, dr)]

    nt = rows // TILE_ROWS
    for t, tile in enumerate((0, nt // 2, nt - 1)):
        r = tile * TILE_ROWS
        ks = min(max(r - TILE_ROWS, 0), rows - KEY_ROWS)
        for j in range(KEY_ROWS):
            kr = ks + j
            for ip in range(TILE_ROWS // 2):
                halves = []
                for side in range(2):
                    qr = r + 2 * ip + side
                    r0 = min(max(qr - NA_ROWS // 2, 0), rows - NA_ROWS)
                    ok = r0 <= kr < r0 + NA_ROWS
                    halves.append(toeplitz(side, kr - qr) if ok else masked)
                blk = jnp.where(col_ok, jnp.where(left, halves[0], halves[1]), MASK_VALUE)
                o_ref[t, 0, j * GRID_W:(j + 1) * GRID_W, ip * lanes:(ip + 1) * lanes] = blk


def _bias_vectors(rpb):
    heads, nr, ncol = rpb.shape
    lanes = 2 * GRID_W
    half = NA_COLS - 1
    zeros = lambda n: jnp.zeros((heads, nr, n), F32)
    flipped = rpb[..., ::-1]
    e_left = jnp.concatenate([flipped[..., half:], zeros(lanes - ncol), flipped[..., :half]], axis=-1)
    e_right = jnp.concatenate([zeros(GRID_W - half), flipped, zeros(lanes - GRID_W + half - ncol)], axis=-1)
    pad = (-nr) % SUBLANES
    e_left = jnp.pad(e_left, ((0, 0), (0, pad), (0, 0)))
    e_right = jnp.pad(e_right, ((0, 0), (0, pad), (0, 0)))
    return e_left, e_right


def kernel(x, c, ctx, c_ctx, ada_w, ada_b, norm1_g, norm2_g, w_in, pool_w, pool_scale,
           q_norm_g, k_norm_g, rpb, w_branch_pool, w_branch_na, w_out, mlp_w1, mlp_w2):
    b, l, d = x.shape
    depth = ada_w.shape[0]
    pw = pool_scale.shape[-1]
    nw = w_branch_na.shape[1]
    heads = nw // HEAD_DIM
    rows = l // GRID_W
    assert depth == 1, "the context-stream update is only needed when another layer follows"
    assert l % (TILE_ROWS * GRID_W) == 0 and rows >= KEY_ROWS
    assert w_in.shape[-1] == pw + 3 * nw + 2 * d and pw == nw
    ctx_row = b
    tm_proj = 1024

    cos_t, sin_t, cos_tt, sin_tt = _rope_tables(l)
    mean_mat = jnp.asarray(np.kron(np.eye(heads), np.full((HEAD_DIM, HEAD_DIM), 1.0 / HEAD_DIM)), BF16)
    pad = (-(b + 1)) % SUBLANES
    cc = jnp.concatenate([c, c_ctx[None], jnp.zeros((pad, d), F32)], axis=0)

    layer = 0
    q0, k0, v0, g0 = pw, pw + nw, pw + 2 * nw, pw + 3 * nw
    qg_tt = q_norm_g[layer][:, None]
    kg_t = jnp.tile(k_norm_g[layer], heads)[None]
    n1 = norm1_g[layer][None]
    mod3, w_all, w_qvt, bias, k_ctx, vt_ctx = _prep(
        cc, ada_w[layer], ada_b[layer], w_in[layer], q0, k0, v0, nw, rpb[layer], rows,
        ctx, n1, kg_t, mean_mat, ctx_row)
    later_weights = (w_branch_pool[layer], w_branch_na[layer], w_out[layer],
                     mlp_w1[layer], mlp_w2[layer])
    (u, qrt, qpt, kr, vt, gp, gn), (wbp, wbn, wo, w1, w2) = _in_proj(
        x, mod3, n1, w_all, w_qvt, cos_t, sin_t, cos_tt, sin_tt, qg_tt, kg_t, mean_mat,
        (0, k0, g0, g0 + d), pw, tm_proj, later_weights)
    na, pool = _mixer(u, qrt, qpt, kr, vt, k_ctx, vt_ctx, bias,
                      pool_w[layer], pool_scale[layer][None])
    return _post(x, mod3, norm2_g[layer][None], na, pool, gp, gn, wbp, wbn, wo, w1, w2, tm=tm_proj)
```

```python
import functools

import numpy as np
import jax
import jax.numpy as jnp
from jax import lax
from jax.experimental import pallas as pl
from jax.experimental.pallas import tpu as pltpu

GRID_W = 64
N_MOD = 6
POOL_WINDOWS = (2, 4, 8, 16)
HEAD_DIM = 64
NA_ROWS = 8
NA_COLS = 16
ROPE_THETA = 10000.0
EPS = 1e-6
MASK_VALUE = -1e30
LOG2E = 1.4426950408889634

V7X_VMEM_BYTES = 64 * 1024 * 1024
VMEM_LIMIT_BYTES = V7X_VMEM_BYTES - 2 * 1024 * 1024
SUBLANES = 8
BF16_ROWS = 16
POOL_HALO = 8
TILE_ROWS = 4
KEY_ROWS = TILE_ROWS + NA_ROWS

BF16 = jnp.bfloat16
F32 = jnp.float32


def _dot(a, b):
    return jnp.dot(a, b, preferred_element_type=F32)


def _dot_nt(a, b):
    return lax.dot_general(a, b, (((1,), (1,)), ((), ())), preferred_element_type=F32)


def _params(sem):
    return pltpu.CompilerParams(dimension_semantics=sem, vmem_limit_bytes=VMEM_LIMIT_BYTES)


def _rms(x):
    return x * lax.rsqrt(jnp.mean(x * x, axis=-1, keepdims=True) + EPS)


def _sigmoid_bf16(z):
    zb = z.astype(BF16)
    return 0.5 * jnp.tanh(0.5 * zb) + 0.5


def _prep_kernel(c_ref, aw_ref, ab_ref, w_rows_ref, w_cols_ref, wk_cols_ref, el_ref, er_ref,
                 ctx_ref, g_ref, kg_ref, mean_ref,
                 mod_ref, w_all_ref, w_qvt_ref, bias_ref, kctx_ref, vtctx_ref, ss_scr, hb_scr,
                 *, grid_rows, per_seg, ctx_row):
    j = pl.program_id(0)
    c = c_ref[...]
    s = (c * jax.nn.sigmoid(c)).astype(BF16)
    mod = _dot(s, aw_ref[...].astype(BF16)) + ab_ref[...]
    mod_ref[0] = mod

    @pl.when(j < 2)
    def _():
        ss_scr[j] = mod

    w_all_ref[...] = w_rows_ref[...].astype(BF16)
    wt = w_cols_ref[...].T.astype(BF16)
    w_qvt_ref[...] = wt
    _bias_kernel(el_ref, er_ref, bias_ref, rows=grid_rows)

    @pl.when(j == per_seg)
    def _():
        for i in range(ctx_ref.shape[0]):
            hb_scr[i] = _modulated_norm(ctx_ref[i], ss_scr, g_ref, 0, ctx_row).astype(BF16)

    @pl.when(j >= per_seg)
    def _():
        wk = wk_cols_ref[...].astype(BF16)
        for i in range(ctx_ref.shape[0]):
            hb = hb_scr[i]
            kctx_ref[i] = _head_norm(_dot(hb, wk), mean_ref[...], kg_ref[...]).astype(BF16)
            vtctx_ref[i] = _dot_nt(wt, hb).astype(BF16)


def _prep(cc, ada_w, ada_b, w, q0, k0, v0, width, rpb, grid_rows, ctx, norm_g, kg_t, mean_mat, ctx_row):
    m, d = cc.shape
    nb, nc, _ = ctx.shape
    n = ada_w.shape[1]
    lanes = 128
    heads = rpb.shape[0]
    steps = 2 * width // lanes
    assert steps == heads and steps >= N_MOD and n == N_MOD * d and lanes == 2 * HEAD_DIM
    assert w.shape[0] % (steps * BF16_ROWS) == 0
    rows = w.shape[0] // steps
    mod_block = lambda j: jnp.minimum(j, N_MOD - 1)
    per_seg = width // lanes
    col_block = lambda j: jnp.where(j < per_seg, q0 // lanes + j, v0 // lanes + j - per_seg)
    ctx_block = lambda j: jnp.maximum(j - per_seg, 0)
    e_left, e_right = _bias_vectors(rpb)
    vec_spec = pl.BlockSpec((1,) + e_left.shape[1:], lambda j: (j, 0, 0))
    tq, tk = TILE_ROWS * GRID_W, KEY_ROWS * GRID_W
    return pl.pallas_call(
        functools.partial(_prep_kernel, grid_rows=grid_rows, per_seg=per_seg, ctx_row=ctx_row),
        grid=(steps,),
        in_specs=[pl.BlockSpec((m, d), lambda j: (0, 0)),
                  pl.BlockSpec((d, d), lambda j: (0, mod_block(j))),
                  pl.BlockSpec((1, d), lambda j: (0, mod_block(j))),
                  pl.BlockSpec((rows, w.shape[1]), lambda j: (j, 0)),
                  pl.BlockSpec((w.shape[0], lanes), lambda j: (0, col_block(j))),
                  pl.BlockSpec((w.shape[0], lanes), lambda j: (0, k0 // lanes + ctx_block(j))),
                  vec_spec, vec_spec,
                  pl.BlockSpec(ctx.shape, lambda j: (0, 0, 0)),
                  pl.BlockSpec((1, d), lambda j: (0, 0)),
                  pl.BlockSpec((1, lanes), lambda j: (0, 0)),
                  pl.BlockSpec((lanes, lanes), lambda j: (0, 0))],
        out_specs=[pl.BlockSpec((1, m, d), lambda j: (mod_block(j), 0, 0)),
                   pl.BlockSpec((rows, w.shape[1]), lambda j: (j, 0)),
                   pl.BlockSpec((lanes, w.shape[0]), lambda j: (j, 0)),
                   pl.BlockSpec((3, 1, tk, tq), lambda j: (0, j, 0, 0)),
                   pl.BlockSpec((nb, nc, lanes), lambda j: (0, 0, ctx_block(j))),
                   pl.BlockSpec((nb, lanes, nc), lambda j: (0, ctx_block(j), 0))],
        out_shape=[jax.ShapeDtypeStruct((N_MOD, m, d), F32),
                   jax.ShapeDtypeStruct(w.shape, BF16),
                   jax.ShapeDtypeStruct((2 * width, w.shape[0]), BF16),
                   jax.ShapeDtypeStruct((3, heads, tk, tq), F32),
                   jax.ShapeDtypeStruct((nb, nc, width), BF16),
                   jax.ShapeDtypeStruct((nb, width, nc), BF16)],
        scratch_shapes=[pltpu.VMEM((2, m, d), F32), pltpu.VMEM((nb, nc, d), BF16)],
        compiler_params=_params(("arbitrary",)),
        name="prep",
    )(cc, ada_w, ada_b.reshape(1, n), w, w, w, e_left, e_right, ctx, norm_g, kg_t, mean_mat)


def _head_norm(a, mean_mat, g):
    ms = _dot((a * a).astype(BF16), mean_mat)
    return a * lax.rsqrt(ms + EPS) * g


def _rope(a, cos, sin_signed, first_half):
    width = a.shape[-1]
    quarter = HEAD_DIM // 4
    up = pltpu.roll(a, width - quarter, axis=1)
    down = pltpu.roll(a, quarter, axis=1)
    swapped = jnp.where(first_half, up, down)
    return a * cos + swapped * sin_signed


def _mod(mod_ref, k, row):
    return mod_ref[k, pl.ds(row, 1), :]


def _modulated_norm(x, mod_ref, g_ref, shift_k, row):
    return _rms(x) * g_ref[...] * (1.0 + _mod(mod_ref, shift_k + 1, row)) + _mod(mod_ref, shift_k, row)


def _in_proj_kernel(*refs, col_starts, n_cast):
    (x_ref, mod_ref, g_ref, w_ref, wt_ref, cos_ref, sin_ref, cost_ref, sint_ref,
     qgt_ref, kg_ref, mean_ref) = refs[:12]
    cast_in = refs[12:12 + n_cast]
    u_ref, qrt_ref, qpt_ref, kr_ref, vt_ref, gp_ref, gn_ref = refs[12 + n_cast:19 + n_cast]
    cast_out = refs[19 + n_cast:]
    for src, dst in zip(cast_in, cast_out):
        dst[...] = src[...].astype(BF16)

    pw = u_ref.shape[-1]
    nw = kr_ref.shape[-1]
    d = x_ref.shape[-1]
    heads = nw // HEAD_DIM
    quarter = HEAD_DIM // 4
    scale = HEAD_DIM ** -0.5 * LOG2E
    u0, k0, gp0, gn0 = col_starts

    hb = _modulated_norm(x_ref[0], mod_ref, g_ref, 0, pl.program_id(0)).astype(BF16)

    cost, sint, qgt = cost_ref[...], sint_ref[...], qgt_ref[...]

    def query_heads(first, last):
        qt = _dot_nt(wt_ref[first * HEAD_DIM:last * HEAD_DIM, :], hb)
        for h in range(first, last):
            rows = slice(h * HEAD_DIM, (h + 1) * HEAD_DIM)
            z = qt[(h - first) * HEAD_DIM:(h - first + 1) * HEAD_DIM, :]
            qn = z * lax.rsqrt(jnp.mean(z * z, axis=0, keepdims=True) + EPS) * (qgt * scale)
            swapped = jnp.concatenate([qn[quarter:2 * quarter], qn[0:quarter],
                                       qn[3 * quarter:], qn[2 * quarter:3 * quarter]], axis=0)
            qpt_ref[0, rows, :] = qn.astype(BF16)
            qrt_ref[0, rows, :] = (qn * cost + swapped * sint).astype(BF16)

    z_gp = _dot(hb, w_ref[:, gp0:gp0 + d])
    query_heads(0, heads // 2)
    z_gn = _dot(hb, w_ref[:, gn0:gn0 + d])
    gp_ref[0] = _sigmoid_bf16(z_gp)
    query_heads(heads // 2, heads)
    z_k = _dot(hb, w_ref[:, k0:k0 + nw])
    gn_ref[0] = _sigmoid_bf16(z_gn)
    z_vt = _dot_nt(wt_ref[nw:, :], hb)

    reps = nw // cos_ref.shape[-1]
    cos = jnp.concatenate([cos_ref[...]] * reps, axis=1)
    sin = jnp.concatenate([sin_ref[...]] * reps, axis=1)
    lane = lax.broadcasted_iota(jnp.int32, (1, nw), 1)
    first_half = (lane % (HEAD_DIM // 2)) < quarter
    k = _head_norm(z_k, mean_ref[...], kg_ref[...])
    kr_ref[0] = _rope(k, cos, sin, first_half).astype(BF16)
    vt_ref[0] = z_vt.astype(BF16)
    u_ref[0] = _dot(hb, w_ref[:, u0:u0 + pw])


def _in_proj(x, mod3, norm_g, w_all, w_qvt, cos_t, sin_t, cos_tt, sin_tt, qg_tt, kg_t, mean_mat,
             col_starts, pw, tm, later_weights):
    b, l, d = x.shape
    nw = kg_t.shape[-1]
    tw = cos_t.shape[-1]
    nt = l // tm
    tok = lambda bi, i: (bi, i, 0)
    chan = lambda bi, i: (bi, 0, i)
    const = lambda bi, i: (0, 0)
    sd = jax.ShapeDtypeStruct
    slab_specs = []
    for w in later_weights:
        assert w.shape[0] % (b * nt * BF16_ROWS) == 0
        slab_specs.append(pl.BlockSpec((w.shape[0] // (b * nt), w.shape[1]),
                                       lambda bi, i: (bi * nt + i, 0)))
    outs = pl.pallas_call(
        functools.partial(_in_proj_kernel, col_starts=col_starts, n_cast=len(later_weights)),
        grid=(b, nt),
        in_specs=[pl.BlockSpec((1, tm, d), tok),
                  pl.BlockSpec(mod3.shape, lambda bi, i: (0, 0, 0)),
                  pl.BlockSpec((1, d), const),
                  pl.BlockSpec(w_all.shape, const, pipeline_mode=pl.Buffered(1)),
                  pl.BlockSpec(w_qvt.shape, const, pipeline_mode=pl.Buffered(1)),
                  pl.BlockSpec((tm, tw), lambda bi, i: (i, 0)),
                  pl.BlockSpec((tm, tw), lambda bi, i: (i, 0)),
                  pl.BlockSpec((HEAD_DIM, tm), lambda bi, i: (0, i)),
                  pl.BlockSpec((HEAD_DIM, tm), lambda bi, i: (0, i)),
                  pl.BlockSpec((HEAD_DIM, 1), const),
                  pl.BlockSpec((1, nw), const),
                  pl.BlockSpec((nw, nw), const)] + slab_specs,
        out_specs=[pl.BlockSpec((1, tm, pw), tok),
                   pl.BlockSpec((1, nw, tm), chan), pl.BlockSpec((1, nw, tm), chan),
                   pl.BlockSpec((1, tm, nw), tok),
                   pl.BlockSpec((1, nw, tm), chan),
                   pl.BlockSpec((1, tm, d), tok), pl.BlockSpec((1, tm, d), tok)] + slab_specs,
        out_shape=[sd((b, l, pw), F32),
                   sd((b, nw, l), BF16), sd((b, nw, l), BF16),
                   sd((b, l, nw), BF16),
                   sd((b, nw, l), BF16),
                   sd((b, l, d), BF16), sd((b, l, d), BF16)]
                  + [sd(w.shape, BF16) for w in later_weights],
        compiler_params=_params(("arbitrary", "arbitrary")),
        name="in_proj",
    )(x, mod3, norm_g, w_all, w_qvt, cos_t, sin_t, cos_tt, sin_tt, qg_tt, kg_t, mean_mat,
      *later_weights)
    return outs[:7], outs[7:]


def _mixer_kernel(u_ref, up_ref, un_ref, qrt_ref, qpt_ref,
                  k0_ref, k1_ref, k2_ref, vt0_ref, vt1_ref, vt2_ref, kc_ref, vct_ref, bias_ref,
                  pw_ref, ps_ref,
                  na_ref, pool_ref, nat_scr, e_scr, a2_scr, a4_scr, a8_scr, *, seq_len):
    i = pl.program_id(1)
    tm = u_ref.shape[1]
    nw = qrt_ref.shape[1]
    heads = nw // HEAD_DIM
    pair = 2 * HEAD_DIM

    zeros = jnp.zeros((HEAD_DIM, tm), BF16)
    ones = jnp.ones((BF16_ROWS, tm), BF16)
    k_refs = (k0_ref, k1_ref, k2_ref)
    vt_refs = (vt0_ref, vt1_ref, vt2_ref)

    def head_rows(h):
        return slice(h * HEAD_DIM, (h + 1) * HEAD_DIM)

    def scores(h):
        rows = head_rows(h)
        lanes = slice((h // 2) * pair, (h // 2 + 1) * pair)
        q_rot, q_plain = qrt_ref[0, rows, :], qpt_ref[0, rows, :]
        if h % 2 == 0:
            w_rot = jnp.concatenate([q_rot, zeros], axis=0)
            w_plain = jnp.concatenate([q_plain, zeros], axis=0)
        else:
            w_rot = jnp.concatenate([zeros, q_rot], axis=0)
            w_plain = jnp.concatenate([zeros, q_plain], axis=0)
        s = [(_dot(k_refs[j][0, :, lanes], w_rot) + bias_ref[0, h, j * tm:(j + 1) * tm, :]).astype(BF16)
             for j in range(3)]
        s.append(_dot(kc_ref[0, :, lanes], w_plain).astype(BF16))
        m = s[0].max(axis=0, keepdims=True)
        for sj in s[1:]:
            m = jnp.maximum(m, sj.max(axis=0, keepdims=True))
        return s, m

    def probabilities(s_and_m):
        s, m = s_and_m
        return [jnp.exp2(sj - m) for sj in s]

    def weighted_values(h, p):
        rows = head_rows(h)
        ot = _dot(jnp.concatenate([vct_ref[0, rows, :], ones], axis=0), p[3])
        for j in range(3):
            ot = ot + _dot(jnp.concatenate([vt_refs[j][0, rows, :], ones], axis=0), p[j])
        nat_scr[rows, :] = ot[:HEAD_DIM] / ot[HEAD_DIM:HEAD_DIM + 1]

    gd = pw_ref.shape[-1]
    hi = tm + 2 * POOL_HALO
    first = POOL_HALO
    g0 = slice(0, gd)

    def rows_at(ref, off, cols):
        return ref[first + off:first + off + tm, cols]

    def pool_fill():
        for scr in (e_scr, a2_scr, a4_scr):
            scr[hi:hi + SUBLANES, :] = jnp.zeros((SUBLANES, scr.shape[1]), F32)
        e_scr[0:first, :] = jnp.where(i > 0, up_ref[0], 0.0)
        e_scr[first:first + tm, :] = u_ref[0]
        e_scr[first + tm:hi, :] = jnp.where((i + 1) * tm < seq_len, un_ref[0], 0.0)
        a2_scr[0:hi, :] = e_scr[0:hi, gd:] + e_scr[1:hi + 1, gd:]

    def pool_double():
        a4_scr[0:hi, :] = a2_scr[0:hi, gd:] + a2_scr[2:hi + 2, gd:]
        a8_scr[0:hi, :] = a4_scr[0:hi, gd:] + a4_scr[4:hi + 4, gd:]

    def pool_group(g):
        w = POOL_WINDOWS[g]
        if g == 0:
            total = rows_at(e_scr, -1, g0) + rows_at(e_scr, 0, g0)
        elif g == 1:
            total = rows_at(a2_scr, -2, g0) + rows_at(a2_scr, 0, g0)
        elif g == 2:
            total = rows_at(a4_scr, -4, g0) + rows_at(a4_scr, 0, g0)
        else:
            total = rows_at(a8_scr, -8, g0) + rows_at(a8_scr, 0, g0)
        pos = i * tm + lax.broadcasted_iota(jnp.int32, (tm, 1), 0)
        cnt = (jnp.minimum(pos + w // 2, seq_len) - jnp.maximum(pos - w // 2, 0)).astype(F32)
        dlt = total / cnt - u_ref[0, :, g * gd:(g + 1) * gd]
        cols = slice(g * gd, (g + 1) * gd)
        return (_dot(dlt.astype(BF16), pw_ref[g].astype(BF16)) * ps_ref[:, cols]).astype(BF16)

    pooled = []
    pool_stages = [pool_fill, pool_double] + [functools.partial(pool_group, g)
                                               for g in range(len(POOL_WINDOWS))]

    p_next = probabilities(scores(0))
    for h in range(heads):
        p_cur = p_next
        if h + 1 < heads:
            p_next = probabilities(scores(h + 1))
        weighted_values(h, p_cur)
        stage = h - (heads - len(pool_stages))
        if stage >= 0:
            out = pool_stages[stage]()
            if out is not None:
                pooled.append(out)
        if h % 2 == 1:
            slab = slice((h - 1) * HEAD_DIM, (h + 1) * HEAD_DIM)
            na_ref[0, :, slab] = nat_scr[slab, :].T.astype(BF16)

    pool_ref[0] = jnp.concatenate(pooled, axis=1)


def _mixer(u, qrt, qpt, kr, vt, kc, vct, bias, pool_w, pool_scale):
    b, l, pw = u.shape
    nw = kr.shape[-1]
    gd = pool_w.shape[-1]
    tm = TILE_ROWS * GRID_W
    nt = l // tm
    hb = tm // POOL_HALO
    n_halo = l // POOL_HALO
    nc = kc.shape[1]
    assert pw == len(POOL_WINDOWS) * gd and POOL_HALO == max(POOL_WINDOWS) // 2

    tok = lambda bi, i: (bi, i, 0)
    chan = lambda bi, i: (bi, 0, i)
    const2 = lambda bi, i: (0, 0)
    first_block = lambda i: jnp.clip(i - 1, 0, nt - 3)

    def k_spec(j):
        return pl.BlockSpec((1, tm, nw), lambda bi, i: (bi, first_block(i) + j, 0))

    def vt_spec(j):
        return pl.BlockSpec((1, nw, tm), lambda bi, i: (bi, 0, first_block(i) + j))

    def bias_idx(bi, i):
        return (jnp.where(i == 0, 0, jnp.where(i == nt - 1, 2, 1)), 0, 0, 0)

    in_specs = [
        pl.BlockSpec((1, tm, pw), tok),
        pl.BlockSpec((1, POOL_HALO, pw), lambda bi, i: (bi, jnp.maximum(i * hb - 1, 0), 0)),
        pl.BlockSpec((1, POOL_HALO, pw), lambda bi, i: (bi, jnp.minimum((i + 1) * hb, n_halo - 1), 0)),
        pl.BlockSpec((1, nw, tm), chan),
        pl.BlockSpec((1, nw, tm), chan),
        k_spec(0), k_spec(1), k_spec(2),
        vt_spec(0), vt_spec(1), vt_spec(2),
        pl.BlockSpec((1, nc, nw), lambda bi, i: (bi, 0, 0)),
        pl.BlockSpec((1, nw, nc), lambda bi, i: (bi, 0, 0)),
        pl.BlockSpec((1,) + bias.shape[1:], bias_idx),
        pl.BlockSpec(pool_w.shape, lambda bi, i: (0, 0, 0)),
        pl.BlockSpec((1, pw), const2),
    ]
    n_scr = tm + 2 * POOL_HALO + SUBLANES
    return pl.pallas_call(
        functools.partial(_mixer_kernel, seq_len=l),
        grid=(b, nt),
        in_specs=in_specs,
        out_specs=[pl.BlockSpec((1, tm, nw), tok), pl.BlockSpec((1, tm, pw), tok)],
        out_shape=[jax.ShapeDtypeStruct((b, l, nw), BF16), jax.ShapeDtypeStruct((b, l, pw), BF16)],
        scratch_shapes=[pltpu.VMEM((nw, tm), F32),
                        pltpu.VMEM((n_scr, pw), F32),
                        pltpu.VMEM((n_scr, pw - gd), F32),
                        pltpu.VMEM((n_scr, pw - 2 * gd), F32),
                        pltpu.VMEM((n_scr, pw - 3 * gd), F32)],
        compiler_params=_params(("arbitrary", "arbitrary")),
        name="mixer",
    )(u, u, u, qrt, qpt, kr, kr, kr, vt, vt, vt, kc, vct, bias, pool_w, pool_scale)


def _post_kernel(x_ref, mod_ref, g_ref, na_ref, pool_ref, gp_ref, gn_ref,
                 wbp_ref, wbn_ref, wo_ref, w1_ref, w2_ref, o_ref, *, chunk):
    merged = (gp_ref[0] * _dot(pool_ref[0], wbp_ref[...]).astype(BF16)
              + gn_ref[0] * _dot(na_ref[0], wbn_ref[...]).astype(BF16))
    row = pl.program_id(0)
    x = x_ref[0] + _mod(mod_ref, 2, row) * _dot(merged, wo_ref[...])
    hb = _modulated_norm(x, mod_ref, g_ref, 3, row).astype(BF16)
    acc = jnp.zeros(x.shape, F32)
    for c0 in range(0, w1_ref.shape[1], chunk):
        a = jnp.maximum(_dot(hb, w1_ref[:, c0:c0 + chunk]), 0.0)
        acc = acc + _dot((a * a).astype(BF16), w2_ref[c0:c0 + chunk, :])
    o_ref[0] = x + _mod(mod_ref, 5, row) * acc


def _post(x, mod3, norm_g, na, pool, gp, gn, wbp, wbn, wo, w1, w2, tm):
    b, l, d = x.shape
    tok = lambda bi, i: (bi, i, 0)
    const = lambda bi, i: (0, 0)
    resident = lambda w: pl.BlockSpec(w.shape, const, pipeline_mode=pl.Buffered(1))
    return pl.pallas_call(
        functools.partial(_post_kernel, chunk=d),
        grid=(b, l // tm),
        in_specs=[pl.BlockSpec((1, tm, d), tok),
                  pl.BlockSpec(mod3.shape, lambda bi, i: (0, 0, 0)),
                  pl.BlockSpec((1, d), const),
                  pl.BlockSpec((1, tm, na.shape[-1]), tok),
                  pl.BlockSpec((1, tm, pool.shape[-1]), tok),
                  pl.BlockSpec((1, tm, d), tok),
                  pl.BlockSpec((1, tm, d), tok),
                  resident(wbp), resident(wbn), resident(wo), resident(w1), resident(w2)],
        out_specs=pl.BlockSpec((1, tm, d), tok),
        out_shape=jax.ShapeDtypeStruct((b, l, d), F32),
        compiler_params=_params(("arbitrary", "arbitrary")),
        name="post",
    )(x, mod3, norm_g, na, pool, gp, gn, wbp, wbn, wo, w1, w2)


def _rope_tables(seq_len):
    n_freq = HEAD_DIM // 4
    t = np.arange(seq_len)
    inv = (ROPE_THETA ** (-np.arange(n_freq, dtype=np.float32) / n_freq)).astype(np.float32)
    ang_row = (t // GRID_W).astype(np.float32)[:, None] * inv
    ang_col = (t % GRID_W).astype(np.float32)[:, None] * inv
    cr, sr, cc, sc = np.cos(ang_row), np.sin(ang_row), np.cos(ang_col), np.sin(ang_col)
    cos = np.concatenate([cr, cr, cc, cc], axis=1).astype(np.float32)
    sin = np.concatenate([-sr, sr, -sc, sc], axis=1).astype(np.float32)
    return (jnp.asarray(np.tile(cos, (1, 2))), jnp.asarray(np.tile(sin, (1, 2))),
            jnp.asarray(np.ascontiguousarray(cos.T)), jnp.asarray(np.ascontiguousarray(sin.T)))


def _bias_kernel(el_ref, er_ref, o_ref, *, rows):
    lanes = 2 * GRID_W
    kc = lax.broadcasted_iota(jnp.int32, (GRID_W, lanes), 0)
    lane = lax.broadcasted_iota(jnp.int32, (GRID_W, lanes), 1)
    qc = lane % GRID_W
    c0 = jnp.clip(qc - NA_COLS // 2, 0, GRID_W - NA_COLS)
    col_ok = (kc >= c0) & (kc < c0 + NA_COLS)
    left = lane < GRID_W
    masked = jnp.full((GRID_W, lanes), MASK_VALUE, F32)

    cache = {}

    def toeplitz(side, dr):
        if (side, dr) not in cache:
            ref = er_ref if side else el_ref
            dd = dr + NA_ROWS - 1
            vec = jnp.broadcast_to(ref[0, dd:dd + 1, :], (GRID_W, lanes))
            cache[(side, dr)] = pltpu.roll(vec, 0, axis=1, stride=1, stride_axis=0) * LOG2E
        return cache[(side, dr)]

    nt = rows // TILE_ROWS
    for t, tile in enumerate((0, nt // 2, nt - 1)):
        r = tile * TILE_ROWS
        ks = min(max(r - TILE_ROWS, 0), rows - KEY_ROWS)
        for j in range(KEY_ROWS):
            kr = ks + j
            for ip in range(TILE_ROWS // 2):
                halves = []
                for side in range(2):
                    qr = r + 2 * ip + side
                    r0 = min(max(qr - NA_ROWS // 2, 0), rows - NA_ROWS)
                    ok = r0 <= kr < r0 + NA_ROWS
                    halves.append(toeplitz(side, kr - qr) if ok else masked)
                blk = jnp.where(col_ok, jnp.where(left, halves[0], halves[1]), MASK_VALUE)
                o_ref[t, 0, j * GRID_W:(j + 1) * GRID_W, ip * lanes:(ip + 1) * lanes] = blk


def _bias_vectors(rpb):
    heads, nr, ncol = rpb.shape
    lanes = 2 * GRID_W
    half = NA_COLS - 1
    zeros = lambda n: jnp.zeros((heads, nr, n), F32)
    flipped = rpb[..., ::-1]
    e_left = jnp.concatenate([flipped[..., half:], zeros(lanes - ncol), flipped[..., :half]], axis=-1)
    e_right = jnp.concatenate([zeros(GRID_W - half), flipped, zeros(lanes - GRID_W + half - ncol)], axis=-1)
    pad = (-nr) % SUBLANES
    e_left = jnp.pad(e_left, ((0, 0), (0, pad), (0, 0)))
    e_right = jnp.pad(e_right, ((0, 0), (0, pad), (0, 0)))
    return e_left, e_right


def kernel(x, c, ctx, c_ctx, ada_w, ada_b, norm1_g, norm2_g, w_in, pool_w, pool_scale,
           q_norm_g, k_norm_g, rpb, w_branch_pool, w_branch_na, w_out, mlp_w1, mlp_w2):
    b, l, d = x.shape
    depth = ada_w.shape[0]
    pw = pool_scale.shape[-1]
    nw = w_branch_na.shape[1]
    heads = nw // HEAD_DIM
    rows = l // GRID_W
    assert depth == 1, "the context-stream update is only needed when another layer follows"
    assert l % (TILE_ROWS * GRID_W) == 0 and rows >= KEY_ROWS
    assert w_in.shape[-1] == pw + 3 * nw + 2 * d and pw == nw
    ctx_row = b
    tm_proj = 1024

    cos_t, sin_t, cos_tt, sin_tt = _rope_tables(l)
    mean_mat = jnp.asarray(np.kron(np.eye(heads), np.full((HEAD_DIM, HEAD_DIM), 1.0 / HEAD_DIM)), BF16)
    pad = (-(b + 1)) % SUBLANES
    cc = jnp.concatenate([c, c_ctx[None], jnp.zeros((pad, d), F32)], axis=0)

    layer = 0
    q0, k0, v0, g0 = pw, pw + nw, pw + 2 * nw, pw + 3 * nw
    qg_tt = q_norm_g[layer][:, None]
    kg_t = jnp.tile(k_norm_g[layer], heads)[None]
    n1 = norm1_g[layer][None]
    mod3, w_all, w_qvt, bias, k_ctx, vt_ctx = _prep(
        cc, ada_w[layer], ada_b[layer], w_in[layer], q0, k0, v0, nw, rpb[layer], rows,
        ctx, n1, kg_t, mean_mat, ctx_row)
    later_weights = (w_branch_pool[layer], w_branch_na[layer], w_out[layer],
                     mlp_w1[layer], mlp_w2[layer])
    (u, qrt, qpt, kr, vt, gp, gn), (wbp, wbn, wo, w1, w2) = _in_proj(
        x, mod3, n1, w_all, w_qvt, cos_t, sin_t, cos_tt, sin_tt, qg_tt, kg_t, mean_mat,
        (0, k0, g0, g0 + d), pw, tm_proj, later_weights)
    na, pool = _mixer(u, qrt, qpt, kr, vt, k_ctx, vt_ctx, bias,
                      pool_w[layer], pool_scale[layer][None])
    return _post(x, mod3, norm2_g[layer][None], na, pool, gp, gn, wbp, wbn, wo, w1, w2, tm=tm_proj)
```
